```python
import math
import jax, jax.numpy as jnp
from jax import lax
import numpy as np

D_MODEL = 1024
BATCH = 32
SEQ = 256
DEPTH = 2
DEC_BATCH = 4
DEC_SEQ = 4096
PAST_LEN = 512

GRID_W = 64
MIX_POOL = D_MODEL // 2
POOL_WINDOWS = (2, 4, 8, 16)
N_POOL_GROUPS = len(POOL_WINDOWS)
POOL_GROUP_CH = MIX_POOL // N_POOL_GROUPS
SSD_INNER = D_MODEL - MIX_POOL
SSD_HEAD_DIM = 64
SSD_HEADS = SSD_INNER // SSD_HEAD_DIM
SSD_GROUPS = 2
SSD_STATE = 128
CONV_K = 5
CHUNK = 128
CONV_DIM = SSD_INNER + 2 * SSD_GROUPS * SSD_STATE
D_IN_PROJ = MIX_POOL + SSD_INNER + CONV_DIM + 2 * SSD_HEADS
N_EXPERTS = 16
EXPERT_FF = D_MODEL
CAPACITY_FACTOR = 2
N_MOD = 6
POS_BASE = 10000.0
EPS = 1e-6

kernel_name = 'hybrid_pool_ssd_ec_diffusion_step'


def rms_norm(x, g):
    xf = x.astype(jnp.float32)
    y = xf * lax.rsqrt(jnp.mean(xf * xf, axis=-1, keepdims=True) + EPS)
    return (y * g.astype(jnp.float32)).astype(x.dtype)


def grid_position_embedding(n_tokens, dim):
    rows = n_tokens // GRID_W
    row = jnp.repeat(jnp.arange(rows), GRID_W).astype(jnp.float32)
    col = jnp.tile(jnp.arange(GRID_W), rows).astype(jnp.float32)
    quarter = dim // 4
    inv_freq = jnp.power(POS_BASE, -jnp.arange(quarter, dtype=jnp.float32) / quarter)
    er = row[:, None] * inv_freq[None]
    ec = col[:, None] * inv_freq[None]
    return jnp.concatenate([jnp.sin(er), jnp.cos(er), jnp.sin(ec), jnp.cos(ec)], axis=-1)


def multiscale_pool(u, w_pool, scale):
    b, L, _ = u.shape
    uf = u.astype(jnp.float32)
    csum = jnp.concatenate([jnp.zeros((b, 1, MIX_POOL), jnp.float32), jnp.cumsum(uf, axis=1)], axis=1)
    t = jnp.arange(L)
    outs = []
    for gi, w in enumerate(POOL_WINDOWS):
        lo = jnp.clip(t - w // 2, 0, L)
        hi = jnp.clip(t - w // 2 + w, 0, L)
        sl = slice(gi * POOL_GROUP_CH, (gi + 1) * POOL_GROUP_CH)
        cs_g = csum[..., sl]
        mean = (cs_g[:, hi] - cs_g[:, lo]) / (hi - lo).astype(jnp.float32)[None, :, None]
        outs.append(mean - uf[..., sl])
    p = jnp.stack(outs, axis=2)
    y = jnp.einsum('blgc,gcd->blgd', p, w_pool.astype(jnp.float32)).reshape(b, L, MIX_POOL)
    return (y * scale.astype(jnp.float32)).astype(u.dtype)


def centred_depthwise_conv(u, w, bias):
    pad = CONV_K // 2
    y = lax.conv_general_dilated(u, w[:, None, :].astype(u.dtype), window_strides=(1,), padding=[(pad, pad)],
                                 dimension_numbers=('NWC', 'WIO', 'NWC'), feature_group_count=u.shape[-1])
    return y + bias.astype(u.dtype)


def ssd_chunked_scan(x, dt, a, bm, cm, h0):
    b, L, H, P = x.shape
    G, N = bm.shape[2], bm.shape[3]
    R = H // G
    nc = L // CHUNK
    X = (x * dt[..., None]).reshape(b, nc, CHUNK, G, R, P)
    a_cum = jnp.cumsum((dt * a).reshape(b, nc, CHUNK, G, R), axis=2)
    bc = bm.reshape(b, nc, CHUNK, G, N)
    cc = cm.reshape(b, nc, CHUNK, G, N)
    lower = jnp.tril(jnp.ones((CHUNK, CHUNK), bool))[None, None, :, :, None, None]
    seg = a_cum[:, :, :, None] - a_cum[:, :, None, :]
    decay_in = jnp.exp(jnp.where(lower, seg, -jnp.inf))
    cb = jnp.einsum('bclgn,bcsgn->bclsg', cc, bc)
    y_diag = jnp.einsum('bclsgr,bcsgrp->bclgrp', cb[..., None] * decay_in, X)
    decay_to_end = jnp.exp(a_cum[:, :, -1:] - a_cum)
    chunk_states = jnp.einsum('bclgn,bclgrp->bcgrpn', bc, X * decay_to_end[..., None])
    chunk_decay = jnp.exp(a_cum[:, :, -1])

    def step(h, inp):
        st, dec = inp
        return h * dec[..., None, None] + st, h

    h_init = h0.astype(jnp.float32).reshape(b, G, R, P, N)
    h_final, h_enter = lax.scan(step, h_init, (jnp.moveaxis(chunk_states, 1, 0), jnp.moveaxis(chunk_decay, 1, 0)))
    h_enter = jnp.moveaxis(h_enter, 0, 1)
    y_off = jnp.einsum('bclgn,bcgrpn->bclgrp', cc, h_enter) * jnp.exp(a_cum)[..., None]
    return (y_diag + y_off).reshape(b, L, H, P), h_final.reshape(b, H, P, N)


def expert_choice_ffn(h, w_router, w_gate, w_up, w_down):
    b, L, D = h.shape
    cap = CAPACITY_FACTOR * L // N_EXPERTS
    affinity = jax.nn.softmax(jnp.einsum('bld,de->ble', h, w_router).astype(jnp.float32), axis=-1)
    gate, idx = lax.top_k(jnp.swapaxes(affinity, 1, 2), cap)
    xe = jax.vmap(lambda hb, ib: hb[ib])(h, idx)
    hid = jax.nn.silu(jnp.einsum('becd,edf->becf', xe, w_gate)) * jnp.einsum('becd,edf->becf', xe, w_up)
    ye = jnp.einsum('becf,efd->becd', hid, w_down) * gate[..., None].astype(h.dtype)
    return jax.vmap(lambda ib, yb: jnp.zeros((L, D), yb.dtype).at[ib.reshape(-1)].add(yb.reshape(-1, D)))(idx, ye)


def trunk_layer(x, cond, h0_fwd, h0_bwd, norm_mix_g, w_ada, b_ada, w_in, pool_w, pool_scale, conv_w, conv_b,
                a_log_fwd, a_log_bwd, dt_bias_fwd, dt_bias_bwd, d_skip, ssm_norm_g, w_out, norm_ffn_g,
                w_router, w_gate, w_up, w_down):
    f32 = jnp.float32
    b, L, _ = x.shape
    mod = (jax.nn.silu(cond.astype(f32)) @ w_ada.astype(f32) + b_ada.astype(f32)).astype(x.dtype)
    sh_a, sc_a, g_a, sh_m, sc_m, g_m = jnp.split(mod[:, None, :], N_MOD, axis=-1)
    h = rms_norm(x, norm_mix_g) * (1 + sc_a) + sh_a
    proj = h @ w_in
    u_pool, z, xbc, dt_raw = jnp.split(proj, [MIX_POOL, MIX_POOL + SSD_INNER, MIX_POOL + SSD_INNER + CONV_DIM], axis=-1)
    y_pool = multiscale_pool(u_pool, pool_w, pool_scale)
    xbc = jax.nn.silu(centred_depthwise_conv(xbc, conv_w, conv_b)).astype(f32)
    xs, bm, cm = jnp.split(xbc, [SSD_INNER, SSD_INNER + SSD_GROUPS * SSD_STATE], axis=-1)
    xh = xs.reshape(b, L, SSD_HEADS, SSD_HEAD_DIM)
    bm = bm.reshape(b, L, SSD_GROUPS, SSD_STATE)
    cm = cm.reshape(b, L, SSD_GROUPS, SSD_STATE)
    dt_raw = dt_raw.astype(f32)
    dt_f = jax.nn.softplus(dt_raw[..., :SSD_HEADS] + dt_bias_fwd.astype(f32))
    dt_b = jax.nn.softplus(dt_raw[..., SSD_HEADS:] + dt_bias_bwd.astype(f32))
    y_f, h_f = ssd_chunked_scan(xh, dt_f, -jnp.exp(a_log_fwd.astype(f32)), bm, cm, h0_fwd)
    y_b, h_b = ssd_chunked_scan(xh[:, ::-1], dt_b[:, ::-1], -jnp.exp(a_log_bwd.astype(f32)),
                                bm[:, ::-1], cm[:, ::-1], h0_bwd)
    y = y_f + y_b[:, ::-1] + d_skip.astype(f32)[:, None] * xh
    y = y.reshape(b, L, SSD_INNER) * jax.nn.silu(z.astype(f32))
    y_ssd = rms_norm(y, ssm_norm_g).astype(x.dtype)
    x = x + g_a * (jnp.concatenate([y_pool, y_ssd], axis=-1) @ w_out)
    h = rms_norm(x, norm_ffn_g) * (1 + sc_m) + sh_m
    x = x + g_m * expert_choice_ffn(h, w_router, w_gate, w_up, w_down)
    return x, h_f, h_b


def setup_inputs(seed: int = 0) -> dict:
    key = jax.random.key(seed)
    ks = jax.random.split(key, 32)
    f32 = jnp.float32

    def nrm(k, shape, scale):
        return jax.random.normal(k, shape, f32) * scale

    state_shape = (DEC_BATCH, DEPTH, SSD_HEADS, SSD_HEAD_DIM, SSD_STATE)
    dt_init = jnp.exp(jax.random.uniform(ks[10], (2, DEPTH, SSD_HEADS), f32, math.log(1e-3), math.log(1e-1)))
    dt_bias = dt_init + jnp.log(-jnp.expm1(-dt_init))
    a_init = jax.random.uniform(ks[11], (2, DEPTH, SSD_HEADS), f32, 1.0, 16.0)
    return {
        'x_prompt': nrm(ks[0], (BATCH, SEQ, D_MODEL), 1.0),
        'x_sample': nrm(ks[1], (DEC_BATCH, DEC_SEQ, D_MODEL), 1.0),
        'state_ssm_fwd': nrm(ks[2], state_shape, 0.1),
        'state_ssm_bwd': nrm(ks[3], state_shape, 0.1),
        'c': nrm(ks[4], (DEC_BATCH, D_MODEL), 1.0),
        'c_ctx': nrm(ks[5], (D_MODEL,), 1.0),
        'norm_mix_g': 1.0 + nrm(ks[6], (DEPTH, D_MODEL), 0.05),
        'w_ada': nrm(ks[7], (DEPTH, D_MODEL, N_MOD * D_MODEL), 0.5 * D_MODEL ** -0.5),
        'b_ada': nrm(ks[8], (DEPTH, N_MOD * D_MODEL), 0.02),
        'w_in': nrm(ks[9], (DEPTH, D_MODEL, D_IN_PROJ), D_MODEL ** -0.5),
        'pool_w': nrm(ks[12], (DEPTH, N_POOL_GROUPS, POOL_GROUP_CH, POOL_GROUP_CH), POOL_GROUP_CH ** -0.5),
        'pool_scale': 0.5 + nrm(ks[13], (DEPTH, MIX_POOL), 0.1),
        'conv_w': nrm(ks[14], (DEPTH, CONV_K, CONV_DIM), CONV_K ** -0.5),
        'conv_b': nrm(ks[15], (DEPTH, CONV_DIM), 0.02),
        'a_log_fwd': jnp.log(a_init[0]),
        'a_log_bwd': jnp.log(a_init[1]),
        'dt_bias_fwd': dt_bias[0],
        'dt_bias_bwd': dt_bias[1],
        'd_skip': 1.0 + nrm(ks[16], (DEPTH, SSD_HEADS), 0.1),
        'ssm_norm_g': 1.0 + nrm(ks[17], (DEPTH, SSD_INNER), 0.05),
        'w_out': nrm(ks[18], (DEPTH, D_MODEL, D_MODEL), D_MODEL ** -0.5),
        'norm_ffn_g': 1.0 + nrm(ks[19], (DEPTH, D_MODEL), 0.05),
        'w_router': nrm(ks[20], (DEPTH, D_MODEL, N_EXPERTS), D_MODEL ** -0.5),
        'w_gate': nrm(ks[21], (DEPTH, N_EXPERTS, D_MODEL, EXPERT_FF), D_MODEL ** -0.5),
        'w_up': nrm(ks[22], (DEPTH, N_EXPERTS, D_MODEL, EXPERT_FF), D_MODEL ** -0.5),
        'w_down': nrm(ks[23], (DEPTH, N_EXPERTS, EXPERT_FF, D_MODEL), EXPERT_FF ** -0.5),
        'final_norm_g': 1.0 + nrm(ks[24], (D_MODEL,), 0.05),
    }


def reference(x_prompt, x_sample, state_ssm_fwd, state_ssm_bwd, c, c_ctx, norm_mix_g, w_ada, b_ada, w_in,
              pool_w, pool_scale, conv_w, conv_b, a_log_fwd, a_log_bwd, dt_bias_fwd, dt_bias_bwd, d_skip,
              ssm_norm_g, w_out, norm_ffn_g, w_router, w_gate, w_up, w_down, final_norm_g):
    def layer_weights(l):
        return (norm_mix_g[l], w_ada[l], b_ada[l], w_in[l], pool_w[l], pool_scale[l], conv_w[l], conv_b[l],
                a_log_fwd[l], a_log_bwd[l], dt_bias_fwd[l], dt_bias_bwd[l], d_skip[l], ssm_norm_g[l], w_out[l],
                norm_ffn_g[l], w_router[l], w_gate[l], w_up[l], w_down[l])

    xp = x_prompt
    zero_state = jnp.zeros((x_prompt.shape[0], SSD_HEADS, SSD_HEAD_DIM, SSD_STATE), jnp.float32)
    ctx_cond = c_ctx[None, :]
    fwd_states, bwd_states = [], []
    for l in range(DEPTH):
        xp, h_f, h_b = trunk_layer(xp, ctx_cond, zero_state, zero_state, *layer_weights(l))
        fwd_states.append(h_f)
        bwd_states.append(h_b)
    y_prompt = rms_norm(xp, final_norm_g)
    new_state_ssm_fwd = jnp.stack(fwd_states, axis=1)
    new_state_ssm_bwd = jnp.stack(bwd_states, axis=1)

    xs = x_sample + grid_position_embedding(x_sample.shape[1], D_MODEL).astype(x_sample.dtype)[None]
    for l in range(DEPTH):
        xs, _, _ = trunk_layer(xs, c, state_ssm_fwd[:, l], state_ssm_bwd[:, l], *layer_weights(l))
    y_sample = rms_norm(xs, final_norm_g)
    return (y_prompt, y_sample, new_state_ssm_fwd, new_state_ssm_bwd)
```

```python
import functools

import jax
import jax.numpy as jnp
from jax import lax
from jax.experimental import pallas as pl
from jax.experimental.pallas import tpu as pltpu

D_MODEL = 1024
DEPTH = 2
GRID_W = 64
MIX_POOL = 512
POOL_WINDOWS = (2, 4, 8, 16)
POOL_GROUP_CH = 128
SSD_INNER = 512
SSD_HEADS = 8
SSD_STATE = 128
CONV_K = 5
CHUNK = 128
CONV_DIM = 1024
N_EXPERTS = 16
CAPACITY_FACTOR = 2
N_MOD = 6
POS_BASE = 10000.0
EPS = 1e-6

HALO = 8
LANES = 128
N_MAIN = MIX_POOL + SSD_INNER + CONV_DIM
HEAD_PAIRS = SSD_HEADS // 2
VMEM_LIMIT = 56 * 1024 * 1024

F32 = jnp.float32
BF16 = jnp.bfloat16
HI = lax.Precision.HIGHEST


def _sigmoid(x):
    return 1.0 / (1.0 + jnp.exp(-x))


def _params(sem):
    return pltpu.CompilerParams(dimension_semantics=sem, vmem_limit_bytes=VMEM_LIMIT)


def _mod_kernel(cond_ref, w_ref, b_ref, out_ref):
    c = cond_ref[...]
    s = c * _sigmoid(c)
    out_ref[0] = jnp.dot(s, w_ref[0], precision=HI, preferred_element_type=F32) + b_ref[0]


def _modulation(cond, w_ada, b_ada):
    n_rows = cond.shape[0]
    n_out = N_MOD * D_MODEL
    tn = 1536
    return pl.pallas_call(
        _mod_kernel,
        grid=(DEPTH, n_out // tn),
        in_specs=[
            pl.BlockSpec((n_rows, D_MODEL), lambda l, j: (0, 0)),
            pl.BlockSpec((1, D_MODEL, tn), lambda l, j: (l, 0, j)),
            pl.BlockSpec((1, 1, tn), lambda l, j: (l, 0, j)),
        ],
        out_specs=pl.BlockSpec((1, n_rows, tn), lambda l, j: (l, 0, j)),
        out_shape=jax.ShapeDtypeStruct((DEPTH, n_rows, n_out), F32),
        compiler_params=_params(("arbitrary", "arbitrary")),
        name="modulation",
    )(cond, w_ada, b_ada.reshape(DEPTH, 1, n_out))


def _in_kernel(xp_ref, xc_ref, xn_ref, mod_ref, g_ref, wmain_ref, wdt_ref, poolw_ref, pscale_ref, cw_ref, cb_ref,
               ypool_ref, z_ref, xs_ref, bc_ref, dt_ref, proj_scr, *, tl, seq_len):
    i = pl.program_id(1)
    n = pl.num_programs(1)
    rows = tl + 2 * HALO
    x_ext = jnp.concatenate([xp_ref[0], xc_ref[0], xn_ref[0]], axis=0)
    mod = mod_ref[0]
    ms = jnp.mean(x_ext * x_ext, axis=-1, keepdims=True)
    h = x_ext * lax.rsqrt(ms + EPS) * (g_ref[...] * (1.0 + mod[1:2])) + mod[0:1]
    hb = h.astype(BF16)
    proj = jnp.dot(hb, wmain_ref[...], preferred_element_type=F32)
    row = lax.broadcasted_iota(jnp.int32, (rows, 1), 0)
    lo_ok = jnp.where(i > 0, 0, HALO)
    hi_ok = jnp.where(i < n - 1, rows, tl + HALO)
    proj_scr[...] = jnp.where((row >= lo_ok) & (row < hi_ok), proj, 0.0)
    dt_full = jnp.dot(hb, wdt_ref[...], preferred_element_type=F32)
    dt_ref[0] = dt_full[HALO:HALO + tl]

    t_glob = i * tl + lax.broadcasted_iota(jnp.int32, (tl, 1), 0)
    outs = []
    for gi, w in enumerate(POOL_WINDOWS):
        c0 = gi * POOL_GROUP_CH
        start = HALO - w // 2
        acc = proj_scr[start:start + tl, c0:c0 + POOL_GROUP_CH]
        for j in range(1, w):
            acc = acc + proj_scr[start + j:start + j + tl, c0:c0 + POOL_GROUP_CH]
        lo = jnp.maximum(t_glob - w // 2, 0)
        hi = jnp.minimum(t_glob - w // 2 + w, seq_len)
        cnt = (hi - lo).astype(F32)
        p = acc / cnt - proj_scr[HALO:HALO + tl, c0:c0 + POOL_GROUP_CH]
        outs.append(jnp.dot(p.astype(BF16), poolw_ref[gi], preferred_element_type=F32))
    ypool_ref[0] = (jnp.concatenate(outs, axis=1) * pscale_ref[...]).astype(BF16)

    c0 = MIX_POOL + SSD_INNER
    acc = cb_ref[...] + cw_ref[0:1, :] * proj_scr[HALO - 2:HALO - 2 + tl, c0:c0 + CONV_DIM]
    for k in range(1, CONV_K):
        acc = acc + cw_ref[k:k + 1, :] * proj_scr[HALO - 2 + k:HALO - 2 + k + tl, c0:c0 + CONV_DIM]
    act = acc * _sigmoid(acc)
    xs_ref[0] = act[:, :SSD_INNER]
    bc_ref[0] = act[:, SSD_INNER:]
    z_ref[0] = proj_scr[HALO:HALO + tl, MIX_POOL:MIX_POOL + SSD_INNER]


def _in_proj(x, mod_l, mod_row, norm_g, w_main, w_dt, pool_w, pool_scale, conv_w, conv_b, tl):
    bn, seq_len, _ = x.shape
    nt = seq_len // tl
    hb = tl // HALO
    last_halo = seq_len // HALO - 1
    const = lambda *shape: pl.BlockSpec(shape, lambda b, i: (0,) * len(shape))
    tok = lambda width: pl.BlockSpec((1, tl, width), lambda b, i: (b, i, 0))
    return pl.pallas_call(
        functools.partial(_in_kernel, tl=tl, seq_len=seq_len),
        grid=(bn, nt),
        in_specs=[
            pl.BlockSpec((1, HALO, D_MODEL), lambda b, i: (b, jnp.maximum(i * hb - 1, 0), 0)),
            tok(D_MODEL),
            pl.BlockSpec((1, HALO, D_MODEL), lambda b, i: (b, jnp.minimum((i + 1) * hb, last_halo), 0)),
            pl.BlockSpec((1, N_MOD, D_MODEL), lambda b, i: (mod_row(b), 0, 0)),
            const(1, D_MODEL),
            const(D_MODEL, N_MAIN),
            const(D_MODEL, LANES),
            const(len(POOL_WINDOWS), POOL_GROUP_CH, POOL_GROUP_CH),
            const(1, MIX_POOL),
            const(CONV_K, CONV_DIM),
            const(1, CONV_DIM),
        ],
        out_specs=[tok(MIX_POOL), tok(SSD_INNER), tok(SSD_INNER), tok(CONV_DIM - SSD_INNER), tok(LANES)],
        out_shape=[
            jax.ShapeDtypeStruct((bn, seq_len, MIX_POOL), BF16),
            jax.ShapeDtypeStruct((bn, seq_len, SSD_INNER), F32),
            jax.ShapeDtypeStruct((bn, seq_len, SSD_INNER), F32),
            jax.ShapeDtypeStruct((bn, seq_len, CONV_DIM - SSD_INNER), F32),
            jax.ShapeDtypeStruct((bn, seq_len, LANES), F32),
        ],
        scratch_shapes=[pltpu.VMEM((tl + 2 * HALO, N_MAIN), F32)],
        compiler_params=_params(("arbitrary", "arbitrary")),
        name="in_proj",
    )(x, x, x, mod_l, norm_g, w_main, w_dt, pool_w, pool_scale, conv_w, conv_b)


def _ssd_kernel(xsf_ref, bcf_ref, dtf_ref, xsb_ref, bcb_ref, dtb_ref, alog_ref, bias_ref, h0f_ref, h0b_ref,
                yf_ref, yb_ref, hTf_ref, hTb_ref, st_ref):
    c = pl.program_id(1)
    nc = pl.num_programs(1)

    @pl.when(c == 0)
    def _():
        st_ref[0] = h0f_ref[0]
        st_ref[1] = h0b_ref[0]

    r = lax.broadcasted_iota(jnp.int32, (CHUNK, CHUNK), 0)
    s = lax.broadcasted_iota(jnp.int32, (CHUNK, CHUNK), 1)
    left = s < (LANES // 2)
    top = r < (CHUNK // 2)
    a_row = -jnp.exp(alog_ref[...])

    def one_dir(d, xs_ref, bc_ref, dt_ref, y_ref):
        mask = (r >= s) if d == 0 else (r <= s)
        off = d * SSD_HEADS
        x_in = dt_ref[0] + bias_ref[...]
        dt = jnp.maximum(x_in, 0.0) + jnp.log1p(jnp.exp(-jnp.abs(x_in)))
        dta = dt * a_row
        cum = jnp.dot(mask.astype(F32), dta, precision=HI, preferred_element_type=F32)
        cum_t = cum.T
        tot = cum[CHUNK - 1:CHUNK, :] if d == 0 else cum[0:1, :]
        ecum = jnp.exp(cum)
        dte = jnp.exp(tot - cum)
        cdec = jnp.exp(tot)
        bcv = bc_ref[0]
        xs = xs_ref[0]
        for g in range(2):
            b_g = bcv[:, g * SSD_STATE:(g + 1) * SSD_STATE].astype(BF16)
            c_g = bcv[:, (2 + g) * SSD_STATE:(3 + g) * SSD_STATE].astype(BF16)
            cb = lax.dot_general(c_g, b_g, (((1,), (1,)), ((), ())), preferred_element_type=F32)
            for jj in range(2):
                j = 2 * g + jj
                ha = off + 2 * j
                hb = ha + 1

                def pair_cols(arr, ha=ha, hb=hb):
                    return jnp.where(left, arr[:, ha:ha + 1], arr[:, hb:hb + 1])

                x_p = xs[:, j * LANES:(j + 1) * LANES] * pair_cols(dt)
                x_b = x_p.astype(BF16)
                ys = []
                for hh in (ha, hb):
                    seg = cum[:, hh:hh + 1] - cum_t[hh:hh + 1, :]
                    dec = jnp.exp(jnp.where(mask, seg, -jnp.inf))
                    ys.append(jnp.dot((cb * dec).astype(BF16), x_b, preferred_element_type=F32))
                y_diag = jnp.where(left, ys[0], ys[1])
                st = st_ref[d, j]
                y_off = lax.dot_general(c_g, st.astype(BF16), (((1,), (1,)), ((), ())),
                                        preferred_element_type=F32) * pair_cols(ecum)
                y_ref[0, :, j * LANES:(j + 1) * LANES] = y_diag + y_off
                x_d = (x_p * pair_cols(dte)).T.astype(BF16)
                cs = jnp.dot(x_d, b_g, preferred_element_type=F32)
                dcol = jnp.where(top, cdec[:, ha:ha + 1], cdec[:, hb:hb + 1])
                st_ref[d, j] = st * dcol + cs

    one_dir(0, xsf_ref, bcf_ref, dtf_ref, yf_ref)
    one_dir(1, xsb_ref, bcb_ref, dtb_ref, yb_ref)

    @pl.when(c == nc - 1)
    def _():
        hTf_ref[0] = st_ref[0]
        hTb_ref[0] = st_ref[1]


def _ssd(xs, bc, dt, alog_row, bias_row, h0f, h0b, h0_row):
    bn, seq_len, _ = xs.shape
    nc = seq_len // CHUNK
    fwd = lambda width: pl.BlockSpec((1, CHUNK, width), lambda b, c: (b, c, 0))
    bwd = lambda width: pl.BlockSpec((1, CHUNK, width), lambda b, c: (b, nc - 1 - c, 0))
    st_spec_in = pl.BlockSpec((1, HEAD_PAIRS, LANES, SSD_STATE), lambda b, c: (h0_row(b), 0, 0, 0))
    st_spec_out = pl.BlockSpec((1, HEAD_PAIRS, LANES, SSD_STATE), lambda b, c: (b, 0, 0, 0))
    row = pl.BlockSpec((1, LANES), lambda b, c: (0, 0))
    st_shape = jax.ShapeDtypeStruct((bn, HEAD_PAIRS, LANES, SSD_STATE), F32)
    return pl.pallas_call(
        _ssd_kernel,
        grid=(bn, nc),
        in_specs=[fwd(SSD_INNER), fwd(CONV_DIM - SSD_INNER), fwd(LANES),
                  bwd(SSD_INNER), bwd(CONV_DIM - SSD_INNER), bwd(LANES),
                  row, row, st_spec_in, st_spec_in],
        out_specs=[fwd(SSD_INNER), bwd(SSD_INNER), st_spec_out, st_spec_out],
        out_shape=[jax.ShapeDtypeStruct((bn, seq_len, SSD_INNER), F32),
                   jax.ShapeDtypeStruct((bn, seq_len, SSD_INNER), F32), st_shape, st_shape],
        scratch_shapes=[pltpu.VMEM((2, HEAD_PAIRS, LANES, SSD_STATE), F32)],
        compiler_params=_params(("arbitrary", "arbitrary")),
        name="ssd",
    )(xs, bc, dt, xs, bc, dt, alog_row, bias_row, h0f, h0b)


def _out_kernel(x_ref, yf_ref, yb_ref, xs_ref, z_ref, yp_ref, dsk_ref, sg_ref, wo_ref, mod_ref, ng_ref, wr_ref,
                x1_ref, h2_ref, afft_ref):
    y = yf_ref[0] + yb_ref[0] + dsk_ref[...] * xs_ref[0]
    z = z_ref[0]
    y = y * (z * _sigmoid(z))
    y = y * lax.rsqrt(jnp.mean(y * y, axis=-1, keepdims=True) + EPS) * sg_ref[...]
    o = (jnp.dot(yp_ref[0], wo_ref[:MIX_POOL, :], preferred_element_type=F32)
         + jnp.dot(y.astype(BF16), wo_ref[MIX_POOL:, :], preferred_element_type=F32))
    mod = mod_ref[0]
    x1 = x_ref[0] + mod[2:3] * o
    x1_ref[0] = x1
    h2 = x1 * lax.rsqrt(jnp.mean(x1 * x1, axis=-1, keepdims=True) + EPS) * (ng_ref[...] * (1.0 + mod[4:5])) + mod[3:4]
    h2_ref[0] = h2.astype(BF16)
    logits = jnp.dot(h2, wr_ref[...], precision=HI, preferred_element_type=F32)
    lane = lax.broadcasted_iota(jnp.int32, logits.shape, 1)
    lg = jnp.where(lane < N_EXPERTS, logits, -jnp.inf)
    e = jnp.exp(lg - jnp.max(lg, axis=-1, keepdims=True))
    aff = e / jnp.sum(e, axis=-1, keepdims=True)
    afft_ref[0] = aff.T[:N_EXPERTS, :]


def _out_proj(x, y_f, y_b, xs, z, y_pool, dsk, ssm_g, w_out, mod_l, mod_row, norm_ffn_g, w_router, tl):
    bn, seq_len, _ = x.shape
    nt = seq_len // tl
    const = lambda *shape: pl.BlockSpec(shape, lambda b, i: (0,) * len(shape))
    tok = lambda width: pl.BlockSpec((1, tl, width), lambda b, i: (b, i, 0))
    return pl.pallas_call(
        _out_kernel,
        grid=(bn, nt),
        in_specs=[tok(D_MODEL), tok(SSD_INNER), tok(SSD_INNER), tok(SSD_INNER), tok(SSD_INNER), tok(MIX_POOL),
                  const(1, SSD_INNER), const(1, SSD_INNER), const(D_MODEL, D_MODEL),
                  pl.BlockSpec((1, N_MOD, D_MODEL), lambda b, i: (mod_row(b), 0, 0)),
                  const(1, D_MODEL), const(D_MODEL, LANES)],
        out_specs=[tok(D_MODEL), tok(D_MODEL), pl.BlockSpec((1, N_EXPERTS, tl), lambda b, i: (b, 0, i))],
        out_shape=[jax.ShapeDtypeStruct((bn, seq_len, D_MODEL), F32),
                   jax.ShapeDtypeStruct((bn, seq_len, D_MODEL), BF16),
                   jax.ShapeDtypeStruct((bn, N_EXPERTS, seq_len), F32)],
        compiler_params=_params(("arbitrary", "arbitrary")),
        name="out_proj",
    )(x, y_f, y_b, xs, z, y_pool, dsk, ssm_g, w_out, mod_l, norm_ffn_g, w_router)


def _topk_kernel(afft_ref, pos_ref, post_ref, tab_ref, *, seq_len, cap, rb, tbs):
    n_rows = rb * N_EXPERTS
    a = afft_ref[...].reshape(n_rows, seq_len)

    def body(it, thr):
        cand_bits = thr | jnp.left_shift(jnp.int32(1), 30 - it)
        cnt = jnp.sum((a >= pltpu.bitcast(cand_bits, F32)).astype(jnp.int32), axis=1, keepdims=True)
        return jnp.where(cnt >= cap, cand_bits, thr)

    thr = lax.fori_loop(0, 31, body, jnp.zeros((n_rows, 1), jnp.int32))
    lo = pltpu.bitcast(thr, F32)
    hi = pltpu.bitcast(thr + 1, F32)
    gt = a >= hi
    eq = (a >= lo) & (a < hi)
    need = cap - jnp.sum(gt.astype(jnp.int32), axis=1, keepdims=True)

    r = lax.broadcasted_iota(jnp.int32, (LANES, LANES), 0)
    s = lax.broadcasted_iota(jnp.int32, (LANES, LANES), 1)
    strict_upper = (r < s).astype(BF16)
    nblk = seq_len // LANES

    def excl_cumsum(m):
        carry = jnp.zeros((n_rows, 1), F32)
        outs = []
        for blk in range(nblk):
            mb = m[:, blk * LANES:(blk + 1) * LANES]
            outs.append(jnp.dot(mb.astype(BF16), strict_upper, preferred_element_type=F32) + carry)
            carry = carry + jnp.sum(mb, axis=1, keepdims=True)
        return jnp.concatenate(outs, axis=1)

    eq_rank = excl_cumsum(jnp.where(eq, 1.0, 0.0))
    sel = gt | (eq & (eq_rank < need.astype(F32)))
    rank = excl_cumsum(jnp.where(sel, 1.0, 0.0))
    pos = jnp.where(sel, rank, -1.0)
    pos_ref[...] = pos.astype(jnp.int32).reshape(rb, N_EXPERTS, seq_len)

    lane = lax.broadcasted_iota(jnp.int32, (n_rows, LANES), 1)
    tab = jnp.full((n_rows, LANES), float(cap), F32)
    for k in range(seq_len // tbs):
        tab = jnp.where(lane == k, rank[:, k * tbs:k * tbs + 1], tab)
    tab_ref[...] = tab.astype(jnp.int32).reshape(rb, N_EXPERTS, LANES)

    pad = jnp.full((LANES - N_EXPERTS, LANES), -1.0, F32)
    for i in range(rb):
        for blk in range(nblk):
            tile = jnp.concatenate([pos[i * N_EXPERTS:(i + 1) * N_EXPERTS, blk * LANES:(blk + 1) * LANES], pad], axis=0)
            post_ref[i, blk * LANES:(blk + 1) * LANES, :] = tile.T.astype(jnp.int32)


def _topk(afft, cap, rb, tbs):
    bn, _, seq_len = afft.shape
    return pl.pallas_call(
        functools.partial(_topk_kernel, seq_len=seq_len, cap=cap, rb=rb, tbs=tbs),
        grid=(bn // rb,),
        in_specs=[pl.BlockSpec((rb, N_EXPERTS, seq_len), lambda b: (b, 0, 0))],
        out_specs=[pl.BlockSpec((rb, N_EXPERTS, seq_len), lambda b: (b, 0, 0)),
                   pl.BlockSpec((rb, seq_len, LANES), lambda b: (b, 0, 0)),
                   pl.BlockSpec((rb, N_EXPERTS, LANES), lambda b: (b, 0, 0))],
        out_shape=[jax.ShapeDtypeStruct((bn, N_EXPERTS, seq_len), jnp.int32),
                   jax.ShapeDtypeStruct((bn, seq_len, LANES), jnp.int32),
                   jax.ShapeDtypeStruct((bn, N_EXPERTS, LANES), jnp.int32)],
        compiler_params=_params(("arbitrary",)),
        name="topk",
    )(afft)


def _gather_kernel(h2_ref, pos_ref, afft_ref, xe_ref, gate_ref, *, cap):
    seq_len = h2_ref.shape[1]
    n_slots = N_EXPERTS * cap
    slot = lax.broadcasted_iota(jnp.int32, (n_slots, seq_len), 0)
    onehot = jnp.zeros((n_slots, seq_len), F32)
    gval = jnp.zeros((n_slots, seq_len), F32)
    for e in range(N_EXPERTS):
        p = pos_ref[0, e:e + 1, :]
        hit = jnp.where(p >= 0, p + e * cap, -1) == slot
        onehot = jnp.where(hit, 1.0, onehot)
        gval = jnp.where(hit, afft_ref[0, e:e + 1, :], gval)
    xe_ref[0] = jnp.dot(onehot.astype(BF16), h2_ref[0], preferred_element_type=F32).astype(BF16)
    gate_ref[0] = jnp.broadcast_to(jnp.sum(gval, axis=1, keepdims=True), (n_slots, LANES))


def _gather(h2, pos, afft, cap):
    bn, seq_len, _ = h2.shape
    n_slots = N_EXPERTS * cap
    return pl.pallas_call(
        functools.partial(_gather_kernel, cap=cap),
        grid=(bn,),
        in_specs=[pl.BlockSpec((1, seq_len, D_MODEL), lambda b: (b, 0, 0)),
                  pl.BlockSpec((1, N_EXPERTS, seq_len), lambda b: (b, 0, 0)),
                  pl.BlockSpec((1, N_EXPERTS, seq_len), lambda b: (b, 0, 0))],
        out_specs=[pl.BlockSpec((1, n_slots, D_MODEL), lambda b: (b, 0, 0)),
                   pl.BlockSpec((1, n_slots, LANES), lambda b: (b, 0, 0))],
        out_shape=[jax.ShapeDtypeStruct((bn, n_slots, D_MODEL), BF16),
                   jax.ShapeDtypeStruct((bn, n_slots, LANES), F32)],
        compiler_params=_params(("arbitrary",)),
        name="gather",
    )(h2, pos, afft)


def _window_starts(tab_ref, b, e0, n_e, t, cap, win):
    starts = []
    over = jnp.int32(-1)
    for ee in range(n_e):
        base = (b * N_EXPERTS + e0 + ee) * LANES + t
        p0 = tab_ref[base]
        p1 = tab_ref[base + 1]
        w0 = jnp.minimum(lax.shift_left(lax.shift_right_logical(p0, 4), 4), cap - win)
        starts.append(w0)
        over = jnp.maximum(over, p1 - w0 - win)
    return starts, over


def _gather_win_kernel(tab_ref, h2_ref, pos_ref, afft_ref, xe_ref, gate_ref, *, cap, win, n_e):
    b = pl.program_id(0)
    e0 = pl.program_id(1) * n_e
    t = pl.program_id(2)
    tbs = h2_ref.shape[1]

    @pl.when(t == 0)
    def _():
        xe_ref[...] = jnp.zeros(xe_ref.shape, BF16)
        gate_ref[...] = jnp.zeros(gate_ref.shape, F32)

    starts, over = _window_starts(tab_ref, b, e0, n_e, t, cap, win)

    def accumulate(row0, n, x, g):
        cur = xe_ref[0, pl.ds(row0, n), :]
        xe_ref[0, pl.ds(row0, n), :] = (cur.astype(F32) + x).astype(BF16)
        gate_ref[0, pl.ds(row0, n), :] = gate_ref[0, pl.ds(row0, n), :] + jnp.broadcast_to(g, (n, LANES))

    @pl.when(over <= 0)
    def _():
        sub = lax.broadcasted_iota(jnp.int32, (win, tbs), 0)
        onehots, gates = [], []
        for ee in range(n_e):
            hit = (pos_ref[0, pl.ds(e0 + ee, 1), :] - starts[ee]) == sub
            onehots.append(jnp.where(hit, 1.0, 0.0).astype(BF16))
            gates.append(jnp.sum(jnp.where(hit, afft_ref[0, pl.ds(e0 + ee, 1), :], 0.0), axis=1, keepdims=True))
        x = jnp.dot(jnp.concatenate(onehots, axis=0), h2_ref[0], preferred_element_type=F32)
        for ee in range(n_e):
            accumulate(pl.multiple_of(ee * cap + starts[ee], 16), win, x[ee * win:(ee + 1) * win], gates[ee])

    @pl.when(over > 0)
    def _():
        sub = lax.broadcasted_iota(jnp.int32, (cap, tbs), 0)
        for ee in range(n_e):
            hit = pos_ref[0, pl.ds(e0 + ee, 1), :] == sub
            x = jnp.dot(jnp.where(hit, 1.0, 0.0).astype(BF16), h2_ref[0], preferred_element_type=F32)
            g = jnp.sum(jnp.where(hit, afft_ref[0, pl.ds(e0 + ee, 1), :], 0.0), axis=1, keepdims=True)
            accumulate(ee * cap, cap, x, g)


def _gather_win(tab, h2, pos, afft, cap, win, tbs):
    bn, seq_len, _ = h2.shape
    n_slots = N_EXPERTS * cap
    n_e = N_EXPERTS // 2
    grid_spec = pltpu.PrefetchScalarGridSpec(
        num_scalar_prefetch=1,
        grid=(bn, N_EXPERTS // n_e, seq_len // tbs),
        in_specs=[pl.BlockSpec((1, tbs, D_MODEL), lambda b, g, t, tab: (b, t, 0)),
                  pl.BlockSpec((1, N_EXPERTS, tbs), lambda b, g, t, tab: (b, 0, t)),
                  pl.BlockSpec((1, N_EXPERTS, tbs), lambda b, g, t, tab: (b, 0, t))],
        out_specs=[pl.BlockSpec((1, n_e * cap, D_MODEL), lambda b, g, t, tab: (b, g, 0)),
                   pl.BlockSpec((1, n_e * cap, LANES), lambda b, g, t, tab: (b, g, 0))],
    )
    return pl.pallas_call(
        functools.partial(_gather_win_kernel, cap=cap, win=win, n_e=n_e),
        grid_spec=grid_spec,
        out_shape=[jax.ShapeDtypeStruct((bn, n_slots, D_MODEL), BF16),
                   jax.ShapeDtypeStruct((bn, n_slots, LANES), F32)],
        compiler_params=_params(("arbitrary", "arbitrary", "arbitrary")),
        name="gather_win",
    )(tab, h2, pos, afft)


def _ffn_kernel(xe_ref, gate_ref, wg_ref, wu_ref, wd_ref, ye_ref, wg_s, wu_s, wd_s, *, rows):
    @pl.when(pl.program_id(1) == 0)
    def _():
        wg_s[...] = wg_ref[0, 0].astype(BF16)
        wu_s[...] = wu_ref[0, 0].astype(BF16)
        wd_s[...] = wd_ref[0, 0].astype(BF16)

    x = xe_ref[...].reshape(rows, D_MODEL)
    g = jnp.dot(x, wg_s[...], preferred_element_type=F32)
    u = jnp.dot(x, wu_s[...], preferred_element_type=F32)
    hid = (g * _sigmoid(g) * u).astype(BF16)
    y = jnp.dot(hid, wd_s[...], preferred_element_type=F32) * gate_ref[...].reshape(rows, LANES)[:, :1]
    ye_ref[...] = y.astype(BF16).reshape(ye_ref.shape)


def _ffn(xe, gate, w_gate, w_up, w_down, layer, cap, bb):
    bn, n_slots, _ = xe.shape
    rows = bb * cap
    wspec = pl.BlockSpec((1, 1, D_MODEL, D_MODEL), lambda e, j: (layer, e, 0, 0))
    return pl.pallas_call(
        functools.partial(_ffn_kernel, rows=rows),
        grid=(N_EXPERTS, bn // bb),
        in_specs=[pl.BlockSpec((bb, cap, D_MODEL), lambda e, j: (j, e, 0)),
                  pl.BlockSpec((bb, cap, LANES), lambda e, j: (j, e, 0)),
                  wspec, wspec, wspec],
        out_specs=pl.BlockSpec((bb, cap, D_MODEL), lambda e, j: (j, e, 0)),
        out_shape=jax.ShapeDtypeStruct((bn, n_slots, D_MODEL), BF16),
        scratch_shapes=[pltpu.VMEM((D_MODEL, D_MODEL), BF16)] * 3,
        compiler_params=_params(("arbitrary", "arbitrary")),
        name="ffn",
    )(xe, gate, w_gate, w_up, w_down)


def _finish(x1_ref, mod_ref, fg_ref, out_ref, acc, final):
    x2 = x1_ref[0] + mod_ref[0][5:6] * acc
    if final:
        x2 = x2 * lax.rsqrt(jnp.mean(x2 * x2, axis=-1, keepdims=True) + EPS) * fg_ref[...]
    out_ref[0] = x2


def _scatter_kernel(x1_ref, ye_ref, post_ref, mod_ref, fg_ref, out_ref, *, cap, final):
    tl = x1_ref.shape[1]
    n_slots = N_EXPERTS * cap
    pt = post_ref[0]
    slot = lax.broadcasted_iota(jnp.int32, (tl, n_slots), 1)
    onehot = jnp.zeros((tl, n_slots), F32)
    for e in range(N_EXPERTS):
        p = pt[:, e:e + 1]
        onehot = jnp.where(jnp.where(p >= 0, p + e * cap, -1) == slot, 1.0, onehot)
    acc = jnp.dot(onehot.astype(BF16), ye_ref[0], preferred_element_type=F32)
    _finish(x1_ref, mod_ref, fg_ref, out_ref, acc, final)


def _scatter(x1, ye, post, mod_l, mod_row, final_g, cap, final):
    bn, seq_len, _ = x1.shape
    n_slots = N_EXPERTS * cap
    return pl.pallas_call(
        functools.partial(_scatter_kernel, cap=cap, final=final),
        grid=(bn,),
        in_specs=[pl.BlockSpec((1, seq_len, D_MODEL), lambda b: (b, 0, 0)),
                  pl.BlockSpec((1, n_slots, D_MODEL), lambda b: (b, 0, 0)),
                  pl.BlockSpec((1, seq_len, LANES), lambda b: (b, 0, 0)),
                  pl.BlockSpec((1, N_MOD, D_MODEL), lambda b: (mod_row(b), 0, 0)),
                  pl.BlockSpec((1, D_MODEL), lambda b: (0, 0))],
        out_specs=pl.BlockSpec((1, seq_len, D_MODEL), lambda b: (b, 0, 0)),
        out_shape=jax.ShapeDtypeStruct((bn, seq_len, D_MODEL), F32),
        compiler_params=_params(("arbitrary",)),
        name="scatter",
    )(x1, ye, post, mod_l, final_g)


def _scatter_win_kernel(tab_ref, x1_ref, ye_ref, post_ref, mod_ref, fg_ref, out_ref, *, cap, win, final):
    b = pl.program_id(0)
    t = pl.program_id(1)
    tbs = x1_ref.shape[1]
    pt = post_ref[0]
    starts, over = _window_starts(tab_ref, b, 0, N_EXPERTS, t, cap, win)

    @pl.when(over <= 0)
    def _():
        lane = lax.broadcasted_iota(jnp.int32, (tbs, win), 1)
        onehots, rows = [], []
        for e in range(N_EXPERTS):
            onehots.append(jnp.where((pt[:, e:e + 1] - starts[e]) == lane, 1.0, 0.0).astype(BF16))
            rows.append(ye_ref[0, pl.ds(pl.multiple_of(e * cap + starts[e], 16), win), :])
        acc = jnp.dot(jnp.concatenate(onehots, axis=1), jnp.concatenate(rows, axis=0), preferred_element_type=F32)
        _finish(x1_ref, mod_ref, fg_ref, out_ref, acc, final)

    @pl.when(over > 0)
    def _():
        lane = lax.broadcasted_iota(jnp.int32, (tbs, cap), 1)
        acc = jnp.zeros((tbs, D_MODEL), F32)
        for e in range(N_EXPERTS):
            onehot = jnp.where(pt[:, e:e + 1] == lane, 1.0, 0.0).astype(BF16)
            acc = acc + jnp.dot(onehot, ye_ref[0, e * cap:(e + 1) * cap, :], preferred_element_type=F32)
        _finish(x1_ref, mod_ref, fg_ref, out_ref, acc, final)


def _scatter_win(tab, x1, ye, post, mod_l, mod_row, final_g, cap, win, tbs, final):
    bn, seq_len, _ = x1.shape
    n_slots = N_EXPERTS * cap
    grid_spec = pltpu.PrefetchScalarGridSpec(
        num_scalar_prefetch=1,
        grid=(bn, seq_len // tbs),
        in_specs=[pl.BlockSpec((1, tbs, D_MODEL), lambda b, t, tab: (b, t, 0)),
                  pl.BlockSpec((1, n_slots, D_MODEL), lambda b, t, tab: (b, 0, 0), pipeline_mode=pl.Buffered(1)),
                  pl.BlockSpec((1, tbs, LANES), lambda b, t, tab: (b, t, 0)),
                  pl.BlockSpec((1, N_MOD, D_MODEL), lambda b, t, tab: (mod_row(b), 0, 0)),
                  pl.BlockSpec((1, D_MODEL), lambda b, t, tab: (0, 0))],
        out_specs=pl.BlockSpec((1, tbs, D_MODEL), lambda b, t, tab: (b, t, 0)),
    )
    return pl.pallas_call(
        functools.partial(_scatter_win_kernel, cap=cap, win=win, final=final),
        grid_spec=grid_spec,
        out_shape=jax.ShapeDtypeStruct((bn, seq_len, D_MODEL), F32),
        compiler_params=_params(("arbitrary", "arbitrary")),
        name="scatter_win",
    )(tab, x1, ye, post, mod_l, final_g)


def _grid_position_embedding(n_tokens, dim):
    rows = n_tokens // GRID_W
    row = jnp.repeat(jnp.arange(rows), GRID_W).astype(F32)
    col = jnp.tile(jnp.arange(GRID_W), rows).astype(F32)
    quarter = dim // 4
    inv_freq = jnp.power(POS_BASE, -jnp.arange(quarter, dtype=F32) / quarter)
    er = row[:, None] * inv_freq[None]
    ec = col[:, None] * inv_freq[None]
    return jnp.concatenate([jnp.sin(er), jnp.cos(er), jnp.sin(ec), jnp.cos(ec)], axis=-1)


def _pad_cols(w, width):
    return jnp.pad(w, ((0, 0), (0, width - w.shape[1])))


def _run_group(x, mod, mod_row, h0f, h0b, h0_row, lw, final_g, tl, ffn_rows_per_step, topk_rows):
    bn, seq_len, _ = x.shape
    cap = CAPACITY_FACTOR * seq_len // N_EXPERTS
    windowed = seq_len >= 1024
    tbs = seq_len // 8 if windowed else seq_len
    win = cap // 4
    states_f, states_b = [], []
    for l in range(DEPTH):
        w = lw[l]
        final = l == DEPTH - 1
        y_pool, z, xs, bc, dt = _in_proj(x, mod[l], mod_row, w["norm_mix_g"], w["w_main"], w["w_dt"], w["pool_w"],
                                         w["pool_scale"], w["conv_w"], w["conv_b"], tl)
        y_f, y_b, hT_f, hT_b = _ssd(xs, bc, dt, w["alog_row"], w["bias_row"], h0f[l], h0b[l], h0_row)
        states_f.append(hT_f)
        states_b.append(hT_b)
        x1, h2, afft = _out_proj(x, y_f, y_b, xs, z, y_pool, w["dsk"], w["ssm_norm_g"], w["w_out"], mod[l], mod_row,
                                 w["norm_ffn_g"], w["w_router"], tl)
        pos, post, tab = _topk(afft, cap, topk_rows, tbs)
        if windowed:
            tab = tab.reshape(-1)
            xe, gate = _gather_win(tab, h2, pos, afft, cap, win, tbs)
        else:
            xe, gate = _gather(h2, pos, afft, cap)
        ye = _ffn(xe, gate, w["w_gate"], w["w_up"], w["w_down"], l, cap, ffn_rows_per_step // cap)
        if windowed:
            x = _scatter_win(tab, x1, ye, post, mod[l], mod_row, final_g, cap, win, tbs, final)
        else:
            x = _scatter(x1, ye, post, mod[l], mod_row, final_g, cap, final)
    return x, states_f, states_b


def kernel(x_prompt, x_sample, state_ssm_fwd, state_ssm_bwd, c, c_ctx, norm_mix_g, w_ada, b_ada, w_in, pool_w,
           pool_scale, conv_w, conv_b, a_log_fwd, a_log_bwd, dt_bias_fwd, dt_bias_bwd, d_skip, ssm_norm_g, w_out,
           norm_ffn_g, w_router, w_gate, w_up, w_down, final_norm_g):
    n_dec = c.shape[0]
    n_ctx = x_prompt.shape[0]
    ctx_row = n_dec
    cond = jnp.concatenate([c, c_ctx[None, :], jnp.zeros((8 - n_dec - 1, D_MODEL), F32)], axis=0)
    mod = _modulation(cond, w_ada, b_ada).reshape(DEPTH, 8, N_MOD, D_MODEL)

    zeros_h = jnp.zeros((LANES - 2 * SSD_HEADS,), F32)
    lw = []
    for l in range(DEPTH):
        lw.append(dict(
            norm_mix_g=norm_mix_g[l][None], w_main=w_in[l][:, :N_MAIN].astype(BF16),
            w_dt=_pad_cols(w_in[l][:, N_MAIN:], LANES).astype(BF16), pool_w=pool_w[l].astype(BF16),
            pool_scale=pool_scale[l][None], conv_w=conv_w[l], conv_b=conv_b[l][None],
            alog_row=jnp.concatenate([a_log_fwd[l], a_log_bwd[l], zeros_h])[None],
            bias_row=jnp.concatenate([dt_bias_fwd[l], dt_bias_bwd[l], zeros_h])[None],
            dsk=jnp.repeat(d_skip[l], SSD_INNER // SSD_HEADS)[None], ssm_norm_g=ssm_norm_g[l][None],
            w_out=w_out[l].astype(BF16), norm_ffn_g=norm_ffn_g[l][None], w_router=_pad_cols(w_router[l], LANES),
            w_gate=w_gate, w_up=w_up, w_down=w_down))
    final_g = final_norm_g[None]
    st_shape = (HEAD_PAIRS, LANES, SSD_STATE)

    zero_state = jnp.zeros((1,) + st_shape, F32)
    y_prompt, sf, sb = _run_group(x_prompt, mod, lambda b: ctx_row, [zero_state] * DEPTH, [zero_state] * DEPTH,
                                  lambda b: 0, lw, final_g, tl=x_prompt.shape[1], ffn_rows_per_step=1024,
                                  topk_rows=8)
    out_state_shape = (n_ctx, SSD_HEADS, SSD_INNER // SSD_HEADS, SSD_STATE)
    new_f = jnp.stack([s.reshape(out_state_shape) for s in sf], axis=1)
    new_b = jnp.stack([s.reshape(out_state_shape) for s in sb], axis=1)

    xs0 = x_sample + _grid_position_embedding(x_sample.shape[1], D_MODEL)[None]
    h0f = [state_ssm_fwd[:, l].reshape((n_dec,) + st_shape) for l in range(DEPTH)]
    h0b = [state_ssm_bwd[:, l].reshape((n_dec,) + st_shape) for l in range(DEPTH)]
    y_sample, _, _ = _run_group(xs0, mod, lambda b: b, h0f, h0b, lambda b: b, lw, final_g, tl=512,
                                ffn_rows_per_step=512, topk_rows=n_dec)
    return (y_prompt, y_sample, new_f, new_b)
```

```python
import functools

import jax
import jax.numpy as jnp
from jax import lax
from jax.experimental import pallas as pl
from jax.experimental.pallas import tpu as pltpu

D_MODEL = 1024
DEPTH = 2
GRID_W = 64
MIX_POOL = 512
POOL_WINDOWS = (2, 4, 8, 16)
POOL_GROUP_CH = 128
SSD_INNER = 512
SSD_HEADS = 8
SSD_STATE = 128
CONV_K = 5
CHUNK = 128
CONV_DIM = 1024
N_EXPERTS = 16
CAPACITY_FACTOR = 2
N_MOD = 6
POS_BASE = 10000.0
EPS = 1e-6

HALO = 8
LANES = 128
N_MAIN = MIX_POOL + SSD_INNER + CONV_DIM
HEAD_PAIRS = SSD_HEADS // 2
VMEM_LIMIT = 56 * 1024 * 1024

F32 = jnp.float32
BF16 = jnp.bfloat16
HI = lax.Precision.HIGHEST


def _sigmoid(x):
    return 1.0 / (1.0 + jnp.exp(-x))


def _params(sem):
    return pltpu.CompilerParams(dimension_semantics=sem, vmem_limit_bytes=VMEM_LIMIT)


def _mod_kernel(cond_ref, w_ref, b_ref, out_ref):
    c = cond_ref[...]
    s = c * _sigmoid(c)
    out_ref[0] = jnp.dot(s, w_ref[0], precision=HI, preferred_element_type=F32) + b_ref[0]


def _modulation(cond, w_ada, b_ada):
    n_rows = cond.shape[0]
    n_out = N_MOD * D_MODEL
    tn = 1536
    return pl.pallas_call(
        _mod_kernel,
        grid=(DEPTH, n_out // tn),
        in_specs=[
            pl.BlockSpec((n_rows, D_MODEL), lambda l, j: (0, 0)),
            pl.BlockSpec((1, D_MODEL, tn), lambda l, j: (l, 0, j)),
            pl.BlockSpec((1, 1, tn), lambda l, j: (l, 0, j)),
        ],
        out_specs=pl.BlockSpec((1, n_rows, tn), lambda l, j: (l, 0, j)),
        out_shape=jax.ShapeDtypeStruct((DEPTH, n_rows, n_out), F32),
        compiler_params=_params(("arbitrary", "arbitrary")),
        name="modulation",
    )(cond, w_ada, b_ada.reshape(DEPTH, 1, n_out))


def _in_kernel(xp_ref, xc_ref, xn_ref, mod_ref, g_ref, wmain_ref, wdt_ref, poolw_ref, pscale_ref, cw_ref, cb_ref,
               ypool_ref, z_ref, xs_ref, bc_ref, dt_ref, proj_scr, *, tl, seq_len):
    i = pl.program_id(1)
    n = pl.num_programs(1)
    rows = tl + 2 * HALO
    x_ext = jnp.concatenate([xp_ref[0], xc_ref[0], xn_ref[0]], axis=0)
    mod = mod_ref[0]
    ms = jnp.mean(x_ext * x_ext, axis=-1, keepdims=True)
    h = x_ext * lax.rsqrt(ms + EPS) * (g_ref[...] * (1.0 + mod[1:2])) + mod[0:1]
    row = lax.broadcasted_iota(jnp.int32, (rows, 1), 0)
    lo_ok = jnp.where(i > 0, 0, HALO)
    hi_ok = jnp.where(i < n - 1, rows, tl + HALO)
    hb = jnp.where((row >= lo_ok) & (row < hi_ok), h, 0.0).astype(BF16)
    proj_scr[...] = jnp.dot(hb, wmain_ref[...], preferred_element_type=F32)
    dt_full = jnp.dot(hb, wdt_ref[...], preferred_element_type=F32)
    dt_ref[0] = dt_full[HALO:HALO + tl]

    def ahead(v, k):
        return pltpu.roll(v, (-k) % rows, axis=0)

    t_glob = i * tl + lax.broadcasted_iota(jnp.int32, (tl, 1), 0)
    outs = []
    for gi, w in enumerate(POOL_WINDOWS):
        c0 = gi * POOL_GROUP_CH
        u = proj_scr[:, c0:c0 + POOL_GROUP_CH]
        acc = u
        span = 1
        while 2 * span < w:
            acc = acc + ahead(acc, span)
            span *= 2
        acc = acc + ahead(acc, -span)
        lo = jnp.maximum(t_glob - w // 2, 0)
        hi = jnp.minimum(t_glob - w // 2 + w, seq_len)
        cnt = (hi - lo).astype(F32)
        p = acc[HALO:HALO + tl] / cnt - u[HALO:HALO + tl]
        outs.append(jnp.dot(p.astype(BF16), poolw_ref[gi], preferred_element_type=F32))
    ypool_ref[0] = (jnp.concatenate(outs, axis=1) * pscale_ref[...]).astype(BF16)

    c0 = MIX_POOL + SSD_INNER
    acc = cb_ref[...] + cw_ref[0:1, :] * proj_scr[HALO - 2:HALO - 2 + tl, c0:c0 + CONV_DIM]
    for k in range(1, CONV_K):
        acc = acc + cw_ref[k:k + 1, :] * proj_scr[HALO - 2 + k:HALO - 2 + k + tl, c0:c0 + CONV_DIM]
    act = acc * _sigmoid(acc)
    xs_ref[0] = act[:, :SSD_INNER]
    bc_ref[0] = act[:, SSD_INNER:]
    z_ref[0] = proj_scr[HALO:HALO + tl, MIX_POOL:MIX_POOL + SSD_INNER]


def _in_proj(x, mod_l, mod_row, norm_g, w_main, w_dt, pool_w, pool_scale, conv_w, conv_b, tl):
    bn, seq_len, _ = x.shape
    nt = seq_len // tl
    hb = tl // HALO
    last_halo = seq_len // HALO - 1
    const = lambda *shape: pl.BlockSpec(shape, lambda b, i: (0,) * len(shape))
    tok = lambda width: pl.BlockSpec((1, tl, width), lambda b, i: (b, i, 0))
    return pl.pallas_call(
        functools.partial(_in_kernel, tl=tl, seq_len=seq_len),
        grid=(bn, nt),
        in_specs=[
            pl.BlockSpec((1, HALO, D_MODEL), lambda b, i: (b, jnp.maximum(i * hb - 1, 0), 0)),
            tok(D_MODEL),
            pl.BlockSpec((1, HALO, D_MODEL), lambda b, i: (b, jnp.minimum((i + 1) * hb, last_halo), 0)),
            pl.BlockSpec((1, N_MOD, D_MODEL), lambda b, i: (mod_row(b), 0, 0)),
            const(1, D_MODEL),
            const(D_MODEL, N_MAIN),
            const(D_MODEL, LANES),
            const(len(POOL_WINDOWS), POOL_GROUP_CH, POOL_GROUP_CH),
            const(1, MIX_POOL),
            const(CONV_K, CONV_DIM),
            const(1, CONV_DIM),
        ],
        out_specs=[tok(MIX_POOL), tok(SSD_INNER), tok(SSD_INNER), tok(CONV_DIM - SSD_INNER), tok(LANES)],
        out_shape=[
            jax.ShapeDtypeStruct((bn, seq_len, MIX_POOL), BF16),
            jax.ShapeDtypeStruct((bn, seq_len, SSD_INNER), F32),
            jax.ShapeDtypeStruct((bn, seq_len, SSD_INNER), F32),
            jax.ShapeDtypeStruct((bn, seq_len, CONV_DIM - SSD_INNER), F32),
            jax.ShapeDtypeStruct((bn, seq_len, LANES), F32),
        ],
        scratch_shapes=[pltpu.VMEM((tl + 2 * HALO, N_MAIN), F32)],
        compiler_params=_params(("arbitrary", "arbitrary")),
        name="in_proj",
    )(x, x, x, mod_l, norm_g, w_main, w_dt, pool_w, pool_scale, conv_w, conv_b)


def _ssd_kernel(xsf_ref, bcf_ref, dtf_ref, xsb_ref, bcb_ref, dtb_ref, alog_ref, bias_ref, h0f_ref, h0b_ref,
                yf_ref, yb_ref, hTf_ref, hTb_ref, st_ref):
    c = pl.program_id(1)
    nc = pl.num_programs(1)

    @pl.when(c == 0)
    def _():
        st_ref[0] = h0f_ref[0]
        st_ref[1] = h0b_ref[0]

    r = lax.broadcasted_iota(jnp.int32, (CHUNK, CHUNK), 0)
    s = lax.broadcasted_iota(jnp.int32, (CHUNK, CHUNK), 1)
    left = s < (LANES // 2)
    top = r < (CHUNK // 2)
    a_row = -jnp.exp(alog_ref[...])

    def one_dir(d, xs_ref, bc_ref, dt_ref, y_ref):
        mask = (r >= s) if d == 0 else (r <= s)
        off = d * SSD_HEADS
        x_in = dt_ref[0] + bias_ref[...]
        dt = jnp.maximum(x_in, 0.0) + jnp.log1p(jnp.exp(-jnp.abs(x_in)))
        dta = dt * a_row
        d_hi = dta.astype(BF16)
        rest = dta - d_hi.astype(F32)
        d_mid = rest.astype(BF16)
        d_lo = (rest - d_mid.astype(F32)).astype(BF16)
        tri = jnp.where(mask, 1.0, 0.0).astype(BF16)
        cum = (jnp.dot(tri, d_hi, preferred_element_type=F32) + jnp.dot(tri, d_mid, preferred_element_type=F32)
               + jnp.dot(tri, d_lo, preferred_element_type=F32))
        tot = cum[CHUNK - 1:CHUNK, :] if d == 0 else cum[0:1, :]
        cdec = jnp.exp(tot)
        cum_t = cum.T
        dt_t = dt.T
        ecum_t = jnp.exp(cum_t)
        w_t = (dt * jnp.exp(tot - cum)).T
        bcv = bc_ref[0]
        xs = xs_ref[0]
        for g in range(2):
            b_g = bcv[:, g * SSD_STATE:(g + 1) * SSD_STATE].astype(BF16)
            c_g = bcv[:, (2 + g) * SSD_STATE:(3 + g) * SSD_STATE].astype(BF16)
            cb = lax.dot_general(c_g, b_g, (((1,), (1,)), ((), ())), preferred_element_type=F32)
            for jj in range(2):
                j = 2 * g + jj
                ha = off + 2 * j
                hb = ha + 1

                def pair_rows(arr_t, ha=ha, hb=hb):
                    return jnp.where(top, arr_t[ha:ha + 1, :], arr_t[hb:hb + 1, :])

                x_b = xs[:, j * LANES:(j + 1) * LANES].astype(BF16)
                ms = []
                for hh in (ha, hb):
                    seg = cum[:, hh:hh + 1] - cum_t[hh:hh + 1, :]
                    dec = jnp.exp(jnp.where(mask, seg, -jnp.inf))
                    ms.append((cb * dec * dt_t[hh:hh + 1, :]).astype(BF16))
                zero = jnp.zeros_like(x_b)
                x_blk = jnp.concatenate([jnp.where(left, x_b, zero), jnp.where(left, zero, x_b)], axis=0)
                y_diag = jnp.dot(jnp.concatenate(ms, axis=1), x_blk, preferred_element_type=F32)
                st = st_ref[d, j]
                y_off_t = lax.dot_general(st.astype(BF16), c_g, (((1,), (1,)), ((), ())),
                                          preferred_element_type=F32) * pair_rows(ecum_t)
                y_ref[0, :, j * LANES:(j + 1) * LANES] = y_diag + y_off_t.T
                x_d = (xs[:, j * LANES:(j + 1) * LANES].T * pair_rows(w_t)).astype(BF16)
                cs = jnp.dot(x_d, b_g, preferred_element_type=F32)
                dcol = jnp.where(top, cdec[:, ha:ha + 1], cdec[:, hb:hb + 1])
                st_ref[d, j] = st * dcol + cs

    one_dir(0, xsf_ref, bcf_ref, dtf_ref, yf_ref)
    one_dir(1, xsb_ref, bcb_ref, dtb_ref, yb_ref)

    @pl.when(c == nc - 1)
    def _():
        hTf_ref[0] = st_ref[0]
        hTb_ref[0] = st_ref[1]


def _ssd(xs, bc, dt, alog_row, bias_row, h0f, h0b, h0_row):
    bn, seq_len, _ = xs.shape
    nc = seq_len // CHUNK
    fwd = lambda width: pl.BlockSpec((1, CHUNK, width), lambda b, c: (b, c, 0))
    bwd = lambda width: pl.BlockSpec((1, CHUNK, width), lambda b, c: (b, nc - 1 - c, 0))
    st_spec_in = pl.BlockSpec((1, HEAD_PAIRS, LANES, SSD_STATE), lambda b, c: (h0_row(b), 0, 0, 0))
    st_spec_out = pl.BlockSpec((1, HEAD_PAIRS, LANES, SSD_STATE), lambda b, c: (b, 0, 0, 0))
    row = pl.BlockSpec((1, LANES), lambda b, c: (0, 0))
    st_shape = jax.ShapeDtypeStruct((bn, HEAD_PAIRS, LANES, SSD_STATE), F32)
    return pl.pallas_call(
        _ssd_kernel,
        grid=(bn, nc),
        in_specs=[fwd(SSD_INNER), fwd(CONV_DIM - SSD_INNER), fwd(LANES),
                  bwd(SSD_INNER), bwd(CONV_DIM - SSD_INNER), bwd(LANES),
                  row, row, st_spec_in, st_spec_in],
        out_specs=[fwd(SSD_INNER), bwd(SSD_INNER), st_spec_out, st_spec_out],
        out_shape=[jax.ShapeDtypeStruct((bn, seq_len, SSD_INNER), F32),
                   jax.ShapeDtypeStruct((bn, seq_len, SSD_INNER), F32), st_shape, st_shape],
        scratch_shapes=[pltpu.VMEM((2, HEAD_PAIRS, LANES, SSD_STATE), F32)],
        compiler_params=_params(("arbitrary", "arbitrary")),
        name="ssd",
    )(xs, bc, dt, xs, bc, dt, alog_row, bias_row, h0f, h0b)


def _out_kernel(x_ref, yf_ref, yb_ref, xs_ref, z_ref, yp_ref, dsk_ref, sg_ref, wo_ref, mod_ref, ng_ref, wr_ref,
                x1_ref, h2_ref, afft_ref):
    y = yf_ref[0] + yb_ref[0] + dsk_ref[...] * xs_ref[0]
    z = z_ref[0]
    y = y * (z * _sigmoid(z))
    y = y * lax.rsqrt(jnp.mean(y * y, axis=-1, keepdims=True) + EPS) * sg_ref[...]
    o = (jnp.dot(yp_ref[0], wo_ref[:MIX_POOL, :], preferred_element_type=F32)
         + jnp.dot(y.astype(BF16), wo_ref[MIX_POOL:, :], preferred_element_type=F32))
    mod = mod_ref[0]
    x1 = x_ref[0] + mod[2:3] * o
    x1_ref[0] = x1
    h2 = x1 * lax.rsqrt(jnp.mean(x1 * x1, axis=-1, keepdims=True) + EPS) * (ng_ref[...] * (1.0 + mod[4:5])) + mod[3:4]
    h2_hi = h2.astype(BF16)
    h2_ref[0] = h2_hi
    h2_lo = (h2 - h2_hi.astype(F32)).astype(BF16)
    pa = jnp.dot(h2_hi, wr_ref[0], preferred_element_type=F32)
    pb = jnp.dot(h2_lo, wr_ref[1], preferred_element_type=F32)
    logits = pa + pltpu.roll(pa, LANES - N_EXPERTS, axis=1) + pb
    lane = lax.broadcasted_iota(jnp.int32, logits.shape, 1)
    lg = jnp.where(lane < N_EXPERTS, logits, -jnp.inf)
    e = jnp.exp(lg - jnp.max(lg, axis=-1, keepdims=True))
    aff = e / jnp.sum(e, axis=-1, keepdims=True)
    afft_ref[0] = aff.T[:N_EXPERTS, :]


def _out_proj(x, y_f, y_b, xs, z, y_pool, dsk, ssm_g, w_out, mod_l, mod_row, norm_ffn_g, w_router, tl):
    bn, seq_len, _ = x.shape
    nt = seq_len // tl
    const = lambda *shape: pl.BlockSpec(shape, lambda b, i: (0,) * len(shape))
    tok = lambda width: pl.BlockSpec((1, tl, width), lambda b, i: (b, i, 0))
    return pl.pallas_call(
        _out_kernel,
        grid=(bn, nt),
        in_specs=[tok(D_MODEL), tok(SSD_INNER), tok(SSD_INNER), tok(SSD_INNER), tok(SSD_INNER), tok(MIX_POOL),
                  const(1, SSD_INNER), const(1, SSD_INNER), const(D_MODEL, D_MODEL),
                  pl.BlockSpec((1, N_MOD, D_MODEL), lambda b, i: (mod_row(b), 0, 0)),
                  const(1, D_MODEL), const(2, D_MODEL, LANES)],
        out_specs=[tok(D_MODEL), tok(D_MODEL), pl.BlockSpec((1, N_EXPERTS, tl), lambda b, i: (b, 0, i))],
        out_shape=[jax.ShapeDtypeStruct((bn, seq_len, D_MODEL), F32),
                   jax.ShapeDtypeStruct((bn, seq_len, D_MODEL), BF16),
                   jax.ShapeDtypeStruct((bn, N_EXPERTS, seq_len), F32)],
        compiler_params=_params(("arbitrary", "arbitrary")),
        name="out_proj",
    )(x, y_f, y_b, xs, z, y_pool, dsk, ssm_g, w_out, mod_l, norm_ffn_g, w_router)


def _topk_kernel(afft_ref, pos_ref, post_ref, tab_ref, *, seq_len, cap, rb, tbs):
    n_rows = rb * N_EXPERTS
    a = afft_ref[...].reshape(n_rows, seq_len)

    def body(it, thr):
        cand_bits = thr | jnp.left_shift(jnp.int32(1), 30 - it)
        cnt = jnp.sum((a >= pltpu.bitcast(cand_bits, F32)).astype(jnp.int32), axis=1, keepdims=True)
        return jnp.where(cnt >= cap, cand_bits, thr)

    thr = lax.fori_loop(0, 31, body, jnp.zeros((n_rows, 1), jnp.int32))
    lo = pltpu.bitcast(thr, F32)
    hi = pltpu.bitcast(thr + 1, F32)
    gt = a >= hi
    eq = (a >= lo) & (a < hi)
    need = cap - jnp.sum(gt.astype(jnp.int32), axis=1, keepdims=True)

    r = lax.broadcasted_iota(jnp.int32, (LANES, LANES), 0)
    s = lax.broadcasted_iota(jnp.int32, (LANES, LANES), 1)
    strict_upper = (r < s).astype(BF16)
    nblk = seq_len // LANES

    def excl_cumsum(m):
        carry = jnp.zeros((n_rows, 1), F32)
        outs = []
        for blk in range(nblk):
            mb = m[:, blk * LANES:(blk + 1) * LANES]
            outs.append(jnp.dot(mb.astype(BF16), strict_upper, preferred_element_type=F32) + carry)
            carry = carry + jnp.sum(mb, axis=1, keepdims=True)
        return jnp.concatenate(outs, axis=1)

    eq_rank = excl_cumsum(jnp.where(eq, 1.0, 0.0))
    sel = gt | (eq & (eq_rank < need.astype(F32)))
    rank = excl_cumsum(jnp.where(sel, 1.0, 0.0))
    pos = jnp.where(sel, rank, -1.0)
    pos_ref[...] = pos.astype(jnp.int32).reshape(rb, N_EXPERTS, seq_len)

    lane = lax.broadcasted_iota(jnp.int32, (n_rows, LANES), 1)
    tab = jnp.full((n_rows, LANES), float(cap), F32)
    for k in range(seq_len // tbs):
        tab = jnp.where(lane == k, rank[:, k * tbs:k * tbs + 1], tab)
    tab_ref[...] = tab.astype(jnp.int32).reshape(rb, N_EXPERTS, LANES)

    pad = jnp.full((LANES - N_EXPERTS, LANES), -1.0, F32)
    for i in range(rb):
        for blk in range(nblk):
            tile = jnp.concatenate([pos[i * N_EXPERTS:(i + 1) * N_EXPERTS, blk * LANES:(blk + 1) * LANES], pad], axis=0)
            post_ref[i, blk * LANES:(blk + 1) * LANES, :] = tile.T.astype(jnp.int32)


def _topk(afft, cap, rb, tbs):
    bn, _, seq_len = afft.shape
    return pl.pallas_call(
        functools.partial(_topk_kernel, seq_len=seq_len, cap=cap, rb=rb, tbs=tbs),
        grid=(bn // rb,),
        in_specs=[pl.BlockSpec((rb, N_EXPERTS, seq_len), lambda b: (b, 0, 0))],
        out_specs=[pl.BlockSpec((rb, N_EXPERTS, seq_len), lambda b: (b, 0, 0)),
                   pl.BlockSpec((rb, seq_len, LANES), lambda b: (b, 0, 0)),
                   pl.BlockSpec((rb, N_EXPERTS, LANES), lambda b: (b, 0, 0))],
        out_shape=[jax.ShapeDtypeStruct((bn, N_EXPERTS, seq_len), jnp.int32),
                   jax.ShapeDtypeStruct((bn, seq_len, LANES), jnp.int32),
                   jax.ShapeDtypeStruct((bn, N_EXPERTS, LANES), jnp.int32)],
        compiler_params=_params(("arbitrary",)),
        name="topk",
    )(afft)


def _gather_kernel(h2_ref, pos_ref, afft_ref, xe_ref, gate_ref, *, cap):
    seq_len = h2_ref.shape[1]
    n_slots = N_EXPERTS * cap
    slot = lax.broadcasted_iota(jnp.int32, (n_slots, seq_len), 0)
    onehot = jnp.zeros((n_slots, seq_len), F32)
    gval = jnp.zeros((n_slots, seq_len), F32)
    for e in range(N_EXPERTS):
        p = pos_ref[0, e:e + 1, :]
        hit = jnp.where(p >= 0, p + e * cap, -1) == slot
        onehot = jnp.where(hit, 1.0, onehot)
        gval = jnp.where(hit, afft_ref[0, e:e + 1, :], gval)
    xe_ref[0] = jnp.dot(onehot.astype(BF16), h2_ref[0], preferred_element_type=F32).astype(BF16)
    gate_ref[0] = jnp.broadcast_to(jnp.sum(gval, axis=1, keepdims=True), (n_slots, LANES))


def _gather(h2, pos, afft, cap):
    bn, seq_len, _ = h2.shape
    n_slots = N_EXPERTS * cap
    return pl.pallas_call(
        functools.partial(_gather_kernel, cap=cap),
        grid=(bn,),
        in_specs=[pl.BlockSpec((1, seq_len, D_MODEL), lambda b: (b, 0, 0)),
                  pl.BlockSpec((1, N_EXPERTS, seq_len), lambda b: (b, 0, 0)),
                  pl.BlockSpec((1, N_EXPERTS, seq_len), lambda b: (b, 0, 0))],
        out_specs=[pl.BlockSpec((1, n_slots, D_MODEL), lambda b: (b, 0, 0)),
                   pl.BlockSpec((1, n_slots, LANES), lambda b: (b, 0, 0))],
        out_shape=[jax.ShapeDtypeStruct((bn, n_slots, D_MODEL), BF16),
                   jax.ShapeDtypeStruct((bn, n_slots, LANES), F32)],
        compiler_params=_params(("arbitrary",)),
        name="gather",
    )(h2, pos, afft)


def _window_starts(tab_ref, b, e0, n_e, t, cap, win):
    starts, overflows = [], []
    for ee in range(n_e):
        base = (b * N_EXPERTS + e0 + ee) * LANES + t
        p0 = tab_ref[base]
        p1 = tab_ref[base + 1]
        w0 = jnp.minimum(lax.shift_left(lax.shift_right_logical(p0, 4), 4), cap - win)
        starts.append(w0)
        overflows.append(p1 - w0 > win)
    return starts, overflows


def _gather_win_kernel(tab_ref, h2_ref, pos_ref, afft_ref, xe_ref, gate_ref, *, cap, win, n_e):
    b = pl.program_id(0)
    e0 = pl.program_id(1) * n_e
    t = pl.program_id(2)
    tbs = h2_ref.shape[1]

    @pl.when(t == 0)
    def _():
        xe_ref[...] = jnp.zeros(xe_ref.shape, BF16)
        gate_ref[...] = jnp.zeros(gate_ref.shape, F32)

    starts, overflows = _window_starts(tab_ref, b, e0, n_e, t, cap, win)

    def accumulate(row0, n, x, g):
        cur = xe_ref[0, pl.ds(row0, n), :]
        xe_ref[0, pl.ds(row0, n), :] = (cur.astype(F32) + x).astype(BF16)
        gate_ref[0, pl.ds(row0, n), :] = gate_ref[0, pl.ds(row0, n), :] + jnp.broadcast_to(g, (n, LANES))

    def onehot_and_gate(ee, shift, n):
        sub = lax.broadcasted_iota(jnp.int32, (n, tbs), 0)
        hit = (pos_ref[0, pl.ds(e0 + ee, 1), :] - shift) == sub
        gate = jnp.sum(jnp.where(hit, afft_ref[0, pl.ds(e0 + ee, 1), :], 0.0), axis=1, keepdims=True)
        return jnp.where(hit, 1.0, 0.0).astype(BF16), gate

    pieces = [onehot_and_gate(ee, jnp.where(overflows[ee], cap, starts[ee]), win) for ee in range(n_e)]
    x = jnp.dot(jnp.concatenate([p[0] for p in pieces], axis=0), h2_ref[0], preferred_element_type=F32)
    for ee in range(n_e):
        accumulate(pl.multiple_of(ee * cap + starts[ee], 16), win, x[ee * win:(ee + 1) * win], pieces[ee][1])

    for ee in range(n_e):
        @pl.when(overflows[ee])
        def _(ee=ee):
            onehot, gate = onehot_and_gate(ee, 0, cap)
            accumulate(ee * cap, cap, jnp.dot(onehot, h2_ref[0], preferred_element_type=F32), gate)


def _gather_win(tab, h2, pos, afft, cap, win, tbs):
    bn, seq_len, _ = h2.shape
    n_slots = N_EXPERTS * cap
    n_e = N_EXPERTS // 2
    grid_spec = pltpu.PrefetchScalarGridSpec(
        num_scalar_prefetch=1,
        grid=(bn, N_EXPERTS // n_e, seq_len // tbs),
        in_specs=[pl.BlockSpec((1, tbs, D_MODEL), lambda b, g, t, tab: (b, t, 0)),
                  pl.BlockSpec((1, N_EXPERTS, tbs), lambda b, g, t, tab: (b, 0, t)),
                  pl.BlockSpec((1, N_EXPERTS, tbs), lambda b, g, t, tab: (b, 0, t))],
        out_specs=[pl.BlockSpec((1, n_e * cap, D_MODEL), lambda b, g, t, tab: (b, g, 0)),
                   pl.BlockSpec((1, n_e * cap, LANES), lambda b, g, t, tab: (b, g, 0))],
    )
    return pl.pallas_call(
        functools.partial(_gather_win_kernel, cap=cap, win=win, n_e=n_e),
        grid_spec=grid_spec,
        out_shape=[jax.ShapeDtypeStruct((bn, n_slots, D_MODEL), BF16),
                   jax.ShapeDtypeStruct((bn, n_slots, LANES), F32)],
        compiler_params=_params(("arbitrary", "arbitrary", "arbitrary")),
        name="gather_win",
    )(tab, h2, pos, afft)


def _ffn_kernel(xe_ref, gate_ref, wg_ref, wu_ref, wd_ref, ye_ref, wg_s, wu_s, wd_s, *, rows):
    @pl.when(pl.program_id(1) == 0)
    def _():
        wg_s[...] = wg_ref[0, 0].astype(BF16)
        wu_s[...] = wu_ref[0, 0].astype(BF16)
        wd_s[...] = wd_ref[0, 0].astype(BF16)

    x = xe_ref[...].reshape(rows, D_MODEL)
    g = jnp.dot(x, wg_s[...], preferred_element_type=F32)
    u = jnp.dot(x, wu_s[...], preferred_element_type=F32)
    hid = (g * _sigmoid(g) * u).astype(BF16)
    y = jnp.dot(hid, wd_s[...], preferred_element_type=F32) * gate_ref[...].reshape(rows, LANES)[:, :1]
    ye_ref[...] = y.astype(BF16).reshape(ye_ref.shape)


def _ffn(xe, gate, w_gate, w_up, w_down, layer, cap, bb):
    bn, n_slots, _ = xe.shape
    rows = bb * cap
    wspec = pl.BlockSpec((1, 1, D_MODEL, D_MODEL), lambda e, j: (layer, e, 0, 0))
    return pl.pallas_call(
        functools.partial(_ffn_kernel, rows=rows),
        grid=(N_EXPERTS, bn // bb),
        in_specs=[pl.BlockSpec((bb, cap, D_MODEL), lambda e, j: (j, e, 0)),
                  pl.BlockSpec((bb, cap, LANES), lambda e, j: (j, e, 0)),
                  wspec, wspec, wspec],
        out_specs=pl.BlockSpec((bb, cap, D_MODEL), lambda e, j: (j, e, 0)),
        out_shape=jax.ShapeDtypeStruct((bn, n_slots, D_MODEL), BF16),
        scratch_shapes=[pltpu.VMEM((D_MODEL, D_MODEL), BF16)] * 3,
        compiler_params=_params(("arbitrary", "arbitrary")),
        name="ffn",
    )(xe, gate, w_gate, w_up, w_down)


def _finish(x1_ref, mod_ref, fg_ref, out_ref, acc, final):
    x2 = x1_ref[0] + mod_ref[0][5:6] * acc
    if final:
        x2 = x2 * lax.rsqrt(jnp.mean(x2 * x2, axis=-1, keepdims=True) + EPS) * fg_ref[...]
    out_ref[0] = x2


def _scatter_kernel(x1_ref, ye_ref, post_ref, mod_ref, fg_ref, out_ref, *, cap, final):
    tl = x1_ref.shape[1]
    n_slots = N_EXPERTS * cap
    pt = post_ref[0]
    slot = lax.broadcasted_iota(jnp.int32, (tl, n_slots), 1)
    onehot = jnp.zeros((tl, n_slots), F32)
    for e in range(N_EXPERTS):
        p = pt[:, e:e + 1]
        onehot = jnp.where(jnp.where(p >= 0, p + e * cap, -1) == slot, 1.0, onehot)
    acc = jnp.dot(onehot.astype(BF16), ye_ref[0], preferred_element_type=F32)
    _finish(x1_ref, mod_ref, fg_ref, out_ref, acc, final)


def _scatter(x1, ye, post, mod_l, mod_row, final_g, cap, final):
    bn, seq_len, _ = x1.shape
    n_slots = N_EXPERTS * cap
    return pl.pallas_call(
        functools.partial(_scatter_kernel, cap=cap, final=final),
        grid=(bn,),
        in_specs=[pl.BlockSpec((1, seq_len, D_MODEL), lambda b: (b, 0, 0)),
                  pl.BlockSpec((1, n_slots, D_MODEL), lambda b: (b, 0, 0)),
                  pl.BlockSpec((1, seq_len, LANES), lambda b: (b, 0, 0)),
                  pl.BlockSpec((1, N_MOD, D_MODEL), lambda b: (mod_row(b), 0, 0)),
                  pl.BlockSpec((1, D_MODEL), lambda b: (0, 0))],
        out_specs=pl.BlockSpec((1, seq_len, D_MODEL), lambda b: (b, 0, 0)),
        out_shape=jax.ShapeDtypeStruct((bn, seq_len, D_MODEL), F32),
        compiler_params=_params(("arbitrary",)),
        name="scatter",
    )(x1, ye, post, mod_l, final_g)


def _scatter_win_kernel(tab_ref, x1_ref, ye_ref, post_ref, mod_ref, fg_ref, out_ref, acc_ref, *, cap, win, final):
    b = pl.program_id(0)
    t = pl.program_id(1)
    tbs = x1_ref.shape[1]
    pt = post_ref[0]
    starts, overflows = _window_starts(tab_ref, b, 0, N_EXPERTS, t, cap, win)

    def onehot(e, shift, n):
        lane = lax.broadcasted_iota(jnp.int32, (tbs, n), 1)
        return jnp.where((pt[:, e:e + 1] - shift) == lane, 1.0, 0.0).astype(BF16)

    onehots = [onehot(e, jnp.where(overflows[e], cap, starts[e]), win) for e in range(N_EXPERTS)]
    rows = [ye_ref[0, pl.ds(pl.multiple_of(e * cap + starts[e], 16), win), :] for e in range(N_EXPERTS)]
    acc_ref[...] = jnp.dot(jnp.concatenate(onehots, axis=1), jnp.concatenate(rows, axis=0),
                           preferred_element_type=F32)

    for e in range(N_EXPERTS):
        @pl.when(overflows[e])
        def _(e=e):
            acc_ref[...] += jnp.dot(onehot(e, 0, cap), ye_ref[0, e * cap:(e + 1) * cap, :],
                                    preferred_element_type=F32)

    _finish(x1_ref, mod_ref, fg_ref, out_ref, acc_ref[...], final)


def _scatter_win(tab, x1, ye, post, mod_l, mod_row, final_g, cap, win, tbs, final):
    bn, seq_len, _ = x1.shape
    n_slots = N_EXPERTS * cap
    grid_spec = pltpu.PrefetchScalarGridSpec(
        num_scalar_prefetch=1,
        grid=(bn, seq_len // tbs),
        in_specs=[pl.BlockSpec((1, tbs, D_MODEL), lambda b, t, tab: (b, t, 0)),
                  pl.BlockSpec((1, n_slots, D_MODEL), lambda b, t, tab: (b, 0, 0), pipeline_mode=pl.Buffered(1)),
                  pl.BlockSpec((1, tbs, LANES), lambda b, t, tab: (b, t, 0)),
                  pl.BlockSpec((1, N_MOD, D_MODEL), lambda b, t, tab: (mod_row(b), 0, 0)),
                  pl.BlockSpec((1, D_MODEL), lambda b, t, tab: (0, 0))],
        out_specs=pl.BlockSpec((1, tbs, D_MODEL), lambda b, t, tab: (b, t, 0)),
        scratch_shapes=[pltpu.VMEM((tbs, D_MODEL), F32)],
    )
    return pl.pallas_call(
        functools.partial(_scatter_win_kernel, cap=cap, win=win, final=final),
        grid_spec=grid_spec,
        out_shape=jax.ShapeDtypeStruct((bn, seq_len, D_MODEL), F32),
        compiler_params=_params(("arbitrary", "arbitrary")),
        name="scatter_win",
    )(tab, x1, ye, post, mod_l, final_g)


def _grid_position_embedding(n_tokens, dim):
    rows = n_tokens // GRID_W
    row = jnp.repeat(jnp.arange(rows), GRID_W).astype(F32)
    col = jnp.tile(jnp.arange(GRID_W), rows).astype(F32)
    quarter = dim // 4
    inv_freq = jnp.power(POS_BASE, -jnp.arange(quarter, dtype=F32) / quarter)
    er = row[:, None] * inv_freq[None]
    ec = col[:, None] * inv_freq[None]
    return jnp.concatenate([jnp.sin(er), jnp.cos(er), jnp.sin(ec), jnp.cos(ec)], axis=-1)


def _pad_cols(w, width):
    return jnp.pad(w, ((0, 0), (0, width - w.shape[1])))


def _router_pieces(w):
    w_hi = w.astype(BF16)
    w_lo = (w - w_hi.astype(F32)).astype(BF16)
    return jnp.stack([_pad_cols(jnp.concatenate([w_hi, w_lo], axis=1), LANES), _pad_cols(w_hi, LANES)])


def _run_group(x, mod, mod_row, h0f, h0b, h0_row, lw, final_g, tl, ffn_rows_per_step, topk_rows):
    bn, seq_len, _ = x.shape
    cap = CAPACITY_FACTOR * seq_len // N_EXPERTS
    windowed = seq_len >= 1024
    tbs = seq_len // 8 if windowed else seq_len
    win = cap // 4
    states_f, states_b = [], []
    for l in range(DEPTH):
        w = lw[l]
        final = l == DEPTH - 1
        y_pool, z, xs, bc, dt = _in_proj(x, mod[l], mod_row, w["norm_mix_g"], w["w_main"], w["w_dt"], w["pool_w"],
                                         w["pool_scale"], w["conv_w"], w["conv_b"], tl)
        y_f, y_b, hT_f, hT_b = _ssd(xs, bc, dt, w["alog_row"], w["bias_row"], h0f[l], h0b[l], h0_row)
        states_f.append(hT_f)
        states_b.append(hT_b)
        x1, h2, afft = _out_proj(x, y_f, y_b, xs, z, y_pool, w["dsk"], w["ssm_norm_g"], w["w_out"], mod[l], mod_row,
                                 w["norm_ffn_g"], w["w_router"], tl)
        pos, post, tab = _topk(afft, cap, topk_rows, tbs)
        if windowed:
            tab = tab.reshape(-1)
            xe, gate = _gather_win(tab, h2, pos, afft, cap, win, tbs)
        else:
            xe, gate = _gather(h2, pos, afft, cap)
        ye = _ffn(xe, gate, w["w_gate"], w["w_up"], w["w_down"], l, cap, ffn_rows_per_step // cap)
        if windowed:
            x = _scatter_win(tab, x1, ye, post, mod[l], mod_row, final_g, cap, win, tbs, final)
        else:
            x = _scatter(x1, ye, post, mod[l], mod_row, final_g, cap, final)
    return x, states_f, states_b


def kernel(x_prompt, x_sample, state_ssm_fwd, state_ssm_bwd, c, c_ctx, norm_mix_g, w_ada, b_ada, w_in, pool_w,
           pool_scale, conv_w, conv_b, a_log_fwd, a_log_bwd, dt_bias_fwd, dt_bias_bwd, d_skip, ssm_norm_g, w_out,
           norm_ffn_g, w_router, w_gate, w_up, w_down, final_norm_g):
    n_dec = c.shape[0]
    n_ctx = x_prompt.shape[0]
    ctx_row = n_dec
    cond = jnp.concatenate([c, c_ctx[None, :], jnp.zeros((8 - n_dec - 1, D_MODEL), F32)], axis=0)
    mod = _modulation(cond, w_ada, b_ada).reshape(DEPTH, 8, N_MOD, D_MODEL)

    zeros_h = jnp.zeros((LANES - 2 * SSD_HEADS,), F32)
    lw = []
    for l in range(DEPTH):
        lw.append(dict(
            norm_mix_g=norm_mix_g[l][None], w_main=w_in[l][:, :N_MAIN].astype(BF16),
            w_dt=_pad_cols(w_in[l][:, N_MAIN:], LANES).astype(BF16), pool_w=pool_w[l].astype(BF16),
            pool_scale=pool_scale[l][None], conv_w=conv_w[l], conv_b=conv_b[l][None],
            alog_row=jnp.concatenate([a_log_fwd[l], a_log_bwd[l], zeros_h])[None],
            bias_row=jnp.concatenate([dt_bias_fwd[l], dt_bias_bwd[l], zeros_h])[None],
            dsk=jnp.repeat(d_skip[l], SSD_INNER // SSD_HEADS)[None], ssm_norm_g=ssm_norm_g[l][None],
            w_out=w_out[l].astype(BF16), norm_ffn_g=norm_ffn_g[l][None], w_router=_router_pieces(w_router[l]),
            w_gate=w_gate, w_up=w_up, w_down=w_down))
    final_g = final_norm_g[None]
    st_shape = (HEAD_PAIRS, LANES, SSD_STATE)

    zero_state = jnp.zeros((1,) + st_shape, F32)
    y_prompt, sf, sb = _run_group(x_prompt, mod, lambda b: ctx_row, [zero_state] * DEPTH, [zero_state] * DEPTH,
                                  lambda b: 0, lw, final_g, tl=x_prompt.shape[1], ffn_rows_per_step=1024,
                                  topk_rows=8)
    out_state_shape = (n_ctx, SSD_HEADS, SSD_INNER // SSD_HEADS, SSD_STATE)
    new_f = jnp.stack([s.reshape(out_state_shape) for s in sf], axis=1)
    new_b = jnp.stack([s.reshape(out_state_shape) for s in sb], axis=1)

    xs0 = x_sample + _grid_position_embedding(x_sample.shape[1], D_MODEL)[None]
    h0f = [state_ssm_fwd[:, l].reshape((n_dec,) + st_shape) for l in range(DEPTH)]
    h0b = [state_ssm_bwd[:, l].reshape((n_dec,) + st_shape) for l in range(DEPTH)]
    y_sample, _, _ = _run_group(xs0, mod, lambda b: b, h0f, h0b, lambda b: b, lw, final_g, tl=512,
                                ffn_rows_per_step=1024, topk_rows=n_dec)
    return (y_prompt, y_sample, new_f, new_b)
```

```python
import functools

import jax
import jax.numpy as jnp
from jax import lax
from jax.experimental import pallas as pl
from jax.experimental.pallas import tpu as pltpu

D_MODEL = 1024
DEPTH = 2
GRID_W = 64
MIX_POOL = 512
POOL_WINDOWS = (2, 4, 8, 16)
POOL_GROUP_CH = 128
SSD_INNER = 512
SSD_HEADS = 8
SSD_HEAD_DIM = SSD_INNER // SSD_HEADS
SSD_STATE = 128
CONV_K = 5
CHUNK = 128
CONV_DIM = 1024
N_EXPERTS = 16
CAPACITY_FACTOR = 2
N_MOD = 6
POS_BASE = 10000.0
EPS = 1e-6

HALO = 8
LANES = 128
N_MAIN = MIX_POOL + SSD_INNER + CONV_DIM
HEAD_PAIRS = SSD_HEADS // 2
SCATTER_BLOCKS_PER_STEP = 2
VMEM_LIMIT = 56 * 1024 * 1024

F32 = jnp.float32
BF16 = jnp.bfloat16
HI = lax.Precision.HIGHEST


def _sigmoid(x):
    return 1.0 / (1.0 + jnp.exp(-x))


def _params(sem):
    return pltpu.CompilerParams(dimension_semantics=sem, vmem_limit_bytes=VMEM_LIMIT)


def _mod_kernel(cond_ref, w_ref, b_ref, out_ref):
    c = cond_ref[...]
    s = c * _sigmoid(c)
    out_ref[0] = jnp.dot(s, w_ref[0], precision=HI, preferred_element_type=F32) + b_ref[0]


def _modulation(cond, w_ada, b_ada):
    n_rows = cond.shape[0]
    n_out = N_MOD * D_MODEL
    tn = 1536
    return pl.pallas_call(
        _mod_kernel,
        grid=(DEPTH, n_out // tn),
        in_specs=[
            pl.BlockSpec((n_rows, D_MODEL), lambda l, j: (0, 0)),
            pl.BlockSpec((1, D_MODEL, tn), lambda l, j: (l, 0, j)),
            pl.BlockSpec((1, 1, tn), lambda l, j: (l, 0, j)),
        ],
        out_specs=pl.BlockSpec((1, n_rows, tn), lambda l, j: (l, 0, j)),
        out_shape=jax.ShapeDtypeStruct((DEPTH, n_rows, n_out), F32),
        compiler_params=_params(("arbitrary", "arbitrary")),
        name="modulation",
    )(cond, w_ada, b_ada.reshape(DEPTH, 1, n_out))


def _cumsum_pieces(dta):
    d_hi = dta.astype(BF16)
    rest = dta - d_hi.astype(F32)
    d_mid = rest.astype(BF16)
    return d_hi, d_mid, (rest - d_mid.astype(F32)).astype(BF16)


def _in_kernel(xp_ref, xc_ref, xn_ref, mod_ref, g_ref, wmain_ref, wdt_ref, poolw_ref, pscale_ref, cw_ref, cb_ref,
               alog_ref, bias_ref, ypool_ref, z_ref, xs_ref, bc_ref, dt_ref, cum_ref, proj_scr, *, tl, seq_len):
    i = pl.program_id(1)
    n = pl.num_programs(1)
    rows = tl + 2 * HALO
    x_ext = jnp.concatenate([xp_ref[0], xc_ref[0], xn_ref[0]], axis=0)
    mod = mod_ref[0]
    ms = jnp.mean(x_ext * x_ext, axis=-1, keepdims=True)
    h = x_ext * lax.rsqrt(ms + EPS) * (g_ref[...] * (1.0 + mod[1:2])) + mod[0:1]
    row = lax.broadcasted_iota(jnp.int32, (rows, 1), 0)
    lo_ok = jnp.where(i > 0, 0, HALO)
    hi_ok = jnp.where(i < n - 1, rows, tl + HALO)
    hb = jnp.where((row >= lo_ok) & (row < hi_ok), h, 0.0).astype(BF16)
    proj_scr[...] = jnp.dot(hb, wmain_ref[...], preferred_element_type=F32)
    dt_full = jnp.dot(hb, wdt_ref[...], preferred_element_type=F32)

    x_in = dt_full[HALO:HALO + tl] + bias_ref[...]
    dt = jnp.maximum(x_in, 0.0) + jnp.log1p(jnp.exp(-jnp.abs(x_in)))
    dt_ref[0] = dt
    r = lax.broadcasted_iota(jnp.int32, (CHUNK, CHUNK), 0)
    s = lax.broadcasted_iota(jnp.int32, (CHUNK, CHUNK), 1)
    tri = jnp.concatenate([jnp.where(r >= s, 1.0, 0.0), jnp.where(r <= s, 1.0, 0.0)], axis=1).astype(BF16)
    head = lax.broadcasted_iota(jnp.int32, (tl, LANES), 1)
    dta = jnp.where(head < 2 * SSD_HEADS, dt * -jnp.exp(alog_ref[...]), 0.0)
    pieces = _cumsum_pieces(dta)
    fwd_cols = s < SSD_HEADS
    for k in range(tl // CHUNK):
        acc = jnp.zeros((CHUNK, LANES), F32)
        for piece in pieces:
            pk = piece[k * CHUNK:(k + 1) * CHUNK]
            zero = jnp.zeros_like(pk)
            stacked = jnp.concatenate([jnp.where(fwd_cols, pk, zero), jnp.where(fwd_cols, zero, pk)], axis=0)
            acc = acc + jnp.dot(tri, stacked, preferred_element_type=F32)
        cum_ref[0, k * CHUNK:(k + 1) * CHUNK, :] = acc

    def ahead(v, k):
        return pltpu.roll(v, (-k) % rows, axis=0)

    t_glob = i * tl + lax.broadcasted_iota(jnp.int32, (tl, 1), 0)
    outs = []
    for gi, w in enumerate(POOL_WINDOWS):
        c0 = gi * POOL_GROUP_CH
        u = proj_scr[:, c0:c0 + POOL_GROUP_CH]
        acc = u
        span = 1
        while 2 * span < w:
            acc = acc + ahead(acc, span)
            span *= 2
        acc = acc + ahead(acc, -span)
        lo = jnp.maximum(t_glob - w // 2, 0)
        hi = jnp.minimum(t_glob - w // 2 + w, seq_len)
        cnt = (hi - lo).astype(F32)
        p = acc[HALO:HALO + tl] / cnt - u[HALO:HALO + tl]
        outs.append(jnp.dot(p.astype(BF16), poolw_ref[gi], preferred_element_type=F32))
    ypool_ref[0] = (jnp.concatenate(outs, axis=1) * pscale_ref[...]).astype(BF16)

    c0 = MIX_POOL + SSD_INNER
    acc = cb_ref[...] + cw_ref[0:1, :] * proj_scr[HALO - 2:HALO - 2 + tl, c0:c0 + CONV_DIM]
    for k in range(1, CONV_K):
        acc = acc + cw_ref[k:k + 1, :] * proj_scr[HALO - 2 + k:HALO - 2 + k + tl, c0:c0 + CONV_DIM]
    act = acc * _sigmoid(acc)
    xs_ref[0] = act[:, :SSD_INNER].astype(BF16)
    bc_ref[0] = act[:, SSD_INNER:].astype(BF16)
    z_ref[0] = proj_scr[HALO:HALO + tl, MIX_POOL:MIX_POOL + SSD_INNER]


def _in_proj(x, mod_l, mod_row, norm_g, w_main, w_dt, pool_w, pool_scale, conv_w, conv_b, alog_row, bias_row, tl):
    bn, seq_len, _ = x.shape
    nt = seq_len // tl
    hb = tl // HALO
    last_halo = seq_len // HALO - 1
    const = lambda *shape: pl.BlockSpec(shape, lambda b, i: (0,) * len(shape))
    tok = lambda width: pl.BlockSpec((1, tl, width), lambda b, i: (b, i, 0))
    return pl.pallas_call(
        functools.partial(_in_kernel, tl=tl, seq_len=seq_len),
        grid=(bn, nt),
        in_specs=[
            pl.BlockSpec((1, HALO, D_MODEL), lambda b, i: (b, jnp.maximum(i * hb - 1, 0), 0)),
            tok(D_MODEL),
            pl.BlockSpec((1, HALO, D_MODEL), lambda b, i: (b, jnp.minimum((i + 1) * hb, last_halo), 0)),
            pl.BlockSpec((1, N_MOD, D_MODEL), lambda b, i: (mod_row(b), 0, 0)),
            const(1, D_MODEL),
            const(D_MODEL, N_MAIN),
            const(D_MODEL, LANES),
            const(len(POOL_WINDOWS), POOL_GROUP_CH, POOL_GROUP_CH),
            const(1, MIX_POOL),
            const(CONV_K, CONV_DIM),
            const(1, CONV_DIM),
            const(1, LANES),
            const(1, LANES),
        ],
        out_specs=[tok(MIX_POOL), tok(SSD_INNER), tok(SSD_INNER), tok(CONV_DIM - SSD_INNER), tok(LANES), tok(LANES)],
        out_shape=[
            jax.ShapeDtypeStruct((bn, seq_len, MIX_POOL), BF16),
            jax.ShapeDtypeStruct((bn, seq_len, SSD_INNER), F32),
            jax.ShapeDtypeStruct((bn, seq_len, SSD_INNER), BF16),
            jax.ShapeDtypeStruct((bn, seq_len, CONV_DIM - SSD_INNER), BF16),
            jax.ShapeDtypeStruct((bn, seq_len, LANES), F32),
            jax.ShapeDtypeStruct((bn, seq_len, LANES), F32),
        ],
        scratch_shapes=[pltpu.VMEM((tl + 2 * HALO, N_MAIN), F32)],
        compiler_params=_params(("arbitrary", "arbitrary")),
        name="in_proj",
    )(x, x, x, mod_l, norm_g, w_main, w_dt, pool_w, pool_scale, conv_w, conv_b, alog_row, bias_row)


def _ssd_kernel(xsf_ref, bcf_ref, dtf_ref, cumf_ref, xsb_ref, bcb_ref, dtb_ref, cumb_ref, h0f_ref, h0b_ref,
                yf_ref, yb_ref, hTf_ref, hTb_ref, st_ref):
    c = pl.program_id(1)
    nc = pl.num_programs(1)

    @pl.when(c == 0)
    def _():
        st_ref[0] = h0f_ref[0]
        st_ref[1] = h0b_ref[0]

    r = lax.broadcasted_iota(jnp.int32, (CHUNK, CHUNK), 0)
    s = lax.broadcasted_iota(jnp.int32, (CHUNK, CHUNK), 1)
    lane2 = lax.broadcasted_iota(jnp.int32, (CHUNK, 2 * LANES), 1)
    lane_head = lax.shift_right_logical(lane2, SSD_HEAD_DIM.bit_length() - 1)
    row2 = lax.broadcasted_iota(jnp.int32, (2 * LANES, SSD_STATE), 0)
    is_fwd = s < SSD_HEADS
    dt = jnp.where(is_fwd, dtf_ref[0], dtb_ref[0])
    cum = jnp.where(is_fwd, cumf_ref[0], cumb_ref[0])
    tot = jnp.where(lax.broadcasted_iota(jnp.int32, (1, LANES), 1) < SSD_HEADS,
                    cum[CHUNK - 1:CHUNK, :], cum[0:1, :])
    cdec = jnp.exp(tot)
    cum_t = cum.T
    dt_t = dt.T
    ecum_t = jnp.exp(cum_t)
    w_t = (dt * jnp.exp(tot - cum)).T

    def head_rows(arr_t, h0):
        out = arr_t[h0 + 3:h0 + 4, :]
        for k in (2, 1, 0):
            out = jnp.where(row2 < (k + 1) * SSD_HEAD_DIM, arr_t[h0 + k:h0 + k + 1, :], out)
        return out

    def one_dir(d, xs_ref, bc_ref, y_ref):
        mask = (r >= s) if d == 0 else (r <= s)
        off = d * SSD_HEADS
        bcv = bc_ref[0]
        b01 = bcv[:, :2 * SSD_STATE].astype(BF16)
        c01 = bcv[:, 2 * SSD_STATE:].astype(BF16)
        zero2 = jnp.zeros_like(b01)
        b_blk = jnp.concatenate([jnp.where(lane2 < SSD_STATE, b01, zero2),
                                 jnp.where(lane2 < SSD_STATE, zero2, b01)], axis=0)
        cb01 = lax.dot_general(c01, b_blk, (((1,), (1,)), ((), ())), preferred_element_type=F32)
        for g in range(2):
            h0 = off + 4 * g
            b_g = b01[:, g * SSD_STATE:(g + 1) * SSD_STATE]
            c_g = c01[:, g * SSD_STATE:(g + 1) * SSD_STATE]
            cb = cb01[:, g * CHUNK:(g + 1) * CHUNK]
            xs_g = xs_ref[0, :, g * 2 * LANES:(g + 1) * 2 * LANES]
            x_b = xs_g.astype(BF16)
            ms, x_parts = [], []
            for k in range(4):
                hh = h0 + k
                seg = cum[:, hh:hh + 1] - cum_t[hh:hh + 1, :]
                dec = jnp.exp(jnp.where(mask, seg, -jnp.inf))
                ms.append((cb * dec * dt_t[hh:hh + 1, :]).astype(BF16))
                x_parts.append(jnp.where(lane_head == k, x_b, jnp.zeros_like(x_b)))
            y_diag = jnp.dot(jnp.concatenate(ms, axis=1), jnp.concatenate(x_parts, axis=0),
                             preferred_element_type=F32)
            st = st_ref[d, 2 * g:2 * g + 2].reshape(2 * LANES, SSD_STATE)
            y_off_t = lax.dot_general(st.astype(BF16), c_g, (((1,), (1,)), ((), ())),
                                      preferred_element_type=F32) * head_rows(ecum_t, h0)
            y_ref[0, :, g * 2 * LANES:(g + 1) * 2 * LANES] = (y_diag + y_off_t.T).astype(BF16)
            x_d = (xs_g.astype(F32).T * head_rows(w_t, h0)).astype(BF16)
            cs = jnp.dot(x_d, b_g, preferred_element_type=F32)
            dcol = cdec[:, h0 + 3:h0 + 4]
            for k in (2, 1, 0):
                dcol = jnp.where(row2 < (k + 1) * SSD_HEAD_DIM, cdec[:, h0 + k:h0 + k + 1], dcol)
            st_ref[d, 2 * g:2 * g + 2] = (st * dcol + cs).reshape(2, LANES, SSD_STATE)

    one_dir(0, xsf_ref, bcf_ref, yf_ref)
    one_dir(1, xsb_ref, bcb_ref, yb_ref)

    @pl.when(c == nc - 1)
    def _():
        hTf_ref[0] = st_ref[0]
        hTb_ref[0] = st_ref[1]


def _ssd(xs, bc, dt, cum, h0f, h0b, h0_row):
    bn, seq_len, _ = xs.shape
    nc = seq_len // CHUNK
    fwd = lambda width: pl.BlockSpec((1, CHUNK, width), lambda b, c: (b, c, 0))
    bwd = lambda width: pl.BlockSpec((1, CHUNK, width), lambda b, c: (b, nc - 1 - c, 0))
    st_spec_in = pl.BlockSpec((1, HEAD_PAIRS, LANES, SSD_STATE), lambda b, c: (h0_row(b), 0, 0, 0))
    st_spec_out = pl.BlockSpec((1, HEAD_PAIRS, LANES, SSD_STATE), lambda b, c: (b, 0, 0, 0))
    st_shape = jax.ShapeDtypeStruct((bn, HEAD_PAIRS, LANES, SSD_STATE), F32)
    return pl.pallas_call(
        _ssd_kernel,
        grid=(bn, nc),
        in_specs=[fwd(SSD_INNER), fwd(CONV_DIM - SSD_INNER), fwd(LANES), fwd(LANES),
                  bwd(SSD_INNER), bwd(CONV_DIM - SSD_INNER), bwd(LANES), bwd(LANES),
                  st_spec_in, st_spec_in],
        out_specs=[fwd(SSD_INNER), bwd(SSD_INNER), st_spec_out, st_spec_out],
        out_shape=[jax.ShapeDtypeStruct((bn, seq_len, SSD_INNER), BF16),
                   jax.ShapeDtypeStruct((bn, seq_len, SSD_INNER), BF16), st_shape, st_shape],
        scratch_shapes=[pltpu.VMEM((2, HEAD_PAIRS, LANES, SSD_STATE), F32)],
        compiler_params=_params(("arbitrary", "arbitrary")),
        name="ssd",
    )(xs, bc, dt, cum, xs, bc, dt, cum, h0f, h0b)


def _out_kernel(x_ref, yf_ref, yb_ref, xs_ref, z_ref, yp_ref, dsk_ref, sg_ref, wo_ref, mod_ref, ng_ref, wr_ref,
                x1_ref, h2_ref, afft_ref):
    y = yf_ref[0].astype(F32) + yb_ref[0].astype(F32) + dsk_ref[...] * xs_ref[0].astype(F32)
    z = z_ref[0]
    y = y * (z * _sigmoid(z))
    y = y * lax.rsqrt(jnp.mean(y * y, axis=-1, keepdims=True) + EPS) * sg_ref[...]
    o = (jnp.dot(yp_ref[0], wo_ref[:MIX_POOL, :], preferred_element_type=F32)
         + jnp.dot(y.astype(BF16), wo_ref[MIX_POOL:, :], preferred_element_type=F32))
    mod = mod_ref[0]
    x1 = x_ref[0] + mod[2:3] * o
    x1_ref[0] = x1
    h2 = x1 * lax.rsqrt(jnp.mean(x1 * x1, axis=-1, keepdims=True) + EPS) * (ng_ref[...] * (1.0 + mod[4:5])) + mod[3:4]
    h2_hi = h2.astype(BF16)
    h2_ref[0] = h2_hi
    h2_lo = (h2 - h2_hi.astype(F32)).astype(BF16)
    pa = jnp.dot(h2_hi, wr_ref[0], preferred_element_type=F32)
    pb = jnp.dot(h2_lo, wr_ref[1], preferred_element_type=F32)
    logits = pa + pltpu.roll(pa, LANES - N_EXPERTS, axis=1) + pb
    lane = lax.broadcasted_iota(jnp.int32, logits.shape, 1)
    lg = jnp.where(lane < N_EXPERTS, logits, -jnp.inf)
    e = jnp.exp(lg - jnp.max(lg, axis=-1, keepdims=True))
    aff = e / jnp.sum(e, axis=-1, keepdims=True)
    afft_ref[0] = aff.T[:N_EXPERTS, :]


def _out_proj(x, y_f, y_b, xs, z, y_pool, dsk, ssm_g, w_out, mod_l, mod_row, norm_ffn_g, w_router, tl):
    bn, seq_len, _ = x.shape
    nt = seq_len // tl
    const = lambda *shape: pl.BlockSpec(shape, lambda b, i: (0,) * len(shape))
    tok = lambda width: pl.BlockSpec((1, tl, width), lambda b, i: (b, i, 0))
    return pl.pallas_call(
        _out_kernel,
        grid=(bn, nt),
        in_specs=[tok(D_MODEL), tok(SSD_INNER), tok(SSD_INNER), tok(SSD_INNER), tok(SSD_INNER), tok(MIX_POOL),
                  const(1, SSD_INNER), const(1, SSD_INNER), const(D_MODEL, D_MODEL),
                  pl.BlockSpec((1, N_MOD, D_MODEL), lambda b, i: (mod_row(b), 0, 0)),
                  const(1, D_MODEL), const(2, D_MODEL, LANES)],
        out_specs=[tok(D_MODEL), tok(D_MODEL), pl.BlockSpec((1, N_EXPERTS, tl), lambda b, i: (b, 0, i))],
        out_shape=[jax.ShapeDtypeStruct((bn, seq_len, D_MODEL), F32),
                   jax.ShapeDtypeStruct((bn, seq_len, D_MODEL), BF16),
                   jax.ShapeDtypeStruct((bn, N_EXPERTS, seq_len), F32)],
        compiler_params=_params(("arbitrary", "arbitrary")),
        name="out_proj",
    )(x, y_f, y_b, xs, z, y_pool, dsk, ssm_g, w_out, mod_l, norm_ffn_g, w_router)


def _topk_kernel(afft_ref, pos_ref, post_ref, tab_ref, *, seq_len, cap, rb, tbs):
    n_rows = rb * N_EXPERTS
    a = afft_ref[...].reshape(n_rows, seq_len)

    def body(it, thr):
        cand_bits = thr | jnp.left_shift(jnp.int32(1), 30 - it)
        cnt = jnp.sum((a >= pltpu.bitcast(cand_bits, F32)).astype(jnp.int32), axis=1, keepdims=True)
        return jnp.where(cnt >= cap, cand_bits, thr)

    thr = lax.fori_loop(0, 31, body, jnp.zeros((n_rows, 1), jnp.int32))
    lo = pltpu.bitcast(thr, F32)
    hi = pltpu.bitcast(thr + 1, F32)
    gt = a >= hi
    eq = (a >= lo) & (a < hi)
    need = cap - jnp.sum(gt.astype(jnp.int32), axis=1, keepdims=True)

    r = lax.broadcasted_iota(jnp.int32, (LANES, LANES), 0)
    s = lax.broadcasted_iota(jnp.int32, (LANES, LANES), 1)
    strict_upper = (r < s).astype(BF16)
    nblk = seq_len // LANES

    def excl_cumsum(m):
        carry = jnp.zeros((n_rows, 1), F32)
        outs = []
        for blk in range(nblk):
            mb = m[:, blk * LANES:(blk + 1) * LANES]
            outs.append(jnp.dot(mb.astype(BF16), strict_upper, preferred_element_type=F32) + carry)
            carry = carry + jnp.sum(mb, axis=1, keepdims=True)
        return jnp.concatenate(outs, axis=1)

    eq_rank = excl_cumsum(jnp.where(eq, 1.0, 0.0))
    sel = gt | (eq & (eq_rank < need.astype(F32)))
    rank = excl_cumsum(jnp.where(sel, 1.0, 0.0))
    pos = jnp.where(sel, rank, -1.0)
    pos_ref[...] = pos.astype(jnp.int32).reshape(rb, N_EXPERTS, seq_len)

    lane = lax.broadcasted_iota(jnp.int32, (n_rows, LANES), 1)
    tab = jnp.full((n_rows, LANES), float(cap), F32)
    for k in range(seq_len // tbs):
        tab = jnp.where(lane == k, rank[:, k * tbs:k * tbs + 1], tab)
    tab_ref[...] = tab.astype(jnp.int32).reshape(rb, N_EXPERTS, LANES)

    pad = jnp.full((LANES - N_EXPERTS, LANES), -1.0, F32)
    for i in range(rb):
        for blk in range(nblk):
            tile = jnp.concatenate([pos[i * N_EXPERTS:(i + 1) * N_EXPERTS, blk * LANES:(blk + 1) * LANES], pad], axis=0)
            post_ref[i, blk * LANES:(blk + 1) * LANES, :] = tile.T.astype(jnp.int32)


def _topk(afft, cap, rb, tbs):
    bn, _, seq_len = afft.shape
    return pl.pallas_call(
        functools.partial(_topk_kernel, seq_len=seq_len, cap=cap, rb=rb, tbs=tbs),
        grid=(bn // rb,),
        in_specs=[pl.BlockSpec((rb, N_EXPERTS, seq_len), lambda b: (b, 0, 0))],
        out_specs=[pl.BlockSpec((rb, N_EXPERTS, seq_len), lambda b: (b, 0, 0)),
                   pl.BlockSpec((rb, seq_len, LANES), lambda b: (b, 0, 0)),
                   pl.BlockSpec((rb, N_EXPERTS, LANES), lambda b: (b, 0, 0))],
        out_shape=[jax.ShapeDtypeStruct((bn, N_EXPERTS, seq_len), jnp.int32),
                   jax.ShapeDtypeStruct((bn, seq_len, LANES), jnp.int32),
                   jax.ShapeDtypeStruct((bn, N_EXPERTS, LANES), jnp.int32)],
        compiler_params=_params(("arbitrary",)),
        name="topk",
    )(afft)


def _gather_kernel(h2_ref, pos_ref, afft_ref, xe_ref, gate_ref, *, cap):
    seq_len = h2_ref.shape[1]
    n_slots = N_EXPERTS * cap
    rank = lax.broadcasted_iota(jnp.int32, (cap, seq_len), 0)
    onehots, gates = [], []
    for e in range(N_EXPERTS):
        hit = pos_ref[0, e:e + 1, :] == rank
        onehots.append(jnp.where(hit, 1.0, 0.0).astype(BF16))
        gates.append(jnp.sum(jnp.where(hit, afft_ref[0, e:e + 1, :], 0.0), axis=1, keepdims=True))
    xe_ref[0] = jnp.dot(jnp.concatenate(onehots, axis=0), h2_ref[0], preferred_element_type=F32).astype(BF16)
    gate_ref[0] = jnp.broadcast_to(jnp.concatenate(gates, axis=0), (n_slots, LANES))


def _gather(h2, pos, afft, cap):
    bn, seq_len, _ = h2.shape
    n_slots = N_EXPERTS * cap
    return pl.pallas_call(
        functools.partial(_gather_kernel, cap=cap),
        grid=(bn,),
        in_specs=[pl.BlockSpec((1, seq_len, D_MODEL), lambda b: (b, 0, 0)),
                  pl.BlockSpec((1, N_EXPERTS, seq_len), lambda b: (b, 0, 0)),
                  pl.BlockSpec((1, N_EXPERTS, seq_len), lambda b: (b, 0, 0))],
        out_specs=[pl.BlockSpec((1, n_slots, D_MODEL), lambda b: (b, 0, 0)),
                   pl.BlockSpec((1, n_slots, LANES), lambda b: (b, 0, 0))],
        out_shape=[jax.ShapeDtypeStruct((bn, n_slots, D_MODEL), BF16),
                   jax.ShapeDtypeStruct((bn, n_slots, LANES), F32)],
        compiler_params=_params(("arbitrary",)),
        name="gather",
    )(h2, pos, afft)


def _window_starts(tab_ref, b, e0, n_e, t, cap, win, stride=1):
    starts, overflows = [], []
    for ee in range(n_e):
        base = (b * N_EXPERTS + e0 + ee) * LANES + t * stride
        p0 = tab_ref[base]
        p1 = tab_ref[base + stride]
        w0 = jnp.minimum(lax.shift_left(lax.shift_right_logical(p0, 4), 4), cap - win)
        starts.append(w0)
        overflows.append(p1 - w0 > win)
    return starts, overflows


def _gather_win_kernel(tab_ref, h2_ref, pos_ref, afft_ref, xe_ref, gate_ref, *, cap, win, n_e):
    b = pl.program_id(0)
    e0 = pl.program_id(1) * n_e
    t = pl.program_id(2)
    tbs = h2_ref.shape[1]

    @pl.when(t == 0)
    def _():
        xe_ref[...] = jnp.zeros(xe_ref.shape, BF16)
        gate_ref[...] = jnp.zeros(gate_ref.shape, F32)

    starts, overflows = _window_starts(tab_ref, b, e0, n_e, t, cap, win)

    def accumulate(row0, n, x, g):
        cur = xe_ref[0, pl.ds(row0, n), :]
        xe_ref[0, pl.ds(row0, n), :] = (cur.astype(F32) + x).astype(BF16)
        gate_ref[0, pl.ds(row0, n), :] = gate_ref[0, pl.ds(row0, n), :] + jnp.broadcast_to(g, (n, LANES))

    def onehot_and_gate(ee, shift, n):
        sub = lax.broadcasted_iota(jnp.int32, (n, tbs), 0)
        hit = (pos_ref[0, pl.ds(e0 + ee, 1), :] - shift) == sub
        gate = jnp.sum(jnp.where(hit, afft_ref[0, pl.ds(e0 + ee, 1), :], 0.0), axis=1, keepdims=True)
        return jnp.where(hit, 1.0, 0.0).astype(BF16), gate

    pieces = [onehot_and_gate(ee, jnp.where(overflows[ee], cap, starts[ee]), win) for ee in range(n_e)]
    x = jnp.dot(jnp.concatenate([p[0] for p in pieces], axis=0), h2_ref[0], preferred_element_type=F32)
    for ee in range(n_e):
        accumulate(pl.multiple_of(ee * cap + starts[ee], 16), win, x[ee * win:(ee + 1) * win], pieces[ee][1])

    for ee in range(n_e):
        @pl.when(overflows[ee])
        def _(ee=ee):
            onehot, gate = onehot_and_gate(ee, 0, cap)
            accumulate(ee * cap, cap, jnp.dot(onehot, h2_ref[0], preferred_element_type=F32), gate)


def _gather_win(tab, h2, pos, afft, cap, win, tbs):
    bn, seq_len, _ = h2.shape
    n_slots = N_EXPERTS * cap
    n_e = N_EXPERTS // 2
    grid_spec = pltpu.PrefetchScalarGridSpec(
        num_scalar_prefetch=1,
        grid=(bn, N_EXPERTS // n_e, seq_len // tbs),
        in_specs=[pl.BlockSpec((1, tbs, D_MODEL), lambda b, g, t, tab: (b, t, 0)),
                  pl.BlockSpec((1, N_EXPERTS, tbs), lambda b, g, t, tab: (b, 0, t)),
                  pl.BlockSpec((1, N_EXPERTS, tbs), lambda b, g, t, tab: (b, 0, t))],
        out_specs=[pl.BlockSpec((1, n_e * cap, D_MODEL), lambda b, g, t, tab: (b, g, 0)),
                   pl.BlockSpec((1, n_e * cap, LANES), lambda b, g, t, tab: (b, g, 0))],
    )
    return pl.pallas_call(
        functools.partial(_gather_win_kernel, cap=cap, win=win, n_e=n_e),
        grid_spec=grid_spec,
        out_shape=[jax.ShapeDtypeStruct((bn, n_slots, D_MODEL), BF16),
                   jax.ShapeDtypeStruct((bn, n_slots, LANES), F32)],
        compiler_params=_params(("arbitrary", "arbitrary", "arbitrary")),
        name="gather_win",
    )(tab, h2, pos, afft)


def _ffn_kernel(xe_ref, gate_ref, wg_ref, wu_ref, wd_ref, ye_ref, wg_s, wu_s, wd_s, *, rows):
    @pl.when(pl.program_id(1) == 0)
    def _():
        wg_s[...] = wg_ref[0, 0].astype(BF16)
        wu_s[...] = wu_ref[0, 0].astype(BF16)
        wd_s[...] = wd_ref[0, 0].astype(BF16)

    x = xe_ref[...].reshape(rows, D_MODEL)
    g = jnp.dot(x, wg_s[...], preferred_element_type=F32)
    u = jnp.dot(x, wu_s[...], preferred_element_type=F32)
    hid = (g * _sigmoid(g) * u).astype(BF16)
    y = jnp.dot(hid, wd_s[...], preferred_element_type=F32) * gate_ref[...].reshape(rows, LANES)[:, :1]
    ye_ref[...] = y.astype(BF16).reshape(ye_ref.shape)


def _ffn(xe, gate, w_gate, w_up, w_down, layer, cap, bb):
    bn, n_slots, _ = xe.shape
    rows = bb * cap
    wspec = pl.BlockSpec((1, 1, D_MODEL, D_MODEL), lambda e, j: (layer, e, 0, 0))
    return pl.pallas_call(
        functools.partial(_ffn_kernel, rows=rows),
        grid=(N_EXPERTS, bn // bb),
        in_specs=[pl.BlockSpec((bb, cap, D_MODEL), lambda e, j: (j, e, 0)),
                  pl.BlockSpec((bb, cap, LANES), lambda e, j: (j, e, 0)),
                  wspec, wspec, wspec],
        out_specs=pl.BlockSpec((bb, cap, D_MODEL), lambda e, j: (j, e, 0)),
        out_shape=jax.ShapeDtypeStruct((bn, n_slots, D_MODEL), BF16),
        scratch_shapes=[pltpu.VMEM((D_MODEL, D_MODEL), BF16)] * 3,
        compiler_params=_params(("arbitrary", "arbitrary")),
        name="ffn",
    )(xe, gate, w_gate, w_up, w_down)


def _finish(x1_ref, mod_ref, fg_ref, out_ref, acc, final):
    x2 = x1_ref[0] + mod_ref[0][5:6] * acc
    if final:
        x2 = x2 * lax.rsqrt(jnp.mean(x2 * x2, axis=-1, keepdims=True) + EPS) * fg_ref[...]
    out_ref[0] = x2


def _scatter_kernel(x1_ref, ye_ref, post_ref, mod_ref, fg_ref, out_ref, *, cap, final):
    tl = x1_ref.shape[1]
    n_slots = N_EXPERTS * cap
    pt = post_ref[0]
    slot = lax.broadcasted_iota(jnp.int32, (1, n_slots), 1)
    expert_of_slot = jnp.zeros((1, n_slots), jnp.int32)
    for e in range(1, N_EXPERTS):
        expert_of_slot = expert_of_slot + (slot >= e * cap).astype(jnp.int32)
    rank_of_slot = (slot - expert_of_slot * cap).astype(F32)
    spread = jnp.where(lax.broadcasted_iota(jnp.int32, (LANES, n_slots), 0) == expert_of_slot, 1.0, 0.0).astype(BF16)
    ranks = jnp.dot(pt.astype(F32).astype(BF16), spread, preferred_element_type=F32)
    onehot = jnp.where(ranks == rank_of_slot, 1.0, 0.0)
    acc = jnp.dot(onehot.astype(BF16), ye_ref[0], preferred_element_type=F32)
    _finish(x1_ref, mod_ref, fg_ref, out_ref, acc, final)


def _scatter(x1, ye, post, mod_l, mod_row, final_g, cap, final):
    bn, seq_len, _ = x1.shape
    assert cap <= 256, "ranks must be exactly representable in bf16"
    n_slots = N_EXPERTS * cap
    return pl.pallas_call(
        functools.partial(_scatter_kernel, cap=cap, final=final),
        grid=(bn,),
        in_specs=[pl.BlockSpec((1, seq_len, D_MODEL), lambda b: (b, 0, 0)),
                  pl.BlockSpec((1, n_slots, D_MODEL), lambda b: (b, 0, 0)),
                  pl.BlockSpec((1, seq_len, LANES), lambda b: (b, 0, 0)),
                  pl.BlockSpec((1, N_MOD, D_MODEL), lambda b: (mod_row(b), 0, 0)),
                  pl.BlockSpec((1, D_MODEL), lambda b: (0, 0))],
        out_specs=pl.BlockSpec((1, seq_len, D_MODEL), lambda b: (b, 0, 0)),
        out_shape=jax.ShapeDtypeStruct((bn, seq_len, D_MODEL), F32),
        compiler_params=_params(("arbitrary",)),
        name="scatter",
    )(x1, ye, post, mod_l, final_g)


def _scatter_win_kernel(tab_ref, x1_ref, ye_ref, post_ref, mod_ref, fg_ref, out_ref, acc_ref, *, cap, win, final):
    b = pl.program_id(0)
    t = pl.program_id(1)
    tbs = x1_ref.shape[1]
    pt = post_ref[0]
    starts, overflows = _window_starts(tab_ref, b, 0, N_EXPERTS, t, cap, win, stride=SCATTER_BLOCKS_PER_STEP)

    def onehot(e, shift, n):
        lane = lax.broadcasted_iota(jnp.int32, (tbs, n), 1)
        return jnp.where((pt[:, e:e + 1] - shift) == lane, 1.0, 0.0).astype(BF16)

    onehots = [onehot(e, jnp.where(overflows[e], cap, starts[e]), win) for e in range(N_EXPERTS)]
    rows = [ye_ref[0, pl.ds(pl.multiple_of(e * cap + starts[e], 16), win), :] for e in range(N_EXPERTS)]
    acc_ref[...] = jnp.dot(jnp.concatenate(onehots, axis=1), jnp.concatenate(rows, axis=0),
                           preferred_element_type=F32)

    for e in range(N_EXPERTS):
        @pl.when(overflows[e])
        def _(e=e):
            acc_ref[...] += jnp.dot(onehot(e, 0, cap), ye_ref[0, e * cap:(e + 1) * cap, :],
                                    preferred_element_type=F32)

    _finish(x1_ref, mod_ref, fg_ref, out_ref, acc_ref[...], final)


def _scatter_win(tab, x1, ye, post, mod_l, mod_row, final_g, cap, win, tbs, final):
    bn, seq_len, _ = x1.shape
    n_slots = N_EXPERTS * cap
    grid_spec = pltpu.PrefetchScalarGridSpec(
        num_scalar_prefetch=1,
        grid=(bn, seq_len // tbs),
        in_specs=[pl.BlockSpec((1, tbs, D_MODEL), lambda b, t, tab: (b, t, 0)),
                  pl.BlockSpec((1, n_slots, D_MODEL), lambda b, t, tab: (b, 0, 0), pipeline_mode=pl.Buffered(1)),
                  pl.BlockSpec((1, tbs, LANES), lambda b, t, tab: (b, t, 0)),
                  pl.BlockSpec((1, N_MOD, D_MODEL), lambda b, t, tab: (mod_row(b), 0, 0)),
                  pl.BlockSpec((1, D_MODEL), lambda b, t, tab: (0, 0))],
        out_specs=pl.BlockSpec((1, tbs, D_MODEL), lambda b, t, tab: (b, t, 0)),
        scratch_shapes=[pltpu.VMEM((tbs, D_MODEL), F32)],
    )
    return pl.pallas_call(
        functools.partial(_scatter_win_kernel, cap=cap, win=win, final=final),
        grid_spec=grid_spec,
        out_shape=jax.ShapeDtypeStruct((bn, seq_len, D_MODEL), F32),
        compiler_params=_params(("arbitrary", "arbitrary")),
        name="scatter_win",
    )(tab, x1, ye, post, mod_l, final_g)


def _grid_position_embedding(n_tokens, dim):
    rows = n_tokens // GRID_W
    row = jnp.repeat(jnp.arange(rows), GRID_W).astype(F32)
    col = jnp.tile(jnp.arange(GRID_W), rows).astype(F32)
    quarter = dim // 4
    inv_freq = jnp.power(POS_BASE, -jnp.arange(quarter, dtype=F32) / quarter)
    er = row[:, None] * inv_freq[None]
    ec = col[:, None] * inv_freq[None]
    return jnp.concatenate([jnp.sin(er), jnp.cos(er), jnp.sin(ec), jnp.cos(ec)], axis=-1)


def _pad_cols(w, width):
    return jnp.pad(w, ((0, 0), (0, width - w.shape[1])))


def _router_pieces(w):
    w_hi = w.astype(BF16)
    w_lo = (w - w_hi.astype(F32)).astype(BF16)
    return jnp.stack([_pad_cols(jnp.concatenate([w_hi, w_lo], axis=1), LANES), _pad_cols(w_hi, LANES)])


def _run_group(x, mod, mod_row, h0f, h0b, h0_row, lw, final_g, tl, ffn_rows_per_step, topk_rows):
    bn, seq_len, _ = x.shape
    cap = CAPACITY_FACTOR * seq_len // N_EXPERTS
    windowed = seq_len >= 1024
    tbs = max(LANES, seq_len // 16) if windowed else seq_len
    gather_win = -(-3 * cap // 16 // 16) * 16
    scatter_win = cap // 4
    states_f, states_b = [], []
    for l in range(DEPTH):
        w = lw[l]
        final = l == DEPTH - 1
        y_pool, z, xs, bc, dt, cum = _in_proj(x, mod[l], mod_row, w["norm_mix_g"], w["w_main"], w["w_dt"],
                                              w["pool_w"], w["pool_scale"], w["conv_w"], w["conv_b"],
                                              w["alog_row"], w["bias_row"], tl)
        y_f, y_b, hT_f, hT_b = _ssd(xs, bc, dt, cum, h0f[l], h0b[l], h0_row)
        states_f.append(hT_f)
        states_b.append(hT_b)
        x1, h2, afft = _out_proj(x, y_f, y_b, xs, z, y_pool, w["dsk"], w["ssm_norm_g"], w["w_out"], mod[l], mod_row,
                                 w["norm_ffn_g"], w["w_router"], tl)
        pos, post, tab = _topk(afft, cap, topk_rows, tbs)
        if windowed:
            tab = tab.reshape(-1)
            xe, gate = _gather_win(tab, h2, pos, afft, cap, gather_win, tbs)
        else:
            xe, gate = _gather(h2, pos, afft, cap)
        ye = _ffn(xe, gate, w["w_gate"], w["w_up"], w["w_down"], l, cap, ffn_rows_per_step // cap)
        if windowed:
            x = _scatter_win(tab, x1, ye, post, mod[l], mod_row, final_g, cap, scatter_win,
                             SCATTER_BLOCKS_PER_STEP * tbs, final)
        else:
            x = _scatter(x1, ye, post, mod[l], mod_row, final_g, cap, final)
    return x, states_f, states_b


def kernel(x_prompt, x_sample, state_ssm_fwd, state_ssm_bwd, c, c_ctx, norm_mix_g, w_ada, b_ada, w_in, pool_w,
           pool_scale, conv_w, conv_b, a_log_fwd, a_log_bwd, dt_bias_fwd, dt_bias_bwd, d_skip, ssm_norm_g, w_out,
           norm_ffn_g, w_router, w_gate, w_up, w_down, final_norm_g):
    n_dec = c.shape[0]
    n_ctx = x_prompt.shape[0]
    ctx_row = n_dec
    cond = jnp.concatenate([c, c_ctx[None, :], jnp.zeros((8 - n_dec - 1, D_MODEL), F32)], axis=0)
    mod = _modulation(cond, w_ada, b_ada).reshape(DEPTH, 8, N_MOD, D_MODEL)

    zeros_h = jnp.zeros((LANES - 2 * SSD_HEADS,), F32)
    lw = []
    for l in range(DEPTH):
        lw.append(dict(
            norm_mix_g=norm_mix_g[l][None], w_main=w_in[l][:, :N_MAIN].astype(BF16),
            w_dt=_pad_cols(w_in[l][:, N_MAIN:], LANES).astype(BF16), pool_w=pool_w[l].astype(BF16),
            pool_scale=pool_scale[l][None], conv_w=conv_w[l], conv_b=conv_b[l][None],
            alog_row=jnp.concatenate([a_log_fwd[l], a_log_bwd[l], zeros_h])[None],
            bias_row=jnp.concatenate([dt_bias_fwd[l], dt_bias_bwd[l], zeros_h])[None],
            dsk=jnp.repeat(d_skip[l], SSD_INNER // SSD_HEADS)[None], ssm_norm_g=ssm_norm_g[l][None],
            w_out=w_out[l].astype(BF16), norm_ffn_g=norm_ffn_g[l][None], w_router=_router_pieces(w_router[l]),
            w_gate=w_gate, w_up=w_up, w_down=w_down))
    final_g = final_norm_g[None]
    st_shape = (HEAD_PAIRS, LANES, SSD_STATE)

    zero_state = jnp.zeros((1,) + st_shape, F32)
    y_prompt, sf, sb = _run_group(x_prompt, mod, lambda b: ctx_row, [zero_state] * DEPTH, [zero_state] * DEPTH,
                                  lambda b: 0, lw, final_g, tl=x_prompt.shape[1], ffn_rows_per_step=1024,
                                  topk_rows=8)
    out_state_shape = (n_ctx, SSD_HEADS, SSD_INNER // SSD_HEADS, SSD_STATE)
    new_f = jnp.stack([s.reshape(out_state_shape) for s in sf], axis=1)
    new_b = jnp.stack([s.reshape(out_state_shape) for s in sb], axis=1)

    xs0 = x_sample + _grid_position_embedding(x_sample.shape[1], D_MODEL)[None]
    h0f = [state_ssm_fwd[:, l].reshape((n_dec,) + st_shape) for l in range(DEPTH)]
    h0b = [state_ssm_bwd[:, l].reshape((n_dec,) + st_shape) for l in range(DEPTH)]
    y_sample, _, _ = _run_group(xs0, mod, lambda b: b, h0f, h0b, lambda b: b, lw, final_g, tl=512,
                                ffn_rows_per_step=1024, topk_rows=n_dec)
    return (y_prompt, y_sample, new_f, new_b)
```

```python
import functools

import jax
import jax.numpy as jnp
from jax import lax
from jax.experimental import pallas as pl
from jax.experimental.pallas import tpu as pltpu

D_MODEL = 1024
DEPTH = 2
GRID_W = 64
MIX_POOL = 512
POOL_WINDOWS = (2, 4, 8, 16)
POOL_GROUP_CH = 128
SSD_INNER = 512
SSD_HEADS = 8
SSD_HEAD_DIM = SSD_INNER // SSD_HEADS
SSD_STATE = 128
CONV_K = 5
CHUNK = 128
CONV_DIM = 1024
N_EXPERTS = 16
CAPACITY_FACTOR = 2
N_MOD = 6
POS_BASE = 10000.0
EPS = 1e-6

HALO = 8
LANES = 128
N_MAIN = MIX_POOL + SSD_INNER + CONV_DIM
HEAD_PAIRS = SSD_HEADS // 2
SSD_MAX_CHUNKS_PER_STEP = 4
VMEM_LIMIT = 56 * 1024 * 1024

F32 = jnp.float32
BF16 = jnp.bfloat16
HI = lax.Precision.HIGHEST


def _sigmoid(x):
    return 1.0 / (1.0 + jnp.exp(-x))


def _params(sem):
    return pltpu.CompilerParams(dimension_semantics=sem, vmem_limit_bytes=VMEM_LIMIT)


def _mod_kernel(cond_ref, w_ref, b_ref, out_ref):
    c = cond_ref[...]
    s = c * _sigmoid(c)
    out_ref[0] = jnp.dot(s, w_ref[0], precision=HI, preferred_element_type=F32) + b_ref[0]


def _modulation(cond, w_ada, b_ada):
    n_rows = cond.shape[0]
    n_out = N_MOD * D_MODEL
    tn = 1536
    return pl.pallas_call(
        _mod_kernel,
        grid=(DEPTH, n_out // tn),
        in_specs=[
            pl.BlockSpec((n_rows, D_MODEL), lambda l, j: (0, 0)),
            pl.BlockSpec((1, D_MODEL, tn), lambda l, j: (l, 0, j)),
            pl.BlockSpec((1, 1, tn), lambda l, j: (l, 0, j)),
        ],
        out_specs=pl.BlockSpec((1, n_rows, tn), lambda l, j: (l, 0, j)),
        out_shape=jax.ShapeDtypeStruct((DEPTH, n_rows, n_out), F32),
        compiler_params=_params(("arbitrary", "arbitrary")),
        name="modulation",
    )(cond, w_ada, b_ada.reshape(DEPTH, 1, n_out))


def _cumsum_pieces(dta):
    d_hi = dta.astype(BF16)
    rest = dta - d_hi.astype(F32)
    d_mid = rest.astype(BF16)
    return d_hi, d_mid, (rest - d_mid.astype(F32)).astype(BF16)


def _grid_position_rows(table_ref, grid_row, col0, n_cols):
    row_part = jnp.broadcast_to(table_ref[pl.ds(grid_row, 1), :], (n_cols, D_MODEL // 2))
    return jnp.concatenate([row_part, table_ref[col0:col0 + n_cols, :]], axis=1)


def _in_kernel(*refs, tl, seq_len, with_pos):
    (xp_ref, xc_ref, xn_ref, mod_ref, g_ref, wmain_ref, wdt_ref, poolw_ref, pscale_ref, cw_ref, cb_ref,
     alog_ref, bias_ref) = refs[:13]
    table_ref = refs[13] if with_pos else None
    ypool_ref, z_ref, xs_ref, bc_ref, dt_ref, cum_ref = refs[13 + with_pos:19 + with_pos]
    x0_ref = refs[19 + with_pos] if with_pos else None
    proj_scr = refs[-1]
    i = pl.program_id(1)
    n = pl.num_programs(1)
    rows = tl + 2 * HALO
    x_ext = jnp.concatenate([xp_ref[0], xc_ref[0], xn_ref[0]], axis=0)
    if with_pos:
        r0 = i * (tl // GRID_W)
        last_row = seq_len // GRID_W - 1
        pieces = [_grid_position_rows(table_ref, jnp.maximum(r0 - 1, 0), GRID_W - HALO, HALO)]
        pieces += [_grid_position_rows(table_ref, r0 + k, 0, GRID_W) for k in range(tl // GRID_W)]
        pieces += [_grid_position_rows(table_ref, jnp.minimum(r0 + tl // GRID_W, last_row), 0, HALO)]
        x_ext = x_ext + jnp.concatenate(pieces, axis=0)
        x0_ref[0] = x_ext[HALO:HALO + tl]
    mod = mod_ref[0]
    ms = jnp.mean(x_ext * x_ext, axis=-1, keepdims=True)
    h = x_ext * lax.rsqrt(ms + EPS) * (g_ref[...] * (1.0 + mod[1:2])) + mod[0:1]
    row = lax.broadcasted_iota(jnp.int32, (rows, 1), 0)
    lo_ok = jnp.where(i > 0, 0, HALO)
    hi_ok = jnp.where(i < n - 1, rows, tl + HALO)
    hb = jnp.where((row >= lo_ok) & (row < hi_ok), h, 0.0).astype(BF16)
    proj_scr[...] = jnp.dot(hb, wmain_ref[...], preferred_element_type=F32)
    dt_full = jnp.dot(hb, wdt_ref[...], preferred_element_type=F32)

    x_in = dt_full[HALO:HALO + tl] + bias_ref[...]
    dt = jnp.maximum(x_in, 0.0) + jnp.log1p(jnp.exp(-jnp.abs(x_in)))
    dt_ref[0] = dt
    r = lax.broadcasted_iota(jnp.int32, (CHUNK, CHUNK), 0)
    s = lax.broadcasted_iota(jnp.int32, (CHUNK, CHUNK), 1)
    tri = jnp.concatenate([jnp.where(r >= s, 1.0, 0.0), jnp.where(r <= s, 1.0, 0.0)], axis=1).astype(BF16)
    head = lax.broadcasted_iota(jnp.int32, (tl, LANES), 1)
    dta = jnp.where(head < 2 * SSD_HEADS, dt * -jnp.exp(alog_ref[...]), 0.0)
    pieces = _cumsum_pieces(dta)
    fwd_cols = s < SSD_HEADS
    for k in range(tl // CHUNK):
        acc = jnp.zeros((CHUNK, LANES), F32)
        for piece in pieces:
            pk = piece[k * CHUNK:(k + 1) * CHUNK]
            zero = jnp.zeros_like(pk)
            stacked = jnp.concatenate([jnp.where(fwd_cols, pk, zero), jnp.where(fwd_cols, zero, pk)], axis=0)
            acc = acc + jnp.dot(tri, stacked, preferred_element_type=F32)
        cum_ref[0, k * CHUNK:(k + 1) * CHUNK, :] = acc

    def ahead(v, k):
        return pltpu.roll(v, (-k) % rows, axis=0)

    t_glob = i * tl + lax.broadcasted_iota(jnp.int32, (tl, 1), 0)
    outs = []
    for gi, w in enumerate(POOL_WINDOWS):
        c0 = gi * POOL_GROUP_CH
        u = proj_scr[:, c0:c0 + POOL_GROUP_CH]
        acc = u
        span = 1
        while 2 * span < w:
            acc = acc + ahead(acc, span)
            span *= 2
        acc = acc + ahead(acc, -span)
        lo = jnp.maximum(t_glob - w // 2, 0)
        hi = jnp.minimum(t_glob - w // 2 + w, seq_len)
        cnt = (hi - lo).astype(F32)
        p = acc[HALO:HALO + tl] / cnt - u[HALO:HALO + tl]
        outs.append(jnp.dot(p.astype(BF16), poolw_ref[gi], preferred_element_type=F32))
    ypool_ref[0] = (jnp.concatenate(outs, axis=1) * pscale_ref[...]).astype(BF16)

    c0 = MIX_POOL + SSD_INNER
    acc = cb_ref[...] + cw_ref[0:1, :] * proj_scr[HALO - 2:HALO - 2 + tl, c0:c0 + CONV_DIM]
    for k in range(1, CONV_K):
        acc = acc + cw_ref[k:k + 1, :] * proj_scr[HALO - 2 + k:HALO - 2 + k + tl, c0:c0 + CONV_DIM]
    act = acc * _sigmoid(acc)
    xs_ref[0] = act[:, :SSD_INNER].astype(BF16)
    bc_ref[0] = act[:, SSD_INNER:].astype(BF16)
    z_ref[0] = proj_scr[HALO:HALO + tl, MIX_POOL:MIX_POOL + SSD_INNER]


def _in_proj(x, mod_l, mod_row, norm_g, w_main, w_dt, pool_w, pool_scale, conv_w, conv_b, alog_row, bias_row, tl,
             pos_table=None):
    bn, seq_len, _ = x.shape
    nt = seq_len // tl
    hb = tl // HALO
    last_halo = seq_len // HALO - 1
    with_pos = pos_table is not None
    const = lambda *shape: pl.BlockSpec(shape, lambda b, i: (0,) * len(shape))
    tok = lambda width: pl.BlockSpec((1, tl, width), lambda b, i: (b, i, 0))
    in_specs = [
        pl.BlockSpec((1, HALO, D_MODEL), lambda b, i: (b, jnp.maximum(i * hb - 1, 0), 0)),
        tok(D_MODEL),
        pl.BlockSpec((1, HALO, D_MODEL), lambda b, i: (b, jnp.minimum((i + 1) * hb, last_halo), 0)),
        pl.BlockSpec((1, N_MOD, D_MODEL), lambda b, i: (mod_row(b), 0, 0)),
        const(1, D_MODEL),
        const(D_MODEL, N_MAIN),
        const(D_MODEL, LANES),
        const(len(POOL_WINDOWS), POOL_GROUP_CH, POOL_GROUP_CH),
        const(1, MIX_POOL),
        const(CONV_K, CONV_DIM),
        const(1, CONV_DIM),
        const(1, LANES),
        const(1, LANES),
    ]
    out_specs = [tok(MIX_POOL), tok(SSD_INNER), tok(SSD_INNER), tok(CONV_DIM - SSD_INNER), tok(LANES), tok(LANES)]
    out_shape = [
        jax.ShapeDtypeStruct((bn, seq_len, MIX_POOL), BF16),
        jax.ShapeDtypeStruct((bn, seq_len, SSD_INNER), F32),
        jax.ShapeDtypeStruct((bn, seq_len, SSD_INNER), BF16),
        jax.ShapeDtypeStruct((bn, seq_len, CONV_DIM - SSD_INNER), BF16),
        jax.ShapeDtypeStruct((bn, seq_len, LANES), F32),
        jax.ShapeDtypeStruct((bn, seq_len, LANES), F32),
    ]
    args = [x, x, x, mod_l, norm_g, w_main, w_dt, pool_w, pool_scale, conv_w, conv_b, alog_row, bias_row]
    if with_pos:
        assert tl % GRID_W == 0 and pos_table.shape == (GRID_W, D_MODEL // 2)
        in_specs.append(const(GRID_W, D_MODEL // 2))
        out_specs.append(tok(D_MODEL))
        out_shape.append(jax.ShapeDtypeStruct((bn, seq_len, D_MODEL), F32))
        args.append(pos_table)
    return pl.pallas_call(
        functools.partial(_in_kernel, tl=tl, seq_len=seq_len, with_pos=with_pos),
        grid=(bn, nt),
        in_specs=in_specs,
        out_specs=out_specs,
        out_shape=out_shape,
        scratch_shapes=[pltpu.VMEM((tl + 2 * HALO, N_MAIN), F32)],
        compiler_params=_params(("arbitrary", "arbitrary")),
        name="in_proj",
    )(*args)


def _ssd_chunk(xsf_ref, bcf_ref, dtf_ref, cumf_ref, xsb_ref, bcb_ref, dtb_ref, cumb_ref, yf_ref, yb_ref, st_ref, rf, rb):
    r = lax.broadcasted_iota(jnp.int32, (CHUNK, CHUNK), 0)
    s = lax.broadcasted_iota(jnp.int32, (CHUNK, CHUNK), 1)
    lane2 = lax.broadcasted_iota(jnp.int32, (CHUNK, 2 * LANES), 1)
    lane_head = lax.shift_right_logical(lane2, SSD_HEAD_DIM.bit_length() - 1)
    row2 = lax.broadcasted_iota(jnp.int32, (2 * LANES, SSD_STATE), 0)
    is_fwd = s < SSD_HEADS
    dt = jnp.where(is_fwd, dtf_ref[0, rf:rf + CHUNK, :], dtb_ref[0, rb:rb + CHUNK, :])
    cum = jnp.where(is_fwd, cumf_ref[0, rf:rf + CHUNK, :], cumb_ref[0, rb:rb + CHUNK, :])
    tot = jnp.where(lax.broadcasted_iota(jnp.int32, (1, LANES), 1) < SSD_HEADS,
                    cum[CHUNK - 1:CHUNK, :], cum[0:1, :])
    cdec = jnp.exp(tot)
    cum_t = cum.T
    dt_t = dt.T
    ecum_t = jnp.exp(cum_t)
    w_t = (dt * jnp.exp(tot - cum)).T

    def head_rows(arr_t, h0):
        out = arr_t[h0 + 3:h0 + 4, :]
        for k in (2, 1, 0):
            out = jnp.where(row2 < (k + 1) * SSD_HEAD_DIM, arr_t[h0 + k:h0 + k + 1, :], out)
        return out

    def one_dir(d, xs_ref, bc_ref, y_ref, r0):
        mask = (r >= s) if d == 0 else (r <= s)
        off = d * SSD_HEADS
        bcv = bc_ref[0, r0:r0 + CHUNK, :]
        b01 = bcv[:, :2 * SSD_STATE].astype(BF16)
        c01 = bcv[:, 2 * SSD_STATE:].astype(BF16)
        zero2 = jnp.zeros_like(b01)
        b_blk = jnp.concatenate([jnp.where(lane2 < SSD_STATE, b01, zero2),
                                 jnp.where(lane2 < SSD_STATE, zero2, b01)], axis=0)
        cb01 = lax.dot_general(c01, b_blk, (((1,), (1,)), ((), ())), preferred_element_type=F32)
        for g in range(2):
            h0 = off + 4 * g
            b_g = b01[:, g * SSD_STATE:(g + 1) * SSD_STATE]
            c_g = c01[:, g * SSD_STATE:(g + 1) * SSD_STATE]
            cb = cb01[:, g * CHUNK:(g + 1) * CHUNK]
            xs_g = xs_ref[0, r0:r0 + CHUNK, g * 2 * LANES:(g + 1) * 2 * LANES]
            x_b = xs_g.astype(BF16)
            ms, x_parts = [], []
            for k in range(4):
                hh = h0 + k
                seg = cum[:, hh:hh + 1] - cum_t[hh:hh + 1, :]
                dec = jnp.exp(jnp.where(mask, seg, -jnp.inf))
                ms.append((cb * dec * dt_t[hh:hh + 1, :]).astype(BF16))
                x_parts.append(jnp.where(lane_head == k, x_b, jnp.zeros_like(x_b)))
            y_diag = jnp.dot(jnp.concatenate(ms, axis=1), jnp.concatenate(x_parts, axis=0),
                             preferred_element_type=F32)
            st = st_ref[d, 2 * g:2 * g + 2].reshape(2 * LANES, SSD_STATE)
            y_off_t = lax.dot_general(st.astype(BF16), c_g, (((1,), (1,)), ((), ())),
                                      preferred_element_type=F32) * head_rows(ecum_t, h0)
            y_ref[0, r0:r0 + CHUNK, g * 2 * LANES:(g + 1) * 2 * LANES] = (y_diag + y_off_t.T).astype(BF16)
            x_d = (xs_g.astype(F32).T * head_rows(w_t, h0)).astype(BF16)
            cs = jnp.dot(x_d, b_g, preferred_element_type=F32)
            dcol = cdec[:, h0 + 3:h0 + 4]
            for k in (2, 1, 0):
                dcol = jnp.where(row2 < (k + 1) * SSD_HEAD_DIM, cdec[:, h0 + k:h0 + k + 1], dcol)
            st_ref[d, 2 * g:2 * g + 2] = (st * dcol + cs).reshape(2, LANES, SSD_STATE)

    one_dir(0, xsf_ref, bcf_ref, yf_ref, rf)
    one_dir(1, xsb_ref, bcb_ref, yb_ref, rb)


def _ssd_kernel(xsf_ref, bcf_ref, dtf_ref, cumf_ref, xsb_ref, bcb_ref, dtb_ref, cumb_ref, h0f_ref, h0b_ref,
                yf_ref, yb_ref, hTf_ref, hTb_ref, st_ref):
    c = pl.program_id(1)
    nc = pl.num_programs(1)

    @pl.when(c == 0)
    def _():
        st_ref[0] = h0f_ref[0]
        st_ref[1] = h0b_ref[0]

    n_chunks = xsf_ref.shape[1] // CHUNK
    for k in range(n_chunks):
        _ssd_chunk(xsf_ref, bcf_ref, dtf_ref, cumf_ref, xsb_ref, bcb_ref, dtb_ref, cumb_ref, yf_ref, yb_ref, st_ref,
                   k * CHUNK, (n_chunks - 1 - k) * CHUNK)

    @pl.when(c == nc - 1)
    def _():
        hTf_ref[0] = st_ref[0]
        hTb_ref[0] = st_ref[1]


def _ssd(xs, bc, dt, cum, h0f, h0b, h0_row):
    bn, seq_len, _ = xs.shape
    step_rows = min(SSD_MAX_CHUNKS_PER_STEP * CHUNK, seq_len)
    nc = seq_len // step_rows
    fwd = lambda width: pl.BlockSpec((1, step_rows, width), lambda b, c: (b, c, 0))
    bwd = lambda width: pl.BlockSpec((1, step_rows, width), lambda b, c: (b, nc - 1 - c, 0))
    st_spec_in = pl.BlockSpec((1, HEAD_PAIRS, LANES, SSD_STATE), lambda b, c: (h0_row(b), 0, 0, 0))
    st_spec_out = pl.BlockSpec((1, HEAD_PAIRS, LANES, SSD_STATE), lambda b, c: (b, 0, 0, 0))
    st_shape = jax.ShapeDtypeStruct((bn, HEAD_PAIRS, LANES, SSD_STATE), F32)
    return pl.pallas_call(
        _ssd_kernel,
        grid=(bn, nc),
        in_specs=[fwd(SSD_INNER), fwd(CONV_DIM - SSD_INNER), fwd(LANES), fwd(LANES),
                  bwd(SSD_INNER), bwd(CONV_DIM - SSD_INNER), bwd(LANES), bwd(LANES),
                  st_spec_in, st_spec_in],
        out_specs=[fwd(SSD_INNER), bwd(SSD_INNER), st_spec_out, st_spec_out],
        out_shape=[jax.ShapeDtypeStruct((bn, seq_len, SSD_INNER), BF16),
                   jax.ShapeDtypeStruct((bn, seq_len, SSD_INNER), BF16), st_shape, st_shape],
        scratch_shapes=[pltpu.VMEM((2, HEAD_PAIRS, LANES, SSD_STATE), F32)],
        compiler_params=_params(("arbitrary", "arbitrary")),
        name="ssd",
    )(xs, bc, dt, cum, xs, bc, dt, cum, h0f, h0b)


def _out_kernel(x_ref, yf_ref, yb_ref, xs_ref, z_ref, yp_ref, dsk_ref, sg_ref, wo_ref, mod_ref, ng_ref, wr_ref,
                x1_ref, h2_ref, afft_ref):
    y = yf_ref[0].astype(F32) + yb_ref[0].astype(F32) + dsk_ref[...] * xs_ref[0].astype(F32)
    z = z_ref[0]
    y = y * (z * _sigmoid(z))
    y = y * lax.rsqrt(jnp.mean(y * y, axis=-1, keepdims=True) + EPS) * sg_ref[...]
    o = (jnp.dot(yp_ref[0], wo_ref[:MIX_POOL, :], preferred_element_type=F32)
         + jnp.dot(y.astype(BF16), wo_ref[MIX_POOL:, :], preferred_element_type=F32))
    mod = mod_ref[0]
    x1 = x_ref[0] + mod[2:3] * o
    x1_ref[0] = x1
    h2 = x1 * lax.rsqrt(jnp.mean(x1 * x1, axis=-1, keepdims=True) + EPS) * (ng_ref[...] * (1.0 + mod[4:5])) + mod[3:4]
    h2_hi = h2.astype(BF16)
    h2_ref[0] = h2_hi
    h2_lo = (h2 - h2_hi.astype(F32)).astype(BF16)
    pa = jnp.dot(h2_hi, wr_ref[0], preferred_element_type=F32)
    pb = jnp.dot(h2_lo, wr_ref[1], preferred_element_type=F32)
    logits = pa + pltpu.roll(pa, LANES - N_EXPERTS, axis=1) + pb
    lane = lax.broadcasted_iota(jnp.int32, logits.shape, 1)
    lg = jnp.where(lane < N_EXPERTS, logits, -jnp.inf)
    e = jnp.exp(lg - jnp.max(lg, axis=-1, keepdims=True))
    aff = e / jnp.sum(e, axis=-1, keepdims=True)
    afft_ref[0] = aff.T[:N_EXPERTS, :]


def _out_proj(x, y_f, y_b, xs, z, y_pool, dsk, ssm_g, w_out, mod_l, mod_row, norm_ffn_g, w_router, tl):
    bn, seq_len, _ = x.shape
    nt = seq_len // tl
    const = lambda *shape: pl.BlockSpec(shape, lambda b, i: (0,) * len(shape))
    tok = lambda width: pl.BlockSpec((1, tl, width), lambda b, i: (b, i, 0))
    return pl.pallas_call(
        _out_kernel,
        grid=(bn, nt),
        in_specs=[tok(D_MODEL), tok(SSD_INNER), tok(SSD_INNER), tok(SSD_INNER), tok(SSD_INNER), tok(MIX_POOL),
                  const(1, SSD_INNER), const(1, SSD_INNER), const(D_MODEL, D_MODEL),
                  pl.BlockSpec((1, N_MOD, D_MODEL), lambda b, i: (mod_row(b), 0, 0)),
                  const(1, D_MODEL), const(2, D_MODEL, LANES)],
        out_specs=[tok(D_MODEL), tok(D_MODEL), pl.BlockSpec((1, N_EXPERTS, tl), lambda b, i: (b, 0, i))],
        out_shape=[jax.ShapeDtypeStruct((bn, seq_len, D_MODEL), F32),
                   jax.ShapeDtypeStruct((bn, seq_len, D_MODEL), BF16),
                   jax.ShapeDtypeStruct((bn, N_EXPERTS, seq_len), F32)],
        compiler_params=_params(("arbitrary", "arbitrary")),
        name="out_proj",
    )(x, y_f, y_b, xs, z, y_pool, dsk, ssm_g, w_out, mod_l, norm_ffn_g, w_router)


def _topk_kernel(afft_ref, pos_ref, post_ref, tab_ref, *, seq_len, cap, rb, tbs):
    n_rows = rb * N_EXPERTS
    a = afft_ref[...].reshape(n_rows, seq_len)

    def body(it, thr):
        cand_bits = thr | jnp.left_shift(jnp.int32(1), 30 - it)
        cnt = jnp.sum((a >= pltpu.bitcast(cand_bits, F32)).astype(jnp.int32), axis=1, keepdims=True)
        return jnp.where(cnt >= cap, cand_bits, thr)

    thr = lax.fori_loop(0, 31, body, jnp.zeros((n_rows, 1), jnp.int32))
    lo = pltpu.bitcast(thr, F32)
    hi = pltpu.bitcast(thr + 1, F32)
    gt = a >= hi
    eq = (a >= lo) & (a < hi)
    need = cap - jnp.sum(gt.astype(jnp.int32), axis=1, keepdims=True)

    r = lax.broadcasted_iota(jnp.int32, (LANES, LANES), 0)
    s = lax.broadcasted_iota(jnp.int32, (LANES, LANES), 1)
    strict_upper = (r < s).astype(BF16)
    nblk = seq_len // LANES

    def excl_cumsum(m):
        carry = jnp.zeros((n_rows, 1), F32)
        outs = []
        for blk in range(nblk):
            mb = m[:, blk * LANES:(blk + 1) * LANES]
            outs.append(jnp.dot(mb.astype(BF16), strict_upper, preferred_element_type=F32) + carry)
            carry = carry + jnp.sum(mb, axis=1, keepdims=True)
        return jnp.concatenate(outs, axis=1)

    eq_rank = excl_cumsum(jnp.where(eq, 1.0, 0.0))
    sel = gt | (eq & (eq_rank < need.astype(F32)))
    rank = excl_cumsum(jnp.where(sel, 1.0, 0.0))
    pos = jnp.where(sel, rank, -1.0)
    pos_ref[...] = pos.astype(jnp.int32).reshape(rb, N_EXPERTS, seq_len)

    lane = lax.broadcasted_iota(jnp.int32, (n_rows, LANES), 1)
    tab = jnp.full((n_rows, LANES), float(cap), F32)
    for k in range(seq_len // tbs):
        tab = jnp.where(lane == k, rank[:, k * tbs:k * tbs + 1], tab)
    tab_ref[...] = tab.astype(jnp.int32).reshape(rb, N_EXPERTS, LANES)

    pad = jnp.full((LANES - N_EXPERTS, LANES), -1.0, F32)
    for i in range(rb):
        for blk in range(nblk):
            tile = jnp.concatenate([pos[i * N_EXPERTS:(i + 1) * N_EXPERTS, blk * LANES:(blk + 1) * LANES], pad], axis=0)
            post_ref[i, blk * LANES:(blk + 1) * LANES, :] = tile.T.astype(jnp.int32)


def _topk(afft, cap, rb, tbs):
    bn, _, seq_len = afft.shape
    return pl.pallas_call(
        functools.partial(_topk_kernel, seq_len=seq_len, cap=cap, rb=rb, tbs=tbs),
        grid=(bn // rb,),
        in_specs=[pl.BlockSpec((rb, N_EXPERTS, seq_len), lambda b: (b, 0, 0))],
        out_specs=[pl.BlockSpec((rb, N_EXPERTS, seq_len), lambda b: (b, 0, 0)),
                   pl.BlockSpec((rb, seq_len, LANES), lambda b: (b, 0, 0)),
                   pl.BlockSpec((rb, N_EXPERTS, LANES), lambda b: (b, 0, 0))],
        out_shape=[jax.ShapeDtypeStruct((bn, N_EXPERTS, seq_len), jnp.int32),
                   jax.ShapeDtypeStruct((bn, seq_len, LANES), jnp.int32),
                   jax.ShapeDtypeStruct((bn, N_EXPERTS, LANES), jnp.int32)],
        compiler_params=_params(("arbitrary",)),
        name="topk",
    )(afft)


def _gather_kernel(h2_ref, pos_ref, afft_ref, xe_ref, gate_ref, *, cap):
    n_rows, seq_len, _ = h2_ref.shape
    n_slots = N_EXPERTS * cap
    rank = lax.broadcasted_iota(jnp.int32, (cap, seq_len), 0)
    for i in range(n_rows):
        onehots, gates = [], []
        for e in range(N_EXPERTS):
            hit = pos_ref[i, e:e + 1, :] == rank
            onehots.append(jnp.where(hit, 1.0, 0.0).astype(BF16))
            gates.append(jnp.sum(jnp.where(hit, afft_ref[i, e:e + 1, :], 0.0), axis=1, keepdims=True))
        xe_ref[i] = jnp.dot(jnp.concatenate(onehots, axis=0), h2_ref[i], preferred_element_type=F32).astype(BF16)
        gate_ref[i] = jnp.broadcast_to(jnp.concatenate(gates, axis=0), (n_slots, LANES))


def _gather(h2, pos, afft, cap, rows_per_step):
    bn, seq_len, _ = h2.shape
    n_slots = N_EXPERTS * cap
    rows = lambda *tail: pl.BlockSpec((rows_per_step,) + tail, lambda b: (b, 0, 0))
    return pl.pallas_call(
        functools.partial(_gather_kernel, cap=cap),
        grid=(bn // rows_per_step,),
        in_specs=[rows(seq_len, D_MODEL), rows(N_EXPERTS, seq_len), rows(N_EXPERTS, seq_len)],
        out_specs=[rows(n_slots, D_MODEL), rows(n_slots, LANES)],
        out_shape=[jax.ShapeDtypeStruct((bn, n_slots, D_MODEL), BF16),
                   jax.ShapeDtypeStruct((bn, n_slots, LANES), F32)],
        compiler_params=_params(("arbitrary",)),
        name="gather",
    )(h2, pos, afft)


def _window_starts(tab_ref, b, e0, n_e, t, cap, win):
    starts, overflows = [], []
    for ee in range(n_e):
        base = (b * N_EXPERTS + e0 + ee) * LANES + t
        p0 = tab_ref[base]
        p1 = tab_ref[base + 1]
        w0 = jnp.minimum(lax.shift_left(lax.shift_right_logical(p0, 4), 4), cap - win)
        starts.append(w0)
        overflows.append(p1 - w0 > win)
    return starts, overflows


def _gather_win_kernel(tab_ref, h2_ref, pos_ref, afft_ref, xe_ref, gate_ref, *, cap, win, n_e):
    b = pl.program_id(0)
    e0 = pl.program_id(1) * n_e
    t = pl.program_id(2)
    tbs = h2_ref.shape[1]

    @pl.when(t == 0)
    def _():
        xe_ref[...] = jnp.zeros(xe_ref.shape, BF16)
        gate_ref[...] = jnp.zeros(gate_ref.shape, F32)

    starts, overflows = _window_starts(tab_ref, b, e0, n_e, t, cap, win)

    def accumulate(row0, n, x, g):
        cur = xe_ref[0, pl.ds(row0, n), :]
        xe_ref[0, pl.ds(row0, n), :] = (cur.astype(F32) + x).astype(BF16)
        gate_ref[0, pl.ds(row0, n), :] = gate_ref[0, pl.ds(row0, n), :] + jnp.broadcast_to(g, (n, LANES))

    def onehot_and_gate(ee, shift, n):
        sub = lax.broadcasted_iota(jnp.int32, (n, tbs), 0)
        hit = (pos_ref[0, pl.ds(e0 + ee, 1), :] - shift) == sub
        gate = jnp.sum(jnp.where(hit, afft_ref[0, pl.ds(e0 + ee, 1), :], 0.0), axis=1, keepdims=True)
        return jnp.where(hit, 1.0, 0.0).astype(BF16), gate

    pieces = [onehot_and_gate(ee, jnp.where(overflows[ee], cap, starts[ee]), win) for ee in range(n_e)]
    x = jnp.dot(jnp.concatenate([p[0] for p in pieces], axis=0), h2_ref[0], preferred_element_type=F32)
    for ee in range(n_e):
        accumulate(pl.multiple_of(ee * cap + starts[ee], 16), win, x[ee * win:(ee + 1) * win], pieces[ee][1])

    for ee in range(n_e):
        @pl.when(overflows[ee])
        def _(ee=ee):
            onehot, gate = onehot_and_gate(ee, 0, cap)
            accumulate(ee * cap, cap, jnp.dot(onehot, h2_ref[0], preferred_element_type=F32), gate)


def _gather_win(tab, h2, pos, afft, cap, win, tbs):
    bn, seq_len, _ = h2.shape
    n_slots = N_EXPERTS * cap
    n_e = N_EXPERTS // 2
    grid_spec = pltpu.PrefetchScalarGridSpec(
        num_scalar_prefetch=1,
        grid=(bn, N_EXPERTS // n_e, seq_len // tbs),
        in_specs=[pl.BlockSpec((1, tbs, D_MODEL), lambda b, g, t, tab: (b, t, 0)),
                  pl.BlockSpec((1, N_EXPERTS, tbs), lambda b, g, t, tab: (b, 0, t)),
                  pl.BlockSpec((1, N_EXPERTS, tbs), lambda b, g, t, tab: (b, 0, t))],
        out_specs=[pl.BlockSpec((1, n_e * cap, D_MODEL), lambda b, g, t, tab: (b, g, 0)),
                   pl.BlockSpec((1, n_e * cap, LANES), lambda b, g, t, tab: (b, g, 0))],
    )
    return pl.pallas_call(
        functools.partial(_gather_win_kernel, cap=cap, win=win, n_e=n_e),
        grid_spec=grid_spec,
        out_shape=[jax.ShapeDtypeStruct((bn, n_slots, D_MODEL), BF16),
                   jax.ShapeDtypeStruct((bn, n_slots, LANES), F32)],
        compiler_params=_params(("arbitrary", "arbitrary", "arbitrary")),
        name="gather_win",
    )(tab, h2, pos, afft)


def _ffn_kernel(xe_ref, gate_ref, wg_ref, wu_ref, wd_ref, ye_ref, wg_s, wu_s, wd_s, *, rows):
    @pl.when(pl.program_id(1) == 0)
    def _():
        wg_s[...] = wg_ref[0, 0].astype(BF16)
        wu_s[...] = wu_ref[0, 0].astype(BF16)
        wd_s[...] = wd_ref[0, 0].astype(BF16)

    x = xe_ref[...].reshape(rows, D_MODEL)
    g = jnp.dot(x, wg_s[...], preferred_element_type=F32)
    u = jnp.dot(x, wu_s[...], preferred_element_type=F32)
    hid = (g * _sigmoid(g) * u).astype(BF16)
    y = jnp.dot(hid, wd_s[...], preferred_element_type=F32) * gate_ref[...].reshape(rows, LANES)[:, :1]
    ye_ref[...] = y.astype(BF16).reshape(ye_ref.shape)


def _ffn(xe, gate, w_gate, w_up, w_down, layer, cap, bb):
    bn, n_slots, _ = xe.shape
    rows = bb * cap
    wspec = pl.BlockSpec((1, 1, D_MODEL, D_MODEL), lambda e, j: (layer, e, 0, 0))
    return pl.pallas_call(
        functools.partial(_ffn_kernel, rows=rows),
        grid=(N_EXPERTS, bn // bb),
        in_specs=[pl.BlockSpec((bb, cap, D_MODEL), lambda e, j: (j, e, 0)),
                  pl.BlockSpec((bb, cap, LANES), lambda e, j: (j, e, 0)),
                  wspec, wspec, wspec],
        out_specs=pl.BlockSpec((bb, cap, D_MODEL), lambda e, j: (j, e, 0)),
        out_shape=jax.ShapeDtypeStruct((bn, n_slots, D_MODEL), BF16),
        scratch_shapes=[pltpu.VMEM((D_MODEL, D_MODEL), BF16)] * 3,
        compiler_params=_params(("arbitrary", "arbitrary")),
        name="ffn",
    )(xe, gate, w_gate, w_up, w_down)


def _finish(x1_ref, mod_ref, fg_ref, out_ref, acc, final, i=0):
    x2 = x1_ref[i] + mod_ref[0][5:6] * acc
    if final:
        x2 = x2 * lax.rsqrt(jnp.mean(x2 * x2, axis=-1, keepdims=True) + EPS) * fg_ref[...]
    out_ref[i] = x2


def _scatter_kernel(x1_ref, ye_ref, post_ref, mod_ref, fg_ref, out_ref, *, cap, final):
    n_rows = x1_ref.shape[0]
    n_slots = N_EXPERTS * cap
    slot = lax.broadcasted_iota(jnp.int32, (1, n_slots), 1)
    expert_of_slot = jnp.zeros((1, n_slots), jnp.int32)
    for e in range(1, N_EXPERTS):
        expert_of_slot = expert_of_slot + (slot >= e * cap).astype(jnp.int32)
    rank_of_slot = (slot - expert_of_slot * cap).astype(F32)
    spread = jnp.where(lax.broadcasted_iota(jnp.int32, (LANES, n_slots), 0) == expert_of_slot, 1.0, 0.0).astype(BF16)
    for i in range(n_rows):
        pt = post_ref[i]
        ranks = jnp.dot(pt.astype(F32).astype(BF16), spread, preferred_element_type=F32)
        onehot = jnp.where(ranks == rank_of_slot, 1.0, 0.0).astype(BF16)
        acc = jnp.dot(onehot, ye_ref[i], preferred_element_type=F32)
        _finish(x1_ref, mod_ref, fg_ref, out_ref, acc, final, i)


def _scatter(x1, ye, post, mod_l, mod_row, final_g, cap, final, rows_per_step):
    bn, seq_len, _ = x1.shape
    assert cap <= 256, "ranks must be exactly representable in bf16"
    n_slots = N_EXPERTS * cap
    rows = lambda *tail: pl.BlockSpec((rows_per_step,) + tail, lambda b: (b, 0, 0))
    return pl.pallas_call(
        functools.partial(_scatter_kernel, cap=cap, final=final),
        grid=(bn // rows_per_step,),
        in_specs=[rows(seq_len, D_MODEL), rows(n_slots, D_MODEL), rows(seq_len, LANES),
                  pl.BlockSpec((1, N_MOD, D_MODEL), lambda b: (mod_row(b * rows_per_step), 0, 0)),
                  pl.BlockSpec((1, D_MODEL), lambda b: (0, 0))],
        out_specs=rows(seq_len, D_MODEL),
        out_shape=jax.ShapeDtypeStruct((bn, seq_len, D_MODEL), F32),
        compiler_params=_params(("arbitrary",)),
        name="scatter",
    )(x1, ye, post, mod_l, final_g)


def _scatter_win_kernel(tab_ref, x1_ref, ye_ref, post_ref, mod_ref, fg_ref, out_ref, acc_ref, *, cap, win, final):
    b = pl.program_id(0)
    t = pl.program_id(1)
    tbs = x1_ref.shape[1]
    pt = post_ref[0]
    starts, overflows = _window_starts(tab_ref, b, 0, N_EXPERTS, t, cap, win)

    def onehot(e, shift, n):
        lane = lax.broadcasted_iota(jnp.int32, (tbs, n), 1)
        return jnp.where((pt[:, e:e + 1] - shift) == lane, 1.0, 0.0).astype(BF16)

    onehots = [onehot(e, jnp.where(overflows[e], cap, starts[e]), win) for e in range(N_EXPERTS)]
    rows = [ye_ref[0, pl.ds(pl.multiple_of(e * cap + starts[e], 16), win), :] for e in range(N_EXPERTS)]
    acc_ref[...] = jnp.dot(jnp.concatenate(onehots, axis=1), jnp.concatenate(rows, axis=0),
                           preferred_element_type=F32)

    for e in range(N_EXPERTS):
        @pl.when(overflows[e])
        def _(e=e):
            acc_ref[...] += jnp.dot(onehot(e, 0, cap), ye_ref[0, e * cap:(e + 1) * cap, :],
                                    preferred_element_type=F32)

    _finish(x1_ref, mod_ref, fg_ref, out_ref, acc_ref[...], final)


def _scatter_win(tab, x1, ye, post, mod_l, mod_row, final_g, cap, win, tbs, final):
    bn, seq_len, _ = x1.shape
    n_slots = N_EXPERTS * cap
    grid_spec = pltpu.PrefetchScalarGridSpec(
        num_scalar_prefetch=1,
        grid=(bn, seq_len // tbs),
        in_specs=[pl.BlockSpec((1, tbs, D_MODEL), lambda b, t, tab: (b, t, 0)),
                  pl.BlockSpec((1, n_slots, D_MODEL), lambda b, t, tab: (b, 0, 0), pipeline_mode=pl.Buffered(1)),
                  pl.BlockSpec((1, tbs, LANES), lambda b, t, tab: (b, t, 0)),
                  pl.BlockSpec((1, N_MOD, D_MODEL), lambda b, t, tab: (mod_row(b), 0, 0)),
                  pl.BlockSpec((1, D_MODEL), lambda b, t, tab: (0, 0))],
        out_specs=pl.BlockSpec((1, tbs, D_MODEL), lambda b, t, tab: (b, t, 0)),
        scratch_shapes=[pltpu.VMEM((tbs, D_MODEL), F32)],
    )
    return pl.pallas_call(
        functools.partial(_scatter_win_kernel, cap=cap, win=win, final=final),
        grid_spec=grid_spec,
        out_shape=jax.ShapeDtypeStruct((bn, seq_len, D_MODEL), F32),
        compiler_params=_params(("arbitrary", "arbitrary")),
        name="scatter_win",
    )(tab, x1, ye, post, mod_l, final_g)


def _grid_position_table(n_tokens, dim):
    assert n_tokens // GRID_W <= GRID_W
    quarter = dim // 4
    inv_freq = jnp.power(POS_BASE, -jnp.arange(quarter, dtype=F32) / quarter)
    angle = jnp.arange(GRID_W).astype(F32)[:, None] * inv_freq[None]
    return jnp.concatenate([jnp.sin(angle), jnp.cos(angle)], axis=-1)


def _pad_cols(w, width):
    return jnp.pad(w, ((0, 0), (0, width - w.shape[1])))


def _router_pieces(w):
    w_hi = w.astype(BF16)
    w_lo = (w - w_hi.astype(F32)).astype(BF16)
    return jnp.stack([_pad_cols(jnp.concatenate([w_hi, w_lo], axis=1), LANES), _pad_cols(w_hi, LANES)])


def _run_group(x, mod, mod_row, h0f, h0b, h0_row, lw, final_g, tl, ffn_rows_per_step, topk_rows, dense_rows=1,
               pos_table=None):
    bn, seq_len, _ = x.shape
    cap = CAPACITY_FACTOR * seq_len // N_EXPERTS
    windowed = seq_len >= 1024
    tbs = max(LANES, seq_len // 8) if windowed else seq_len
    win = cap // 4
    states_f, states_b = [], []
    for l in range(DEPTH):
        w = lw[l]
        final = l == DEPTH - 1
        outs = _in_proj(x, mod[l], mod_row, w["norm_mix_g"], w["w_main"], w["w_dt"], w["pool_w"], w["pool_scale"],
                        w["conv_w"], w["conv_b"], w["alog_row"], w["bias_row"], tl,
                        pos_table if l == 0 else None)
        y_pool, z, xs, bc, dt, cum = outs[:6]
        if l == 0 and pos_table is not None:
            x = outs[6]
        y_f, y_b, hT_f, hT_b = _ssd(xs, bc, dt, cum, h0f[l], h0b[l], h0_row)
        states_f.append(hT_f)
        states_b.append(hT_b)
        x1, h2, afft = _out_proj(x, y_f, y_b, xs, z, y_pool, w["dsk"], w["ssm_norm_g"], w["w_out"], mod[l], mod_row,
                                 w["norm_ffn_g"], w["w_router"], tl)
        pos, post, tab = _topk(afft, cap, topk_rows, tbs)
        if windowed:
            tab = tab.reshape(-1)
            xe, gate = _gather_win(tab, h2, pos, afft, cap, win, tbs)
        else:
            xe, gate = _gather(h2, pos, afft, cap, dense_rows)
        ye = _ffn(xe, gate, w["w_gate"], w["w_up"], w["w_down"], l, cap, ffn_rows_per_step // cap)
        if windowed:
            x = _scatter_win(tab, x1, ye, post, mod[l], mod_row, final_g, cap, win, tbs, final)
        else:
            x = _scatter(x1, ye, post, mod[l], mod_row, final_g, cap, final, dense_rows)
    return x, states_f, states_b


def kernel(x_prompt, x_sample, state_ssm_fwd, state_ssm_bwd, c, c_ctx, norm_mix_g, w_ada, b_ada, w_in, pool_w,
           pool_scale, conv_w, conv_b, a_log_fwd, a_log_bwd, dt_bias_fwd, dt_bias_bwd, d_skip, ssm_norm_g, w_out,
           norm_ffn_g, w_router, w_gate, w_up, w_down, final_norm_g):
    n_dec = c.shape[0]
    n_ctx = x_prompt.shape[0]
    ctx_row = n_dec
    cond = jnp.concatenate([c, c_ctx[None, :], jnp.zeros((8 - n_dec - 1, D_MODEL), F32)], axis=0)
    mod = _modulation(cond, w_ada, b_ada).reshape(DEPTH, 8, N_MOD, D_MODEL)

    zeros_h = jnp.zeros((LANES - 2 * SSD_HEADS,), F32)
    lw = []
    for l in range(DEPTH):
        lw.append(dict(
            norm_mix_g=norm_mix_g[l][None], w_main=w_in[l][:, :N_MAIN].astype(BF16),
            w_dt=_pad_cols(w_in[l][:, N_MAIN:], LANES).astype(BF16), pool_w=pool_w[l].astype(BF16),
            pool_scale=pool_scale[l][None], conv_w=conv_w[l], conv_b=conv_b[l][None],
            alog_row=jnp.concatenate([a_log_fwd[l], a_log_bwd[l], zeros_h])[None],
            bias_row=jnp.concatenate([dt_bias_fwd[l], dt_bias_bwd[l], zeros_h])[None],
            dsk=jnp.repeat(d_skip[l], SSD_INNER // SSD_HEADS)[None], ssm_norm_g=ssm_norm_g[l][None],
            w_out=w_out[l].astype(BF16), norm_ffn_g=norm_ffn_g[l][None], w_router=_router_pieces(w_router[l]),
            w_gate=w_gate, w_up=w_up, w_down=w_down))
    final_g = final_norm_g[None]
    st_shape = (HEAD_PAIRS, LANES, SSD_STATE)

    zero_state = jnp.zeros((1,) + st_shape, F32)
    y_prompt, sf, sb = _run_group(x_prompt, mod, lambda b: ctx_row, [zero_state] * DEPTH, [zero_state] * DEPTH,
                                  lambda b: 0, lw, final_g, tl=x_prompt.shape[1], ffn_rows_per_step=1024,
                                  topk_rows=8, dense_rows=4)
    out_state_shape = (n_ctx, SSD_HEADS, SSD_INNER // SSD_HEADS, SSD_STATE)
    new_f = jnp.stack([s.reshape(out_state_shape) for s in sf], axis=1)
    new_b = jnp.stack([s.reshape(out_state_shape) for s in sb], axis=1)

    pos_table = _grid_position_table(x_sample.shape[1], D_MODEL)
    h0f = [state_ssm_fwd[:, l].reshape((n_dec,) + st_shape) for l in range(DEPTH)]
    h0b = [state_ssm_bwd[:, l].reshape((n_dec,) + st_shape) for l in range(DEPTH)]
    y_sample, _, _ = _run_group(x_sample, mod, lambda b: b, h0f, h0b, lambda b: b, lw, final_g, tl=512,
                                ffn_rows_per_step=1024, topk_rows=n_dec, pos_table=pos_table)
    return (y_prompt, y_sample, new_f, new_b)
```

```python
import functools

import jax
import jax.numpy as jnp
from jax import lax
from jax.experimental import pallas as pl
from jax.experimental.pallas import tpu as pltpu

D_MODEL = 1024
DEPTH = 2
GRID_W = 64
MIX_POOL = 512
POOL_WINDOWS = (2, 4, 8, 16)
POOL_GROUP_CH = 128
SSD_INNER = 512
SSD_HEADS = 8
SSD_HEAD_DIM = SSD_INNER // SSD_HEADS
SSD_STATE = 128
CONV_K = 5
CHUNK = 128
CONV_DIM = 1024
N_EXPERTS = 16
CAPACITY_FACTOR = 2
N_MOD = 6
POS_BASE = 10000.0
EPS = 1e-6

HALO = 8
LANES = 128
N_MAIN = MIX_POOL + SSD_INNER + CONV_DIM
HEAD_PAIRS = SSD_HEADS // 2
SSD_MAX_CHUNKS_PER_STEP = 4
VMEM_LIMIT = 56 * 1024 * 1024

F32 = jnp.float32
BF16 = jnp.bfloat16
HI = lax.Precision.HIGHEST


def _sigmoid(x):
    return 1.0 / (1.0 + jnp.exp(-x))


def _params(sem):
    return pltpu.CompilerParams(dimension_semantics=sem, vmem_limit_bytes=VMEM_LIMIT)


def _mod_kernel(cond_ref, w_ref, b_ref, out_ref):
    c = cond_ref[...]
    s = c * _sigmoid(c)
    out_ref[0] = jnp.dot(s, w_ref[0], precision=HI, preferred_element_type=F32) + b_ref[0]


def _modulation(cond, w_ada, b_ada):
    n_rows = cond.shape[0]
    n_out = N_MOD * D_MODEL
    tn = 1536
    return pl.pallas_call(
        _mod_kernel,
        grid=(DEPTH, n_out // tn),
        in_specs=[
            pl.BlockSpec((n_rows, D_MODEL), lambda l, j: (0, 0)),
            pl.BlockSpec((1, D_MODEL, tn), lambda l, j: (l, 0, j)),
            pl.BlockSpec((1, 1, tn), lambda l, j: (l, 0, j)),
        ],
        out_specs=pl.BlockSpec((1, n_rows, tn), lambda l, j: (l, 0, j)),
        out_shape=jax.ShapeDtypeStruct((DEPTH, n_rows, n_out), F32),
        compiler_params=_params(("arbitrary", "arbitrary")),
        name="modulation",
    )(cond, w_ada, b_ada.reshape(DEPTH, 1, n_out))


def _cumsum_pieces(dta):
    d_hi = dta.astype(BF16)
    rest = dta - d_hi.astype(F32)
    d_mid = rest.astype(BF16)
    return d_hi, d_mid, (rest - d_mid.astype(F32)).astype(BF16)


def _grid_position_rows(table_ref, grid_row, col0, n_cols):
    row_part = jnp.broadcast_to(table_ref[pl.ds(grid_row, 1), :], (n_cols, D_MODEL // 2))
    return jnp.concatenate([row_part, table_ref[col0:col0 + n_cols, :]], axis=1)


def _in_kernel(*refs, tl, seq_len, with_pos):
    (xp_ref, xc_ref, xn_ref, mod_ref, g_ref, wmain_ref, wdt_ref, poolw_ref, pscale_ref, cw_ref, cb_ref,
     alog_ref, bias_ref) = refs[:13]
    table_ref = refs[13] if with_pos else None
    ypool_ref, z_ref, xs_ref, bc_ref, dt_ref, cum_ref = refs[13 + with_pos:19 + with_pos]
    x0_ref = refs[19 + with_pos] if with_pos else None
    proj_scr = refs[-1]
    i = pl.program_id(1)
    n = pl.num_programs(1)
    rows = tl + 2 * HALO
    x_ext = jnp.concatenate([xp_ref[0], xc_ref[0], xn_ref[0]], axis=0)
    if with_pos:
        r0 = i * (tl // GRID_W)
        last_row = seq_len // GRID_W - 1
        pieces = [_grid_position_rows(table_ref, jnp.maximum(r0 - 1, 0), GRID_W - HALO, HALO)]
        pieces += [_grid_position_rows(table_ref, r0 + k, 0, GRID_W) for k in range(tl // GRID_W)]
        pieces += [_grid_position_rows(table_ref, jnp.minimum(r0 + tl // GRID_W, last_row), 0, HALO)]
        x_ext = x_ext + jnp.concatenate(pieces, axis=0)
        x0_ref[0] = x_ext[HALO:HALO + tl]
    mod = mod_ref[0]
    ms = jnp.mean(x_ext * x_ext, axis=-1, keepdims=True)
    h = x_ext * lax.rsqrt(ms + EPS) * (g_ref[...] * (1.0 + mod[1:2])) + mod[0:1]
    row = lax.broadcasted_iota(jnp.int32, (rows, 1), 0)
    lo_ok = jnp.where(i > 0, 0, HALO)
    hi_ok = jnp.where(i < n - 1, rows, tl + HALO)
    hb = jnp.where((row >= lo_ok) & (row < hi_ok), h, 0.0).astype(BF16)
    proj_scr[...] = jnp.dot(hb, wmain_ref[...], preferred_element_type=F32)
    dt_full = jnp.dot(hb, wdt_ref[...], preferred_element_type=F32)

    x_in = dt_full[HALO:HALO + tl] + bias_ref[...]
    dt = jnp.maximum(x_in, 0.0) + jnp.log1p(jnp.exp(-jnp.abs(x_in)))
    dt_ref[0] = dt
    r = lax.broadcasted_iota(jnp.int32, (CHUNK, CHUNK), 0)
    s = lax.broadcasted_iota(jnp.int32, (CHUNK, CHUNK), 1)
    tri = jnp.concatenate([jnp.where(r >= s, 1.0, 0.0), jnp.where(r <= s, 1.0, 0.0)], axis=1).astype(BF16)
    head = lax.broadcasted_iota(jnp.int32, (tl, LANES), 1)
    dta = jnp.where(head < 2 * SSD_HEADS, dt * -jnp.exp(alog_ref[...]), 0.0)
    pieces = _cumsum_pieces(dta)
    fwd_cols = s < SSD_HEADS
    for k in range(tl // CHUNK):
        acc = jnp.zeros((CHUNK, LANES), F32)
        for piece in pieces:
            pk = piece[k * CHUNK:(k + 1) * CHUNK]
            zero = jnp.zeros_like(pk)
            stacked = jnp.concatenate([jnp.where(fwd_cols, pk, zero), jnp.where(fwd_cols, zero, pk)], axis=0)
            acc = acc + jnp.dot(tri, stacked, preferred_element_type=F32)
        cum_ref[0, k * CHUNK:(k + 1) * CHUNK, :] = acc

    def ahead(v, k):
        return pltpu.roll(v, (-k) % rows, axis=0)

    t_glob = i * tl + lax.broadcasted_iota(jnp.int32, (tl, 1), 0)
    outs = []
    for gi, w in enumerate(POOL_WINDOWS):
        c0 = gi * POOL_GROUP_CH
        u = proj_scr[:, c0:c0 + POOL_GROUP_CH]
        acc = u
        span = 1
        while 2 * span < w:
            acc = acc + ahead(acc, span)
            span *= 2
        acc = acc + ahead(acc, -span)
        lo = jnp.maximum(t_glob - w // 2, 0)
        hi = jnp.minimum(t_glob - w // 2 + w, seq_len)
        cnt = (hi - lo).astype(F32)
        p = acc[HALO:HALO + tl] / cnt - u[HALO:HALO + tl]
        outs.append(jnp.dot(p.astype(BF16), poolw_ref[gi], preferred_element_type=F32))
    ypool_ref[0] = (jnp.concatenate(outs, axis=1) * pscale_ref[...]).astype(BF16)

    c0 = MIX_POOL + SSD_INNER
    acc = cb_ref[...] + cw_ref[0:1, :] * proj_scr[HALO - 2:HALO - 2 + tl, c0:c0 + CONV_DIM]
    for k in range(1, CONV_K):
        acc = acc + cw_ref[k:k + 1, :] * proj_scr[HALO - 2 + k:HALO - 2 + k + tl, c0:c0 + CONV_DIM]
    act = acc * _sigmoid(acc)
    xs_ref[0] = act[:, :SSD_INNER].astype(BF16)
    bc_ref[0] = act[:, SSD_INNER:].astype(BF16)
    z_ref[0] = proj_scr[HALO:HALO + tl, MIX_POOL:MIX_POOL + SSD_INNER]


def _in_proj(x, mod_l, mod_row, norm_g, w_main, w_dt, pool_w, pool_scale, conv_w, conv_b, alog_row, bias_row, tl,
             pos_table=None):
    bn, seq_len, _ = x.shape
    nt = seq_len // tl
    hb = tl // HALO
    last_halo = seq_len // HALO - 1
    with_pos = pos_table is not None
    const = lambda *shape: pl.BlockSpec(shape, lambda b, i: (0,) * len(shape))
    tok = lambda width: pl.BlockSpec((1, tl, width), lambda b, i: (b, i, 0))
    in_specs = [
        pl.BlockSpec((1, HALO, D_MODEL), lambda b, i: (b, jnp.maximum(i * hb - 1, 0), 0)),
        tok(D_MODEL),
        pl.BlockSpec((1, HALO, D_MODEL), lambda b, i: (b, jnp.minimum((i + 1) * hb, last_halo), 0)),
        pl.BlockSpec((1, N_MOD, D_MODEL), lambda b, i: (mod_row(b), 0, 0)),
        const(1, D_MODEL),
        const(D_MODEL, N_MAIN),
        const(D_MODEL, LANES),
        const(len(POOL_WINDOWS), POOL_GROUP_CH, POOL_GROUP_CH),
        const(1, MIX_POOL),
        const(CONV_K, CONV_DIM),
        const(1, CONV_DIM),
        const(1, LANES),
        const(1, LANES),
    ]
    out_specs = [tok(MIX_POOL), tok(SSD_INNER), tok(SSD_INNER), tok(CONV_DIM - SSD_INNER), tok(LANES), tok(LANES)]
    out_shape = [
        jax.ShapeDtypeStruct((bn, seq_len, MIX_POOL), BF16),
        jax.ShapeDtypeStruct((bn, seq_len, SSD_INNER), F32),
        jax.ShapeDtypeStruct((bn, seq_len, SSD_INNER), BF16),
        jax.ShapeDtypeStruct((bn, seq_len, CONV_DIM - SSD_INNER), BF16),
        jax.ShapeDtypeStruct((bn, seq_len, LANES), F32),
        jax.ShapeDtypeStruct((bn, seq_len, LANES), F32),
    ]
    args = [x, x, x, mod_l, norm_g, w_main, w_dt, pool_w, pool_scale, conv_w, conv_b, alog_row, bias_row]
    if with_pos:
        assert tl % GRID_W == 0 and pos_table.shape == (GRID_W, D_MODEL // 2)
        in_specs.append(const(GRID_W, D_MODEL // 2))
        out_specs.append(tok(D_MODEL))
        out_shape.append(jax.ShapeDtypeStruct((bn, seq_len, D_MODEL), F32))
        args.append(pos_table)
    return pl.pallas_call(
        functools.partial(_in_kernel, tl=tl, seq_len=seq_len, with_pos=with_pos),
        grid=(bn, nt),
        in_specs=in_specs,
        out_specs=out_specs,
        out_shape=out_shape,
        scratch_shapes=[pltpu.VMEM((tl + 2 * HALO, N_MAIN), F32)],
        compiler_params=_params(("arbitrary", "arbitrary")),
        name="in_proj",
    )(*args)


def _ssd_chunk(xsf_ref, bcf_ref, dtf_ref, cumf_ref, xsb_ref, bcb_ref, dtb_ref, cumb_ref, yf_ref, yb_ref, st_ref, rf, rb):
    r = lax.broadcasted_iota(jnp.int32, (CHUNK, CHUNK), 0)
    s = lax.broadcasted_iota(jnp.int32, (CHUNK, CHUNK), 1)
    lane2 = lax.broadcasted_iota(jnp.int32, (CHUNK, 2 * LANES), 1)
    lane_head = lax.shift_right_logical(lane2, SSD_HEAD_DIM.bit_length() - 1)
    row2 = lax.broadcasted_iota(jnp.int32, (2 * LANES, SSD_STATE), 0)
    is_fwd = s < SSD_HEADS
    dt = jnp.where(is_fwd, dtf_ref[0, rf:rf + CHUNK, :], dtb_ref[0, rb:rb + CHUNK, :])
    cum = jnp.where(is_fwd, cumf_ref[0, rf:rf + CHUNK, :], cumb_ref[0, rb:rb + CHUNK, :])
    tot = jnp.where(lax.broadcasted_iota(jnp.int32, (1, LANES), 1) < SSD_HEADS,
                    cum[CHUNK - 1:CHUNK, :], cum[0:1, :])
    cdec = jnp.exp(tot)
    cum_t = cum.T
    dt_t = dt.T
    ecum_t = jnp.exp(cum_t)
    w_t = (dt * jnp.exp(tot - cum)).T

    def head_rows(arr_t, h0):
        out = arr_t[h0 + 3:h0 + 4, :]
        for k in (2, 1, 0):
            out = jnp.where(row2 < (k + 1) * SSD_HEAD_DIM, arr_t[h0 + k:h0 + k + 1, :], out)
        return out

    def one_dir(d, xs_ref, bc_ref, y_ref, r0):
        mask = (r >= s) if d == 0 else (r <= s)
        off = d * SSD_HEADS
        bcv = bc_ref[0, r0:r0 + CHUNK, :]
        b01 = bcv[:, :2 * SSD_STATE].astype(BF16)
        c01 = bcv[:, 2 * SSD_STATE:].astype(BF16)
        zero2 = jnp.zeros_like(b01)
        b_blk = jnp.concatenate([jnp.where(lane2 < SSD_STATE, b01, zero2),
                                 jnp.where(lane2 < SSD_STATE, zero2, b01)], axis=0)
        cb01 = lax.dot_general(c01, b_blk, (((1,), (1,)), ((), ())), preferred_element_type=F32)
        for g in range(2):
            h0 = off + 4 * g
            b_g = b01[:, g * SSD_STATE:(g + 1) * SSD_STATE]
            c_g = c01[:, g * SSD_STATE:(g + 1) * SSD_STATE]
            cb = cb01[:, g * CHUNK:(g + 1) * CHUNK]
            xs_g = xs_ref[0, r0:r0 + CHUNK, g * 2 * LANES:(g + 1) * 2 * LANES]
            x_b = xs_g.astype(BF16)
            ms, x_parts = [], []
            for k in range(4):
                hh = h0 + k
                seg = cum[:, hh:hh + 1] - cum_t[hh:hh + 1, :]
                dec = jnp.exp(jnp.where(mask, seg, -jnp.inf))
                ms.append((cb * dec * dt_t[hh:hh + 1, :]).astype(BF16))
                x_parts.append(jnp.where(lane_head == k, x_b, jnp.zeros_like(x_b)))
            y_diag = jnp.dot(jnp.concatenate(ms, axis=1), jnp.concatenate(x_parts, axis=0),
                             preferred_element_type=F32)
            st = st_ref[d, 2 * g:2 * g + 2].reshape(2 * LANES, SSD_STATE)
            y_off_t = lax.dot_general(st.astype(BF16), c_g, (((1,), (1,)), ((), ())),
                                      preferred_element_type=F32) * head_rows(ecum_t, h0)
            y_ref[0, r0:r0 + CHUNK, g * 2 * LANES:(g + 1) * 2 * LANES] = (y_diag + y_off_t.T).astype(BF16)
            x_d = (xs_g.astype(F32).T * head_rows(w_t, h0)).astype(BF16)
            cs = jnp.dot(x_d, b_g, preferred_element_type=F32)
            dcol = cdec[:, h0 + 3:h0 + 4]
            for k in (2, 1, 0):
                dcol = jnp.where(row2 < (k + 1) * SSD_HEAD_DIM, cdec[:, h0 + k:h0 + k + 1], dcol)
            st_ref[d, 2 * g:2 * g + 2] = (st * dcol + cs).reshape(2, LANES, SSD_STATE)

    one_dir(0, xsf_ref, bcf_ref, yf_ref, rf)
    one_dir(1, xsb_ref, bcb_ref, yb_ref, rb)


def _ssd_kernel(xsf_ref, bcf_ref, dtf_ref, cumf_ref, xsb_ref, bcb_ref, dtb_ref, cumb_ref, h0f_ref, h0b_ref,
                yf_ref, yb_ref, hTf_ref, hTb_ref, st_ref):
    c = pl.program_id(1)
    nc = pl.num_programs(1)

    @pl.when(c == 0)
    def _():
        st_ref[0] = h0f_ref[0]
        st_ref[1] = h0b_ref[0]

    n_chunks = xsf_ref.shape[1] // CHUNK
    for k in range(n_chunks):
        _ssd_chunk(xsf_ref, bcf_ref, dtf_ref, cumf_ref, xsb_ref, bcb_ref, dtb_ref, cumb_ref, yf_ref, yb_ref, st_ref,
                   k * CHUNK, (n_chunks - 1 - k) * CHUNK)

    @pl.when(c == nc - 1)
    def _():
        hTf_ref[0] = st_ref[0]
        hTb_ref[0] = st_ref[1]


def _ssd(xs, bc, dt, cum, h0f, h0b, h0_row):
    bn, seq_len, _ = xs.shape
    step_rows = min(SSD_MAX_CHUNKS_PER_STEP * CHUNK, seq_len)
    nc = seq_len // step_rows
    fwd = lambda width: pl.BlockSpec((1, step_rows, width), lambda b, c: (b, c, 0))
    bwd = lambda width: pl.BlockSpec((1, step_rows, width), lambda b, c: (b, nc - 1 - c, 0))
    st_spec_in = pl.BlockSpec((1, HEAD_PAIRS, LANES, SSD_STATE), lambda b, c: (h0_row(b), 0, 0, 0))
    st_spec_out = pl.BlockSpec((1, HEAD_PAIRS, LANES, SSD_STATE), lambda b, c: (b, 0, 0, 0))
    st_shape = jax.ShapeDtypeStruct((bn, HEAD_PAIRS, LANES, SSD_STATE), F32)
    return pl.pallas_call(
        _ssd_kernel,
        grid=(bn, nc),
        in_specs=[fwd(SSD_INNER), fwd(CONV_DIM - SSD_INNER), fwd(LANES), fwd(LANES),
                  bwd(SSD_INNER), bwd(CONV_DIM - SSD_INNER), bwd(LANES), bwd(LANES),
                  st_spec_in, st_spec_in],
        out_specs=[fwd(SSD_INNER), bwd(SSD_INNER), st_spec_out, st_spec_out],
        out_shape=[jax.ShapeDtypeStruct((bn, seq_len, SSD_INNER), BF16),
                   jax.ShapeDtypeStruct((bn, seq_len, SSD_INNER), BF16), st_shape, st_shape],
        scratch_shapes=[pltpu.VMEM((2, HEAD_PAIRS, LANES, SSD_STATE), F32)],
        compiler_params=_params(("arbitrary", "arbitrary")),
        name="ssd",
    )(xs, bc, dt, cum, xs, bc, dt, cum, h0f, h0b)


def _out_kernel(x_ref, yf_ref, yb_ref, xs_ref, z_ref, yp_ref, dsk_ref, sg_ref, wo_ref, mod_ref, ng_ref, wr_ref,
                x1_ref, h2_ref, afft_ref):
    y = yf_ref[0].astype(F32) + yb_ref[0].astype(F32) + dsk_ref[...] * xs_ref[0].astype(F32)
    z = z_ref[0]
    y = y * (z * _sigmoid(z))
    y = y * lax.rsqrt(jnp.mean(y * y, axis=-1, keepdims=True) + EPS) * sg_ref[...]
    o = (jnp.dot(yp_ref[0], wo_ref[:MIX_POOL, :], preferred_element_type=F32)
         + jnp.dot(y.astype(BF16), wo_ref[MIX_POOL:, :], preferred_element_type=F32))
    mod = mod_ref[0]
    x1 = x_ref[0] + mod[2:3] * o
    x1_ref[0] = x1
    h2 = x1 * lax.rsqrt(jnp.mean(x1 * x1, axis=-1, keepdims=True) + EPS) * (ng_ref[...] * (1.0 + mod[4:5])) + mod[3:4]
    h2_hi = h2.astype(BF16)
    h2_ref[0] = h2_hi
    h2_lo = (h2 - h2_hi.astype(F32)).astype(BF16)
    pa = jnp.dot(h2_hi, wr_ref[0], preferred_element_type=F32)
    pb = jnp.dot(h2_lo, wr_ref[1], preferred_element_type=F32)
    logits = pa + pltpu.roll(pa, LANES - N_EXPERTS, axis=1) + pb
    lane = lax.broadcasted_iota(jnp.int32, logits.shape, 1)
    lg = jnp.where(lane < N_EXPERTS, logits, -jnp.inf)
    e = jnp.exp(lg - jnp.max(lg, axis=-1, keepdims=True))
    aff = e / jnp.sum(e, axis=-1, keepdims=True)
    afft_ref[0] = aff.T[:N_EXPERTS, :]


def _out_proj(x, y_f, y_b, xs, z, y_pool, dsk, ssm_g, w_out, mod_l, mod_row, norm_ffn_g, w_router, tl):
    bn, seq_len, _ = x.shape
    nt = seq_len // tl
    const = lambda *shape: pl.BlockSpec(shape, lambda b, i: (0,) * len(shape))
    tok = lambda width: pl.BlockSpec((1, tl, width), lambda b, i: (b, i, 0))
    return pl.pallas_call(
        _out_kernel,
        grid=(bn, nt),
        in_specs=[tok(D_MODEL), tok(SSD_INNER), tok(SSD_INNER), tok(SSD_INNER), tok(SSD_INNER), tok(MIX_POOL),
                  const(1, SSD_INNER), const(1, SSD_INNER), const(D_MODEL, D_MODEL),
                  pl.BlockSpec((1, N_MOD, D_MODEL), lambda b, i: (mod_row(b), 0, 0)),
                  const(1, D_MODEL), const(2, D_MODEL, LANES)],
        out_specs=[tok(D_MODEL), tok(D_MODEL), pl.BlockSpec((1, N_EXPERTS, tl), lambda b, i: (b, 0, i))],
        out_shape=[jax.ShapeDtypeStruct((bn, seq_len, D_MODEL), F32),
                   jax.ShapeDtypeStruct((bn, seq_len, D_MODEL), BF16),
                   jax.ShapeDtypeStruct((bn, N_EXPERTS, seq_len), F32)],
        compiler_params=_params(("arbitrary", "arbitrary")),
        name="out_proj",
    )(x, y_f, y_b, xs, z, y_pool, dsk, ssm_g, w_out, mod_l, norm_ffn_g, w_router)


def _topk_kernel(afft_ref, pos_ref, post_ref, tab_ref, *, seq_len, cap, rb, tbs):
    n_rows = rb * N_EXPERTS
    a = afft_ref[...].reshape(n_rows, seq_len)

    def body(it, thr):
        cand_bits = thr | jnp.left_shift(jnp.int32(1), 30 - it)
        cnt = jnp.sum((a >= pltpu.bitcast(cand_bits, F32)).astype(jnp.int32), axis=1, keepdims=True)
        return jnp.where(cnt >= cap, cand_bits, thr)

    thr = lax.fori_loop(0, 31, body, jnp.zeros((n_rows, 1), jnp.int32))
    lo = pltpu.bitcast(thr, F32)
    hi = pltpu.bitcast(thr + 1, F32)
    gt = a >= hi
    eq = (a >= lo) & (a < hi)
    need = cap - jnp.sum(gt.astype(jnp.int32), axis=1, keepdims=True)

    r = lax.broadcasted_iota(jnp.int32, (LANES, LANES), 0)
    s = lax.broadcasted_iota(jnp.int32, (LANES, LANES), 1)
    strict_upper = (r < s).astype(BF16)
    nblk = seq_len // LANES

    def excl_cumsum(m):
        carry = jnp.zeros((n_rows, 1), F32)
        outs = []
        for blk in range(nblk):
            mb = m[:, blk * LANES:(blk + 1) * LANES]
            outs.append(jnp.dot(mb.astype(BF16), strict_upper, preferred_element_type=F32) + carry)
            carry = carry + jnp.sum(mb, axis=1, keepdims=True)
        return jnp.concatenate(outs, axis=1)

    eq_rank = excl_cumsum(jnp.where(eq, 1.0, 0.0))
    sel = gt | (eq & (eq_rank < need.astype(F32)))
    rank = excl_cumsum(jnp.where(sel, 1.0, 0.0))
    pos = jnp.where(sel, rank, -1.0)
    pos_ref[...] = pos.astype(jnp.int32).reshape(rb, N_EXPERTS, seq_len)

    lane = lax.broadcasted_iota(jnp.int32, (n_rows, LANES), 1)
    tab = jnp.full((n_rows, LANES), float(cap), F32)
    for k in range(seq_len // tbs):
        tab = jnp.where(lane == k, rank[:, k * tbs:k * tbs + 1], tab)
    tab_ref[...] = tab.astype(jnp.int32).reshape(rb, N_EXPERTS, LANES)

    pad = jnp.full((LANES - N_EXPERTS, LANES), -1.0, F32)
    for i in range(rb):
        for blk in range(nblk):
            tile = jnp.concatenate([pos[i * N_EXPERTS:(i + 1) * N_EXPERTS, blk * LANES:(blk + 1) * LANES], pad], axis=0)
            post_ref[i, blk * LANES:(blk + 1) * LANES, :] = tile.T.astype(jnp.int32)


def _topk(afft, cap, rb, tbs):
    bn, _, seq_len = afft.shape
    return pl.pallas_call(
        functools.partial(_topk_kernel, seq_len=seq_len, cap=cap, rb=rb, tbs=tbs),
        grid=(bn // rb,),
        in_specs=[pl.BlockSpec((rb, N_EXPERTS, seq_len), lambda b: (b, 0, 0))],
        out_specs=[pl.BlockSpec((rb, N_EXPERTS, seq_len), lambda b: (b, 0, 0)),
                   pl.BlockSpec((rb, seq_len, LANES), lambda b: (b, 0, 0)),
                   pl.BlockSpec((rb, N_EXPERTS, LANES), lambda b: (b, 0, 0))],
        out_shape=[jax.ShapeDtypeStruct((bn, N_EXPERTS, seq_len), jnp.int32),
                   jax.ShapeDtypeStruct((bn, seq_len, LANES), jnp.int32),
                   jax.ShapeDtypeStruct((bn, N_EXPERTS, LANES), jnp.int32)],
        compiler_params=_params(("arbitrary",)),
        name="topk",
    )(afft)


def _gather_kernel(h2_ref, pos_ref, afft_ref, xe_ref, gate_ref, *, cap):
    n_rows, seq_len, _ = h2_ref.shape
    n_slots = N_EXPERTS * cap
    rank = lax.broadcasted_iota(jnp.int32, (cap, seq_len), 0)
    for i in range(n_rows):
        onehots, gates = [], []
        for e in range(N_EXPERTS):
            hit = pos_ref[i, e:e + 1, :] == rank
            onehots.append(jnp.where(hit, 1.0, 0.0).astype(BF16))
            gates.append(jnp.sum(jnp.where(hit, afft_ref[i, e:e + 1, :], 0.0), axis=1, keepdims=True))
        xe_ref[i] = jnp.dot(jnp.concatenate(onehots, axis=0), h2_ref[i], preferred_element_type=F32).astype(BF16)
        gate_ref[i] = jnp.broadcast_to(jnp.concatenate(gates, axis=0), (n_slots, LANES))


def _gather(h2, pos, afft, cap, rows_per_step):
    bn, seq_len, _ = h2.shape
    n_slots = N_EXPERTS * cap
    rows = lambda *tail: pl.BlockSpec((rows_per_step,) + tail, lambda b: (b, 0, 0))
    return pl.pallas_call(
        functools.partial(_gather_kernel, cap=cap),
        grid=(bn // rows_per_step,),
        in_specs=[rows(seq_len, D_MODEL), rows(N_EXPERTS, seq_len), rows(N_EXPERTS, seq_len)],
        out_specs=[rows(n_slots, D_MODEL), rows(n_slots, LANES)],
        out_shape=[jax.ShapeDtypeStruct((bn, n_slots, D_MODEL), BF16),
                   jax.ShapeDtypeStruct((bn, n_slots, LANES), F32)],
        compiler_params=_params(("arbitrary",)),
        name="gather",
    )(h2, pos, afft)


def _window_starts(tab_ref, b, e0, n_e, t, cap, win):
    starts, overflows = [], []
    for ee in range(n_e):
        base = (b * N_EXPERTS + e0 + ee) * LANES + t
        p0 = tab_ref[base]
        p1 = tab_ref[base + 1]
        w0 = jnp.minimum(lax.shift_left(lax.shift_right_logical(p0, 4), 4), cap - win)
        starts.append(w0)
        overflows.append(p1 - w0 > win)
    return starts, overflows


def _gather_win_kernel(tab_ref, h2_ref, pos_ref, afft_ref, xe_ref, gate_ref, *, cap, win, n_e):
    b = pl.program_id(0)
    e0 = pl.program_id(1) * n_e
    t = pl.program_id(2)
    tbs = h2_ref.shape[1]

    @pl.when(t == 0)
    def _():
        xe_ref[...] = jnp.zeros(xe_ref.shape, BF16)
        gate_ref[...] = jnp.zeros(gate_ref.shape, F32)

    starts, overflows = _window_starts(tab_ref, b, e0, n_e, t, cap, win)

    def accumulate(row0, n, x, g):
        cur = xe_ref[0, pl.ds(row0, n), :]
        xe_ref[0, pl.ds(row0, n), :] = jnp.where(g > 0.0, x.astype(BF16), cur)
        gate_ref[0, pl.ds(row0, n), :] = gate_ref[0, pl.ds(row0, n), :] + jnp.broadcast_to(g, (n, LANES))

    def onehot_and_gate(ee, shift, n):
        sub = lax.broadcasted_iota(jnp.int32, (n, tbs), 0)
        hit = (pos_ref[0, pl.ds(e0 + ee, 1), :] - shift) == sub
        gate = jnp.sum(jnp.where(hit, afft_ref[0, pl.ds(e0 + ee, 1), :], 0.0), axis=1, keepdims=True)
        return jnp.where(hit, 1.0, 0.0).astype(BF16), gate

    pieces = [onehot_and_gate(ee, jnp.where(overflows[ee], cap, starts[ee]), win) for ee in range(n_e)]
    x = jnp.dot(jnp.concatenate([p[0] for p in pieces], axis=0), h2_ref[0], preferred_element_type=F32)
    for ee in range(n_e):
        accumulate(pl.multiple_of(ee * cap + starts[ee], 16), win, x[ee * win:(ee + 1) * win], pieces[ee][1])

    for ee in range(n_e):
        @pl.when(overflows[ee])
        def _(ee=ee):
            onehot, gate = onehot_and_gate(ee, 0, cap)
            accumulate(ee * cap, cap, jnp.dot(onehot, h2_ref[0], preferred_element_type=F32), gate)


def _gather_win(tab, h2, pos, afft, cap, win, tbs):
    bn, seq_len, _ = h2.shape
    n_slots = N_EXPERTS * cap
    n_e = N_EXPERTS // 2
    grid_spec = pltpu.PrefetchScalarGridSpec(
        num_scalar_prefetch=1,
        grid=(bn, N_EXPERTS // n_e, seq_len // tbs),
        in_specs=[pl.BlockSpec((1, tbs, D_MODEL), lambda b, g, t, tab: (b, t, 0)),
                  pl.BlockSpec((1, N_EXPERTS, tbs), lambda b, g, t, tab: (b, 0, t)),
                  pl.BlockSpec((1, N_EXPERTS, tbs), lambda b, g, t, tab: (b, 0, t))],
        out_specs=[pl.BlockSpec((1, n_e * cap, D_MODEL), lambda b, g, t, tab: (b, g, 0)),
                   pl.BlockSpec((1, n_e * cap, LANES), lambda b, g, t, tab: (b, g, 0))],
    )
    return pl.pallas_call(
        functools.partial(_gather_win_kernel, cap=cap, win=win, n_e=n_e),
        grid_spec=grid_spec,
        out_shape=[jax.ShapeDtypeStruct((bn, n_slots, D_MODEL), BF16),
                   jax.ShapeDtypeStruct((bn, n_slots, LANES), F32)],
        compiler_params=_params(("arbitrary", "arbitrary", "arbitrary")),
        name="gather_win",
    )(tab, h2, pos, afft)


def _ffn_kernel(*refs, tiles):
    n = len(tiles)
    xe_refs, gate_refs = refs[0:2 * n:2], refs[1:2 * n:2]
    wg_ref, wu_ref, wd_ref = refs[2 * n:2 * n + 3]
    ye_refs = refs[2 * n + 3:3 * n + 3]
    wg_s, wu_s, wd_s = refs[3 * n + 3:]
    j = pl.program_id(1)

    @pl.when(j == 0)
    def _():
        wg_s[...] = wg_ref[0, 0].astype(BF16)
        wu_s[...] = wu_ref[0, 0].astype(BF16)
        wd_s[...] = wd_ref[0, 0].astype(BF16)

    for (first, steps, rows), xe_ref, gate_ref, ye_ref in zip(tiles, xe_refs, gate_refs, ye_refs):
        @pl.when(jnp.logical_and(j >= first, j < first + steps))
        def _(rows=rows, xe_ref=xe_ref, gate_ref=gate_ref, ye_ref=ye_ref):
            x = xe_ref[...].reshape(rows, D_MODEL)
            g = jnp.dot(x, wg_s[...], preferred_element_type=F32)
            u = jnp.dot(x, wu_s[...], preferred_element_type=F32)
            hid = (g * _sigmoid(g) * u).astype(BF16)
            y = jnp.dot(hid, wd_s[...], preferred_element_type=F32) * gate_ref[...].reshape(rows, LANES)[:, :1]
            ye_ref[...] = y.astype(BF16).reshape(ye_ref.shape)


def _ffn(groups, w_gate, w_up, w_down, layer):
    tiles, in_specs, out_specs, out_shape, args = [], [], [], [], []
    first = 0
    for xe, gate, cap, bb in groups:
        steps = xe.shape[0] // bb
        tile = lambda e, j, first=first, steps=steps: (jnp.clip(j - first, 0, steps - 1), e, 0)
        in_specs += [pl.BlockSpec((bb, cap, D_MODEL), tile), pl.BlockSpec((bb, cap, LANES), tile)]
        out_specs.append(pl.BlockSpec((bb, cap, D_MODEL), tile))
        out_shape.append(jax.ShapeDtypeStruct(xe.shape, BF16))
        args += [xe, gate]
        tiles.append((first, steps, bb * cap))
        first += steps
    wspec = pl.BlockSpec((1, 1, D_MODEL, D_MODEL), lambda e, j: (layer, e, 0, 0))
    return pl.pallas_call(
        functools.partial(_ffn_kernel, tiles=tuple(tiles)),
        grid=(N_EXPERTS, first),
        in_specs=in_specs + [wspec, wspec, wspec],
        out_specs=out_specs,
        out_shape=out_shape,
        scratch_shapes=[pltpu.VMEM((D_MODEL, D_MODEL), BF16)] * 3,
        compiler_params=_params(("arbitrary", "arbitrary")),
        name="ffn",
    )(*args, w_gate, w_up, w_down)


def _finish(x1_ref, mod_ref, fg_ref, out_ref, acc, final, i=0):
    x2 = x1_ref[i] + mod_ref[0][5:6] * acc
    if final:
        x2 = x2 * lax.rsqrt(jnp.mean(x2 * x2, axis=-1, keepdims=True) + EPS) * fg_ref[...]
    out_ref[i] = x2


def _scatter_kernel(x1_ref, ye_ref, post_ref, mod_ref, fg_ref, out_ref, *, cap, final):
    n_rows = x1_ref.shape[0]
    n_slots = N_EXPERTS * cap
    slot = lax.broadcasted_iota(jnp.int32, (1, n_slots), 1)
    expert_of_slot = jnp.zeros((1, n_slots), jnp.int32)
    for e in range(1, N_EXPERTS):
        expert_of_slot = expert_of_slot + (slot >= e * cap).astype(jnp.int32)
    rank_of_slot = (slot - expert_of_slot * cap).astype(F32)
    spread = jnp.where(lax.broadcasted_iota(jnp.int32, (LANES, n_slots), 0) == expert_of_slot, 1.0, 0.0).astype(BF16)
    for i in range(n_rows):
        pt = post_ref[i]
        ranks = jnp.dot(pt.astype(F32).astype(BF16), spread, preferred_element_type=F32)
        onehot = jnp.where(ranks == rank_of_slot, 1.0, 0.0).astype(BF16)
        acc = jnp.dot(onehot, ye_ref[i], preferred_element_type=F32)
        _finish(x1_ref, mod_ref, fg_ref, out_ref, acc, final, i)


def _scatter(x1, ye, post, mod_l, mod_row, final_g, cap, final, rows_per_step):
    bn, seq_len, _ = x1.shape
    assert cap <= 256, "ranks must be exactly representable in bf16"
    n_slots = N_EXPERTS * cap
    rows = lambda *tail: pl.BlockSpec((rows_per_step,) + tail, lambda b: (b, 0, 0))
    return pl.pallas_call(
        functools.partial(_scatter_kernel, cap=cap, final=final),
        grid=(bn // rows_per_step,),
        in_specs=[rows(seq_len, D_MODEL), rows(n_slots, D_MODEL), rows(seq_len, LANES),
                  pl.BlockSpec((1, N_MOD, D_MODEL), lambda b: (mod_row(b * rows_per_step), 0, 0)),
                  pl.BlockSpec((1, D_MODEL), lambda b: (0, 0))],
        out_specs=rows(seq_len, D_MODEL),
        out_shape=jax.ShapeDtypeStruct((bn, seq_len, D_MODEL), F32),
        compiler_params=_params(("arbitrary",)),
        name="scatter",
    )(x1, ye, post, mod_l, final_g)


def _scatter_win_kernel(tab_ref, x1_ref, ye_ref, post_ref, mod_ref, fg_ref, out_ref, acc_ref, *, cap, win, final):
    b = pl.program_id(0)
    t = pl.program_id(1)
    tbs = x1_ref.shape[1]
    pt = post_ref[0]
    starts, overflows = _window_starts(tab_ref, b, 0, N_EXPERTS, t, cap, win)

    def onehot(e, shift, n):
        lane = lax.broadcasted_iota(jnp.int32, (tbs, n), 1)
        return jnp.where((pt[:, e:e + 1] - shift) == lane, 1.0, 0.0).astype(BF16)

    onehots = [onehot(e, jnp.where(overflows[e], cap, starts[e]), win) for e in range(N_EXPERTS)]
    rows = [ye_ref[0, pl.ds(pl.multiple_of(e * cap + starts[e], 16), win), :] for e in range(N_EXPERTS)]
    acc_ref[...] = jnp.dot(jnp.concatenate(onehots, axis=1), jnp.concatenate(rows, axis=0),
                           preferred_element_type=F32)

    for e in range(N_EXPERTS):
        @pl.when(overflows[e])
        def _(e=e):
            acc_ref[...] += jnp.dot(onehot(e, 0, cap), ye_ref[0, e * cap:(e + 1) * cap, :],
                                    preferred_element_type=F32)

    _finish(x1_ref, mod_ref, fg_ref, out_ref, acc_ref[...], final)


def _scatter_win(tab, x1, ye, post, mod_l, mod_row, final_g, cap, win, tbs, final):
    bn, seq_len, _ = x1.shape
    n_slots = N_EXPERTS * cap
    grid_spec = pltpu.PrefetchScalarGridSpec(
        num_scalar_prefetch=1,
        grid=(bn, seq_len // tbs),
        in_specs=[pl.BlockSpec((1, tbs, D_MODEL), lambda b, t, tab: (b, t, 0)),
                  pl.BlockSpec((1, n_slots, D_MODEL), lambda b, t, tab: (b, 0, 0), pipeline_mode=pl.Buffered(1)),
                  pl.BlockSpec((1, tbs, LANES), lambda b, t, tab: (b, t, 0)),
                  pl.BlockSpec((1, N_MOD, D_MODEL), lambda b, t, tab: (mod_row(b), 0, 0)),
                  pl.BlockSpec((1, D_MODEL), lambda b, t, tab: (0, 0))],
        out_specs=pl.BlockSpec((1, tbs, D_MODEL), lambda b, t, tab: (b, t, 0)),
        scratch_shapes=[pltpu.VMEM((tbs, D_MODEL), F32)],
    )
    return pl.pallas_call(
        functools.partial(_scatter_win_kernel, cap=cap, win=win, final=final),
        grid_spec=grid_spec,
        out_shape=jax.ShapeDtypeStruct((bn, seq_len, D_MODEL), F32),
        compiler_params=_params(("arbitrary", "arbitrary")),
        name="scatter_win",
    )(tab, x1, ye, post, mod_l, final_g)


def _grid_position_table(n_tokens, dim):
    assert n_tokens // GRID_W <= GRID_W
    quarter = dim // 4
    inv_freq = jnp.power(POS_BASE, -jnp.arange(quarter, dtype=F32) / quarter)
    angle = jnp.arange(GRID_W).astype(F32)[:, None] * inv_freq[None]
    return jnp.concatenate([jnp.sin(angle), jnp.cos(angle)], axis=-1)


def _pad_cols(w, width):
    return jnp.pad(w, ((0, 0), (0, width - w.shape[1])))


def _router_pieces(w):
    w_hi = w.astype(BF16)
    w_lo = (w - w_hi.astype(F32)).astype(BF16)
    return jnp.stack([_pad_cols(jnp.concatenate([w_hi, w_lo], axis=1), LANES), _pad_cols(w_hi, LANES)])


def _run_groups(groups, mod, lw, final_g):
    xs_res = [g["x"] for g in groups]
    states = [([], []) for _ in groups]
    for l in range(DEPTH):
        w = lw[l]
        final = l == DEPTH - 1
        mids = []
        for gi, g in enumerate(groups):
            x = xs_res[gi]
            bn, seq_len, _ = x.shape
            cap = CAPACITY_FACTOR * seq_len // N_EXPERTS
            windowed = seq_len >= 1024
            tbs = max(LANES, seq_len // 8) if windowed else seq_len
            win = cap // 4
            mod_row, tl = g["mod_row"], g["tl"]
            pos_table = g.get("pos_table") if l == 0 else None
            outs = _in_proj(x, mod[l], mod_row, w["norm_mix_g"], w["w_main"], w["w_dt"], w["pool_w"],
                            w["pool_scale"], w["conv_w"], w["conv_b"], w["alog_row"], w["bias_row"], tl, pos_table)
            y_pool, z, xs, bc, dt, cum = outs[:6]
            if pos_table is not None:
                x = outs[6]
            y_f, y_b, hT_f, hT_b = _ssd(xs, bc, dt, cum, g["h0f"][l], g["h0b"][l], g["h0_row"])
            states[gi][0].append(hT_f)
            states[gi][1].append(hT_b)
            x1, h2, afft = _out_proj(x, y_f, y_b, xs, z, y_pool, w["dsk"], w["ssm_norm_g"], w["w_out"], mod[l],
                                     mod_row, w["norm_ffn_g"], w["w_router"], tl)
            pos, post, tab = _topk(afft, cap, g["topk_rows"], tbs)
            if windowed:
                tab = tab.reshape(-1)
                xe, gate = _gather_win(tab, h2, pos, afft, cap, win, tbs)
            else:
                xe, gate = _gather(h2, pos, afft, cap, g.get("dense_rows", 1))
            mids.append(dict(x1=x1, post=post, tab=tab, xe=xe, gate=gate, cap=cap, win=win, tbs=tbs,
                             windowed=windowed))
        yes = _ffn([(m["xe"], m["gate"], m["cap"], g["ffn_rows_per_step"] // m["cap"])
                    for m, g in zip(mids, groups)], w["w_gate"], w["w_up"], w["w_down"], l)
        for gi, (g, m, ye) in enumerate(zip(groups, mids, yes)):
            if m["windowed"]:
                xs_res[gi] = _scatter_win(m["tab"], m["x1"], ye, m["post"], mod[l], g["mod_row"], final_g, m["cap"],
                                          m["win"], m["tbs"], final)
            else:
                xs_res[gi] = _scatter(m["x1"], ye, m["post"], mod[l], g["mod_row"], final_g, m["cap"], final,
                                      g.get("dense_rows", 1))
    return [(x, sf, sb) for x, (sf, sb) in zip(xs_res, states)]


def kernel(x_prompt, x_sample, state_ssm_fwd, state_ssm_bwd, c, c_ctx, norm_mix_g, w_ada, b_ada, w_in, pool_w,
           pool_scale, conv_w, conv_b, a_log_fwd, a_log_bwd, dt_bias_fwd, dt_bias_bwd, d_skip, ssm_norm_g, w_out,
           norm_ffn_g, w_router, w_gate, w_up, w_down, final_norm_g):
    n_dec = c.shape[0]
    n_ctx = x_prompt.shape[0]
    ctx_row = n_dec
    cond = jnp.concatenate([c, c_ctx[None, :], jnp.zeros((8 - n_dec - 1, D_MODEL), F32)], axis=0)
    mod = _modulation(cond, w_ada, b_ada).reshape(DEPTH, 8, N_MOD, D_MODEL)

    zeros_h = jnp.zeros((LANES - 2 * SSD_HEADS,), F32)
    lw = []
    for l in range(DEPTH):
        lw.append(dict(
            norm_mix_g=norm_mix_g[l][None], w_main=w_in[l][:, :N_MAIN].astype(BF16),
            w_dt=_pad_cols(w_in[l][:, N_MAIN:], LANES).astype(BF16), pool_w=pool_w[l].astype(BF16),
            pool_scale=pool_scale[l][None], conv_w=conv_w[l], conv_b=conv_b[l][None],
            alog_row=jnp.concatenate([a_log_fwd[l], a_log_bwd[l], zeros_h])[None],
            bias_row=jnp.concatenate([dt_bias_fwd[l], dt_bias_bwd[l], zeros_h])[None],
            dsk=jnp.repeat(d_skip[l], SSD_INNER // SSD_HEADS)[None], ssm_norm_g=ssm_norm_g[l][None],
            w_out=w_out[l].astype(BF16), norm_ffn_g=norm_ffn_g[l][None], w_router=_router_pieces(w_router[l]),
            w_gate=w_gate, w_up=w_up, w_down=w_down))
    final_g = final_norm_g[None]
    st_shape = (HEAD_PAIRS, LANES, SSD_STATE)

    zero_state = jnp.zeros((1,) + st_shape, F32)
    prompt = dict(x=x_prompt, mod_row=lambda b: ctx_row, h0f=[zero_state] * DEPTH, h0b=[zero_state] * DEPTH,
                  h0_row=lambda b: 0, tl=x_prompt.shape[1], ffn_rows_per_step=1024, topk_rows=8, dense_rows=4)
    sample = dict(x=x_sample, mod_row=lambda b: b, h0_row=lambda b: b, tl=1024, ffn_rows_per_step=1024,
                  topk_rows=n_dec, pos_table=_grid_position_table(x_sample.shape[1], D_MODEL),
                  h0f=[state_ssm_fwd[:, l].reshape((n_dec,) + st_shape) for l in range(DEPTH)],
                  h0b=[state_ssm_bwd[:, l].reshape((n_dec,) + st_shape) for l in range(DEPTH)])
    (y_prompt, sf, sb), (y_sample, _, _) = _run_groups([prompt, sample], mod, lw, final_g)
    out_state_shape = (n_ctx, SSD_HEADS, SSD_INNER // SSD_HEADS, SSD_STATE)
    new_f = jnp.stack([s.reshape(out_state_shape) for s in sf], axis=1)
    new_b = jnp.stack([s.reshape(out_state_shape) for s in sb], axis=1)
    return (y_prompt, y_sample, new_f, new_b)
```

```python
import functools

import jax
import jax.numpy as jnp
from jax import lax
from jax.experimental import pallas as pl
from jax.experimental.pallas import tpu as pltpu

D_MODEL = 1024
DEPTH = 2
GRID_W = 64
MIX_POOL = 512
POOL_WINDOWS = (2, 4, 8, 16)
POOL_GROUP_CH = 128
SSD_INNER = 512
SSD_HEADS = 8
SSD_HEAD_DIM = SSD_INNER // SSD_HEADS
SSD_STATE = 128
CONV_K = 5
CHUNK = 128
CONV_DIM = 1024
N_EXPERTS = 16
CAPACITY_FACTOR = 2
N_MOD = 6
POS_BASE = 10000.0
EPS = 1e-6
LOG2_E = 1.4426950408889634

HALO = 8
LANES = 128
N_MAIN = MIX_POOL + SSD_INNER + CONV_DIM
HEAD_PAIRS = SSD_HEADS // 2
EXPERTS_PER_DOT = 2
SSD_MAX_CHUNKS_PER_STEP = 4
VMEM_LIMIT = 56 * 1024 * 1024

F32 = jnp.float32
BF16 = jnp.bfloat16
HI = lax.Precision.HIGHEST


def _sigmoid(x):
    return 1.0 / (1.0 + jnp.exp(-x))


def _params(sem):
    return pltpu.CompilerParams(dimension_semantics=sem, vmem_limit_bytes=VMEM_LIMIT)


def _mod_kernel(cond_ref, w_ref, b_ref, out_ref):
    c = cond_ref[...]
    s = c * _sigmoid(c)
    out_ref[0] = jnp.dot(s, w_ref[0], precision=HI, preferred_element_type=F32) + b_ref[0]


def _modulation(cond, w_ada, b_ada):
    n_rows = cond.shape[0]
    n_out = N_MOD * D_MODEL
    tn = 1536
    return pl.pallas_call(
        _mod_kernel,
        grid=(DEPTH, n_out // tn),
        in_specs=[
            pl.BlockSpec((n_rows, D_MODEL), lambda l, j: (0, 0)),
            pl.BlockSpec((1, D_MODEL, tn), lambda l, j: (l, 0, j)),
            pl.BlockSpec((1, 1, tn), lambda l, j: (l, 0, j)),
        ],
        out_specs=pl.BlockSpec((1, n_rows, tn), lambda l, j: (l, 0, j)),
        out_shape=jax.ShapeDtypeStruct((DEPTH, n_rows, n_out), F32),
        compiler_params=_params(("arbitrary", "arbitrary")),
        name="modulation",
    )(cond, w_ada, b_ada.reshape(DEPTH, 1, n_out))


def _cumsum_pieces(dta):
    d_hi = dta.astype(BF16)
    rest = dta - d_hi.astype(F32)
    d_mid = rest.astype(BF16)
    return d_hi, d_mid, (rest - d_mid.astype(F32)).astype(BF16)


def _grid_position_rows(table_ref, grid_row, col0, n_cols):
    row_part = jnp.broadcast_to(table_ref[pl.ds(grid_row, 1), :], (n_cols, D_MODEL // 2))
    return jnp.concatenate([row_part, table_ref[col0:col0 + n_cols, :]], axis=1)


def _in_kernel(*refs, tl, seq_len, with_pos):
    (xp_ref, xc_ref, xn_ref, mod_ref, g_ref, wmain_ref, wdt_ref, poolw_ref, pscale_ref, cw_ref, cb_ref,
     alog_ref, bias_ref) = refs[:13]
    table_ref = refs[13] if with_pos else None
    ypool_ref, z_ref, xs_ref, bc_ref, dt_ref, cum_ref = refs[13 + with_pos:19 + with_pos]
    x0_ref = refs[19 + with_pos] if with_pos else None
    proj_scr = refs[-1]
    i = pl.program_id(1)
    n = pl.num_programs(1)
    rows = tl + 2 * HALO
    x_ext = jnp.concatenate([xp_ref[0], xc_ref[0], xn_ref[0]], axis=0)
    if with_pos:
        r0 = i * (tl // GRID_W)
        last_row = seq_len // GRID_W - 1
        pieces = [_grid_position_rows(table_ref, jnp.maximum(r0 - 1, 0), GRID_W - HALO, HALO)]
        pieces += [_grid_position_rows(table_ref, r0 + k, 0, GRID_W) for k in range(tl // GRID_W)]
        pieces += [_grid_position_rows(table_ref, jnp.minimum(r0 + tl // GRID_W, last_row), 0, HALO)]
        x_ext = x_ext + jnp.concatenate(pieces, axis=0)
        x0_ref[0] = x_ext[HALO:HALO + tl]
    mod = mod_ref[0]
    ms = jnp.mean(x_ext * x_ext, axis=-1, keepdims=True)
    h = x_ext * lax.rsqrt(ms + EPS) * (g_ref[...] * (1.0 + mod[1:2])) + mod[0:1]
    row = lax.broadcasted_iota(jnp.int32, (rows, 1), 0)
    lo_ok = jnp.where(i > 0, 0, HALO)
    hi_ok = jnp.where(i < n - 1, rows, tl + HALO)
    hb = jnp.where((row >= lo_ok) & (row < hi_ok), h, 0.0).astype(BF16)
    proj_scr[...] = jnp.dot(hb, wmain_ref[...], preferred_element_type=F32)
    dt_full = jnp.dot(hb, wdt_ref[...], preferred_element_type=F32)

    x_in = dt_full[HALO:HALO + tl] + bias_ref[...]
    dt = jnp.maximum(x_in, 0.0) + jnp.log1p(jnp.exp(-jnp.abs(x_in)))
    dt_ref[0] = dt
    r = lax.broadcasted_iota(jnp.int32, (CHUNK, CHUNK), 0)
    s = lax.broadcasted_iota(jnp.int32, (CHUNK, CHUNK), 1)
    tri = jnp.concatenate([jnp.where(r >= s, 1.0, 0.0), jnp.where(r <= s, 1.0, 0.0)], axis=1).astype(BF16)
    head = lax.broadcasted_iota(jnp.int32, (tl, LANES), 1)
    dta = jnp.where(head < 2 * SSD_HEADS, dt * (-LOG2_E * jnp.exp(alog_ref[...])), 0.0)
    pieces = _cumsum_pieces(dta)
    fwd_cols = s < SSD_HEADS
    for k in range(tl // CHUNK):
        acc = jnp.zeros((CHUNK, LANES), F32)
        for piece in pieces:
            pk = piece[k * CHUNK:(k + 1) * CHUNK]
            zero = jnp.zeros_like(pk)
            stacked = jnp.concatenate([jnp.where(fwd_cols, pk, zero), jnp.where(fwd_cols, zero, pk)], axis=0)
            acc = acc + jnp.dot(tri, stacked, preferred_element_type=F32)
        cum_ref[0, k * CHUNK:(k + 1) * CHUNK, :] = acc

    def ahead(v, k):
        return pltpu.roll(v, (-k) % rows, axis=0)

    t_glob = i * tl + lax.broadcasted_iota(jnp.int32, (tl, 1), 0)
    outs = []
    for gi, w in enumerate(POOL_WINDOWS):
        c0 = gi * POOL_GROUP_CH
        u = proj_scr[:, c0:c0 + POOL_GROUP_CH]
        acc = u
        span = 1
        while 2 * span < w:
            acc = acc + ahead(acc, span)
            span *= 2
        acc = acc + ahead(acc, -span)
        lo = jnp.maximum(t_glob - w // 2, 0)
        hi = jnp.minimum(t_glob - w // 2 + w, seq_len)
        cnt = (hi - lo).astype(F32)
        p = acc[HALO:HALO + tl] / cnt - u[HALO:HALO + tl]
        outs.append(jnp.dot(p.astype(BF16), poolw_ref[gi], preferred_element_type=F32))
    ypool_ref[0] = (jnp.concatenate(outs, axis=1) * pscale_ref[...]).astype(BF16)

    c0 = MIX_POOL + SSD_INNER
    acc = cb_ref[...] + cw_ref[0:1, :] * proj_scr[HALO - 2:HALO - 2 + tl, c0:c0 + CONV_DIM]
    for k in range(1, CONV_K):
        acc = acc + cw_ref[k:k + 1, :] * proj_scr[HALO - 2 + k:HALO - 2 + k + tl, c0:c0 + CONV_DIM]
    act = acc * _sigmoid(acc)
    xs_ref[0] = act[:, :SSD_INNER].astype(BF16)
    bc_ref[0] = act[:, SSD_INNER:].astype(BF16)
    z_ref[0] = proj_scr[HALO:HALO + tl, MIX_POOL:MIX_POOL + SSD_INNER]


def _in_proj(x, mod_l, mod_row, norm_g, w_main, w_dt, pool_w, pool_scale, conv_w, conv_b, alog_row, bias_row, tl,
             pos_table=None):
    bn, seq_len, _ = x.shape
    nt = seq_len // tl
    hb = tl // HALO
    last_halo = seq_len // HALO - 1
    with_pos = pos_table is not None
    const = lambda *shape: pl.BlockSpec(shape, lambda b, i: (0,) * len(shape))
    tok = lambda width: pl.BlockSpec((1, tl, width), lambda b, i: (b, i, 0))
    in_specs = [
        pl.BlockSpec((1, HALO, D_MODEL), lambda b, i: (b, jnp.maximum(i * hb - 1, 0), 0)),
        tok(D_MODEL),
        pl.BlockSpec((1, HALO, D_MODEL), lambda b, i: (b, jnp.minimum((i + 1) * hb, last_halo), 0)),
        pl.BlockSpec((1, N_MOD, D_MODEL), lambda b, i: (mod_row(b), 0, 0)),
        const(1, D_MODEL),
        const(D_MODEL, N_MAIN),
        const(D_MODEL, LANES),
        const(len(POOL_WINDOWS), POOL_GROUP_CH, POOL_GROUP_CH),
        const(1, MIX_POOL),
        const(CONV_K, CONV_DIM),
        const(1, CONV_DIM),
        const(1, LANES),
        const(1, LANES),
    ]
    out_specs = [tok(MIX_POOL), tok(SSD_INNER), tok(SSD_INNER), tok(CONV_DIM - SSD_INNER), tok(LANES), tok(LANES)]
    out_shape = [
        jax.ShapeDtypeStruct((bn, seq_len, MIX_POOL), BF16),
        jax.ShapeDtypeStruct((bn, seq_len, SSD_INNER), F32),
        jax.ShapeDtypeStruct((bn, seq_len, SSD_INNER), BF16),
        jax.ShapeDtypeStruct((bn, seq_len, CONV_DIM - SSD_INNER), BF16),
        jax.ShapeDtypeStruct((bn, seq_len, LANES), F32),
        jax.ShapeDtypeStruct((bn, seq_len, LANES), F32),
    ]
    args = [x, x, x, mod_l, norm_g, w_main, w_dt, pool_w, pool_scale, conv_w, conv_b, alog_row, bias_row]
    if with_pos:
        assert tl % GRID_W == 0 and pos_table.shape == (GRID_W, D_MODEL // 2)
        in_specs.append(const(GRID_W, D_MODEL // 2))
        out_specs.append(tok(D_MODEL))
        out_shape.append(jax.ShapeDtypeStruct((bn, seq_len, D_MODEL), F32))
        args.append(pos_table)
    return pl.pallas_call(
        functools.partial(_in_kernel, tl=tl, seq_len=seq_len, with_pos=with_pos),
        grid=(bn, nt),
        in_specs=in_specs,
        out_specs=out_specs,
        out_shape=out_shape,
        scratch_shapes=[pltpu.VMEM((tl + 2 * HALO, N_MAIN), F32)],
        compiler_params=_params(("arbitrary", "arbitrary")),
        name="in_proj",
    )(*args)


def _ssd_chunk(xsf_ref, bcf_ref, dtf_ref, cumf_ref, xsb_ref, bcb_ref, dtb_ref, cumb_ref, yf_ref, yb_ref, st_ref, rf, rb):
    r = lax.broadcasted_iota(jnp.int32, (CHUNK, CHUNK), 0)
    s = lax.broadcasted_iota(jnp.int32, (CHUNK, CHUNK), 1)
    lane2 = lax.broadcasted_iota(jnp.int32, (CHUNK, 2 * LANES), 1)
    lane_head = lax.shift_right_logical(lane2, SSD_HEAD_DIM.bit_length() - 1)
    row2 = lax.broadcasted_iota(jnp.int32, (2 * LANES, SSD_STATE), 0)
    is_fwd = s < SSD_HEADS
    dt = jnp.where(is_fwd, dtf_ref[0, rf:rf + CHUNK, :], dtb_ref[0, rb:rb + CHUNK, :])
    cum = jnp.where(is_fwd, cumf_ref[0, rf:rf + CHUNK, :], cumb_ref[0, rb:rb + CHUNK, :])
    tot = jnp.where(lax.broadcasted_iota(jnp.int32, (1, LANES), 1) < SSD_HEADS,
                    cum[CHUNK - 1:CHUNK, :], cum[0:1, :])
    cdec = jnp.exp2(tot)
    cum_t = cum.T
    dt_t = dt.T
    ecum_t = jnp.exp2(cum_t)
    w_t = (dt * jnp.exp2(tot - cum)).T

    def head_rows(arr_t, h0):
        return jnp.concatenate([jnp.broadcast_to(arr_t[h0 + k:h0 + k + 1, :], (SSD_HEAD_DIM, CHUNK))
                                for k in range(4)], axis=0)

    def one_dir(d, xs_ref, bc_ref, y_ref, r0):
        mask = (r >= s) if d == 0 else (r <= s)
        off = d * SSD_HEADS
        bcv = bc_ref[0, r0:r0 + CHUNK, :]
        b01 = bcv[:, :2 * SSD_STATE].astype(BF16)
        c01 = bcv[:, 2 * SSD_STATE:].astype(BF16)
        zero2 = jnp.zeros_like(b01)
        b_blk = jnp.concatenate([jnp.where(lane2 < SSD_STATE, b01, zero2),
                                 jnp.where(lane2 < SSD_STATE, zero2, b01)], axis=0)
        cb01 = lax.dot_general(c01, b_blk, (((1,), (1,)), ((), ())), preferred_element_type=F32)
        for g in range(2):
            h0 = off + 4 * g
            b_g = b01[:, g * SSD_STATE:(g + 1) * SSD_STATE]
            c_g = c01[:, g * SSD_STATE:(g + 1) * SSD_STATE]
            cb = cb01[:, g * CHUNK:(g + 1) * CHUNK]
            xs_g = xs_ref[0, r0:r0 + CHUNK, g * 2 * LANES:(g + 1) * 2 * LANES]
            x_b = xs_g.astype(BF16)
            ms, x_parts = [], []
            for k in range(4):
                hh = h0 + k
                seg = cum[:, hh:hh + 1] - cum_t[hh:hh + 1, :]
                dec = jnp.exp2(jnp.where(mask, seg, -jnp.inf))
                ms.append((cb * dec * dt_t[hh:hh + 1, :]).astype(BF16))
                x_parts.append(jnp.where(lane_head == k, x_b, jnp.zeros_like(x_b)))
            y_diag = jnp.dot(jnp.concatenate(ms, axis=1), jnp.concatenate(x_parts, axis=0),
                             preferred_element_type=F32)
            st = st_ref[d, 2 * g:2 * g + 2].reshape(2 * LANES, SSD_STATE)
            y_off_t = lax.dot_general(st.astype(BF16), c_g, (((1,), (1,)), ((), ())),
                                      preferred_element_type=F32) * head_rows(ecum_t, h0)
            y_ref[0, r0:r0 + CHUNK, g * 2 * LANES:(g + 1) * 2 * LANES] = (y_diag + y_off_t.T).astype(BF16)
            x_d = (xs_g.astype(F32).T * head_rows(w_t, h0)).astype(BF16)
            cs = jnp.dot(x_d, b_g, preferred_element_type=F32)
            dcol = cdec[:, h0 + 3:h0 + 4]
            for k in (2, 1, 0):
                dcol = jnp.where(row2 < (k + 1) * SSD_HEAD_DIM, cdec[:, h0 + k:h0 + k + 1], dcol)
            st_ref[d, 2 * g:2 * g + 2] = (st * dcol + cs).reshape(2, LANES, SSD_STATE)

    one_dir(0, xsf_ref, bcf_ref, yf_ref, rf)
    one_dir(1, xsb_ref, bcb_ref, yb_ref, rb)


def _ssd_kernel(xsf_ref, bcf_ref, dtf_ref, cumf_ref, xsb_ref, bcb_ref, dtb_ref, cumb_ref, h0f_ref, h0b_ref,
                yf_ref, yb_ref, hTf_ref, hTb_ref, st_ref):
    c = pl.program_id(1)
    nc = pl.num_programs(1)

    @pl.when(c == 0)
    def _():
        st_ref[0] = h0f_ref[0]
        st_ref[1] = h0b_ref[0]

    n_chunks = xsf_ref.shape[1] // CHUNK
    for k in range(n_chunks):
        _ssd_chunk(xsf_ref, bcf_ref, dtf_ref, cumf_ref, xsb_ref, bcb_ref, dtb_ref, cumb_ref, yf_ref, yb_ref, st_ref,
                   k * CHUNK, (n_chunks - 1 - k) * CHUNK)

    @pl.when(c == nc - 1)
    def _():
        hTf_ref[0] = st_ref[0]
        hTb_ref[0] = st_ref[1]


def _ssd(xs, bc, dt, cum, h0f, h0b, h0_row):
    bn, seq_len, _ = xs.shape
    step_rows = min(SSD_MAX_CHUNKS_PER_STEP * CHUNK, seq_len)
    nc = seq_len // step_rows
    fwd = lambda width: pl.BlockSpec((1, step_rows, width), lambda b, c: (b, c, 0))
    bwd = lambda width: pl.BlockSpec((1, step_rows, width), lambda b, c: (b, nc - 1 - c, 0))
    st_spec_in = pl.BlockSpec((1, HEAD_PAIRS, LANES, SSD_STATE), lambda b, c: (h0_row(b), 0, 0, 0))
    st_spec_out = pl.BlockSpec((1, HEAD_PAIRS, LANES, SSD_STATE), lambda b, c: (b, 0, 0, 0))
    st_shape = jax.ShapeDtypeStruct((bn, HEAD_PAIRS, LANES, SSD_STATE), F32)
    return pl.pallas_call(
        _ssd_kernel,
        grid=(bn, nc),
        in_specs=[fwd(SSD_INNER), fwd(CONV_DIM - SSD_INNER), fwd(LANES), fwd(LANES),
                  bwd(SSD_INNER), bwd(CONV_DIM - SSD_INNER), bwd(LANES), bwd(LANES),
                  st_spec_in, st_spec_in],
        out_specs=[fwd(SSD_INNER), bwd(SSD_INNER), st_spec_out, st_spec_out],
        out_shape=[jax.ShapeDtypeStruct((bn, seq_len, SSD_INNER), BF16),
                   jax.ShapeDtypeStruct((bn, seq_len, SSD_INNER), BF16), st_shape, st_shape],
        scratch_shapes=[pltpu.VMEM((2, HEAD_PAIRS, LANES, SSD_STATE), F32)],
        compiler_params=_params(("arbitrary", "arbitrary")),
        name="ssd",
    )(xs, bc, dt, cum, xs, bc, dt, cum, h0f, h0b)


def _out_kernel(x_ref, yf_ref, yb_ref, xs_ref, z_ref, yp_ref, dsk_ref, sg_ref, wo_ref, mod_ref, ng_ref, wr_ref,
                x1_ref, h2_ref, afft_ref):
    y = yf_ref[0].astype(F32) + yb_ref[0].astype(F32) + dsk_ref[...] * xs_ref[0].astype(F32)
    z = z_ref[0]
    y = y * (z * _sigmoid(z))
    y = y * lax.rsqrt(jnp.mean(y * y, axis=-1, keepdims=True) + EPS) * sg_ref[...]
    o = (jnp.dot(yp_ref[0], wo_ref[:MIX_POOL, :], preferred_element_type=F32)
         + jnp.dot(y.astype(BF16), wo_ref[MIX_POOL:, :], preferred_element_type=F32))
    mod = mod_ref[0]
    x1 = x_ref[0] + mod[2:3] * o
    x1_ref[0] = x1
    h2 = x1 * lax.rsqrt(jnp.mean(x1 * x1, axis=-1, keepdims=True) + EPS) * (ng_ref[...] * (1.0 + mod[4:5])) + mod[3:4]
    h2_hi = h2.astype(BF16)
    h2_ref[0] = h2_hi
    h2_lo = (h2 - h2_hi.astype(F32)).astype(BF16)
    pa = jnp.dot(h2_hi, wr_ref[0], preferred_element_type=F32)
    pb = jnp.dot(h2_lo, wr_ref[1], preferred_element_type=F32)
    logits = pa + pltpu.roll(pa, LANES - N_EXPERTS, axis=1) + pb
    lane = lax.broadcasted_iota(jnp.int32, logits.shape, 1)
    lg = jnp.where(lane < N_EXPERTS, logits, -jnp.inf)
    e = jnp.exp(lg - jnp.max(lg, axis=-1, keepdims=True))
    aff = e / jnp.sum(e, axis=-1, keepdims=True)
    afft_ref[0] = aff.T[:N_EXPERTS, :]


def _out_proj(x, y_f, y_b, xs, z, y_pool, dsk, ssm_g, w_out, mod_l, mod_row, norm_ffn_g, w_router, tl):
    bn, seq_len, _ = x.shape
    nt = seq_len // tl
    const = lambda *shape: pl.BlockSpec(shape, lambda b, i: (0,) * len(shape))
    tok = lambda width: pl.BlockSpec((1, tl, width), lambda b, i: (b, i, 0))
    return pl.pallas_call(
        _out_kernel,
        grid=(bn, nt),
        in_specs=[tok(D_MODEL), tok(SSD_INNER), tok(SSD_INNER), tok(SSD_INNER), tok(SSD_INNER), tok(MIX_POOL),
                  const(1, SSD_INNER), const(1, SSD_INNER), const(D_MODEL, D_MODEL),
                  pl.BlockSpec((1, N_MOD, D_MODEL), lambda b, i: (mod_row(b), 0, 0)),
                  const(1, D_MODEL), const(2, D_MODEL, LANES)],
        out_specs=[tok(D_MODEL), tok(D_MODEL), pl.BlockSpec((1, N_EXPERTS, tl), lambda b, i: (b, 0, i))],
        out_shape=[jax.ShapeDtypeStruct((bn, seq_len, D_MODEL), F32),
                   jax.ShapeDtypeStruct((bn, seq_len, D_MODEL), BF16),
                   jax.ShapeDtypeStruct((bn, N_EXPERTS, seq_len), F32)],
        compiler_params=_params(("arbitrary", "arbitrary")),
        name="out_proj",
    )(x, y_f, y_b, xs, z, y_pool, dsk, ssm_g, w_out, mod_l, norm_ffn_g, w_router)


def _topk_kernel(afft_ref, pos_ref, post_ref, tab_ref, *, seq_len, cap, rb, tbs):
    n_rows = rb * N_EXPERTS
    a = afft_ref[...].reshape(n_rows, seq_len)

    def body(it, thr):
        cand_bits = thr | jnp.left_shift(jnp.int32(1), 30 - it)
        cnt = jnp.sum((a >= pltpu.bitcast(cand_bits, F32)).astype(jnp.int32), axis=1, keepdims=True)
        return jnp.where(cnt >= cap, cand_bits, thr)

    thr = lax.fori_loop(0, 31, body, jnp.zeros((n_rows, 1), jnp.int32))
    lo = pltpu.bitcast(thr, F32)
    hi = pltpu.bitcast(thr + 1, F32)
    gt = a >= hi
    eq = (a >= lo) & (a < hi)
    need = cap - jnp.sum(gt.astype(jnp.int32), axis=1, keepdims=True)

    r = lax.broadcasted_iota(jnp.int32, (LANES, LANES), 0)
    s = lax.broadcasted_iota(jnp.int32, (LANES, LANES), 1)
    strict_upper = (r < s).astype(BF16)
    nblk = seq_len // LANES

    def excl_cumsum(m):
        carry = jnp.zeros((n_rows, 1), F32)
        outs = []
        for blk in range(nblk):
            mb = m[:, blk * LANES:(blk + 1) * LANES]
            outs.append(jnp.dot(mb.astype(BF16), strict_upper, preferred_element_type=F32) + carry)
            carry = carry + jnp.sum(mb, axis=1, keepdims=True)
        return jnp.concatenate(outs, axis=1)

    eq_rank = excl_cumsum(jnp.where(eq, 1.0, 0.0))
    sel = gt | (eq & (eq_rank < need.astype(F32)))
    rank = excl_cumsum(jnp.where(sel, 1.0, 0.0))
    pos = jnp.where(sel, rank, -1.0)
    pos_ref[...] = pos.astype(jnp.int32).reshape(rb, N_EXPERTS, seq_len)

    lane = lax.broadcasted_iota(jnp.int32, (n_rows, LANES), 1)
    tab = jnp.full((n_rows, LANES), float(cap), F32)
    for k in range(seq_len // tbs):
        tab = jnp.where(lane == k, rank[:, k * tbs:k * tbs + 1], tab)
    tab_ref[...] = tab.astype(jnp.int32).reshape(rb, N_EXPERTS, LANES)

    pad = jnp.full((LANES - N_EXPERTS, LANES), -1.0, F32)
    for i in range(rb):
        for blk in range(nblk):
            tile = jnp.concatenate([pos[i * N_EXPERTS:(i + 1) * N_EXPERTS, blk * LANES:(blk + 1) * LANES], pad], axis=0)
            post_ref[i, blk * LANES:(blk + 1) * LANES, :] = tile.T.astype(jnp.int32)


def _topk(afft, cap, rb, tbs):
    bn, _, seq_len = afft.shape
    return pl.pallas_call(
        functools.partial(_topk_kernel, seq_len=seq_len, cap=cap, rb=rb, tbs=tbs),
        grid=(bn // rb,),
        in_specs=[pl.BlockSpec((rb, N_EXPERTS, seq_len), lambda b: (b, 0, 0))],
        out_specs=[pl.BlockSpec((rb, N_EXPERTS, seq_len), lambda b: (b, 0, 0)),
                   pl.BlockSpec((rb, seq_len, LANES), lambda b: (b, 0, 0)),
                   pl.BlockSpec((rb, N_EXPERTS, LANES), lambda b: (b, 0, 0))],
        out_shape=[jax.ShapeDtypeStruct((bn, N_EXPERTS, seq_len), jnp.int32),
                   jax.ShapeDtypeStruct((bn, seq_len, LANES), jnp.int32),
                   jax.ShapeDtypeStruct((bn, N_EXPERTS, LANES), jnp.int32)],
        compiler_params=_params(("arbitrary",)),
        name="topk",
    )(afft)


def _gather_kernel(h2_ref, pos_ref, afft_ref, xe_ref, gate_ref, *, cap):
    n_rows, seq_len, _ = h2_ref.shape
    n_slots = N_EXPERTS * cap
    rank = lax.broadcasted_iota(jnp.int32, (cap, seq_len), 0)
    for i in range(n_rows):
        onehots, gates = [], []
        for e in range(N_EXPERTS):
            hit = pos_ref[i, e:e + 1, :] == rank
            onehots.append(jnp.where(hit, 1.0, 0.0).astype(BF16))
            gates.append(jnp.sum(jnp.where(hit, afft_ref[i, e:e + 1, :], 0.0), axis=1, keepdims=True))
        xe_ref[i] = jnp.dot(jnp.concatenate(onehots, axis=0), h2_ref[i], preferred_element_type=F32).astype(BF16)
        gate_ref[i] = jnp.broadcast_to(jnp.concatenate(gates, axis=0), (n_slots, LANES))


def _gather(h2, pos, afft, cap, rows_per_step):
    bn, seq_len, _ = h2.shape
    n_slots = N_EXPERTS * cap
    rows = lambda *tail: pl.BlockSpec((rows_per_step,) + tail, lambda b: (b, 0, 0))
    return pl.pallas_call(
        functools.partial(_gather_kernel, cap=cap),
        grid=(bn // rows_per_step,),
        in_specs=[rows(seq_len, D_MODEL), rows(N_EXPERTS, seq_len), rows(N_EXPERTS, seq_len)],
        out_specs=[rows(n_slots, D_MODEL), rows(n_slots, LANES)],
        out_shape=[jax.ShapeDtypeStruct((bn, n_slots, D_MODEL), BF16),
                   jax.ShapeDtypeStruct((bn, n_slots, LANES), F32)],
        compiler_params=_params(("arbitrary",)),
        name="gather",
    )(h2, pos, afft)


def _window_starts(tab_ref, b, e0, n_e, t, cap, win):
    starts, overflows = [], []
    for ee in range(n_e):
        base = (b * N_EXPERTS + e0 + ee) * LANES + t
        p0 = tab_ref[base]
        p1 = tab_ref[base + 1]
        w0 = jnp.minimum(lax.shift_left(lax.shift_right_logical(p0, 4), 4), cap - win)
        starts.append(w0)
        overflows.append(p1 - w0 > win)
    return starts, overflows


def _gather_win_kernel(tab_ref, h2_ref, pos_ref, afft_ref, xe_ref, gate_ref, *, cap, win, n_e):
    b = pl.program_id(0)
    e0 = pl.program_id(1) * n_e
    t = pl.program_id(2)
    tbs = h2_ref.shape[1]

    @pl.when(t == 0)
    def _():
        xe_ref[...] = jnp.zeros(xe_ref.shape, BF16)
        gate_ref[...] = jnp.zeros(gate_ref.shape, F32)

    starts, overflows = _window_starts(tab_ref, b, e0, n_e, t, cap, win)

    def accumulate(row0, n, x, g):
        cur = xe_ref[0, pl.ds(row0, n), :]
        xe_ref[0, pl.ds(row0, n), :] = jnp.where(g > 0.0, x.astype(BF16), cur)
        gate_ref[0, pl.ds(row0, n), :] = gate_ref[0, pl.ds(row0, n), :] + jnp.broadcast_to(g, (n, LANES))

    def onehot_and_gate(ee, shift, n):
        sub = lax.broadcasted_iota(jnp.int32, (n, tbs), 0)
        hit = (pos_ref[0, pl.ds(e0 + ee, 1), :] - shift) == sub
        gate = jnp.sum(jnp.where(hit, afft_ref[0, pl.ds(e0 + ee, 1), :], 0.0), axis=1, keepdims=True)
        return jnp.where(hit, 1.0, 0.0).astype(BF16), gate

    pieces = [onehot_and_gate(ee, jnp.where(overflows[ee], cap, starts[ee]), win) for ee in range(n_e)]
    x = jnp.dot(jnp.concatenate([p[0] for p in pieces], axis=0), h2_ref[0], preferred_element_type=F32)
    for ee in range(n_e):
        accumulate(pl.multiple_of(ee * cap + starts[ee], 16), win, x[ee * win:(ee + 1) * win], pieces[ee][1])

    for ee in range(n_e):
        @pl.when(overflows[ee])
        def _(ee=ee):
            onehot, gate = onehot_and_gate(ee, 0, cap)
            accumulate(ee * cap, cap, jnp.dot(onehot, h2_ref[0], preferred_element_type=F32), gate)


def _gather_win(tab, h2, pos, afft, cap, win, tbs):
    bn, seq_len, _ = h2.shape
    n_slots = N_EXPERTS * cap
    n_e = N_EXPERTS // 2
    grid_spec = pltpu.PrefetchScalarGridSpec(
        num_scalar_prefetch=1,
        grid=(bn, N_EXPERTS // n_e, seq_len // tbs),
        in_specs=[pl.BlockSpec((1, tbs, D_MODEL), lambda b, g, t, tab: (b, t, 0)),
                  pl.BlockSpec((1, N_EXPERTS, tbs), lambda b, g, t, tab: (b, 0, t)),
                  pl.BlockSpec((1, N_EXPERTS, tbs), lambda b, g, t, tab: (b, 0, t))],
        out_specs=[pl.BlockSpec((1, n_e * cap, D_MODEL), lambda b, g, t, tab: (b, g, 0)),
                   pl.BlockSpec((1, n_e * cap, LANES), lambda b, g, t, tab: (b, g, 0))],
    )
    return pl.pallas_call(
        functools.partial(_gather_win_kernel, cap=cap, win=win, n_e=n_e),
        grid_spec=grid_spec,
        out_shape=[jax.ShapeDtypeStruct((bn, n_slots, D_MODEL), BF16),
                   jax.ShapeDtypeStruct((bn, n_slots, LANES), F32)],
        compiler_params=_params(("arbitrary", "arbitrary", "arbitrary")),
        name="gather_win",
    )(tab, h2, pos, afft)


def _ffn_kernel(*refs, tiles):
    n = len(tiles)
    xe_refs, gate_refs = refs[0:2 * n:2], refs[1:2 * n:2]
    wg_ref, wu_ref, wd_ref = refs[2 * n:2 * n + 3]
    ye_refs = refs[2 * n + 3:3 * n + 3]
    wg_s, wu_s, wd_s = refs[3 * n + 3:]
    j = pl.program_id(1)

    @pl.when(j == 0)
    def _():
        wg_s[...] = wg_ref[0, 0].astype(BF16)
        wu_s[...] = wu_ref[0, 0].astype(BF16)
        wd_s[...] = wd_ref[0, 0].astype(BF16)

    for (first, steps, rows), xe_ref, gate_ref, ye_ref in zip(tiles, xe_refs, gate_refs, ye_refs):
        @pl.when(jnp.logical_and(j >= first, j < first + steps))
        def _(rows=rows, xe_ref=xe_ref, gate_ref=gate_ref, ye_ref=ye_ref):
            x = xe_ref[...].reshape(rows, D_MODEL)
            g = jnp.dot(x, wg_s[...], preferred_element_type=F32)
            u = jnp.dot(x, wu_s[...], preferred_element_type=F32)
            hid = (g * _sigmoid(g) * u).astype(BF16)
            y = jnp.dot(hid, wd_s[...], preferred_element_type=F32) * gate_ref[...].reshape(rows, LANES)[:, :1]
            ye_ref[...] = y.astype(BF16).reshape(ye_ref.shape)


def _ffn(groups, w_gate, w_up, w_down, layer):
    tiles, in_specs, out_specs, out_shape, args = [], [], [], [], []
    first = 0
    for xe, gate, cap, bb in groups:
        steps = xe.shape[0] // bb
        tile = lambda e, j, first=first, steps=steps: (jnp.clip(j - first, 0, steps - 1), e, 0)
        in_specs += [pl.BlockSpec((bb, cap, D_MODEL), tile), pl.BlockSpec((bb, cap, LANES), tile)]
        out_specs.append(pl.BlockSpec((bb, cap, D_MODEL), tile))
        out_shape.append(jax.ShapeDtypeStruct(xe.shape, BF16))
        args += [xe, gate]
        tiles.append((first, steps, bb * cap))
        first += steps
    wspec = pl.BlockSpec((1, 1, D_MODEL, D_MODEL), lambda e, j: (layer, e, 0, 0))
    return pl.pallas_call(
        functools.partial(_ffn_kernel, tiles=tuple(tiles)),
        grid=(N_EXPERTS, first),
        in_specs=in_specs + [wspec, wspec, wspec],
        out_specs=out_specs,
        out_shape=out_shape,
        scratch_shapes=[pltpu.VMEM((D_MODEL, D_MODEL), BF16)] * 3,
        compiler_params=_params(("arbitrary", "arbitrary")),
        name="ffn",
    )(*args, w_gate, w_up, w_down)


def _finish(x1_ref, mod_ref, fg_ref, out_ref, acc, final, i=0):
    x2 = x1_ref[i] + mod_ref[0][5:6] * acc
    if final:
        x2 = x2 * lax.rsqrt(jnp.mean(x2 * x2, axis=-1, keepdims=True) + EPS) * fg_ref[...]
    out_ref[i] = x2


def _scatter_kernel(x1_ref, ye_ref, post_ref, mod_ref, fg_ref, out_ref, *, cap, final):
    n_rows = x1_ref.shape[0]
    n_slots = N_EXPERTS * cap
    slot = lax.broadcasted_iota(jnp.int32, (1, n_slots), 1)
    expert_of_slot = jnp.zeros((1, n_slots), jnp.int32)
    for e in range(1, N_EXPERTS):
        expert_of_slot = expert_of_slot + (slot >= e * cap).astype(jnp.int32)
    rank_of_slot = (slot - expert_of_slot * cap).astype(F32)
    spread = jnp.where(lax.broadcasted_iota(jnp.int32, (LANES, n_slots), 0) == expert_of_slot, 1.0, 0.0).astype(BF16)
    for i in range(n_rows):
        pt = post_ref[i]
        ranks = jnp.dot(pt.astype(F32).astype(BF16), spread, preferred_element_type=F32)
        onehot = jnp.where(ranks == rank_of_slot, 1.0, 0.0).astype(BF16)
        acc = jnp.dot(onehot, ye_ref[i], preferred_element_type=F32)
        _finish(x1_ref, mod_ref, fg_ref, out_ref, acc, final, i)


def _scatter(x1, ye, post, mod_l, mod_row, final_g, cap, final, rows_per_step):
    bn, seq_len, _ = x1.shape
    assert cap <= 256, "ranks must be exactly representable in bf16"
    n_slots = N_EXPERTS * cap
    rows = lambda *tail: pl.BlockSpec((rows_per_step,) + tail, lambda b: (b, 0, 0))
    return pl.pallas_call(
        functools.partial(_scatter_kernel, cap=cap, final=final),
        grid=(bn // rows_per_step,),
        in_specs=[rows(seq_len, D_MODEL), rows(n_slots, D_MODEL), rows(seq_len, LANES),
                  pl.BlockSpec((1, N_MOD, D_MODEL), lambda b: (mod_row(b * rows_per_step), 0, 0)),
                  pl.BlockSpec((1, D_MODEL), lambda b: (0, 0))],
        out_specs=rows(seq_len, D_MODEL),
        out_shape=jax.ShapeDtypeStruct((bn, seq_len, D_MODEL), F32),
        compiler_params=_params(("arbitrary",)),
        name="scatter",
    )(x1, ye, post, mod_l, final_g)


def _scatter_win_kernel(tab_ref, x1_ref, ye_ref, post_ref, mod_ref, fg_ref, out_ref, acc_ref, *, cap, win, final):
    b = pl.program_id(0)
    t = pl.program_id(1)
    tbs = x1_ref.shape[1]
    pt = post_ref[0]
    starts, overflows = _window_starts(tab_ref, b, 0, N_EXPERTS, t, cap, win)

    def onehot(e, shift, n):
        lane = lax.broadcasted_iota(jnp.int32, (tbs, n), 1)
        return jnp.where((pt[:, e:e + 1] - shift) == lane, 1.0, 0.0).astype(BF16)

    acc = None
    for e0 in range(0, N_EXPERTS, EXPERTS_PER_DOT):
        group = range(e0, e0 + EXPERTS_PER_DOT)
        onehots = [onehot(e, jnp.where(overflows[e], cap, starts[e]), win) for e in group]
        rows = [ye_ref[0, pl.ds(pl.multiple_of(e * cap + starts[e], 16), win), :] for e in group]
        part = jnp.dot(jnp.concatenate(onehots, axis=1), jnp.concatenate(rows, axis=0), preferred_element_type=F32)
        acc = part if acc is None else acc + part
    acc_ref[...] = acc

    for e in range(N_EXPERTS):
        @pl.when(overflows[e])
        def _(e=e):
            acc_ref[...] += jnp.dot(onehot(e, 0, cap), ye_ref[0, e * cap:(e + 1) * cap, :],
                                    preferred_element_type=F32)

    _finish(x1_ref, mod_ref, fg_ref, out_ref, acc_ref[...], final)


def _scatter_win(tab, x1, ye, post, mod_l, mod_row, final_g, cap, win, tbs, final):
    bn, seq_len, _ = x1.shape
    n_slots = N_EXPERTS * cap
    grid_spec = pltpu.PrefetchScalarGridSpec(
        num_scalar_prefetch=1,
        grid=(bn, seq_len // tbs),
        in_specs=[pl.BlockSpec((1, tbs, D_MODEL), lambda b, t, tab: (b, t, 0)),
                  pl.BlockSpec((1, n_slots, D_MODEL), lambda b, t, tab: (b, 0, 0)),
                  pl.BlockSpec((1, tbs, LANES), lambda b, t, tab: (b, t, 0)),
                  pl.BlockSpec((1, N_MOD, D_MODEL), lambda b, t, tab: (mod_row(b), 0, 0)),
                  pl.BlockSpec((1, D_MODEL), lambda b, t, tab: (0, 0))],
        out_specs=pl.BlockSpec((1, tbs, D_MODEL), lambda b, t, tab: (b, t, 0)),
        scratch_shapes=[pltpu.VMEM((tbs, D_MODEL), F32)],
    )
    return pl.pallas_call(
        functools.partial(_scatter_win_kernel, cap=cap, win=win, final=final),
        grid_spec=grid_spec,
        out_shape=jax.ShapeDtypeStruct((bn, seq_len, D_MODEL), F32),
        compiler_params=_params(("arbitrary", "arbitrary")),
        name="scatter_win",
    )(tab, x1, ye, post, mod_l, final_g)


def _grid_position_table(n_tokens, dim):
    assert n_tokens // GRID_W <= GRID_W
    quarter = dim // 4
    inv_freq = jnp.power(POS_BASE, -jnp.arange(quarter, dtype=F32) / quarter)
    angle = jnp.arange(GRID_W).astype(F32)[:, None] * inv_freq[None]
    return jnp.concatenate([jnp.sin(angle), jnp.cos(angle)], axis=-1)


def _pad_cols(w, width):
    return jnp.pad(w, ((0, 0), (0, width - w.shape[1])))


def _router_pieces(w):
    w_hi = w.astype(BF16)
    w_lo = (w - w_hi.astype(F32)).astype(BF16)
    return jnp.stack([_pad_cols(jnp.concatenate([w_hi, w_lo], axis=1), LANES), _pad_cols(w_hi, LANES)])


def _run_groups(groups, mod, lw, final_g):
    xs_res = [g["x"] for g in groups]
    states = [([], []) for _ in groups]
    for l in range(DEPTH):
        w = lw[l]
        final = l == DEPTH - 1
        mids = []
        for gi, g in enumerate(groups):
            x = xs_res[gi]
            bn, seq_len, _ = x.shape
            cap = CAPACITY_FACTOR * seq_len // N_EXPERTS
            windowed = seq_len >= 1024
            tbs = max(LANES, seq_len // 8) if windowed else seq_len
            win = cap // 4
            mod_row, tl = g["mod_row"], g["tl"]
            pos_table = g.get("pos_table") if l == 0 else None
            outs = _in_proj(x, mod[l], mod_row, w["norm_mix_g"], w["w_main"], w["w_dt"], w["pool_w"],
                            w["pool_scale"], w["conv_w"], w["conv_b"], w["alog_row"], w["bias_row"], tl, pos_table)
            y_pool, z, xs, bc, dt, cum = outs[:6]
            if pos_table is not None:
                x = outs[6]
            y_f, y_b, hT_f, hT_b = _ssd(xs, bc, dt, cum, g["h0f"][l], g["h0b"][l], g["h0_row"])
            states[gi][0].append(hT_f)
            states[gi][1].append(hT_b)
            x1, h2, afft = _out_proj(x, y_f, y_b, xs, z, y_pool, w["dsk"], w["ssm_norm_g"], w["w_out"], mod[l],
                                     mod_row, w["norm_ffn_g"], w["w_router"], tl)
            pos, post, tab = _topk(afft, cap, g["topk_rows"], tbs)
            if windowed:
                tab = tab.reshape(-1)
                xe, gate = _gather_win(tab, h2, pos, afft, cap, win, tbs)
            else:
                xe, gate = _gather(h2, pos, afft, cap, g.get("dense_rows", 1))
            mids.append(dict(x1=x1, post=post, tab=tab, xe=xe, gate=gate, cap=cap, win=win, tbs=tbs,
                             windowed=windowed))
        yes = _ffn([(m["xe"], m["gate"], m["cap"], g["ffn_rows_per_step"] // m["cap"])
                    for m, g in zip(mids, groups)], w["w_gate"], w["w_up"], w["w_down"], l)
        for gi, (g, m, ye) in enumerate(zip(groups, mids, yes)):
            if m["windowed"]:
                xs_res[gi] = _scatter_win(m["tab"], m["x1"], ye, m["post"], mod[l], g["mod_row"], final_g, m["cap"],
                                          m["win"], m["tbs"], final)
            else:
                xs_res[gi] = _scatter(m["x1"], ye, m["post"], mod[l], g["mod_row"], final_g, m["cap"], final,
                                      g.get("dense_rows", 1))
    return [(x, sf, sb) for x, (sf, sb) in zip(xs_res, states)]


def kernel(x_prompt, x_sample, state_ssm_fwd, state_ssm_bwd, c, c_ctx, norm_mix_g, w_ada, b_ada, w_in, pool_w,
           pool_scale, conv_w, conv_b, a_log_fwd, a_log_bwd, dt_bias_fwd, dt_bias_bwd, d_skip, ssm_norm_g, w_out,
           norm_ffn_g, w_router, w_gate, w_up, w_down, final_norm_g):
    n_dec = c.shape[0]
    n_ctx = x_prompt.shape[0]
    ctx_row = n_dec
    cond = jnp.concatenate([c, c_ctx[None, :], jnp.zeros((8 - n_dec - 1, D_MODEL), F32)], axis=0)
    mod = _modulation(cond, w_ada, b_ada).reshape(DEPTH, 8, N_MOD, D_MODEL)

    zeros_h = jnp.zeros((LANES - 2 * SSD_HEADS,), F32)
    lw = []
    for l in range(DEPTH):
        lw.append(dict(
            norm_mix_g=norm_mix_g[l][None], w_main=w_in[l][:, :N_MAIN].astype(BF16),
            w_dt=_pad_cols(w_in[l][:, N_MAIN:], LANES).astype(BF16), pool_w=pool_w[l].astype(BF16),
            pool_scale=pool_scale[l][None], conv_w=conv_w[l], conv_b=conv_b[l][None],
            alog_row=jnp.concatenate([a_log_fwd[l], a_log_bwd[l], zeros_h])[None],
            bias_row=jnp.concatenate([dt_bias_fwd[l], dt_bias_bwd[l], zeros_h])[None],
            dsk=jnp.repeat(d_skip[l], SSD_INNER // SSD_HEADS)[None], ssm_norm_g=ssm_norm_g[l][None],
            w_out=w_out[l].astype(BF16), norm_ffn_g=norm_ffn_g[l][None], w_router=_router_pieces(w_router[l]),
            w_gate=w_gate, w_up=w_up, w_down=w_down))
    final_g = final_norm_g[None]
    st_shape = (HEAD_PAIRS, LANES, SSD_STATE)

    zero_state = jnp.zeros((1,) + st_shape, F32)
    prompt = dict(x=x_prompt, mod_row=lambda b: ctx_row, h0f=[zero_state] * DEPTH, h0b=[zero_state] * DEPTH,
                  h0_row=lambda b: 0, tl=x_prompt.shape[1], ffn_rows_per_step=1024, topk_rows=8, dense_rows=4)
    sample = dict(x=x_sample, mod_row=lambda b: b, h0_row=lambda b: b, tl=1024, ffn_rows_per_step=1024,
                  topk_rows=n_dec, pos_table=_grid_position_table(x_sample.shape[1], D_MODEL),
                  h0f=[state_ssm_fwd[:, l].reshape((n_dec,) + st_shape) for l in range(DEPTH)],
                  h0b=[state_ssm_bwd[:, l].reshape((n_dec,) + st_shape) for l in range(DEPTH)])
    (y_prompt, sf, sb), (y_sample, _, _) = _run_groups([prompt, sample], mod, lw, final_g)
    out_state_shape = (n_ctx, SSD_HEADS, SSD_INNER // SSD_HEADS, SSD_STATE)
    new_f = jnp.stack([s.reshape(out_state_shape) for s in sf], axis=1)
    new_b = jnp.stack([s.reshape(out_state_shape) for s in sb], axis=1)
    return (y_prompt, y_sample, new_f, new_b)
```

```python
import functools

import jax
import jax.numpy as jnp
from jax import lax
from jax.experimental import pallas as pl
from jax.experimental.pallas import tpu as pltpu

D_MODEL = 1024
DEPTH = 2
GRID_W = 64
MIX_POOL = 512
POOL_WINDOWS = (2, 4, 8, 16)
POOL_GROUP_CH = 128
SSD_INNER = 512
SSD_HEADS = 8
SSD_HEAD_DIM = SSD_INNER // SSD_HEADS
SSD_STATE = 128
CONV_K = 5
CHUNK = 128
CONV_DIM = 1024
N_EXPERTS = 16
CAPACITY_FACTOR = 2
N_MOD = 6
POS_BASE = 10000.0
EPS = 1e-6
LOG2_E = 1.4426950408889634

HALO = 8
LANES = 128
N_MAIN = MIX_POOL + SSD_INNER + CONV_DIM
HEAD_PAIRS = SSD_HEADS // 2
EXPERTS_PER_DOT = 2
SSD_MAX_CHUNKS_PER_STEP = 4
VMEM_LIMIT = 56 * 1024 * 1024

F32 = jnp.float32
BF16 = jnp.bfloat16
HI = lax.Precision.HIGHEST


def _sigmoid(x):
    return 1.0 / (1.0 + jnp.exp(-x))


def _params(sem):
    return pltpu.CompilerParams(dimension_semantics=sem, vmem_limit_bytes=VMEM_LIMIT)


def _mod_kernel(cond_ref, w_ref, b_ref, out_ref):
    c = cond_ref[...]
    s = c * _sigmoid(c)
    out_ref[0] = jnp.dot(s, w_ref[0], precision=HI, preferred_element_type=F32) + b_ref[0]


def _modulation(cond, w_ada, b_ada):
    n_rows = cond.shape[0]
    n_out = N_MOD * D_MODEL
    tn = 1536
    return pl.pallas_call(
        _mod_kernel,
        grid=(DEPTH, n_out // tn),
        in_specs=[
            pl.BlockSpec((n_rows, D_MODEL), lambda l, j: (0, 0)),
            pl.BlockSpec((1, D_MODEL, tn), lambda l, j: (l, 0, j)),
            pl.BlockSpec((1, 1, tn), lambda l, j: (l, 0, j)),
        ],
        out_specs=pl.BlockSpec((1, n_rows, tn), lambda l, j: (l, 0, j)),
        out_shape=jax.ShapeDtypeStruct((DEPTH, n_rows, n_out), F32),
        compiler_params=_params(("arbitrary", "arbitrary")),
        name="modulation",
    )(cond, w_ada, b_ada.reshape(DEPTH, 1, n_out))


def _cumsum_pieces(dta):
    d_hi = dta.astype(BF16)
    rest = dta - d_hi.astype(F32)
    d_mid = rest.astype(BF16)
    return d_hi, d_mid, (rest - d_mid.astype(F32)).astype(BF16)


def _grid_position_rows(table_ref, grid_row, col0, n_cols):
    row_part = jnp.broadcast_to(table_ref[pl.ds(grid_row, 1), :], (n_cols, D_MODEL // 2))
    return jnp.concatenate([row_part, table_ref[col0:col0 + n_cols, :]], axis=1)


def _in_row(*refs, tl, seq_len, with_pos, req):
    (xp_ref, xc_ref, xn_ref, mod_ref, g_ref, wmain_ref, wdt_ref, poolw_ref, pscale_ref, cw_ref, cb_ref,
     alog_ref, bias_ref) = refs[:13]
    table_ref = refs[13] if with_pos else None
    ypool_ref, z_ref, xs_ref, bc_ref, dt_ref, cum_ref = refs[13 + with_pos:19 + with_pos]
    x0_ref = refs[19 + with_pos] if with_pos else None
    proj_scr = refs[-1].at[req]
    i = pl.program_id(1)
    n = pl.num_programs(1)
    rows = tl + 2 * HALO
    x_ext = jnp.concatenate([xp_ref[0], xc_ref[req], xn_ref[0]], axis=0)
    if with_pos:
        r0 = i * (tl // GRID_W)
        last_row = seq_len // GRID_W - 1
        pieces = [_grid_position_rows(table_ref, jnp.maximum(r0 - 1, 0), GRID_W - HALO, HALO)]
        pieces += [_grid_position_rows(table_ref, r0 + k, 0, GRID_W) for k in range(tl // GRID_W)]
        pieces += [_grid_position_rows(table_ref, jnp.minimum(r0 + tl // GRID_W, last_row), 0, HALO)]
        x_ext = x_ext + jnp.concatenate(pieces, axis=0)
        x0_ref[req] = x_ext[HALO:HALO + tl]
    mod = mod_ref[0]
    ms = jnp.mean(x_ext * x_ext, axis=-1, keepdims=True)
    h = x_ext * lax.rsqrt(ms + EPS) * (g_ref[...] * (1.0 + mod[1:2])) + mod[0:1]
    row = lax.broadcasted_iota(jnp.int32, (rows, 1), 0)
    lo_ok = jnp.where(i > 0, 0, HALO)
    hi_ok = jnp.where(i < n - 1, rows, tl + HALO)
    hb = jnp.where((row >= lo_ok) & (row < hi_ok), h, 0.0).astype(BF16)
    proj_scr[...] = jnp.dot(hb, wmain_ref[...], preferred_element_type=F32)
    dt_full = jnp.dot(hb, wdt_ref[...], preferred_element_type=F32)

    x_in = dt_full[HALO:HALO + tl] + bias_ref[...]
    dt = jnp.maximum(x_in, 0.0) + jnp.log1p(jnp.exp(-jnp.abs(x_in)))
    dt_ref[req] = dt
    r = lax.broadcasted_iota(jnp.int32, (CHUNK, CHUNK), 0)
    s = lax.broadcasted_iota(jnp.int32, (CHUNK, CHUNK), 1)
    tri = jnp.concatenate([jnp.where(r >= s, 1.0, 0.0), jnp.where(r <= s, 1.0, 0.0)], axis=1).astype(BF16)
    head = lax.broadcasted_iota(jnp.int32, (tl, LANES), 1)
    dta = jnp.where(head < 2 * SSD_HEADS, dt * (-LOG2_E * jnp.exp(alog_ref[...])), 0.0)
    pieces = _cumsum_pieces(dta)
    fwd_cols = s < SSD_HEADS
    for k in range(tl // CHUNK):
        acc = jnp.zeros((CHUNK, LANES), F32)
        for piece in pieces:
            pk = piece[k * CHUNK:(k + 1) * CHUNK]
            zero = jnp.zeros_like(pk)
            stacked = jnp.concatenate([jnp.where(fwd_cols, pk, zero), jnp.where(fwd_cols, zero, pk)], axis=0)
            acc = acc + jnp.dot(tri, stacked, preferred_element_type=F32)
        cum_ref[req, k * CHUNK:(k + 1) * CHUNK, :] = acc

    def ahead(v, k):
        return pltpu.roll(v, (-k) % rows, axis=0)

    t_glob = i * tl + lax.broadcasted_iota(jnp.int32, (tl, 1), 0)
    outs = []
    for gi, w in enumerate(POOL_WINDOWS):
        c0 = gi * POOL_GROUP_CH
        u = proj_scr[:, c0:c0 + POOL_GROUP_CH]
        acc = u
        span = 1
        while 2 * span < w:
            acc = acc + ahead(acc, span)
            span *= 2
        acc = acc + ahead(acc, -span)
        lo = jnp.maximum(t_glob - w // 2, 0)
        hi = jnp.minimum(t_glob - w // 2 + w, seq_len)
        cnt = (hi - lo).astype(F32)
        p = acc[HALO:HALO + tl] / cnt - u[HALO:HALO + tl]
        outs.append(jnp.dot(p.astype(BF16), poolw_ref[gi], preferred_element_type=F32))
    ypool_ref[req] = (jnp.concatenate(outs, axis=1) * pscale_ref[...]).astype(BF16)

    c0 = MIX_POOL + SSD_INNER
    acc = cb_ref[...] + cw_ref[0:1, :] * proj_scr[HALO - 2:HALO - 2 + tl, c0:c0 + CONV_DIM]
    for k in range(1, CONV_K):
        acc = acc + cw_ref[k:k + 1, :] * proj_scr[HALO - 2 + k:HALO - 2 + k + tl, c0:c0 + CONV_DIM]
    act = acc * _sigmoid(acc)
    xs_ref[req] = act[:, :SSD_INNER].astype(BF16)
    bc_ref[req] = act[:, SSD_INNER:].astype(BF16)
    z_ref[req] = proj_scr[HALO:HALO + tl, MIX_POOL:MIX_POOL + SSD_INNER]


def _in_kernel(*refs, tl, seq_len, with_pos):
    for req in range(refs[1].shape[0]):
        _in_row(*refs, tl=tl, seq_len=seq_len, with_pos=with_pos, req=req)


def _in_proj(x, mod_l, mod_row, norm_g, w_main, w_dt, pool_w, pool_scale, conv_w, conv_b, alog_row, bias_row, tl,
             pos_table=None, rows_per_step=1):
    bn, seq_len, _ = x.shape
    nt = seq_len // tl
    rps = rows_per_step
    assert rps == 1 or nt == 1
    hb = tl // HALO
    last_halo = seq_len // HALO - 1
    with_pos = pos_table is not None
    const = lambda *shape: pl.BlockSpec(shape, lambda b, i: (0,) * len(shape))
    tok = lambda width: pl.BlockSpec((rps, tl, width), lambda b, i: (b, i, 0))
    in_specs = [
        pl.BlockSpec((1, HALO, D_MODEL), lambda b, i: (b * rps, jnp.maximum(i * hb - 1, 0), 0)),
        tok(D_MODEL),
        pl.BlockSpec((1, HALO, D_MODEL), lambda b, i: (b * rps, jnp.minimum((i + 1) * hb, last_halo), 0)),
        pl.BlockSpec((1, N_MOD, D_MODEL), lambda b, i: (mod_row(b * rps), 0, 0)),
        const(1, D_MODEL),
        const(D_MODEL, N_MAIN),
        const(D_MODEL, LANES),
        const(len(POOL_WINDOWS), POOL_GROUP_CH, POOL_GROUP_CH),
        const(1, MIX_POOL),
        const(CONV_K, CONV_DIM),
        const(1, CONV_DIM),
        const(1, LANES),
        const(1, LANES),
    ]
    out_specs = [tok(MIX_POOL), tok(SSD_INNER), tok(SSD_INNER), tok(CONV_DIM - SSD_INNER), tok(LANES), tok(LANES)]
    out_shape = [
        jax.ShapeDtypeStruct((bn, seq_len, MIX_POOL), BF16),
        jax.ShapeDtypeStruct((bn, seq_len, SSD_INNER), F32),
        jax.ShapeDtypeStruct((bn, seq_len, SSD_INNER), BF16),
        jax.ShapeDtypeStruct((bn, seq_len, CONV_DIM - SSD_INNER), BF16),
        jax.ShapeDtypeStruct((bn, seq_len, LANES), F32),
        jax.ShapeDtypeStruct((bn, seq_len, LANES), F32),
    ]
    args = [x, x, x, mod_l, norm_g, w_main, w_dt, pool_w, pool_scale, conv_w, conv_b, alog_row, bias_row]
    if with_pos:
        assert tl % GRID_W == 0 and pos_table.shape == (GRID_W, D_MODEL // 2)
        in_specs.append(const(GRID_W, D_MODEL // 2))
        out_specs.append(tok(D_MODEL))
        out_shape.append(jax.ShapeDtypeStruct((bn, seq_len, D_MODEL), F32))
        args.append(pos_table)
    return pl.pallas_call(
        functools.partial(_in_kernel, tl=tl, seq_len=seq_len, with_pos=with_pos),
        grid=(bn // rps, nt),
        in_specs=in_specs,
        out_specs=out_specs,
        out_shape=out_shape,
        scratch_shapes=[pltpu.VMEM((rps, tl + 2 * HALO, N_MAIN), F32)],
        compiler_params=_params(("arbitrary", "arbitrary")),
        name="in_proj",
    )(*args)


def _ssd_chunk(xsf_ref, bcf_ref, dtf_ref, cumf_ref, xsb_ref, bcb_ref, dtb_ref, cumb_ref, yf_ref, yb_ref, st_ref, rf, rb):
    r = lax.broadcasted_iota(jnp.int32, (CHUNK, CHUNK), 0)
    s = lax.broadcasted_iota(jnp.int32, (CHUNK, CHUNK), 1)
    lane2 = lax.broadcasted_iota(jnp.int32, (CHUNK, 2 * LANES), 1)
    lane_head = lax.shift_right_logical(lane2, SSD_HEAD_DIM.bit_length() - 1)
    row2 = lax.broadcasted_iota(jnp.int32, (2 * LANES, SSD_STATE), 0)
    is_fwd = s < SSD_HEADS
    dt = jnp.where(is_fwd, dtf_ref[0, rf:rf + CHUNK, :], dtb_ref[0, rb:rb + CHUNK, :])
    cum = jnp.where(is_fwd, cumf_ref[0, rf:rf + CHUNK, :], cumb_ref[0, rb:rb + CHUNK, :])
    tot = jnp.where(lax.broadcasted_iota(jnp.int32, (1, LANES), 1) < SSD_HEADS,
                    cum[CHUNK - 1:CHUNK, :], cum[0:1, :])
    cdec = jnp.exp2(tot)
    cum_t = cum.T
    dt_t = dt.T
    ecum_t = jnp.exp2(cum_t)
    w_t = (dt * jnp.exp2(tot - cum)).T

    def head_rows(arr_t, h0):
        return jnp.concatenate([jnp.broadcast_to(arr_t[h0 + k:h0 + k + 1, :], (SSD_HEAD_DIM, CHUNK))
                                for k in range(4)], axis=0)

    def one_dir(d, xs_ref, bc_ref, y_ref, r0):
        mask = (r >= s) if d == 0 else (r <= s)
        off = d * SSD_HEADS
        bcv = bc_ref[0, r0:r0 + CHUNK, :]
        b01 = bcv[:, :2 * SSD_STATE].astype(BF16)
        c01 = bcv[:, 2 * SSD_STATE:].astype(BF16)
        zero2 = jnp.zeros_like(b01)
        b_blk = jnp.concatenate([jnp.where(lane2 < SSD_STATE, b01, zero2),
                                 jnp.where(lane2 < SSD_STATE, zero2, b01)], axis=0)
        cb01 = lax.dot_general(c01, b_blk, (((1,), (1,)), ((), ())), preferred_element_type=F32)
        for g in range(2):
            h0 = off + 4 * g
            b_g = b01[:, g * SSD_STATE:(g + 1) * SSD_STATE]
            c_g = c01[:, g * SSD_STATE:(g + 1) * SSD_STATE]
            cb = cb01[:, g * CHUNK:(g + 1) * CHUNK]
            xs_g = xs_ref[0, r0:r0 + CHUNK, g * 2 * LANES:(g + 1) * 2 * LANES]
            x_b = xs_g.astype(BF16)
            ms, x_parts = [], []
            for k in range(4):
                hh = h0 + k
                seg = cum[:, hh:hh + 1] - cum_t[hh:hh + 1, :]
                dec = jnp.exp2(jnp.where(mask, seg, -jnp.inf))
                ms.append((cb * dec * dt_t[hh:hh + 1, :]).astype(BF16))
                x_parts.append(jnp.where(lane_head == k, x_b, jnp.zeros_like(x_b)))
            y_diag = jnp.dot(jnp.concatenate(ms, axis=1), jnp.concatenate(x_parts, axis=0),
                             preferred_element_type=F32)
            st = st_ref[d, 2 * g:2 * g + 2].reshape(2 * LANES, SSD_STATE)
            y_off_t = lax.dot_general(st.astype(BF16), c_g, (((1,), (1,)), ((), ())),
                                      preferred_element_type=F32) * head_rows(ecum_t, h0)
            y_ref[0, r0:r0 + CHUNK, g * 2 * LANES:(g + 1) * 2 * LANES] = (y_diag + y_off_t.T).astype(BF16)
            x_d = (xs_g.astype(F32).T * head_rows(w_t, h0)).astype(BF16)
            cs = jnp.dot(x_d, b_g, preferred_element_type=F32)
            dcol = cdec[:, h0 + 3:h0 + 4]
            for k in (2, 1, 0):
                dcol = jnp.where(row2 < (k + 1) * SSD_HEAD_DIM, cdec[:, h0 + k:h0 + k + 1], dcol)
            st_ref[d, 2 * g:2 * g + 2] = (st * dcol + cs).reshape(2, LANES, SSD_STATE)

    one_dir(0, xsf_ref, bcf_ref, yf_ref, rf)
    one_dir(1, xsb_ref, bcb_ref, yb_ref, rb)


def _ssd_kernel(xsf_ref, bcf_ref, dtf_ref, cumf_ref, xsb_ref, bcb_ref, dtb_ref, cumb_ref, h0f_ref, h0b_ref,
                yf_ref, yb_ref, hTf_ref, hTb_ref, st_ref):
    c = pl.program_id(1)
    nc = pl.num_programs(1)

    @pl.when(c == 0)
    def _():
        st_ref[0] = h0f_ref[0]
        st_ref[1] = h0b_ref[0]

    n_chunks = xsf_ref.shape[1] // CHUNK
    for k in range(n_chunks):
        _ssd_chunk(xsf_ref, bcf_ref, dtf_ref, cumf_ref, xsb_ref, bcb_ref, dtb_ref, cumb_ref, yf_ref, yb_ref, st_ref,
                   k * CHUNK, (n_chunks - 1 - k) * CHUNK)

    @pl.when(c == nc - 1)
    def _():
        hTf_ref[0] = st_ref[0]
        hTb_ref[0] = st_ref[1]


def _ssd(xs, bc, dt, cum, h0f, h0b, h0_row):
    bn, seq_len, _ = xs.shape
    step_rows = min(SSD_MAX_CHUNKS_PER_STEP * CHUNK, seq_len)
    nc = seq_len // step_rows
    fwd = lambda width: pl.BlockSpec((1, step_rows, width), lambda b, c: (b, c, 0))
    bwd = lambda width: pl.BlockSpec((1, step_rows, width), lambda b, c: (b, nc - 1 - c, 0))
    st_spec_in = pl.BlockSpec((1, HEAD_PAIRS, LANES, SSD_STATE), lambda b, c: (h0_row(b), 0, 0, 0))
    st_spec_out = pl.BlockSpec((1, HEAD_PAIRS, LANES, SSD_STATE), lambda b, c: (b, 0, 0, 0))
    st_shape = jax.ShapeDtypeStruct((bn, HEAD_PAIRS, LANES, SSD_STATE), F32)
    return pl.pallas_call(
        _ssd_kernel,
        grid=(bn, nc),
        in_specs=[fwd(SSD_INNER), fwd(CONV_DIM - SSD_INNER), fwd(LANES), fwd(LANES),
                  bwd(SSD_INNER), bwd(CONV_DIM - SSD_INNER), bwd(LANES), bwd(LANES),
                  st_spec_in, st_spec_in],
        out_specs=[fwd(SSD_INNER), bwd(SSD_INNER), st_spec_out, st_spec_out],
        out_shape=[jax.ShapeDtypeStruct((bn, seq_len, SSD_INNER), BF16),
                   jax.ShapeDtypeStruct((bn, seq_len, SSD_INNER), BF16), st_shape, st_shape],
        scratch_shapes=[pltpu.VMEM((2, HEAD_PAIRS, LANES, SSD_STATE), F32)],
        compiler_params=_params(("arbitrary", "arbitrary")),
        name="ssd",
    )(xs, bc, dt, cum, xs, bc, dt, cum, h0f, h0b)


def _out_row(x_ref, yf_ref, yb_ref, xs_ref, z_ref, yp_ref, dsk_ref, sg_ref, wo_ref, mod_ref, ng_ref, wr_ref,
             x1_ref, h2_ref, afft_ref, req):
    y = yf_ref[req].astype(F32) + yb_ref[req].astype(F32) + dsk_ref[...] * xs_ref[req].astype(F32)
    z = z_ref[req]
    y = y * (z * _sigmoid(z))
    y = y * lax.rsqrt(jnp.mean(y * y, axis=-1, keepdims=True) + EPS) * sg_ref[...]
    o = (jnp.dot(yp_ref[req], wo_ref[:MIX_POOL, :], preferred_element_type=F32)
         + jnp.dot(y.astype(BF16), wo_ref[MIX_POOL:, :], preferred_element_type=F32))
    mod = mod_ref[0]
    x1 = x_ref[req] + mod[2:3] * o
    x1_ref[req] = x1
    h2 = x1 * lax.rsqrt(jnp.mean(x1 * x1, axis=-1, keepdims=True) + EPS) * (ng_ref[...] * (1.0 + mod[4:5])) + mod[3:4]
    h2_hi = h2.astype(BF16)
    h2_ref[req] = h2_hi
    h2_lo = (h2 - h2_hi.astype(F32)).astype(BF16)
    pa = jnp.dot(h2_hi, wr_ref[0], preferred_element_type=F32)
    pb = jnp.dot(h2_lo, wr_ref[1], preferred_element_type=F32)
    logits = pa + pltpu.roll(pa, LANES - N_EXPERTS, axis=1) + pb
    lane = lax.broadcasted_iota(jnp.int32, logits.shape, 1)
    lg = jnp.where(lane < N_EXPERTS, logits, -jnp.inf)
    e = jnp.exp(lg - jnp.max(lg, axis=-1, keepdims=True))
    aff = e / jnp.sum(e, axis=-1, keepdims=True)
    afft_ref[req] = aff.T[:N_EXPERTS, :]


def _out_kernel(*refs):
    for req in range(refs[0].shape[0]):
        _out_row(*refs, req=req)


def _out_proj(x, y_f, y_b, xs, z, y_pool, dsk, ssm_g, w_out, mod_l, mod_row, norm_ffn_g, w_router, tl,
              rows_per_step=1):
    bn, seq_len, _ = x.shape
    nt = seq_len // tl
    rps = rows_per_step
    const = lambda *shape: pl.BlockSpec(shape, lambda b, i: (0,) * len(shape))
    tok = lambda width: pl.BlockSpec((rps, tl, width), lambda b, i: (b, i, 0))
    return pl.pallas_call(
        _out_kernel,
        grid=(bn // rps, nt),
        in_specs=[tok(D_MODEL), tok(SSD_INNER), tok(SSD_INNER), tok(SSD_INNER), tok(SSD_INNER), tok(MIX_POOL),
                  const(1, SSD_INNER), const(1, SSD_INNER), const(D_MODEL, D_MODEL),
                  pl.BlockSpec((1, N_MOD, D_MODEL), lambda b, i: (mod_row(b * rps), 0, 0)),
                  const(1, D_MODEL), const(2, D_MODEL, LANES)],
        out_specs=[tok(D_MODEL), tok(D_MODEL), pl.BlockSpec((rps, N_EXPERTS, tl), lambda b, i: (b, 0, i))],
        out_shape=[jax.ShapeDtypeStruct((bn, seq_len, D_MODEL), F32),
                   jax.ShapeDtypeStruct((bn, seq_len, D_MODEL), BF16),
                   jax.ShapeDtypeStruct((bn, N_EXPERTS, seq_len), F32)],
        compiler_params=_params(("arbitrary", "arbitrary")),
        name="out_proj",
    )(x, y_f, y_b, xs, z, y_pool, dsk, ssm_g, w_out, mod_l, norm_ffn_g, w_router)


def _topk_kernel(afft_ref, pos_ref, post_ref, tab_ref, *, seq_len, cap, rb, tbs):
    n_rows = rb * N_EXPERTS
    a = afft_ref[...].reshape(n_rows, seq_len)

    def body(it, thr):
        cand_bits = thr | jnp.left_shift(jnp.int32(1), 30 - it)
        cnt = jnp.sum((a >= pltpu.bitcast(cand_bits, F32)).astype(jnp.int32), axis=1, keepdims=True)
        return jnp.where(cnt >= cap, cand_bits, thr)

    thr = lax.fori_loop(0, 31, body, jnp.zeros((n_rows, 1), jnp.int32))
    lo = pltpu.bitcast(thr, F32)
    hi = pltpu.bitcast(thr + 1, F32)
    gt = a >= hi
    eq = (a >= lo) & (a < hi)
    need = cap - jnp.sum(gt.astype(jnp.int32), axis=1, keepdims=True)

    r = lax.broadcasted_iota(jnp.int32, (LANES, LANES), 0)
    s = lax.broadcasted_iota(jnp.int32, (LANES, LANES), 1)
    strict_upper = (r < s).astype(BF16)
    nblk = seq_len // LANES

    def excl_cumsum(m):
        carry = jnp.zeros((n_rows, 1), F32)
        outs = []
        for blk in range(nblk):
            mb = m[:, blk * LANES:(blk + 1) * LANES]
            outs.append(jnp.dot(mb.astype(BF16), strict_upper, preferred_element_type=F32) + carry)
            carry = carry + jnp.sum(mb, axis=1, keepdims=True)
        return jnp.concatenate(outs, axis=1)

    eq_rank = excl_cumsum(jnp.where(eq, 1.0, 0.0))
    sel = gt | (eq & (eq_rank < need.astype(F32)))
    rank = excl_cumsum(jnp.where(sel, 1.0, 0.0))
    pos = jnp.where(sel, rank, -1.0)
    pos_ref[...] = pos.astype(jnp.int32).reshape(rb, N_EXPERTS, seq_len)

    lane = lax.broadcasted_iota(jnp.int32, (n_rows, LANES), 1)
    tab = jnp.full((n_rows, LANES), float(cap), F32)
    for k in range(seq_len // tbs):
        tab = jnp.where(lane == k, rank[:, k * tbs:k * tbs + 1], tab)
    tab_ref[...] = tab.astype(jnp.int32).reshape(rb, N_EXPERTS, LANES)

    pad = jnp.full((LANES - N_EXPERTS, LANES), -1.0, F32)
    for i in range(rb):
        for blk in range(nblk):
            tile = jnp.concatenate([pos[i * N_EXPERTS:(i + 1) * N_EXPERTS, blk * LANES:(blk + 1) * LANES], pad], axis=0)
            post_ref[i, blk * LANES:(blk + 1) * LANES, :] = tile.T.astype(jnp.int32)


def _topk(afft, cap, rb, tbs):
    bn, _, seq_len = afft.shape
    return pl.pallas_call(
        functools.partial(_topk_kernel, seq_len=seq_len, cap=cap, rb=rb, tbs=tbs),
        grid=(bn // rb,),
        in_specs=[pl.BlockSpec((rb, N_EXPERTS, seq_len), lambda b: (b, 0, 0))],
        out_specs=[pl.BlockSpec((rb, N_EXPERTS, seq_len), lambda b: (b, 0, 0)),
                   pl.BlockSpec((rb, seq_len, LANES), lambda b: (b, 0, 0)),
                   pl.BlockSpec((rb, N_EXPERTS, LANES), lambda b: (b, 0, 0))],
        out_shape=[jax.ShapeDtypeStruct((bn, N_EXPERTS, seq_len), jnp.int32),
                   jax.ShapeDtypeStruct((bn, seq_len, LANES), jnp.int32),
                   jax.ShapeDtypeStruct((bn, N_EXPERTS, LANES), jnp.int32)],
        compiler_params=_params(("arbitrary",)),
        name="topk",
    )(afft)


def _gather_kernel(h2_ref, pos_ref, afft_ref, xe_ref, gate_ref, *, cap):
    n_rows, seq_len, _ = h2_ref.shape
    n_slots = N_EXPERTS * cap
    rank = lax.broadcasted_iota(jnp.int32, (cap, seq_len), 0)
    for i in range(n_rows):
        onehots, gates = [], []
        for e in range(N_EXPERTS):
            hit = pos_ref[i, e:e + 1, :] == rank
            onehots.append(jnp.where(hit, 1.0, 0.0).astype(BF16))
            gates.append(jnp.sum(jnp.where(hit, afft_ref[i, e:e + 1, :], 0.0), axis=1, keepdims=True))
        xe_ref[i] = jnp.dot(jnp.concatenate(onehots, axis=0), h2_ref[i], preferred_element_type=F32).astype(BF16)
        gate_ref[i] = jnp.broadcast_to(jnp.concatenate(gates, axis=0), (n_slots, LANES))


def _gather(h2, pos, afft, cap, rows_per_step):
    bn, seq_len, _ = h2.shape
    n_slots = N_EXPERTS * cap
    rows = lambda *tail: pl.BlockSpec((rows_per_step,) + tail, lambda b: (b, 0, 0))
    return pl.pallas_call(
        functools.partial(_gather_kernel, cap=cap),
        grid=(bn // rows_per_step,),
        in_specs=[rows(seq_len, D_MODEL), rows(N_EXPERTS, seq_len), rows(N_EXPERTS, seq_len)],
        out_specs=[rows(n_slots, D_MODEL), rows(n_slots, LANES)],
        out_shape=[jax.ShapeDtypeStruct((bn, n_slots, D_MODEL), BF16),
                   jax.ShapeDtypeStruct((bn, n_slots, LANES), F32)],
        compiler_params=_params(("arbitrary",)),
        name="gather",
    )(h2, pos, afft)


def _window_starts(tab_ref, b, e0, n_e, t, cap, win):
    starts, overflows = [], []
    for ee in range(n_e):
        base = (b * N_EXPERTS + e0 + ee) * LANES + t
        p0 = tab_ref[base]
        p1 = tab_ref[base + 1]
        w0 = jnp.minimum(lax.shift_left(lax.shift_right_logical(p0, 4), 4), cap - win)
        starts.append(w0)
        overflows.append(p1 - w0 > win)
    return starts, overflows


def _gather_win_kernel(tab_ref, h2_ref, pos_ref, afft_ref, xe_ref, gate_ref, *, cap, win, n_e):
    b = pl.program_id(0)
    e0 = pl.program_id(1) * n_e
    t = pl.program_id(2)
    tbs = h2_ref.shape[1]

    @pl.when(t == 0)
    def _():
        xe_ref[...] = jnp.zeros(xe_ref.shape, BF16)
        gate_ref[...] = jnp.zeros(gate_ref.shape, F32)

    starts, overflows = _window_starts(tab_ref, b, e0, n_e, t, cap, win)

    def accumulate(row0, n, x, g):
        cur = xe_ref[0, pl.ds(row0, n), :]
        xe_ref[0, pl.ds(row0, n), :] = jnp.where(g > 0.0, x.astype(BF16), cur)
        gate_ref[0, pl.ds(row0, n), :] = gate_ref[0, pl.ds(row0, n), :] + jnp.broadcast_to(g, (n, LANES))

    def onehot_and_gate(ee, shift, n):
        sub = lax.broadcasted_iota(jnp.int32, (n, tbs), 0)
        hit = (pos_ref[0, pl.ds(e0 + ee, 1), :] - shift) == sub
        gate = jnp.sum(jnp.where(hit, afft_ref[0, pl.ds(e0 + ee, 1), :], 0.0), axis=1, keepdims=True)
        return jnp.where(hit, 1.0, 0.0).astype(BF16), gate

    pieces = [onehot_and_gate(ee, jnp.where(overflows[ee], cap, starts[ee]), win) for ee in range(n_e)]
    x = jnp.dot(jnp.concatenate([p[0] for p in pieces], axis=0), h2_ref[0], preferred_element_type=F32)
    for ee in range(n_e):
        accumulate(pl.multiple_of(ee * cap + starts[ee], 16), win, x[ee * win:(ee + 1) * win], pieces[ee][1])

    for ee in range(n_e):
        @pl.when(overflows[ee])
        def _(ee=ee):
            onehot, gate = onehot_and_gate(ee, 0, cap)
            accumulate(ee * cap, cap, jnp.dot(onehot, h2_ref[0], preferred_element_type=F32), gate)


def _gather_win(tab, h2, pos, afft, cap, win, tbs):
    bn, seq_len, _ = h2.shape
    n_slots = N_EXPERTS * cap
    n_e = N_EXPERTS // 2
    grid_spec = pltpu.PrefetchScalarGridSpec(
        num_scalar_prefetch=1,
        grid=(bn, N_EXPERTS // n_e, seq_len // tbs),
        in_specs=[pl.BlockSpec((1, tbs, D_MODEL), lambda b, g, t, tab: (b, t, 0)),
                  pl.BlockSpec((1, N_EXPERTS, tbs), lambda b, g, t, tab: (b, 0, t)),
                  pl.BlockSpec((1, N_EXPERTS, tbs), lambda b, g, t, tab: (b, 0, t))],
        out_specs=[pl.BlockSpec((1, n_e * cap, D_MODEL), lambda b, g, t, tab: (b, g, 0)),
                   pl.BlockSpec((1, n_e * cap, LANES), lambda b, g, t, tab: (b, g, 0))],
    )
    return pl.pallas_call(
        functools.partial(_gather_win_kernel, cap=cap, win=win, n_e=n_e),
        grid_spec=grid_spec,
        out_shape=[jax.ShapeDtypeStruct((bn, n_slots, D_MODEL), BF16),
                   jax.ShapeDtypeStruct((bn, n_slots, LANES), F32)],
        compiler_params=_params(("arbitrary", "arbitrary", "arbitrary")),
        name="gather_win",
    )(tab, h2, pos, afft)


def _ffn_kernel(*refs, tiles):
    n = len(tiles)
    xe_refs, gate_refs = refs[0:2 * n:2], refs[1:2 * n:2]
    wg_ref, wu_ref, wd_ref = refs[2 * n:2 * n + 3]
    ye_refs = refs[2 * n + 3:3 * n + 3]
    wg_s, wu_s, wd_s = refs[3 * n + 3:]
    j = pl.program_id(1)

    @pl.when(j == 0)
    def _():
        wg_s[...] = wg_ref[0, 0].astype(BF16)
        wu_s[...] = wu_ref[0, 0].astype(BF16)
        wd_s[...] = wd_ref[0, 0].astype(BF16)

    for (first, steps, rows), xe_ref, gate_ref, ye_ref in zip(tiles, xe_refs, gate_refs, ye_refs):
        @pl.when(jnp.logical_and(j >= first, j < first + steps))
        def _(rows=rows, xe_ref=xe_ref, gate_ref=gate_ref, ye_ref=ye_ref):
            x = xe_ref[...].reshape(rows, D_MODEL)
            g = jnp.dot(x, wg_s[...], preferred_element_type=F32)
            u = jnp.dot(x, wu_s[...], preferred_element_type=F32)
            hid = (g * _sigmoid(g) * u).astype(BF16)
            y = jnp.dot(hid, wd_s[...], preferred_element_type=F32) * gate_ref[...].reshape(rows, LANES)[:, :1]
            ye_ref[...] = y.astype(BF16).reshape(ye_ref.shape)


def _ffn(groups, w_gate, w_up, w_down, layer):
    tiles, in_specs, out_specs, out_shape, args = [], [], [], [], []
    first = 0
    for xe, gate, cap, bb in groups:
        steps = xe.shape[0] // bb
        tile = lambda e, j, first=first, steps=steps: (jnp.clip(j - first, 0, steps - 1), e, 0)
        in_specs += [pl.BlockSpec((bb, cap, D_MODEL), tile), pl.BlockSpec((bb, cap, LANES), tile)]
        out_specs.append(pl.BlockSpec((bb, cap, D_MODEL), tile))
        out_shape.append(jax.ShapeDtypeStruct(xe.shape, BF16))
        args += [xe, gate]
        tiles.append((first, steps, bb * cap))
        first += steps
    wspec = pl.BlockSpec((1, 1, D_MODEL, D_MODEL), lambda e, j: (layer, e, 0, 0))
    return pl.pallas_call(
        functools.partial(_ffn_kernel, tiles=tuple(tiles)),
        grid=(N_EXPERTS, first),
        in_specs=in_specs + [wspec, wspec, wspec],
        out_specs=out_specs,
        out_shape=out_shape,
        scratch_shapes=[pltpu.VMEM((D_MODEL, D_MODEL), BF16)] * 3,
        compiler_params=_params(("arbitrary", "arbitrary")),
        name="ffn",
    )(*args, w_gate, w_up, w_down)


def _finish(x1_ref, mod_ref, fg_ref, out_ref, acc, final, i=0):
    x2 = x1_ref[i] + mod_ref[0][5:6] * acc
    if final:
        x2 = x2 * lax.rsqrt(jnp.mean(x2 * x2, axis=-1, keepdims=True) + EPS) * fg_ref[...]
    out_ref[i] = x2


def _scatter_kernel(x1_ref, ye_ref, post_ref, mod_ref, fg_ref, out_ref, *, cap, final):
    n_rows = x1_ref.shape[0]
    n_slots = N_EXPERTS * cap
    slot = lax.broadcasted_iota(jnp.int32, (1, n_slots), 1)
    expert_of_slot = jnp.zeros((1, n_slots), jnp.int32)
    for e in range(1, N_EXPERTS):
        expert_of_slot = expert_of_slot + (slot >= e * cap).astype(jnp.int32)
    rank_of_slot = (slot - expert_of_slot * cap).astype(F32)
    spread = jnp.where(lax.broadcasted_iota(jnp.int32, (LANES, n_slots), 0) == expert_of_slot, 1.0, 0.0).astype(BF16)
    for i in range(n_rows):
        pt = post_ref[i]
        ranks = jnp.dot(pt.astype(F32).astype(BF16), spread, preferred_element_type=F32)
        onehot = jnp.where(ranks == rank_of_slot, 1.0, 0.0).astype(BF16)
        acc = jnp.dot(onehot, ye_ref[i], preferred_element_type=F32)
        _finish(x1_ref, mod_ref, fg_ref, out_ref, acc, final, i)


def _scatter(x1, ye, post, mod_l, mod_row, final_g, cap, final, rows_per_step):
    bn, seq_len, _ = x1.shape
    assert cap <= 256, "ranks must be exactly representable in bf16"
    n_slots = N_EXPERTS * cap
    rows = lambda *tail: pl.BlockSpec((rows_per_step,) + tail, lambda b: (b, 0, 0))
    return pl.pallas_call(
        functools.partial(_scatter_kernel, cap=cap, final=final),
        grid=(bn // rows_per_step,),
        in_specs=[rows(seq_len, D_MODEL), rows(n_slots, D_MODEL), rows(seq_len, LANES),
                  pl.BlockSpec((1, N_MOD, D_MODEL), lambda b: (mod_row(b * rows_per_step), 0, 0)),
                  pl.BlockSpec((1, D_MODEL), lambda b: (0, 0))],
        out_specs=rows(seq_len, D_MODEL),
        out_shape=jax.ShapeDtypeStruct((bn, seq_len, D_MODEL), F32),
        compiler_params=_params(("arbitrary",)),
        name="scatter",
    )(x1, ye, post, mod_l, final_g)


def _scatter_win_kernel(tab_ref, x1_ref, ye_ref, post_ref, mod_ref, fg_ref, out_ref, acc_ref, *, cap, win, final):
    b = pl.program_id(0)
    t = pl.program_id(1)
    tbs = x1_ref.shape[1]
    pt = post_ref[0]
    starts, overflows = _window_starts(tab_ref, b, 0, N_EXPERTS, t, cap, win)

    def onehot(e, shift, n):
        lane = lax.broadcasted_iota(jnp.int32, (tbs, n), 1)
        return jnp.where((pt[:, e:e + 1] - shift) == lane, 1.0, 0.0).astype(BF16)

    acc = None
    for e0 in range(0, N_EXPERTS, EXPERTS_PER_DOT):
        group = range(e0, e0 + EXPERTS_PER_DOT)
        onehots = [onehot(e, jnp.where(overflows[e], cap, starts[e]), win) for e in group]
        rows = [ye_ref[0, pl.ds(pl.multiple_of(e * cap + starts[e], 16), win), :] for e in group]
        part = jnp.dot(jnp.concatenate(onehots, axis=1), jnp.concatenate(rows, axis=0), preferred_element_type=F32)
        acc = part if acc is None else acc + part
    acc_ref[...] = acc

    for e in range(N_EXPERTS):
        @pl.when(overflows[e])
        def _(e=e):
            acc_ref[...] += jnp.dot(onehot(e, 0, cap), ye_ref[0, e * cap:(e + 1) * cap, :],
                                    preferred_element_type=F32)

    _finish(x1_ref, mod_ref, fg_ref, out_ref, acc_ref[...], final)


def _scatter_win(tab, x1, ye, post, mod_l, mod_row, final_g, cap, win, tbs, final):
    bn, seq_len, _ = x1.shape
    n_slots = N_EXPERTS * cap
    grid_spec = pltpu.PrefetchScalarGridSpec(
        num_scalar_prefetch=1,
        grid=(bn, seq_len // tbs),
        in_specs=[pl.BlockSpec((1, tbs, D_MODEL), lambda b, t, tab: (b, t, 0)),
                  pl.BlockSpec((1, n_slots, D_MODEL), lambda b, t, tab: (b, 0, 0)),
                  pl.BlockSpec((1, tbs, LANES), lambda b, t, tab: (b, t, 0)),
                  pl.BlockSpec((1, N_MOD, D_MODEL), lambda b, t, tab: (mod_row(b), 0, 0)),
                  pl.BlockSpec((1, D_MODEL), lambda b, t, tab: (0, 0))],
        out_specs=pl.BlockSpec((1, tbs, D_MODEL), lambda b, t, tab: (b, t, 0)),
        scratch_shapes=[pltpu.VMEM((tbs, D_MODEL), F32)],
    )
    return pl.pallas_call(
        functools.partial(_scatter_win_kernel, cap=cap, win=win, final=final),
        grid_spec=grid_spec,
        out_shape=jax.ShapeDtypeStruct((bn, seq_len, D_MODEL), F32),
        compiler_params=_params(("arbitrary", "arbitrary")),
        name="scatter_win",
    )(tab, x1, ye, post, mod_l, final_g)


def _grid_position_table(n_tokens, dim):
    assert n_tokens // GRID_W <= GRID_W
    quarter = dim // 4
    inv_freq = jnp.power(POS_BASE, -jnp.arange(quarter, dtype=F32) / quarter)
    angle = jnp.arange(GRID_W).astype(F32)[:, None] * inv_freq[None]
    return jnp.concatenate([jnp.sin(angle), jnp.cos(angle)], axis=-1)


def _pad_cols(w, width):
    return jnp.pad(w, ((0, 0), (0, width - w.shape[1])))


def _router_pieces(w):
    w_hi = w.astype(BF16)
    w_lo = (w - w_hi.astype(F32)).astype(BF16)
    return jnp.stack([_pad_cols(jnp.concatenate([w_hi, w_lo], axis=1), LANES), _pad_cols(w_hi, LANES)])


def _run_groups(groups, mod, lw, final_g):
    xs_res = [g["x"] for g in groups]
    states = [([], []) for _ in groups]
    for l in range(DEPTH):
        w = lw[l]
        final = l == DEPTH - 1
        mids = []
        for gi, g in enumerate(groups):
            x = xs_res[gi]
            bn, seq_len, _ = x.shape
            cap = CAPACITY_FACTOR * seq_len // N_EXPERTS
            windowed = seq_len >= 1024
            tbs = max(LANES, seq_len // 8) if windowed else seq_len
            win = cap // 4
            mod_row, tl = g["mod_row"], g["tl"]
            pos_table = g.get("pos_table") if l == 0 else None
            outs = _in_proj(x, mod[l], mod_row, w["norm_mix_g"], w["w_main"], w["w_dt"], w["pool_w"],
                            w["pool_scale"], w["conv_w"], w["conv_b"], w["alog_row"], w["bias_row"], tl, pos_table,
                            g.get("in_proj_rows", 1))
            y_pool, z, xs, bc, dt, cum = outs[:6]
            if pos_table is not None:
                x = outs[6]
            y_f, y_b, hT_f, hT_b = _ssd(xs, bc, dt, cum, g["h0f"][l], g["h0b"][l], g["h0_row"])
            states[gi][0].append(hT_f)
            states[gi][1].append(hT_b)
            x1, h2, afft = _out_proj(x, y_f, y_b, xs, z, y_pool, w["dsk"], w["ssm_norm_g"], w["w_out"], mod[l],
                                     mod_row, w["norm_ffn_g"], w["w_router"], tl, g.get("in_proj_rows", 1))
            pos, post, tab = _topk(afft, cap, g["topk_rows"], tbs)
            if windowed:
                tab = tab.reshape(-1)
                xe, gate = _gather_win(tab, h2, pos, afft, cap, win, tbs)
            else:
                xe, gate = _gather(h2, pos, afft, cap, g.get("dense_rows", 1))
            mids.append(dict(x1=x1, post=post, tab=tab, xe=xe, gate=gate, cap=cap, win=win, tbs=tbs,
                             windowed=windowed))
        yes = _ffn([(m["xe"], m["gate"], m["cap"], g["ffn_rows_per_step"] // m["cap"])
                    for m, g in zip(mids, groups)], w["w_gate"], w["w_up"], w["w_down"], l)
        for gi, (g, m, ye) in enumerate(zip(groups, mids, yes)):
            if m["windowed"]:
                xs_res[gi] = _scatter_win(m["tab"], m["x1"], ye, m["post"], mod[l], g["mod_row"], final_g, m["cap"],
                                          m["win"], m["tbs"], final)
            else:
                xs_res[gi] = _scatter(m["x1"], ye, m["post"], mod[l], g["mod_row"], final_g, m["cap"], final,
                                      g.get("dense_rows", 1))
    return [(x, sf, sb) for x, (sf, sb) in zip(xs_res, states)]


def kernel(x_prompt, x_sample, state_ssm_fwd, state_ssm_bwd, c, c_ctx, norm_mix_g, w_ada, b_ada, w_in, pool_w,
           pool_scale, conv_w, conv_b, a_log_fwd, a_log_bwd, dt_bias_fwd, dt_bias_bwd, d_skip, ssm_norm_g, w_out,
           norm_ffn_g, w_router, w_gate, w_up, w_down, final_norm_g):
    n_dec = c.shape[0]
    n_ctx = x_prompt.shape[0]
    ctx_row = n_dec
    cond = jnp.concatenate([c, c_ctx[None, :], jnp.zeros((8 - n_dec - 1, D_MODEL), F32)], axis=0)
    mod = _modulation(cond, w_ada, b_ada).reshape(DEPTH, 8, N_MOD, D_MODEL)

    zeros_h = jnp.zeros((LANES - 2 * SSD_HEADS,), F32)
    lw = []
    for l in range(DEPTH):
        lw.append(dict(
            norm_mix_g=norm_mix_g[l][None], w_main=w_in[l][:, :N_MAIN].astype(BF16),
            w_dt=_pad_cols(w_in[l][:, N_MAIN:], LANES).astype(BF16), pool_w=pool_w[l].astype(BF16),
            pool_scale=pool_scale[l][None], conv_w=conv_w[l], conv_b=conv_b[l][None],
            alog_row=jnp.concatenate([a_log_fwd[l], a_log_bwd[l], zeros_h])[None],
            bias_row=jnp.concatenate([dt_bias_fwd[l], dt_bias_bwd[l], zeros_h])[None],
            dsk=jnp.repeat(d_skip[l], SSD_INNER // SSD_HEADS)[None], ssm_norm_g=ssm_norm_g[l][None],
            w_out=w_out[l].astype(BF16), norm_ffn_g=norm_ffn_g[l][None], w_router=_router_pieces(w_router[l]),
            w_gate=w_gate, w_up=w_up, w_down=w_down))
    final_g = final_norm_g[None]
    st_shape = (HEAD_PAIRS, LANES, SSD_STATE)

    zero_state = jnp.zeros((1,) + st_shape, F32)
    prompt = dict(x=x_prompt, mod_row=lambda b: ctx_row, h0f=[zero_state] * DEPTH, h0b=[zero_state] * DEPTH,
                  h0_row=lambda b: 0, tl=x_prompt.shape[1], ffn_rows_per_step=1024, topk_rows=8, dense_rows=4,
                  in_proj_rows=4)
    sample = dict(x=x_sample, mod_row=lambda b: b, h0_row=lambda b: b, tl=1024, ffn_rows_per_step=1024,
                  topk_rows=n_dec, pos_table=_grid_position_table(x_sample.shape[1], D_MODEL),
                  h0f=[state_ssm_fwd[:, l].reshape((n_dec,) + st_shape) for l in range(DEPTH)],
                  h0b=[state_ssm_bwd[:, l].reshape((n_dec,) + st_shape) for l in range(DEPTH)])
    (y_prompt, sf, sb), (y_sample, _, _) = _run_groups([prompt, sample], mod, lw, final_g)
    out_state_shape = (n_ctx, SSD_HEADS, SSD_INNER // SSD_HEADS, SSD_STATE)
    new_f = jnp.stack([s.reshape(out_state_shape) for s in sf], axis=1)
    new_b = jnp.stack([s.reshape(out_state_shape) for s in sb], axis=1)
    return (y_prompt, y_sample, new_f, new_b)
```

```python
import functools

import jax
import jax.numpy as jnp
from jax import lax
from jax.experimental import pallas as pl
from jax.experimental.pallas import tpu as pltpu

D_MODEL = 1024
DEPTH = 2
GRID_W = 64
MIX_POOL = 512
POOL_WINDOWS = (2, 4, 8, 16)
POOL_GROUP_CH = 128
SSD_INNER = 512
SSD_HEADS = 8
SSD_HEAD_DIM = SSD_INNER // SSD_HEADS
SSD_STATE = 128
CONV_K = 5
CHUNK = 128
CONV_DIM = 1024
N_EXPERTS = 16
CAPACITY_FACTOR = 2
N_MOD = 6
POS_BASE = 10000.0
EPS = 1e-6
LOG2_E = 1.4426950408889634

SUBLANES = 8
LANES = 128
PACKED_ROWS = 16
HALO = SUBLANES
N_MAIN = MIX_POOL + SSD_INNER + CONV_DIM
HEAD_PAIRS = SSD_HEADS // 2
EXPERTS_PER_DOT = 2
SSD_MAX_CHUNKS_PER_STEP = 4
MOD_COLS_PER_STEP = 1536
WINDOWED_MIN_SEQ = 1024
TOKEN_BLOCKS = 8
WINDOW_DIVISOR = 4
GATHER_EXPERT_GROUPS = 2
FLOAT_MAGNITUDE_BITS = 31
VMEM_LIMIT = 56 * 1024 * 1024

F32 = jnp.float32
BF16 = jnp.bfloat16
HI = lax.Precision.HIGHEST


def _sigmoid(x):
    return 1.0 / (1.0 + jnp.exp(-x))


def _params(sem):
    return pltpu.CompilerParams(dimension_semantics=sem, vmem_limit_bytes=VMEM_LIMIT)


def _mod_kernel(cond_ref, w_ref, b_ref, out_ref):
    c = cond_ref[...]
    s = c * _sigmoid(c)
    out_ref[0] = jnp.dot(s, w_ref[0], precision=HI, preferred_element_type=F32) + b_ref[0]


def _modulation(cond, w_ada, b_ada):
    n_rows = cond.shape[0]
    n_out = N_MOD * D_MODEL
    tn = MOD_COLS_PER_STEP
    return pl.pallas_call(
        _mod_kernel,
        grid=(DEPTH, n_out // tn),
        in_specs=[
            pl.BlockSpec((n_rows, D_MODEL), lambda l, j: (0, 0)),
            pl.BlockSpec((1, D_MODEL, tn), lambda l, j: (l, 0, j)),
            pl.BlockSpec((1, 1, tn), lambda l, j: (l, 0, j)),
        ],
        out_specs=pl.BlockSpec((1, n_rows, tn), lambda l, j: (l, 0, j)),
        out_shape=jax.ShapeDtypeStruct((DEPTH, n_rows, n_out), F32),
        compiler_params=_params(("arbitrary", "arbitrary")),
        name="modulation",
    )(cond, w_ada, b_ada.reshape(DEPTH, 1, n_out))


def _cumsum_pieces(dta):
    d_hi = dta.astype(BF16)
    rest = dta - d_hi.astype(F32)
    d_mid = rest.astype(BF16)
    return d_hi, d_mid, (rest - d_mid.astype(F32)).astype(BF16)


def _grid_position_rows(table_ref, grid_row, col0, n_cols):
    row_part = jnp.broadcast_to(table_ref[pl.ds(grid_row, 1), :], (n_cols, D_MODEL // 2))
    return jnp.concatenate([row_part, table_ref[col0:col0 + n_cols, :]], axis=1)


def _in_row(*refs, tl, seq_len, with_pos, req):
    (xp_ref, xc_ref, xn_ref, mod_ref, g_ref, wmain_ref, wdt_ref, poolw_ref, pscale_ref, cw_ref, cb_ref,
     alog_ref, bias_ref) = refs[:13]
    table_ref = refs[13] if with_pos else None
    ypool_ref, z_ref, xs_ref, bc_ref, dt_ref, cum_ref = refs[13 + with_pos:19 + with_pos]
    x0_ref = refs[19 + with_pos] if with_pos else None
    proj_scr = refs[-1].at[req]
    i = pl.program_id(1)
    n = pl.num_programs(1)
    rows = tl + 2 * HALO
    x_ext = jnp.concatenate([xp_ref[0], xc_ref[req], xn_ref[0]], axis=0)
    if with_pos:
        r0 = i * (tl // GRID_W)
        last_row = seq_len // GRID_W - 1
        pieces = [_grid_position_rows(table_ref, jnp.maximum(r0 - 1, 0), GRID_W - HALO, HALO)]
        pieces += [_grid_position_rows(table_ref, r0 + k, 0, GRID_W) for k in range(tl // GRID_W)]
        pieces += [_grid_position_rows(table_ref, jnp.minimum(r0 + tl // GRID_W, last_row), 0, HALO)]
        x_ext = x_ext + jnp.concatenate(pieces, axis=0)
        x0_ref[req] = x_ext[HALO:HALO + tl]
    mod = mod_ref[0]
    ms = jnp.mean(x_ext * x_ext, axis=-1, keepdims=True)
    h = x_ext * lax.rsqrt(ms + EPS) * (g_ref[...] * (1.0 + mod[1:2])) + mod[0:1]
    row = lax.broadcasted_iota(jnp.int32, (rows, 1), 0)
    lo_ok = jnp.where(i > 0, 0, HALO)
    hi_ok = jnp.where(i < n - 1, rows, tl + HALO)
    hb = jnp.where((row >= lo_ok) & (row < hi_ok), h, 0.0).astype(BF16)
    proj_scr[...] = jnp.dot(hb, wmain_ref[...], preferred_element_type=F32)
    dt_full = jnp.dot(hb, wdt_ref[...], preferred_element_type=F32)

    x_in = dt_full[HALO:HALO + tl] + bias_ref[...]
    dt = jnp.maximum(x_in, 0.0) + jnp.log1p(jnp.exp(-jnp.abs(x_in)))
    dt_ref[req] = dt
    r = lax.broadcasted_iota(jnp.int32, (CHUNK, CHUNK), 0)
    s = lax.broadcasted_iota(jnp.int32, (CHUNK, CHUNK), 1)
    tri = jnp.concatenate([jnp.where(r >= s, 1.0, 0.0), jnp.where(r <= s, 1.0, 0.0)], axis=1).astype(BF16)
    head = lax.broadcasted_iota(jnp.int32, (tl, LANES), 1)
    dta = jnp.where(head < 2 * SSD_HEADS, dt * (-LOG2_E * jnp.exp(alog_ref[...])), 0.0)
    pieces = _cumsum_pieces(dta)
    fwd_cols = s < SSD_HEADS
    for k in range(tl // CHUNK):
        acc = jnp.zeros((CHUNK, LANES), F32)
        for piece in pieces:
            pk = piece[k * CHUNK:(k + 1) * CHUNK]
            zero = jnp.zeros_like(pk)
            stacked = jnp.concatenate([jnp.where(fwd_cols, pk, zero), jnp.where(fwd_cols, zero, pk)], axis=0)
            acc = acc + jnp.dot(tri, stacked, preferred_element_type=F32)
        cum_ref[req, k * CHUNK:(k + 1) * CHUNK, :] = acc

    def ahead(v, k):
        return pltpu.roll(v, (-k) % rows, axis=0)

    t_glob = i * tl + lax.broadcasted_iota(jnp.int32, (tl, 1), 0)
    outs = []
    for gi, w in enumerate(POOL_WINDOWS):
        c0 = gi * POOL_GROUP_CH
        u = proj_scr[:, c0:c0 + POOL_GROUP_CH]
        acc = u
        span = 1
        while 2 * span < w:
            acc = acc + ahead(acc, span)
            span *= 2
        acc = acc + ahead(acc, -span)
        lo = jnp.maximum(t_glob - w // 2, 0)
        hi = jnp.minimum(t_glob - w // 2 + w, seq_len)
        cnt = (hi - lo).astype(F32)
        p = acc[HALO:HALO + tl] / cnt - u[HALO:HALO + tl]
        outs.append(jnp.dot(p.astype(BF16), poolw_ref[gi], preferred_element_type=F32))
    ypool_ref[req] = (jnp.concatenate(outs, axis=1) * pscale_ref[...]).astype(BF16)

    c0 = MIX_POOL + SSD_INNER
    acc = cb_ref[...] + cw_ref[0:1, :] * proj_scr[HALO - 2:HALO - 2 + tl, c0:c0 + CONV_DIM]
    for k in range(1, CONV_K):
        acc = acc + cw_ref[k:k + 1, :] * proj_scr[HALO - 2 + k:HALO - 2 + k + tl, c0:c0 + CONV_DIM]
    act = acc * _sigmoid(acc)
    xs_ref[req] = act[:, :SSD_INNER].astype(BF16)
    bc_ref[req] = act[:, SSD_INNER:].astype(BF16)
    z_ref[req] = proj_scr[HALO:HALO + tl, MIX_POOL:MIX_POOL + SSD_INNER]


def _in_kernel(*refs, tl, seq_len, with_pos):
    for req in range(refs[1].shape[0]):
        _in_row(*refs, tl=tl, seq_len=seq_len, with_pos=with_pos, req=req)


def _in_proj(x, mod_l, mod_row, norm_g, w_main, w_dt, pool_w, pool_scale, conv_w, conv_b, alog_row, bias_row, tl,
             pos_table=None, rows_per_step=1):
    bn, seq_len, _ = x.shape
    nt = seq_len // tl
    rps = rows_per_step
    assert rps == 1 or nt == 1
    hb = tl // HALO
    last_halo = seq_len // HALO - 1
    with_pos = pos_table is not None
    const = lambda *shape: pl.BlockSpec(shape, lambda b, i: (0,) * len(shape))
    tok = lambda width: pl.BlockSpec((rps, tl, width), lambda b, i: (b, i, 0))
    in_specs = [
        pl.BlockSpec((1, HALO, D_MODEL), lambda b, i: (b * rps, jnp.maximum(i * hb - 1, 0), 0)),
        tok(D_MODEL),
        pl.BlockSpec((1, HALO, D_MODEL), lambda b, i: (b * rps, jnp.minimum((i + 1) * hb, last_halo), 0)),
        pl.BlockSpec((1, N_MOD, D_MODEL), lambda b, i: (mod_row(b * rps), 0, 0)),
        const(1, D_MODEL),
        const(D_MODEL, N_MAIN),
        const(D_MODEL, LANES),
        const(len(POOL_WINDOWS), POOL_GROUP_CH, POOL_GROUP_CH),
        const(1, MIX_POOL),
        const(CONV_K, CONV_DIM),
        const(1, CONV_DIM),
        const(1, LANES),
        const(1, LANES),
    ]
    out_specs = [tok(MIX_POOL), tok(SSD_INNER), tok(SSD_INNER), tok(CONV_DIM - SSD_INNER), tok(LANES), tok(LANES)]
    out_shape = [
        jax.ShapeDtypeStruct((bn, seq_len, MIX_POOL), BF16),
        jax.ShapeDtypeStruct((bn, seq_len, SSD_INNER), F32),
        jax.ShapeDtypeStruct((bn, seq_len, SSD_INNER), BF16),
        jax.ShapeDtypeStruct((bn, seq_len, CONV_DIM - SSD_INNER), BF16),
        jax.ShapeDtypeStruct((bn, seq_len, LANES), F32),
        jax.ShapeDtypeStruct((bn, seq_len, LANES), F32),
    ]
    args = [x, x, x, mod_l, norm_g, w_main, w_dt, pool_w, pool_scale, conv_w, conv_b, alog_row, bias_row]
    if with_pos:
        assert tl % GRID_W == 0 and pos_table.shape == (GRID_W, D_MODEL // 2)
        in_specs.append(const(GRID_W, D_MODEL // 2))
        out_specs.append(tok(D_MODEL))
        out_shape.append(jax.ShapeDtypeStruct((bn, seq_len, D_MODEL), F32))
        args.append(pos_table)
    return pl.pallas_call(
        functools.partial(_in_kernel, tl=tl, seq_len=seq_len, with_pos=with_pos),
        grid=(bn // rps, nt),
        in_specs=in_specs,
        out_specs=out_specs,
        out_shape=out_shape,
        scratch_shapes=[pltpu.VMEM((rps, tl + 2 * HALO, N_MAIN), F32)],
        compiler_params=_params(("arbitrary", "arbitrary")),
        name="in_proj",
    )(*args)


def _ssd_chunk(xsf_ref, bcf_ref, dtf_ref, cumf_ref, xsb_ref, bcb_ref, dtb_ref, cumb_ref, yf_ref, yb_ref, st_ref, rf, rb):
    r = lax.broadcasted_iota(jnp.int32, (CHUNK, CHUNK), 0)
    s = lax.broadcasted_iota(jnp.int32, (CHUNK, CHUNK), 1)
    lane2 = lax.broadcasted_iota(jnp.int32, (CHUNK, 2 * LANES), 1)
    lane_head = lax.shift_right_logical(lane2, SSD_HEAD_DIM.bit_length() - 1)
    row2 = lax.broadcasted_iota(jnp.int32, (2 * LANES, SSD_STATE), 0)
    is_fwd = s < SSD_HEADS
    dt = jnp.where(is_fwd, dtf_ref[0, rf:rf + CHUNK, :], dtb_ref[0, rb:rb + CHUNK, :])
    cum = jnp.where(is_fwd, cumf_ref[0, rf:rf + CHUNK, :], cumb_ref[0, rb:rb + CHUNK, :])
    tot = jnp.where(lax.broadcasted_iota(jnp.int32, (1, LANES), 1) < SSD_HEADS,
                    cum[CHUNK - 1:CHUNK, :], cum[0:1, :])
    cdec = jnp.exp2(tot)
    cum_t = cum.T
    dt_t = dt.T
    ecum_t = jnp.exp2(cum_t)
    w_t = (dt * jnp.exp2(tot - cum)).T

    def head_rows(arr_t, h0):
        return jnp.concatenate([jnp.broadcast_to(arr_t[h0 + k:h0 + k + 1, :], (SSD_HEAD_DIM, CHUNK))
                                for k in range(4)], axis=0)

    def one_dir(d, xs_ref, bc_ref, y_ref, r0):
        mask = (r >= s) if d == 0 else (r <= s)
        off = d * SSD_HEADS
        bcv = bc_ref[0, r0:r0 + CHUNK, :]
        b01 = bcv[:, :2 * SSD_STATE].astype(BF16)
        c01 = bcv[:, 2 * SSD_STATE:].astype(BF16)
        zero2 = jnp.zeros_like(b01)
        b_blk = jnp.concatenate([jnp.where(lane2 < SSD_STATE, b01, zero2),
                                 jnp.where(lane2 < SSD_STATE, zero2, b01)], axis=0)
        cb01 = lax.dot_general(c01, b_blk, (((1,), (1,)), ((), ())), preferred_element_type=F32)
        for g in range(2):
            h0 = off + 4 * g
            b_g = b01[:, g * SSD_STATE:(g + 1) * SSD_STATE]
            c_g = c01[:, g * SSD_STATE:(g + 1) * SSD_STATE]
            cb = cb01[:, g * CHUNK:(g + 1) * CHUNK]
            xs_g = xs_ref[0, r0:r0 + CHUNK, g * 2 * LANES:(g + 1) * 2 * LANES]
            x_b = xs_g.astype(BF16)
            ms, x_parts = [], []
            for k in range(4):
                hh = h0 + k
                seg = cum[:, hh:hh + 1] - cum_t[hh:hh + 1, :]
                dec = jnp.exp2(jnp.where(mask, seg, -jnp.inf))
                ms.append((cb * dec * dt_t[hh:hh + 1, :]).astype(BF16))
                x_parts.append(jnp.where(lane_head == k, x_b, jnp.zeros_like(x_b)))
            y_diag = jnp.dot(jnp.concatenate(ms, axis=1), jnp.concatenate(x_parts, axis=0),
                             preferred_element_type=F32)
            st = st_ref[d, 2 * g:2 * g + 2].reshape(2 * LANES, SSD_STATE)
            y_off_t = lax.dot_general(st.astype(BF16), c_g, (((1,), (1,)), ((), ())),
                                      preferred_element_type=F32) * head_rows(ecum_t, h0)
            y_ref[0, r0:r0 + CHUNK, g * 2 * LANES:(g + 1) * 2 * LANES] = (y_diag + y_off_t.T).astype(BF16)
            x_d = (xs_g.astype(F32).T * head_rows(w_t, h0)).astype(BF16)
            cs = jnp.dot(x_d, b_g, preferred_element_type=F32)
            dcol = cdec[:, h0 + 3:h0 + 4]
            for k in (2, 1, 0):
                dcol = jnp.where(row2 < (k + 1) * SSD_HEAD_DIM, cdec[:, h0 + k:h0 + k + 1], dcol)
            st_ref[d, 2 * g:2 * g + 2] = (st * dcol + cs).reshape(2, LANES, SSD_STATE)

    one_dir(0, xsf_ref, bcf_ref, yf_ref, rf)
    one_dir(1, xsb_ref, bcb_ref, yb_ref, rb)


def _ssd_kernel(xsf_ref, bcf_ref, dtf_ref, cumf_ref, xsb_ref, bcb_ref, dtb_ref, cumb_ref, h0f_ref, h0b_ref,
                yf_ref, yb_ref, hTf_ref, hTb_ref, st_ref):
    c = pl.program_id(1)
    nc = pl.num_programs(1)

    @pl.when(c == 0)
    def _():
        st_ref[0] = h0f_ref[0]
        st_ref[1] = h0b_ref[0]

    n_chunks = xsf_ref.shape[1] // CHUNK
    for k in range(n_chunks):
        _ssd_chunk(xsf_ref, bcf_ref, dtf_ref, cumf_ref, xsb_ref, bcb_ref, dtb_ref, cumb_ref, yf_ref, yb_ref, st_ref,
                   k * CHUNK, (n_chunks - 1 - k) * CHUNK)

    @pl.when(c == nc - 1)
    def _():
        hTf_ref[0] = st_ref[0]
        hTb_ref[0] = st_ref[1]


def _ssd(xs, bc, dt, cum, h0f, h0b, h0_row):
    bn, seq_len, _ = xs.shape
    step_rows = min(SSD_MAX_CHUNKS_PER_STEP * CHUNK, seq_len)
    nc = seq_len // step_rows
    fwd = lambda width: pl.BlockSpec((1, step_rows, width), lambda b, c: (b, c, 0))
    bwd = lambda width: pl.BlockSpec((1, step_rows, width), lambda b, c: (b, nc - 1 - c, 0))
    st_spec_in = pl.BlockSpec((1, HEAD_PAIRS, LANES, SSD_STATE), lambda b, c: (h0_row(b), 0, 0, 0))
    st_spec_out = pl.BlockSpec((1, HEAD_PAIRS, LANES, SSD_STATE), lambda b, c: (b, 0, 0, 0))
    st_shape = jax.ShapeDtypeStruct((bn, HEAD_PAIRS, LANES, SSD_STATE), F32)
    return pl.pallas_call(
        _ssd_kernel,
        grid=(bn, nc),
        in_specs=[fwd(SSD_INNER), fwd(CONV_DIM - SSD_INNER), fwd(LANES), fwd(LANES),
                  bwd(SSD_INNER), bwd(CONV_DIM - SSD_INNER), bwd(LANES), bwd(LANES),
                  st_spec_in, st_spec_in],
        out_specs=[fwd(SSD_INNER), bwd(SSD_INNER), st_spec_out, st_spec_out],
        out_shape=[jax.ShapeDtypeStruct((bn, seq_len, SSD_INNER), BF16),
                   jax.ShapeDtypeStruct((bn, seq_len, SSD_INNER), BF16), st_shape, st_shape],
        scratch_shapes=[pltpu.VMEM((2, HEAD_PAIRS, LANES, SSD_STATE), F32)],
        compiler_params=_params(("arbitrary", "arbitrary")),
        name="ssd",
    )(xs, bc, dt, cum, xs, bc, dt, cum, h0f, h0b)


def _out_row(x_ref, yf_ref, yb_ref, xs_ref, z_ref, yp_ref, dsk_ref, sg_ref, wo_ref, mod_ref, ng_ref, wr_ref,
             x1_ref, h2_ref, afft_ref, req):
    y = yf_ref[req].astype(F32) + yb_ref[req].astype(F32) + dsk_ref[...] * xs_ref[req].astype(F32)
    z = z_ref[req]
    y = y * (z * _sigmoid(z))
    y = y * lax.rsqrt(jnp.mean(y * y, axis=-1, keepdims=True) + EPS) * sg_ref[...]
    o = (jnp.dot(yp_ref[req], wo_ref[:MIX_POOL, :], preferred_element_type=F32)
         + jnp.dot(y.astype(BF16), wo_ref[MIX_POOL:, :], preferred_element_type=F32))
    mod = mod_ref[0]
    x1 = x_ref[req] + mod[2:3] * o
    x1_ref[req] = x1
    h2 = x1 * lax.rsqrt(jnp.mean(x1 * x1, axis=-1, keepdims=True) + EPS) * (ng_ref[...] * (1.0 + mod[4:5])) + mod[3:4]
    h2_hi = h2.astype(BF16)
    h2_ref[req] = h2_hi
    h2_lo = (h2 - h2_hi.astype(F32)).astype(BF16)
    pa = jnp.dot(h2_hi, wr_ref[0], preferred_element_type=F32)
    pb = jnp.dot(h2_lo, wr_ref[1], preferred_element_type=F32)
    logits = pa + pltpu.roll(pa, LANES - N_EXPERTS, axis=1) + pb
    lane = lax.broadcasted_iota(jnp.int32, logits.shape, 1)
    lg = jnp.where(lane < N_EXPERTS, logits, -jnp.inf)
    e = jnp.exp(lg - jnp.max(lg, axis=-1, keepdims=True))
    aff = e / jnp.sum(e, axis=-1, keepdims=True)
    afft_ref[req] = aff.T[:N_EXPERTS, :]


def _out_kernel(*refs):
    for req in range(refs[0].shape[0]):
        _out_row(*refs, req=req)


def _out_proj(x, y_f, y_b, xs, z, y_pool, dsk, ssm_g, w_out, mod_l, mod_row, norm_ffn_g, w_router, tl,
              rows_per_step=1):
    bn, seq_len, _ = x.shape
    nt = seq_len // tl
    rps = rows_per_step
    const = lambda *shape: pl.BlockSpec(shape, lambda b, i: (0,) * len(shape))
    tok = lambda width: pl.BlockSpec((rps, tl, width), lambda b, i: (b, i, 0))
    return pl.pallas_call(
        _out_kernel,
        grid=(bn // rps, nt),
        in_specs=[tok(D_MODEL), tok(SSD_INNER), tok(SSD_INNER), tok(SSD_INNER), tok(SSD_INNER), tok(MIX_POOL),
                  const(1, SSD_INNER), const(1, SSD_INNER), const(D_MODEL, D_MODEL),
                  pl.BlockSpec((1, N_MOD, D_MODEL), lambda b, i: (mod_row(b * rps), 0, 0)),
                  const(1, D_MODEL), const(2, D_MODEL, LANES)],
        out_specs=[tok(D_MODEL), tok(D_MODEL), pl.BlockSpec((rps, N_EXPERTS, tl), lambda b, i: (b, 0, i))],
        out_shape=[jax.ShapeDtypeStruct((bn, seq_len, D_MODEL), F32),
                   jax.ShapeDtypeStruct((bn, seq_len, D_MODEL), BF16),
                   jax.ShapeDtypeStruct((bn, N_EXPERTS, seq_len), F32)],
        compiler_params=_params(("arbitrary", "arbitrary")),
        name="out_proj",
    )(x, y_f, y_b, xs, z, y_pool, dsk, ssm_g, w_out, mod_l, norm_ffn_g, w_router)


def _topk_kernel(afft_ref, pos_ref, post_ref, tab_ref, *, seq_len, cap, rb, tbs):
    n_rows = rb * N_EXPERTS
    a = afft_ref[...].reshape(n_rows, seq_len)

    def body(it, thr):
        cand_bits = thr | jnp.left_shift(jnp.int32(1), FLOAT_MAGNITUDE_BITS - 1 - it)
        cnt = jnp.sum((a >= pltpu.bitcast(cand_bits, F32)).astype(jnp.int32), axis=1, keepdims=True)
        return jnp.where(cnt >= cap, cand_bits, thr)

    thr = lax.fori_loop(0, FLOAT_MAGNITUDE_BITS, body, jnp.zeros((n_rows, 1), jnp.int32))
    lo = pltpu.bitcast(thr, F32)
    hi = pltpu.bitcast(thr + 1, F32)
    gt = a >= hi
    eq = (a >= lo) & (a < hi)
    need = cap - jnp.sum(gt.astype(jnp.int32), axis=1, keepdims=True)

    r = lax.broadcasted_iota(jnp.int32, (LANES, LANES), 0)
    s = lax.broadcasted_iota(jnp.int32, (LANES, LANES), 1)
    strict_upper = (r < s).astype(BF16)
    nblk = seq_len // LANES

    def excl_cumsum(m):
        carry = jnp.zeros((n_rows, 1), F32)
        outs = []
        for blk in range(nblk):
            mb = m[:, blk * LANES:(blk + 1) * LANES]
            outs.append(jnp.dot(mb.astype(BF16), strict_upper, preferred_element_type=F32) + carry)
            carry = carry + jnp.sum(mb, axis=1, keepdims=True)
        return jnp.concatenate(outs, axis=1)

    eq_rank = excl_cumsum(jnp.where(eq, 1.0, 0.0))
    sel = gt | (eq & (eq_rank < need.astype(F32)))
    rank = excl_cumsum(jnp.where(sel, 1.0, 0.0))
    pos = jnp.where(sel, rank, -1.0)
    pos_ref[...] = pos.astype(jnp.int32).reshape(rb, N_EXPERTS, seq_len)

    lane = lax.broadcasted_iota(jnp.int32, (n_rows, LANES), 1)
    tab = jnp.full((n_rows, LANES), float(cap), F32)
    for k in range(seq_len // tbs):
        tab = jnp.where(lane == k, rank[:, k * tbs:k * tbs + 1], tab)
    tab_ref[...] = tab.astype(jnp.int32).reshape(rb, N_EXPERTS, LANES)

    pad = jnp.full((LANES - N_EXPERTS, LANES), -1.0, F32)
    for i in range(rb):
        for blk in range(nblk):
            tile = jnp.concatenate([pos[i * N_EXPERTS:(i + 1) * N_EXPERTS, blk * LANES:(blk + 1) * LANES], pad], axis=0)
            post_ref[i, blk * LANES:(blk + 1) * LANES, :] = tile.T.astype(jnp.int32)


def _topk(afft, cap, rb, tbs):
    bn, _, seq_len = afft.shape
    return pl.pallas_call(
        functools.partial(_topk_kernel, seq_len=seq_len, cap=cap, rb=rb, tbs=tbs),
        grid=(bn // rb,),
        in_specs=[pl.BlockSpec((rb, N_EXPERTS, seq_len), lambda b: (b, 0, 0))],
        out_specs=[pl.BlockSpec((rb, N_EXPERTS, seq_len), lambda b: (b, 0, 0)),
                   pl.BlockSpec((rb, seq_len, LANES), lambda b: (b, 0, 0)),
                   pl.BlockSpec((rb, N_EXPERTS, LANES), lambda b: (b, 0, 0))],
        out_shape=[jax.ShapeDtypeStruct((bn, N_EXPERTS, seq_len), jnp.int32),
                   jax.ShapeDtypeStruct((bn, seq_len, LANES), jnp.int32),
                   jax.ShapeDtypeStruct((bn, N_EXPERTS, LANES), jnp.int32)],
        compiler_params=_params(("arbitrary",)),
        name="topk",
    )(afft)


def _gather_kernel(h2_ref, pos_ref, afft_ref, xe_ref, gate_ref, *, cap):
    n_rows, seq_len, _ = h2_ref.shape
    n_slots = N_EXPERTS * cap
    rank = lax.broadcasted_iota(jnp.int32, (cap, seq_len), 0)
    for i in range(n_rows):
        onehots, gates = [], []
        for e in range(N_EXPERTS):
            hit = pos_ref[i, e:e + 1, :] == rank
            onehots.append(jnp.where(hit, 1.0, 0.0).astype(BF16))
            gates.append(jnp.sum(jnp.where(hit, afft_ref[i, e:e + 1, :], 0.0), axis=1, keepdims=True))
        xe_ref[i] = jnp.dot(jnp.concatenate(onehots, axis=0), h2_ref[i], preferred_element_type=F32).astype(BF16)
        gate_ref[i] = jnp.broadcast_to(jnp.concatenate(gates, axis=0), (n_slots, LANES))


def _gather(h2, pos, afft, cap, rows_per_step):
    bn, seq_len, _ = h2.shape
    n_slots = N_EXPERTS * cap
    rows = lambda *tail: pl.BlockSpec((rows_per_step,) + tail, lambda b: (b, 0, 0))
    return pl.pallas_call(
        functools.partial(_gather_kernel, cap=cap),
        grid=(bn // rows_per_step,),
        in_specs=[rows(seq_len, D_MODEL), rows(N_EXPERTS, seq_len), rows(N_EXPERTS, seq_len)],
        out_specs=[rows(n_slots, D_MODEL), rows(n_slots, LANES)],
        out_shape=[jax.ShapeDtypeStruct((bn, n_slots, D_MODEL), BF16),
                   jax.ShapeDtypeStruct((bn, n_slots, LANES), F32)],
        compiler_params=_params(("arbitrary",)),
        name="gather",
    )(h2, pos, afft)


def _window_starts(tab_ref, b, e0, n_e, t, cap, win):
    starts, overflows = [], []
    for ee in range(n_e):
        base = (b * N_EXPERTS + e0 + ee) * LANES + t
        p0 = tab_ref[base]
        p1 = tab_ref[base + 1]
        align_bits = PACKED_ROWS.bit_length() - 1
        w0 = jnp.minimum(lax.shift_left(lax.shift_right_logical(p0, align_bits), align_bits), cap - win)
        starts.append(w0)
        overflows.append(p1 - w0 > win)
    return starts, overflows


def _gather_win_kernel(tab_ref, h2_ref, pos_ref, afft_ref, xe_ref, gate_ref, *, cap, win, n_e):
    b = pl.program_id(0)
    e0 = pl.program_id(1) * n_e
    t = pl.program_id(2)
    tbs = h2_ref.shape[1]

    @pl.when(t == 0)
    def _():
        xe_ref[...] = jnp.zeros(xe_ref.shape, BF16)
        gate_ref[...] = jnp.zeros(gate_ref.shape, F32)

    starts, overflows = _window_starts(tab_ref, b, e0, n_e, t, cap, win)

    def accumulate(row0, n, x, g):
        cur = xe_ref[0, pl.ds(row0, n), :]
        xe_ref[0, pl.ds(row0, n), :] = jnp.where(g > 0.0, x.astype(BF16), cur)
        gate_ref[0, pl.ds(row0, n), :] = gate_ref[0, pl.ds(row0, n), :] + jnp.broadcast_to(g, (n, LANES))

    def onehot_and_gate(ee, shift, n):
        sub = lax.broadcasted_iota(jnp.int32, (n, tbs), 0)
        hit = (pos_ref[0, pl.ds(e0 + ee, 1), :] - shift) == sub
        gate = jnp.sum(jnp.where(hit, afft_ref[0, pl.ds(e0 + ee, 1), :], 0.0), axis=1, keepdims=True)
        return jnp.where(hit, 1.0, 0.0).astype(BF16), gate

    pieces = [onehot_and_gate(ee, jnp.where(overflows[ee], cap, starts[ee]), win) for ee in range(n_e)]
    x = jnp.dot(jnp.concatenate([p[0] for p in pieces], axis=0), h2_ref[0], preferred_element_type=F32)
    for ee in range(n_e):
        accumulate(pl.multiple_of(ee * cap + starts[ee], PACKED_ROWS), win, x[ee * win:(ee + 1) * win], pieces[ee][1])

    for ee in range(n_e):
        @pl.when(overflows[ee])
        def _(ee=ee):
            onehot, gate = onehot_and_gate(ee, 0, cap)
            accumulate(ee * cap, cap, jnp.dot(onehot, h2_ref[0], preferred_element_type=F32), gate)


def _gather_win(tab, h2, pos, afft, cap, win, tbs):
    bn, seq_len, _ = h2.shape
    n_slots = N_EXPERTS * cap
    n_e = N_EXPERTS // GATHER_EXPERT_GROUPS
    grid_spec = pltpu.PrefetchScalarGridSpec(
        num_scalar_prefetch=1,
        grid=(bn, N_EXPERTS // n_e, seq_len // tbs),
        in_specs=[pl.BlockSpec((1, tbs, D_MODEL), lambda b, g, t, tab: (b, t, 0)),
                  pl.BlockSpec((1, N_EXPERTS, tbs), lambda b, g, t, tab: (b, 0, t)),
                  pl.BlockSpec((1, N_EXPERTS, tbs), lambda b, g, t, tab: (b, 0, t))],
        out_specs=[pl.BlockSpec((1, n_e * cap, D_MODEL), lambda b, g, t, tab: (b, g, 0)),
                   pl.BlockSpec((1, n_e * cap, LANES), lambda b, g, t, tab: (b, g, 0))],
    )
    return pl.pallas_call(
        functools.partial(_gather_win_kernel, cap=cap, win=win, n_e=n_e),
        grid_spec=grid_spec,
        out_shape=[jax.ShapeDtypeStruct((bn, n_slots, D_MODEL), BF16),
                   jax.ShapeDtypeStruct((bn, n_slots, LANES), F32)],
        compiler_params=_params(("arbitrary", "arbitrary", "arbitrary")),
        name="gather_win",
    )(tab, h2, pos, afft)


def _ffn_kernel(*refs, tiles):
    n = len(tiles)
    xe_refs, gate_refs = refs[0:2 * n:2], refs[1:2 * n:2]
    wg_ref, wu_ref, wd_ref = refs[2 * n:2 * n + 3]
    ye_refs = refs[2 * n + 3:3 * n + 3]
    wg_s, wu_s, wd_s = refs[3 * n + 3:]
    j = pl.program_id(1)

    @pl.when(j == 0)
    def _():
        wg_s[...] = wg_ref[0, 0].astype(BF16)
        wu_s[...] = wu_ref[0, 0].astype(BF16)
        wd_s[...] = wd_ref[0, 0].astype(BF16)

    for (first, steps, rows), xe_ref, gate_ref, ye_ref in zip(tiles, xe_refs, gate_refs, ye_refs):
        @pl.when(jnp.logical_and(j >= first, j < first + steps))
        def _(rows=rows, xe_ref=xe_ref, gate_ref=gate_ref, ye_ref=ye_ref):
            x = xe_ref[...].reshape(rows, D_MODEL)
            g = jnp.dot(x, wg_s[...], preferred_element_type=F32)
            u = jnp.dot(x, wu_s[...], preferred_element_type=F32)
            hid = (g * _sigmoid(g) * u).astype(BF16)
            y = jnp.dot(hid, wd_s[...], preferred_element_type=F32) * gate_ref[...].reshape(rows, LANES)[:, :1]
            ye_ref[...] = y.astype(BF16).reshape(ye_ref.shape)


def _ffn(groups, w_gate, w_up, w_down, layer):
    tiles, in_specs, out_specs, out_shape, args = [], [], [], [], []
    first = 0
    for xe, gate, cap, bb in groups:
        steps = xe.shape[0] // bb
        tile = lambda e, j, first=first, steps=steps: (jnp.clip(j - first, 0, steps - 1), e, 0)
        in_specs += [pl.BlockSpec((bb, cap, D_MODEL), tile), pl.BlockSpec((bb, cap, LANES), tile)]
        out_specs.append(pl.BlockSpec((bb, cap, D_MODEL), tile))
        out_shape.append(jax.ShapeDtypeStruct(xe.shape, BF16))
        args += [xe, gate]
        tiles.append((first, steps, bb * cap))
        first += steps
    wspec = pl.BlockSpec((1, 1, D_MODEL, D_MODEL), lambda e, j: (layer, e, 0, 0))
    return pl.pallas_call(
        functools.partial(_ffn_kernel, tiles=tuple(tiles)),
        grid=(N_EXPERTS, first),
        in_specs=in_specs + [wspec, wspec, wspec],
        out_specs=out_specs,
        out_shape=out_shape,
        scratch_shapes=[pltpu.VMEM((D_MODEL, D_MODEL), BF16)] * 3,
        compiler_params=_params(("arbitrary", "arbitrary")),
        name="ffn",
    )(*args, w_gate, w_up, w_down)


def _finish(x1_ref, mod_ref, fg_ref, out_ref, acc, final, i=0):
    x2 = x1_ref[i] + mod_ref[0][5:6] * acc
    if final:
        x2 = x2 * lax.rsqrt(jnp.mean(x2 * x2, axis=-1, keepdims=True) + EPS) * fg_ref[...]
    out_ref[i] = x2


def _scatter_kernel(x1_ref, ye_ref, post_ref, mod_ref, fg_ref, out_ref, *, cap, final):
    n_rows = x1_ref.shape[0]
    n_slots = N_EXPERTS * cap
    slot = lax.broadcasted_iota(jnp.int32, (1, n_slots), 1)
    expert_of_slot = jnp.zeros((1, n_slots), jnp.int32)
    for e in range(1, N_EXPERTS):
        expert_of_slot = expert_of_slot + (slot >= e * cap).astype(jnp.int32)
    rank_of_slot = (slot - expert_of_slot * cap).astype(F32)
    spread = jnp.where(lax.broadcasted_iota(jnp.int32, (LANES, n_slots), 0) == expert_of_slot, 1.0, 0.0).astype(BF16)
    for i in range(n_rows):
        pt = post_ref[i]
        ranks = jnp.dot(pt.astype(F32).astype(BF16), spread, preferred_element_type=F32)
        onehot = jnp.where(ranks == rank_of_slot, 1.0, 0.0).astype(BF16)
        acc = jnp.dot(onehot, ye_ref[i], preferred_element_type=F32)
        _finish(x1_ref, mod_ref, fg_ref, out_ref, acc, final, i)


def _scatter(x1, ye, post, mod_l, mod_row, final_g, cap, final, rows_per_step):
    bn, seq_len, _ = x1.shape
    assert cap <= 256, "ranks must be exactly representable in bf16"
    n_slots = N_EXPERTS * cap
    rows = lambda *tail: pl.BlockSpec((rows_per_step,) + tail, lambda b: (b, 0, 0))
    return pl.pallas_call(
        functools.partial(_scatter_kernel, cap=cap, final=final),
        grid=(bn // rows_per_step,),
        in_specs=[rows(seq_len, D_MODEL), rows(n_slots, D_MODEL), rows(seq_len, LANES),
                  pl.BlockSpec((1, N_MOD, D_MODEL), lambda b: (mod_row(b * rows_per_step), 0, 0)),
                  pl.BlockSpec((1, D_MODEL), lambda b: (0, 0))],
        out_specs=rows(seq_len, D_MODEL),
        out_shape=jax.ShapeDtypeStruct((bn, seq_len, D_MODEL), F32),
        compiler_params=_params(("arbitrary",)),
        name="scatter",
    )(x1, ye, post, mod_l, final_g)


def _scatter_win_kernel(tab_ref, x1_ref, ye_ref, post_ref, mod_ref, fg_ref, out_ref, acc_ref, *, cap, win, final):
    b = pl.program_id(0)
    t = pl.program_id(1)
    tbs = x1_ref.shape[1]
    pt = post_ref[0]
    starts, overflows = _window_starts(tab_ref, b, 0, N_EXPERTS, t, cap, win)

    def onehot(e, shift, n):
        lane = lax.broadcasted_iota(jnp.int32, (tbs, n), 1)
        return jnp.where((pt[:, e:e + 1] - shift) == lane, 1.0, 0.0).astype(BF16)

    acc = None
    for e0 in range(0, N_EXPERTS, EXPERTS_PER_DOT):
        group = range(e0, e0 + EXPERTS_PER_DOT)
        onehots = [onehot(e, jnp.where(overflows[e], cap, starts[e]), win) for e in group]
        rows = [ye_ref[0, pl.ds(pl.multiple_of(e * cap + starts[e], PACKED_ROWS), win), :] for e in group]
        part = jnp.dot(jnp.concatenate(onehots, axis=1), jnp.concatenate(rows, axis=0), preferred_element_type=F32)
        acc = part if acc is None else acc + part
    acc_ref[...] = acc

    for e in range(N_EXPERTS):
        @pl.when(overflows[e])
        def _(e=e):
            acc_ref[...] += jnp.dot(onehot(e, 0, cap), ye_ref[0, e * cap:(e + 1) * cap, :],
                                    preferred_element_type=F32)

    _finish(x1_ref, mod_ref, fg_ref, out_ref, acc_ref[...], final)


def _scatter_win(tab, x1, ye, post, mod_l, mod_row, final_g, cap, win, tbs, final):
    bn, seq_len, _ = x1.shape
    n_slots = N_EXPERTS * cap
    grid_spec = pltpu.PrefetchScalarGridSpec(
        num_scalar_prefetch=1,
        grid=(bn, seq_len // tbs),
        in_specs=[pl.BlockSpec((1, tbs, D_MODEL), lambda b, t, tab: (b, t, 0)),
                  pl.BlockSpec((1, n_slots, D_MODEL), lambda b, t, tab: (b, 0, 0)),
                  pl.BlockSpec((1, tbs, LANES), lambda b, t, tab: (b, t, 0)),
                  pl.BlockSpec((1, N_MOD, D_MODEL), lambda b, t, tab: (mod_row(b), 0, 0)),
                  pl.BlockSpec((1, D_MODEL), lambda b, t, tab: (0, 0))],
        out_specs=pl.BlockSpec((1, tbs, D_MODEL), lambda b, t, tab: (b, t, 0)),
        scratch_shapes=[pltpu.VMEM((tbs, D_MODEL), F32)],
    )
    return pl.pallas_call(
        functools.partial(_scatter_win_kernel, cap=cap, win=win, final=final),
        grid_spec=grid_spec,
        out_shape=jax.ShapeDtypeStruct((bn, seq_len, D_MODEL), F32),
        compiler_params=_params(("arbitrary", "arbitrary")),
        name="scatter_win",
    )(tab, x1, ye, post, mod_l, final_g)


def _grid_position_table(n_tokens, dim):
    assert n_tokens // GRID_W <= GRID_W
    quarter = dim // 4
    inv_freq = jnp.power(POS_BASE, -jnp.arange(quarter, dtype=F32) / quarter)
    angle = jnp.arange(GRID_W).astype(F32)[:, None] * inv_freq[None]
    return jnp.concatenate([jnp.sin(angle), jnp.cos(angle)], axis=-1)


def _pad_cols(w, width):
    return jnp.pad(w, ((0, 0), (0, width - w.shape[1])))


def _router_pieces(w):
    w_hi = w.astype(BF16)
    w_lo = (w - w_hi.astype(F32)).astype(BF16)
    return jnp.stack([_pad_cols(jnp.concatenate([w_hi, w_lo], axis=1), LANES), _pad_cols(w_hi, LANES)])


def _run_groups(groups, mod, lw, final_g):
    xs_res = [g["x"] for g in groups]
    states = [([], []) for _ in groups]
    for l in range(DEPTH):
        w = lw[l]
        final = l == DEPTH - 1
        mids = []
        for gi, g in enumerate(groups):
            x = xs_res[gi]
            bn, seq_len, _ = x.shape
            cap = CAPACITY_FACTOR * seq_len // N_EXPERTS
            windowed = seq_len >= WINDOWED_MIN_SEQ
            tbs = max(LANES, seq_len // TOKEN_BLOCKS) if windowed else seq_len
            win = cap // WINDOW_DIVISOR
            mod_row, tl = g["mod_row"], g["tl"]
            pos_table = g.get("pos_table") if l == 0 else None
            outs = _in_proj(x, mod[l], mod_row, w["norm_mix_g"], w["w_main"], w["w_dt"], w["pool_w"],
                            w["pool_scale"], w["conv_w"], w["conv_b"], w["alog_row"], w["bias_row"], tl, pos_table,
                            g.get("in_proj_rows", 1))
            y_pool, z, xs, bc, dt, cum = outs[:6]
            if pos_table is not None:
                x = outs[6]
            y_f, y_b, hT_f, hT_b = _ssd(xs, bc, dt, cum, g["h0f"][l], g["h0b"][l], g["h0_row"])
            states[gi][0].append(hT_f)
            states[gi][1].append(hT_b)
            x1, h2, afft = _out_proj(x, y_f, y_b, xs, z, y_pool, w["dsk"], w["ssm_norm_g"], w["w_out"], mod[l],
                                     mod_row, w["norm_ffn_g"], w["w_router"], tl, g.get("in_proj_rows", 1))
            pos, post, tab = _topk(afft, cap, g["topk_rows"], tbs)
            if windowed:
                tab = tab.reshape(-1)
                xe, gate = _gather_win(tab, h2, pos, afft, cap, win, tbs)
            else:
                xe, gate = _gather(h2, pos, afft, cap, g.get("dense_rows", 1))
            mids.append(dict(x1=x1, post=post, tab=tab, xe=xe, gate=gate, cap=cap, win=win, tbs=tbs,
                             windowed=windowed))
        yes = _ffn([(m["xe"], m["gate"], m["cap"], g["ffn_rows_per_step"] // m["cap"])
                    for m, g in zip(mids, groups)], w["w_gate"], w["w_up"], w["w_down"], l)
        for gi, (g, m, ye) in enumerate(zip(groups, mids, yes)):
            if m["windowed"]:
                xs_res[gi] = _scatter_win(m["tab"], m["x1"], ye, m["post"], mod[l], g["mod_row"], final_g, m["cap"],
                                          m["win"], m["tbs"], final)
            else:
                xs_res[gi] = _scatter(m["x1"], ye, m["post"], mod[l], g["mod_row"], final_g, m["cap"], final,
                                      g.get("dense_rows", 1))
    return [(x, sf, sb) for x, (sf, sb) in zip(xs_res, states)]


def kernel(x_prompt, x_sample, state_ssm_fwd, state_ssm_bwd, c, c_ctx, norm_mix_g, w_ada, b_ada, w_in, pool_w,
           pool_scale, conv_w, conv_b, a_log_fwd, a_log_bwd, dt_bias_fwd, dt_bias_bwd, d_skip, ssm_norm_g, w_out,
           norm_ffn_g, w_router, w_gate, w_up, w_down, final_norm_g):
    n_dec = c.shape[0]
    n_ctx = x_prompt.shape[0]
    ctx_row = n_dec
    cond = jnp.concatenate([c, c_ctx[None, :], jnp.zeros((SUBLANES - n_dec - 1, D_MODEL), F32)], axis=0)
    mod = _modulation(cond, w_ada, b_ada).reshape(DEPTH, SUBLANES, N_MOD, D_MODEL)

    zeros_h = jnp.zeros((LANES - 2 * SSD_HEADS,), F32)
    lw = []
    for l in range(DEPTH):
        lw.append(dict(
            norm_mix_g=norm_mix_g[l][None], w_main=w_in[l][:, :N_MAIN].astype(BF16),
            w_dt=_pad_cols(w_in[l][:, N_MAIN:], LANES).astype(BF16), pool_w=pool_w[l].astype(BF16),
            pool_scale=pool_scale[l][None], conv_w=conv_w[l], conv_b=conv_b[l][None],
            alog_row=jnp.concatenate([a_log_fwd[l], a_log_bwd[l], zeros_h])[None],
            bias_row=jnp.concatenate([dt_bias_fwd[l], dt_bias_bwd[l], zeros_h])[None],
            dsk=jnp.repeat(d_skip[l], SSD_INNER // SSD_HEADS)[None], ssm_norm_g=ssm_norm_g[l][None],
            w_out=w_out[l].astype(BF16), norm_ffn_g=norm_ffn_g[l][None], w_router=_router_pieces(w_router[l]),
            w_gate=w_gate, w_up=w_up, w_down=w_down))
    final_g = final_norm_g[None]
    st_shape = (HEAD_PAIRS, LANES, SSD_STATE)

    zero_state = jnp.zeros((1,) + st_shape, F32)
    prompt = dict(x=x_prompt, mod_row=lambda b: ctx_row, h0f=[zero_state] * DEPTH, h0b=[zero_state] * DEPTH,
                  h0_row=lambda b: 0, tl=x_prompt.shape[1], ffn_rows_per_step=1024, topk_rows=n_ctx, dense_rows=4,
                  in_proj_rows=4)
    sample = dict(x=x_sample, mod_row=lambda b: b, h0_row=lambda b: b, tl=1024, ffn_rows_per_step=1024,
                  topk_rows=n_dec, pos_table=_grid_position_table(x_sample.shape[1], D_MODEL),
                  h0f=[state_ssm_fwd[:, l].reshape((n_dec,) + st_shape) for l in range(DEPTH)],
                  h0b=[state_ssm_bwd[:, l].reshape((n_dec,) + st_shape) for l in range(DEPTH)])
    (y_prompt, sf, sb), (y_sample, _, _) = _run_groups([prompt, sample], mod, lw, final_g)
    out_state_shape = (n_ctx, SSD_HEADS, SSD_INNER // SSD_HEADS, SSD_STATE)
    new_f = jnp.stack([s.reshape(out_state_shape) for s in sf], axis=1)
    new_b = jnp.stack([s.reshape(out_state_shape) for s in sb], axis=1)
    return (y_prompt, y_sample, new_f, new_b)
```

```python
import functools

import jax
import jax.numpy as jnp
from jax import lax
from jax.experimental import pallas as pl
from jax.experimental.pallas import tpu as pltpu

D_MODEL = 1024
DEPTH = 2
GRID_W = 64
MIX_POOL = 512
POOL_WINDOWS = (2, 4, 8, 16)
POOL_GROUP_CH = 128
SSD_INNER = 512
SSD_HEADS = 8
SSD_HEAD_DIM = SSD_INNER // SSD_HEADS
SSD_STATE = 128
CONV_K = 5
CHUNK = 128
CONV_DIM = 1024
N_EXPERTS = 16
CAPACITY_FACTOR = 2
N_MOD = 6
POS_BASE = 10000.0
EPS = 1e-6
LOG2_E = 1.4426950408889634

SUBLANES = 8
LANES = 128
PACKED_ROWS = 16
HALO = SUBLANES
N_MAIN = MIX_POOL + SSD_INNER + CONV_DIM
HEAD_PAIRS = SSD_HEADS // 2
EXPERTS_PER_DOT = 2
SSD_MAX_CHUNKS_PER_STEP = 4
MOD_COLS_PER_STEP = 1536
WINDOWED_MIN_SEQ = 1024
TOKEN_BLOCKS = 8
WINDOW_DIVISOR = 4
GATHER_EXPERT_GROUPS = 2
FLOAT_MAGNITUDE_BITS = 31
VMEM_LIMIT = 56 * 1024 * 1024

F32 = jnp.float32
BF16 = jnp.bfloat16
HI = lax.Precision.HIGHEST


def _sigmoid(x):
    return 1.0 / (1.0 + jnp.exp(-x))


def _params(sem):
    return pltpu.CompilerParams(dimension_semantics=sem, vmem_limit_bytes=VMEM_LIMIT)


def _mod_kernel(cond_ref, w_ref, b_ref, out_ref):
    c = cond_ref[...]
    s = c * _sigmoid(c)
    out_ref[0] = jnp.dot(s, w_ref[0], precision=HI, preferred_element_type=F32) + b_ref[0]


def _modulation(cond, w_ada, b_ada):
    n_rows = cond.shape[0]
    n_out = N_MOD * D_MODEL
    tn = MOD_COLS_PER_STEP
    return pl.pallas_call(
        _mod_kernel,
        grid=(DEPTH, n_out // tn),
        in_specs=[
            pl.BlockSpec((n_rows, D_MODEL), lambda l, j: (0, 0)),
            pl.BlockSpec((1, D_MODEL, tn), lambda l, j: (l, 0, j)),
            pl.BlockSpec((1, 1, tn), lambda l, j: (l, 0, j)),
        ],
        out_specs=pl.BlockSpec((1, n_rows, tn), lambda l, j: (l, 0, j)),
        out_shape=jax.ShapeDtypeStruct((DEPTH, n_rows, n_out), F32),
        compiler_params=_params(("arbitrary", "arbitrary")),
        name="modulation",
    )(cond, w_ada, b_ada.reshape(DEPTH, 1, n_out))


def _cumsum_pieces(dta):
    d_hi = dta.astype(BF16)
    rest = dta - d_hi.astype(F32)
    d_mid = rest.astype(BF16)
    return d_hi, d_mid, (rest - d_mid.astype(F32)).astype(BF16)


def _grid_position_rows(table_ref, grid_row, col0, n_cols):
    row_part = jnp.broadcast_to(table_ref[pl.ds(grid_row, 1), :], (n_cols, D_MODEL // 2))
    return jnp.concatenate([row_part, table_ref[col0:col0 + n_cols, :]], axis=1)


def _in_row(*refs, tl, seq_len, with_pos, req):
    (xp_ref, xc_ref, xn_ref, mod_ref, g_ref, wmain_ref, wdt_ref, poolw_ref, pscale_ref, cw_ref, cb_ref,
     alog_ref, bias_ref) = refs[:13]
    table_ref = refs[13] if with_pos else None
    ypool_ref, z_ref, xs_ref, bc_ref, dt_ref, cum_ref = refs[13 + with_pos:19 + with_pos]
    x0_ref = refs[19 + with_pos] if with_pos else None
    proj_scr = refs[-1].at[req]
    i = pl.program_id(1)
    n = pl.num_programs(1)
    rows = tl + 2 * HALO
    x_ext = jnp.concatenate([xp_ref[0], xc_ref[req], xn_ref[0]], axis=0)
    if with_pos:
        r0 = i * (tl // GRID_W)
        last_row = seq_len // GRID_W - 1
        pieces = [_grid_position_rows(table_ref, jnp.maximum(r0 - 1, 0), GRID_W - HALO, HALO)]
        pieces += [_grid_position_rows(table_ref, r0 + k, 0, GRID_W) for k in range(tl // GRID_W)]
        pieces += [_grid_position_rows(table_ref, jnp.minimum(r0 + tl // GRID_W, last_row), 0, HALO)]
        x_ext = x_ext + jnp.concatenate(pieces, axis=0)
        x0_ref[req] = x_ext[HALO:HALO + tl]
    mod = mod_ref[0]
    ms = jnp.mean(x_ext * x_ext, axis=-1, keepdims=True)
    h = x_ext * lax.rsqrt(ms + EPS) * (g_ref[...] * (1.0 + mod[1:2])) + mod[0:1]
    row = lax.broadcasted_iota(jnp.int32, (rows, 1), 0)
    lo_ok = jnp.where(i > 0, 0, HALO)
    hi_ok = jnp.where(i < n - 1, rows, tl + HALO)
    hb = jnp.where((row >= lo_ok) & (row < hi_ok), h, 0.0).astype(BF16)
    proj_scr[...] = jnp.dot(hb, wmain_ref[...], preferred_element_type=F32)
    dt_full = jnp.dot(hb, wdt_ref[...], preferred_element_type=F32)

    x_in = dt_full[HALO:HALO + tl] + bias_ref[...]
    dt = jnp.maximum(x_in, 0.0) + jnp.log1p(jnp.exp(-jnp.abs(x_in)))
    dt_ref[req] = dt
    r = lax.broadcasted_iota(jnp.int32, (CHUNK, CHUNK), 0)
    s = lax.broadcasted_iota(jnp.int32, (CHUNK, CHUNK), 1)
    tri = jnp.concatenate([jnp.where(r >= s, 1.0, 0.0), jnp.where(r <= s, 1.0, 0.0)], axis=1).astype(BF16)
    head = lax.broadcasted_iota(jnp.int32, (tl, LANES), 1)
    dta = jnp.where(head < 2 * SSD_HEADS, dt * (-LOG2_E * jnp.exp(alog_ref[...])), 0.0)
    pieces = _cumsum_pieces(dta)
    fwd_cols = s < SSD_HEADS
    for k in range(tl // CHUNK):
        acc = jnp.zeros((CHUNK, LANES), F32)
        for piece in pieces:
            pk = piece[k * CHUNK:(k + 1) * CHUNK]
            zero = jnp.zeros_like(pk)
            stacked = jnp.concatenate([jnp.where(fwd_cols, pk, zero), jnp.where(fwd_cols, zero, pk)], axis=0)
            acc = acc + jnp.dot(tri, stacked, preferred_element_type=F32)
        cum_ref[req, k * CHUNK:(k + 1) * CHUNK, :] = acc

    def ahead(v, k):
        return pltpu.roll(v, (-k) % rows, axis=0)

    t_glob = i * tl + lax.broadcasted_iota(jnp.int32, (tl, 1), 0)
    outs = []
    for gi, w in enumerate(POOL_WINDOWS):
        c0 = gi * POOL_GROUP_CH
        u = proj_scr[:, c0:c0 + POOL_GROUP_CH]
        acc = u
        span = 1
        while 2 * span < w:
            acc = acc + ahead(acc, span)
            span *= 2
        acc = acc + ahead(acc, -span)
        lo = jnp.maximum(t_glob - w // 2, 0)
        hi = jnp.minimum(t_glob - w // 2 + w, seq_len)
        cnt = (hi - lo).astype(F32)
        p = acc[HALO:HALO + tl] / cnt - u[HALO:HALO + tl]
        outs.append(jnp.dot(p.astype(BF16), poolw_ref[gi], preferred_element_type=F32))
    ypool_ref[req] = (jnp.concatenate(outs, axis=1) * pscale_ref[...]).astype(BF16)

    c0 = MIX_POOL + SSD_INNER
    acc = cb_ref[...] + cw_ref[0:1, :] * proj_scr[HALO - 2:HALO - 2 + tl, c0:c0 + CONV_DIM]
    for k in range(1, CONV_K):
        acc = acc + cw_ref[k:k + 1, :] * proj_scr[HALO - 2 + k:HALO - 2 + k + tl, c0:c0 + CONV_DIM]
    act = acc * _sigmoid(acc)
    xs_ref[req] = act[:, :SSD_INNER].astype(BF16)
    bc_ref[req] = act[:, SSD_INNER:].astype(BF16)
    z_ref[req] = proj_scr[HALO:HALO + tl, MIX_POOL:MIX_POOL + SSD_INNER]


def _in_kernel(*refs, tl, seq_len, with_pos):
    for req in range(refs[1].shape[0]):
        _in_row(*refs, tl=tl, seq_len=seq_len, with_pos=with_pos, req=req)


def _in_proj(x, mod_l, mod_row, norm_g, w_main, w_dt, pool_w, pool_scale, conv_w, conv_b, alog_row, bias_row, tl,
             pos_table=None, rows_per_step=1):
    bn, seq_len, _ = x.shape
    nt = seq_len // tl
    rps = rows_per_step
    assert rps == 1 or nt == 1
    hb = tl // HALO
    last_halo = seq_len // HALO - 1
    with_pos = pos_table is not None
    const = lambda *shape: pl.BlockSpec(shape, lambda b, i: (0,) * len(shape))
    tok = lambda width: pl.BlockSpec((rps, tl, width), lambda b, i: (b, i, 0))
    in_specs = [
        pl.BlockSpec((1, HALO, D_MODEL), lambda b, i: (b * rps, jnp.maximum(i * hb - 1, 0), 0)),
        tok(D_MODEL),
        pl.BlockSpec((1, HALO, D_MODEL), lambda b, i: (b * rps, jnp.minimum((i + 1) * hb, last_halo), 0)),
        pl.BlockSpec((1, N_MOD, D_MODEL), lambda b, i: (mod_row(b * rps), 0, 0)),
        const(1, D_MODEL),
        const(D_MODEL, N_MAIN),
        const(D_MODEL, LANES),
        const(len(POOL_WINDOWS), POOL_GROUP_CH, POOL_GROUP_CH),
        const(1, MIX_POOL),
        const(CONV_K, CONV_DIM),
        const(1, CONV_DIM),
        const(1, LANES),
        const(1, LANES),
    ]
    out_specs = [tok(MIX_POOL), tok(SSD_INNER), tok(SSD_INNER), tok(CONV_DIM - SSD_INNER), tok(LANES), tok(LANES)]
    out_shape = [
        jax.ShapeDtypeStruct((bn, seq_len, MIX_POOL), BF16),
        jax.ShapeDtypeStruct((bn, seq_len, SSD_INNER), F32),
        jax.ShapeDtypeStruct((bn, seq_len, SSD_INNER), BF16),
        jax.ShapeDtypeStruct((bn, seq_len, CONV_DIM - SSD_INNER), BF16),
        jax.ShapeDtypeStruct((bn, seq_len, LANES), F32),
        jax.ShapeDtypeStruct((bn, seq_len, LANES), F32),
    ]
    args = [x, x, x, mod_l, norm_g, w_main, w_dt, pool_w, pool_scale, conv_w, conv_b, alog_row, bias_row]
    if with_pos:
        assert tl % GRID_W == 0 and pos_table.shape == (GRID_W, D_MODEL // 2)
        in_specs.append(const(GRID_W, D_MODEL // 2))
        out_specs.append(tok(D_MODEL))
        out_shape.append(jax.ShapeDtypeStruct((bn, seq_len, D_MODEL), F32))
        args.append(pos_table)
    return pl.pallas_call(
        functools.partial(_in_kernel, tl=tl, seq_len=seq_len, with_pos=with_pos),
        grid=(bn // rps, nt),
        in_specs=in_specs,
        out_specs=out_specs,
        out_shape=out_shape,
        scratch_shapes=[pltpu.VMEM((rps, tl + 2 * HALO, N_MAIN), F32)],
        compiler_params=_params(("arbitrary", "arbitrary")),
        name="in_proj",
    )(*args)


def _ssd_chunk(xsf_ref, bcf_ref, dtf_ref, cumf_ref, xsb_ref, bcb_ref, dtb_ref, cumb_ref, yf_ref, yb_ref, st_ref, rf, rb):
    r = lax.broadcasted_iota(jnp.int32, (CHUNK, CHUNK), 0)
    s = lax.broadcasted_iota(jnp.int32, (CHUNK, CHUNK), 1)
    lane2 = lax.broadcasted_iota(jnp.int32, (CHUNK, 2 * LANES), 1)
    lane_head = lax.shift_right_logical(lane2, SSD_HEAD_DIM.bit_length() - 1)
    row2 = lax.broadcasted_iota(jnp.int32, (2 * LANES, SSD_STATE), 0)
    is_fwd = s < SSD_HEADS
    dt = jnp.where(is_fwd, dtf_ref[0, rf:rf + CHUNK, :], dtb_ref[0, rb:rb + CHUNK, :])
    cum = jnp.where(is_fwd, cumf_ref[0, rf:rf + CHUNK, :], cumb_ref[0, rb:rb + CHUNK, :])
    tot = jnp.where(lax.broadcasted_iota(jnp.int32, (1, LANES), 1) < SSD_HEADS,
                    cum[CHUNK - 1:CHUNK, :], cum[0:1, :])
    cdec = jnp.exp2(tot)
    cum_t = cum.T
    dt_t = dt.T
    ecum_t = jnp.exp2(cum_t)
    w_t = (dt * jnp.exp2(tot - cum)).T

    def head_rows(arr_t, h0):
        return jnp.concatenate([jnp.broadcast_to(arr_t[h0 + k:h0 + k + 1, :], (SSD_HEAD_DIM, CHUNK))
                                for k in range(4)], axis=0)

    def one_dir(d, xs_ref, bc_ref, y_ref, r0):
        mask = (r >= s) if d == 0 else (r <= s)
        off = d * SSD_HEADS
        bcv = bc_ref[0, r0:r0 + CHUNK, :]
        b01 = bcv[:, :2 * SSD_STATE].astype(BF16)
        c01 = bcv[:, 2 * SSD_STATE:].astype(BF16)
        zero2 = jnp.zeros_like(b01)
        b_blk = jnp.concatenate([jnp.where(lane2 < SSD_STATE, b01, zero2),
                                 jnp.where(lane2 < SSD_STATE, zero2, b01)], axis=0)
        cb01 = lax.dot_general(c01, b_blk, (((1,), (1,)), ((), ())), preferred_element_type=F32)
        for g in range(2):
            h0 = off + 4 * g
            b_g = b01[:, g * SSD_STATE:(g + 1) * SSD_STATE]
            c_g = c01[:, g * SSD_STATE:(g + 1) * SSD_STATE]
            cb = cb01[:, g * CHUNK:(g + 1) * CHUNK]
            xs_g = xs_ref[0, r0:r0 + CHUNK, g * 2 * LANES:(g + 1) * 2 * LANES]
            x_b = xs_g.astype(BF16)
            ms, x_parts = [], []
            for k in range(4):
                hh = h0 + k
                seg = cum[:, hh:hh + 1] - cum_t[hh:hh + 1, :]
                dec = jnp.exp2(jnp.where(mask, seg, -jnp.inf))
                ms.append((cb * dec * dt_t[hh:hh + 1, :]).astype(BF16))
                x_parts.append(jnp.where(lane_head == k, x_b, jnp.zeros_like(x_b)))
            y_diag = jnp.dot(jnp.concatenate(ms, axis=1), jnp.concatenate(x_parts, axis=0),
                             preferred_element_type=F32)
            st = st_ref[d, 2 * g:2 * g + 2].reshape(2 * LANES, SSD_STATE)
            y_off_t = lax.dot_general(st.astype(BF16), c_g, (((1,), (1,)), ((), ())),
                                      preferred_element_type=F32) * head_rows(ecum_t, h0)
            y_ref[0, r0:r0 + CHUNK, g * 2 * LANES:(g + 1) * 2 * LANES] = (y_diag + y_off_t.T).astype(BF16)
            x_d = (xs_g.astype(F32).T * head_rows(w_t, h0)).astype(BF16)
            cs = jnp.dot(x_d, b_g, preferred_element_type=F32)
            dcol = cdec[:, h0 + 3:h0 + 4]
            for k in (2, 1, 0):
                dcol = jnp.where(row2 < (k + 1) * SSD_HEAD_DIM, cdec[:, h0 + k:h0 + k + 1], dcol)
            st_ref[d, 2 * g:2 * g + 2] = (st * dcol + cs).reshape(2, LANES, SSD_STATE)

    one_dir(0, xsf_ref, bcf_ref, yf_ref, rf)
    one_dir(1, xsb_ref, bcb_ref, yb_ref, rb)


def _ssd_kernel(xsf_ref, bcf_ref, dtf_ref, cumf_ref, xsb_ref, bcb_ref, dtb_ref, cumb_ref, h0f_ref, h0b_ref,
                yf_ref, yb_ref, hTf_ref, hTb_ref, st_ref):
    c = pl.program_id(1)
    nc = pl.num_programs(1)

    @pl.when(c == 0)
    def _():
        st_ref[0] = h0f_ref[0]
        st_ref[1] = h0b_ref[0]

    n_chunks = xsf_ref.shape[1] // CHUNK
    for k in range(n_chunks):
        _ssd_chunk(xsf_ref, bcf_ref, dtf_ref, cumf_ref, xsb_ref, bcb_ref, dtb_ref, cumb_ref, yf_ref, yb_ref, st_ref,
                   k * CHUNK, (n_chunks - 1 - k) * CHUNK)

    @pl.when(c == nc - 1)
    def _():
        hTf_ref[0] = st_ref[0]
        hTb_ref[0] = st_ref[1]


def _ssd(xs, bc, dt, cum, h0f, h0b, h0_row):
    bn, seq_len, _ = xs.shape
    step_rows = min(SSD_MAX_CHUNKS_PER_STEP * CHUNK, seq_len)
    nc = seq_len // step_rows
    fwd = lambda width: pl.BlockSpec((1, step_rows, width), lambda b, c: (b, c, 0))
    bwd = lambda width: pl.BlockSpec((1, step_rows, width), lambda b, c: (b, nc - 1 - c, 0))
    st_spec_in = pl.BlockSpec((1, HEAD_PAIRS, LANES, SSD_STATE), lambda b, c: (h0_row(b), 0, 0, 0))
    st_spec_out = pl.BlockSpec((1, HEAD_PAIRS, LANES, SSD_STATE), lambda b, c: (b, 0, 0, 0))
    st_shape = jax.ShapeDtypeStruct((bn, HEAD_PAIRS, LANES, SSD_STATE), F32)
    return pl.pallas_call(
        _ssd_kernel,
        grid=(bn, nc),
        in_specs=[fwd(SSD_INNER), fwd(CONV_DIM - SSD_INNER), fwd(LANES), fwd(LANES),
                  bwd(SSD_INNER), bwd(CONV_DIM - SSD_INNER), bwd(LANES), bwd(LANES),
                  st_spec_in, st_spec_in],
        out_specs=[fwd(SSD_INNER), bwd(SSD_INNER), st_spec_out, st_spec_out],
        out_shape=[jax.ShapeDtypeStruct((bn, seq_len, SSD_INNER), BF16),
                   jax.ShapeDtypeStruct((bn, seq_len, SSD_INNER), BF16), st_shape, st_shape],
        scratch_shapes=[pltpu.VMEM((2, HEAD_PAIRS, LANES, SSD_STATE), F32)],
        compiler_params=_params(("arbitrary", "arbitrary")),
        name="ssd",
    )(xs, bc, dt, cum, xs, bc, dt, cum, h0f, h0b)


def _out_row(x_ref, yf_ref, yb_ref, xs_ref, z_ref, yp_ref, dsk_ref, sg_ref, wo_ref, mod_ref, ng_ref, wr_ref,
             x1_ref, h2_ref, afft_ref, req):
    y = yf_ref[req].astype(F32) + yb_ref[req].astype(F32) + dsk_ref[...] * xs_ref[req].astype(F32)
    z = z_ref[req]
    y = y * (z * _sigmoid(z))
    y = y * lax.rsqrt(jnp.mean(y * y, axis=-1, keepdims=True) + EPS) * sg_ref[...]
    o = (jnp.dot(yp_ref[req], wo_ref[:MIX_POOL, :], preferred_element_type=F32)
         + jnp.dot(y.astype(BF16), wo_ref[MIX_POOL:, :], preferred_element_type=F32))
    mod = mod_ref[0]
    x1 = x_ref[req] + mod[2:3] * o
    x1_ref[req] = x1
    h2 = x1 * lax.rsqrt(jnp.mean(x1 * x1, axis=-1, keepdims=True) + EPS) * (ng_ref[...] * (1.0 + mod[4:5])) + mod[3:4]
    h2_hi = h2.astype(BF16)
    h2_ref[req] = h2_hi
    h2_lo = (h2 - h2_hi.astype(F32)).astype(BF16)
    pa = jnp.dot(h2_hi, wr_ref[0], preferred_element_type=F32)
    pb = jnp.dot(h2_lo, wr_ref[1], preferred_element_type=F32)
    logits = pa + pltpu.roll(pa, LANES - N_EXPERTS, axis=1) + pb
    lane = lax.broadcasted_iota(jnp.int32, logits.shape, 1)
    lg = jnp.where(lane < N_EXPERTS, logits, -jnp.inf)
    e = jnp.exp(lg - jnp.max(lg, axis=-1, keepdims=True))
    aff = e / jnp.sum(e, axis=-1, keepdims=True)
    afft_ref[req] = aff.T[:N_EXPERTS, :]


def _out_kernel(*refs):
    for req in range(refs[0].shape[0]):
        _out_row(*refs, req=req)


def _out_proj(x, y_f, y_b, xs, z, y_pool, dsk, ssm_g, w_out, mod_l, mod_row, norm_ffn_g, w_router, tl,
              rows_per_step=1):
    bn, seq_len, _ = x.shape
    nt = seq_len // tl
    rps = rows_per_step
    const = lambda *shape: pl.BlockSpec(shape, lambda b, i: (0,) * len(shape))
    tok = lambda width: pl.BlockSpec((rps, tl, width), lambda b, i: (b, i, 0))
    return pl.pallas_call(
        _out_kernel,
        grid=(bn // rps, nt),
        in_specs=[tok(D_MODEL), tok(SSD_INNER), tok(SSD_INNER), tok(SSD_INNER), tok(SSD_INNER), tok(MIX_POOL),
                  const(1, SSD_INNER), const(1, SSD_INNER), const(D_MODEL, D_MODEL),
                  pl.BlockSpec((1, N_MOD, D_MODEL), lambda b, i: (mod_row(b * rps), 0, 0)),
                  const(1, D_MODEL), const(2, D_MODEL, LANES)],
        out_specs=[tok(D_MODEL), tok(D_MODEL), pl.BlockSpec((rps, N_EXPERTS, tl), lambda b, i: (b, 0, i))],
        out_shape=[jax.ShapeDtypeStruct((bn, seq_len, D_MODEL), F32),
                   jax.ShapeDtypeStruct((bn, seq_len, D_MODEL), BF16),
                   jax.ShapeDtypeStruct((bn, N_EXPERTS, seq_len), F32)],
        compiler_params=_params(("arbitrary", "arbitrary")),
        name="out_proj",
    )(x, y_f, y_b, xs, z, y_pool, dsk, ssm_g, w_out, mod_l, norm_ffn_g, w_router)


def _topk_kernel(afft_ref, pos_ref, post_ref, tab_ref, *, seq_len, cap, rb, tbs):
    n_rows = rb * N_EXPERTS
    a = afft_ref[...].reshape(n_rows, seq_len)

    def body(it, thr):
        cand_bits = thr | jnp.left_shift(jnp.int32(1), FLOAT_MAGNITUDE_BITS - 1 - it)
        cnt = jnp.sum((a >= pltpu.bitcast(cand_bits, F32)).astype(jnp.int32), axis=1, keepdims=True)
        return jnp.where(cnt >= cap, cand_bits, thr)

    thr = lax.fori_loop(0, FLOAT_MAGNITUDE_BITS, body, jnp.zeros((n_rows, 1), jnp.int32))
    lo = pltpu.bitcast(thr, F32)
    hi = pltpu.bitcast(thr + 1, F32)
    gt = a >= hi
    eq = (a >= lo) & (a < hi)
    need = cap - jnp.sum(gt.astype(jnp.int32), axis=1, keepdims=True)

    r = lax.broadcasted_iota(jnp.int32, (LANES, LANES), 0)
    s = lax.broadcasted_iota(jnp.int32, (LANES, LANES), 1)
    strict_upper = (r < s).astype(BF16)
    nblk = seq_len // LANES

    def excl_cumsum(m):
        carry = jnp.zeros((n_rows, 1), F32)
        outs = []
        for blk in range(nblk):
            mb = m[:, blk * LANES:(blk + 1) * LANES]
            outs.append(jnp.dot(mb.astype(BF16), strict_upper, preferred_element_type=F32) + carry)
            carry = carry + jnp.sum(mb, axis=1, keepdims=True)
        return jnp.concatenate(outs, axis=1)

    eq_rank = excl_cumsum(jnp.where(eq, 1.0, 0.0))
    sel = gt | (eq & (eq_rank < need.astype(F32)))
    rank = excl_cumsum(jnp.where(sel, 1.0, 0.0))
    pos = jnp.where(sel, rank, -1.0)
    pos_ref[...] = pos.astype(jnp.int32).reshape(rb, N_EXPERTS, seq_len)

    lane = lax.broadcasted_iota(jnp.int32, (n_rows, LANES), 1)
    tab = jnp.full((n_rows, LANES), float(cap), F32)
    for k in range(seq_len // tbs):
        tab = jnp.where(lane == k, rank[:, k * tbs:k * tbs + 1], tab)
    tab_ref[...] = tab.astype(jnp.int32).reshape(rb, N_EXPERTS, LANES)

    pad = jnp.full((LANES - N_EXPERTS, LANES), -1.0, F32)
    for i in range(rb):
        for blk in range(nblk):
            tile = jnp.concatenate([pos[i * N_EXPERTS:(i + 1) * N_EXPERTS, blk * LANES:(blk + 1) * LANES], pad], axis=0)
            post_ref[i, blk * LANES:(blk + 1) * LANES, :] = tile.T.astype(jnp.int32)


def _topk(afft, cap, rb, tbs):
    bn, _, seq_len = afft.shape
    return pl.pallas_call(
        functools.partial(_topk_kernel, seq_len=seq_len, cap=cap, rb=rb, tbs=tbs),
        grid=(bn // rb,),
        in_specs=[pl.BlockSpec((rb, N_EXPERTS, seq_len), lambda b: (b, 0, 0))],
        out_specs=[pl.BlockSpec((rb, N_EXPERTS, seq_len), lambda b: (b, 0, 0)),
                   pl.BlockSpec((rb, seq_len, LANES), lambda b: (b, 0, 0)),
                   pl.BlockSpec((rb, N_EXPERTS, LANES), lambda b: (b, 0, 0))],
        out_shape=[jax.ShapeDtypeStruct((bn, N_EXPERTS, seq_len), jnp.int32),
                   jax.ShapeDtypeStruct((bn, seq_len, LANES), jnp.int32),
                   jax.ShapeDtypeStruct((bn, N_EXPERTS, LANES), jnp.int32)],
        compiler_params=_params(("arbitrary",)),
        name="topk",
    )(afft)


def _gather_kernel(h2_ref, pos_ref, afft_ref, xe_ref, gate_ref, *, cap):
    n_rows, seq_len, _ = h2_ref.shape
    n_slots = N_EXPERTS * cap
    rank = lax.broadcasted_iota(jnp.int32, (cap, seq_len), 0)
    for i in range(n_rows):
        onehots, gates = [], []
        for e in range(N_EXPERTS):
            hit = pos_ref[i, e:e + 1, :] == rank
            onehots.append(jnp.where(hit, 1.0, 0.0).astype(BF16))
            gates.append(jnp.sum(jnp.where(hit, afft_ref[i, e:e + 1, :], 0.0), axis=1, keepdims=True))
        xe_ref[i] = jnp.dot(jnp.concatenate(onehots, axis=0), h2_ref[i], preferred_element_type=F32).astype(BF16)
        gate_ref[i] = jnp.broadcast_to(jnp.concatenate(gates, axis=0), (n_slots, LANES))


def _gather(h2, pos, afft, cap, rows_per_step):
    bn, seq_len, _ = h2.shape
    n_slots = N_EXPERTS * cap
    rows = lambda *tail: pl.BlockSpec((rows_per_step,) + tail, lambda b: (b, 0, 0))
    return pl.pallas_call(
        functools.partial(_gather_kernel, cap=cap),
        grid=(bn // rows_per_step,),
        in_specs=[rows(seq_len, D_MODEL), rows(N_EXPERTS, seq_len), rows(N_EXPERTS, seq_len)],
        out_specs=[rows(n_slots, D_MODEL), rows(n_slots, LANES)],
        out_shape=[jax.ShapeDtypeStruct((bn, n_slots, D_MODEL), BF16),
                   jax.ShapeDtypeStruct((bn, n_slots, LANES), F32)],
        compiler_params=_params(("arbitrary",)),
        name="gather",
    )(h2, pos, afft)


def _window_starts(tab_ref, b, e0, n_e, t, cap, win):
    starts, overflows = [], []
    for ee in range(n_e):
        base = (b * N_EXPERTS + e0 + ee) * LANES + t
        p0 = tab_ref[base]
        p1 = tab_ref[base + 1]
        align_bits = PACKED_ROWS.bit_length() - 1
        w0 = jnp.minimum(lax.shift_left(lax.shift_right_logical(p0, align_bits), align_bits), cap - win)
        starts.append(w0)
        overflows.append(p1 - w0 > win)
    return starts, overflows


def _gather_win_kernel(tab_ref, h2_ref, pos_ref, afft_ref, xe_ref, gate_ref, *, cap, win, n_e):
    b = pl.program_id(0)
    e0 = pl.program_id(1) * n_e
    t = pl.program_id(2)
    tbs = h2_ref.shape[1]

    @pl.when(t == 0)
    def _():
        xe_ref[...] = jnp.zeros(xe_ref.shape, BF16)
        gate_ref[...] = jnp.zeros(gate_ref.shape, F32)

    starts, overflows = _window_starts(tab_ref, b, e0, n_e, t, cap, win)

    def accumulate(row0, n, x, g):
        cur = xe_ref[0, pl.ds(row0, n), :]
        xe_ref[0, pl.ds(row0, n), :] = jnp.where(g > 0.0, x.astype(BF16), cur)
        gate_ref[0, pl.ds(row0, n), :] = gate_ref[0, pl.ds(row0, n), :] + jnp.broadcast_to(g, (n, LANES))

    def onehot_and_gate(ee, shift, n):
        sub = lax.broadcasted_iota(jnp.int32, (n, tbs), 0)
        hit = (pos_ref[0, pl.ds(e0 + ee, 1), :] - shift) == sub
        gate = jnp.sum(jnp.where(hit, afft_ref[0, pl.ds(e0 + ee, 1), :], 0.0), axis=1, keepdims=True)
        return jnp.where(hit, 1.0, 0.0).astype(BF16), gate

    pieces = [onehot_and_gate(ee, jnp.where(overflows[ee], cap, starts[ee]), win) for ee in range(n_e)]
    x = jnp.dot(jnp.concatenate([p[0] for p in pieces], axis=0), h2_ref[0], preferred_element_type=F32)
    for ee in range(n_e):
        accumulate(pl.multiple_of(ee * cap + starts[ee], PACKED_ROWS), win, x[ee * win:(ee + 1) * win], pieces[ee][1])

    @pl.when(functools.reduce(jnp.logical_or, overflows))
    def _():
        for ee in range(n_e):
            @pl.when(overflows[ee])
            def _(ee=ee):
                onehot, gate = onehot_and_gate(ee, 0, cap)
                accumulate(ee * cap, cap, jnp.dot(onehot, h2_ref[0], preferred_element_type=F32), gate)


def _gather_win(tab, h2, pos, afft, cap, win, tbs):
    bn, seq_len, _ = h2.shape
    n_slots = N_EXPERTS * cap
    n_e = N_EXPERTS // GATHER_EXPERT_GROUPS
    grid_spec = pltpu.PrefetchScalarGridSpec(
        num_scalar_prefetch=1,
        grid=(bn, N_EXPERTS // n_e, seq_len // tbs),
        in_specs=[pl.BlockSpec((1, tbs, D_MODEL), lambda b, g, t, tab: (b, t, 0)),
                  pl.BlockSpec((1, N_EXPERTS, tbs), lambda b, g, t, tab: (b, 0, t)),
                  pl.BlockSpec((1, N_EXPERTS, tbs), lambda b, g, t, tab: (b, 0, t))],
        out_specs=[pl.BlockSpec((1, n_e * cap, D_MODEL), lambda b, g, t, tab: (b, g, 0)),
                   pl.BlockSpec((1, n_e * cap, LANES), lambda b, g, t, tab: (b, g, 0))],
    )
    return pl.pallas_call(
        functools.partial(_gather_win_kernel, cap=cap, win=win, n_e=n_e),
        grid_spec=grid_spec,
        out_shape=[jax.ShapeDtypeStruct((bn, n_slots, D_MODEL), BF16),
                   jax.ShapeDtypeStruct((bn, n_slots, LANES), F32)],
        compiler_params=_params(("arbitrary", "arbitrary", "arbitrary")),
        name="gather_win",
    )(tab, h2, pos, afft)


def _ffn_kernel(*refs, tiles):
    n = len(tiles)
    xe_refs, gate_refs = refs[0:2 * n:2], refs[1:2 * n:2]
    wg_ref, wu_ref, wd_ref = refs[2 * n:2 * n + 3]
    ye_refs = refs[2 * n + 3:3 * n + 3]
    wg_s, wu_s, wd_s = refs[3 * n + 3:]
    j = pl.program_id(1)

    @pl.when(j == 0)
    def _():
        wg_s[...] = wg_ref[0, 0].astype(BF16)
        wu_s[...] = wu_ref[0, 0].astype(BF16)
        wd_s[...] = wd_ref[0, 0].astype(BF16)

    for (first, steps, rows), xe_ref, gate_ref, ye_ref in zip(tiles, xe_refs, gate_refs, ye_refs):
        @pl.when(jnp.logical_and(j >= first, j < first + steps))
        def _(rows=rows, xe_ref=xe_ref, gate_ref=gate_ref, ye_ref=ye_ref):
            x = xe_ref[...].reshape(rows, D_MODEL)
            g = jnp.dot(x, wg_s[...], preferred_element_type=F32)
            u = jnp.dot(x, wu_s[...], preferred_element_type=F32)
            hid = (g * _sigmoid(g) * u).astype(BF16)
            y = jnp.dot(hid, wd_s[...], preferred_element_type=F32) * gate_ref[...].reshape(rows, LANES)[:, :1]
            ye_ref[...] = y.astype(BF16).reshape(ye_ref.shape)


def _ffn(groups, w_gate, w_up, w_down, layer):
    tiles, in_specs, out_specs, out_shape, args = [], [], [], [], []
    first = 0
    for xe, gate, cap, bb in groups:
        steps = xe.shape[0] // bb
        tile = lambda e, j, first=first, steps=steps: (jnp.clip(j - first, 0, steps - 1), e, 0)
        in_specs += [pl.BlockSpec((bb, cap, D_MODEL), tile), pl.BlockSpec((bb, cap, LANES), tile)]
        out_specs.append(pl.BlockSpec((bb, cap, D_MODEL), tile))
        out_shape.append(jax.ShapeDtypeStruct(xe.shape, BF16))
        args += [xe, gate]
        tiles.append((first, steps, bb * cap))
        first += steps
    wspec = pl.BlockSpec((1, 1, D_MODEL, D_MODEL), lambda e, j: (layer, e, 0, 0))
    return pl.pallas_call(
        functools.partial(_ffn_kernel, tiles=tuple(tiles)),
        grid=(N_EXPERTS, first),
        in_specs=in_specs + [wspec, wspec, wspec],
        out_specs=out_specs,
        out_shape=out_shape,
        scratch_shapes=[pltpu.VMEM((D_MODEL, D_MODEL), BF16)] * 3,
        compiler_params=_params(("arbitrary", "arbitrary")),
        name="ffn",
    )(*args, w_gate, w_up, w_down)


def _finish(x1_ref, mod_ref, fg_ref, out_ref, acc, final, i=0):
    x2 = x1_ref[i] + mod_ref[0][5:6] * acc
    if final:
        x2 = x2 * lax.rsqrt(jnp.mean(x2 * x2, axis=-1, keepdims=True) + EPS) * fg_ref[...]
    out_ref[i] = x2


def _scatter_kernel(x1_ref, ye_ref, post_ref, mod_ref, fg_ref, out_ref, *, cap, final):
    n_rows = x1_ref.shape[0]
    n_slots = N_EXPERTS * cap
    slot = lax.broadcasted_iota(jnp.int32, (1, n_slots), 1)
    expert_of_slot = jnp.zeros((1, n_slots), jnp.int32)
    for e in range(1, N_EXPERTS):
        expert_of_slot = expert_of_slot + (slot >= e * cap).astype(jnp.int32)
    rank_of_slot = (slot - expert_of_slot * cap).astype(F32)
    spread = jnp.where(lax.broadcasted_iota(jnp.int32, (LANES, n_slots), 0) == expert_of_slot, 1.0, 0.0).astype(BF16)
    for i in range(n_rows):
        pt = post_ref[i]
        ranks = jnp.dot(pt.astype(F32).astype(BF16), spread, preferred_element_type=F32)
        onehot = jnp.where(ranks == rank_of_slot, 1.0, 0.0).astype(BF16)
        acc = jnp.dot(onehot, ye_ref[i], preferred_element_type=F32)
        _finish(x1_ref, mod_ref, fg_ref, out_ref, acc, final, i)


def _scatter(x1, ye, post, mod_l, mod_row, final_g, cap, final, rows_per_step):
    bn, seq_len, _ = x1.shape
    assert cap <= 256, "ranks must be exactly representable in bf16"
    n_slots = N_EXPERTS * cap
    rows = lambda *tail: pl.BlockSpec((rows_per_step,) + tail, lambda b: (b, 0, 0))
    return pl.pallas_call(
        functools.partial(_scatter_kernel, cap=cap, final=final),
        grid=(bn // rows_per_step,),
        in_specs=[rows(seq_len, D_MODEL), rows(n_slots, D_MODEL), rows(seq_len, LANES),
                  pl.BlockSpec((1, N_MOD, D_MODEL), lambda b: (mod_row(b * rows_per_step), 0, 0)),
                  pl.BlockSpec((1, D_MODEL), lambda b: (0, 0))],
        out_specs=rows(seq_len, D_MODEL),
        out_shape=jax.ShapeDtypeStruct((bn, seq_len, D_MODEL), F32),
        compiler_params=_params(("arbitrary",)),
        name="scatter",
    )(x1, ye, post, mod_l, final_g)


def _scatter_win_kernel(tab_ref, x1_ref, ye_ref, post_ref, mod_ref, fg_ref, out_ref, acc_ref, *, cap, win, final):
    b = pl.program_id(0)
    t = pl.program_id(1)
    tbs = x1_ref.shape[1]
    pt = post_ref[0]
    starts, overflows = _window_starts(tab_ref, b, 0, N_EXPERTS, t, cap, win)

    def onehot(e, shift, n):
        lane = lax.broadcasted_iota(jnp.int32, (tbs, n), 1)
        return jnp.where((pt[:, e:e + 1] - shift) == lane, 1.0, 0.0).astype(BF16)

    acc = None
    for e0 in range(0, N_EXPERTS, EXPERTS_PER_DOT):
        group = range(e0, e0 + EXPERTS_PER_DOT)
        onehots = [onehot(e, jnp.where(overflows[e], cap, starts[e]), win) for e in group]
        rows = [ye_ref[0, pl.ds(pl.multiple_of(e * cap + starts[e], PACKED_ROWS), win), :] for e in group]
        part = jnp.dot(jnp.concatenate(onehots, axis=1), jnp.concatenate(rows, axis=0), preferred_element_type=F32)
        acc = part if acc is None else acc + part
    acc_ref[...] = acc

    @pl.when(functools.reduce(jnp.logical_or, overflows))
    def _():
        for e in range(N_EXPERTS):
            @pl.when(overflows[e])
            def _(e=e):
                acc_ref[...] += jnp.dot(onehot(e, 0, cap), ye_ref[0, e * cap:(e + 1) * cap, :],
                                        preferred_element_type=F32)

    _finish(x1_ref, mod_ref, fg_ref, out_ref, acc_ref[...], final)


def _scatter_win(tab, x1, ye, post, mod_l, mod_row, final_g, cap, win, tbs, final):
    bn, seq_len, _ = x1.shape
    n_slots = N_EXPERTS * cap
    grid_spec = pltpu.PrefetchScalarGridSpec(
        num_scalar_prefetch=1,
        grid=(bn, seq_len // tbs),
        in_specs=[pl.BlockSpec((1, tbs, D_MODEL), lambda b, t, tab: (b, t, 0)),
                  pl.BlockSpec((1, n_slots, D_MODEL), lambda b, t, tab: (b, 0, 0)),
                  pl.BlockSpec((1, tbs, LANES), lambda b, t, tab: (b, t, 0)),
                  pl.BlockSpec((1, N_MOD, D_MODEL), lambda b, t, tab: (mod_row(b), 0, 0)),
                  pl.BlockSpec((1, D_MODEL), lambda b, t, tab: (0, 0))],
        out_specs=pl.BlockSpec((1, tbs, D_MODEL), lambda b, t, tab: (b, t, 0)),
        scratch_shapes=[pltpu.VMEM((tbs, D_MODEL), F32)],
    )
    return pl.pallas_call(
        functools.partial(_scatter_win_kernel, cap=cap, win=win, final=final),
        grid_spec=grid_spec,
        out_shape=jax.ShapeDtypeStruct((bn, seq_len, D_MODEL), F32),
        compiler_params=_params(("arbitrary", "arbitrary")),
        name="scatter_win",
    )(tab, x1, ye, post, mod_l, final_g)


def _grid_position_table(n_tokens, dim):
    assert n_tokens // GRID_W <= GRID_W
    quarter = dim // 4
    inv_freq = jnp.power(POS_BASE, -jnp.arange(quarter, dtype=F32) / quarter)
    angle = jnp.arange(GRID_W).astype(F32)[:, None] * inv_freq[None]
    return jnp.concatenate([jnp.sin(angle), jnp.cos(angle)], axis=-1)


def _pad_cols(w, width):
    return jnp.pad(w, ((0, 0), (0, width - w.shape[1])))


def _router_pieces(w):
    w_hi = w.astype(BF16)
    w_lo = (w - w_hi.astype(F32)).astype(BF16)
    return jnp.stack([_pad_cols(jnp.concatenate([w_hi, w_lo], axis=1), LANES), _pad_cols(w_hi, LANES)])


def _run_groups(groups, mod, lw, final_g):
    xs_res = [g["x"] for g in groups]
    states = [([], []) for _ in groups]
    for l in range(DEPTH):
        w = lw[l]
        final = l == DEPTH - 1
        mids = []
        for gi, g in enumerate(groups):
            x = xs_res[gi]
            bn, seq_len, _ = x.shape
            cap = CAPACITY_FACTOR * seq_len // N_EXPERTS
            windowed = seq_len >= WINDOWED_MIN_SEQ
            tbs = max(LANES, seq_len // TOKEN_BLOCKS) if windowed else seq_len
            win = cap // WINDOW_DIVISOR
            mod_row, tl = g["mod_row"], g["tl"]
            pos_table = g.get("pos_table") if l == 0 else None
            outs = _in_proj(x, mod[l], mod_row, w["norm_mix_g"], w["w_main"], w["w_dt"], w["pool_w"],
                            w["pool_scale"], w["conv_w"], w["conv_b"], w["alog_row"], w["bias_row"], tl, pos_table,
                            g.get("in_proj_rows", 1))
            y_pool, z, xs, bc, dt, cum = outs[:6]
            if pos_table is not None:
                x = outs[6]
            y_f, y_b, hT_f, hT_b = _ssd(xs, bc, dt, cum, g["h0f"][l], g["h0b"][l], g["h0_row"])
            states[gi][0].append(hT_f)
            states[gi][1].append(hT_b)
            x1, h2, afft = _out_proj(x, y_f, y_b, xs, z, y_pool, w["dsk"], w["ssm_norm_g"], w["w_out"], mod[l],
                                     mod_row, w["norm_ffn_g"], w["w_router"], tl, g.get("in_proj_rows", 1))
            pos, post, tab = _topk(afft, cap, g["topk_rows"], tbs)
            if windowed:
                tab = tab.reshape(-1)
                xe, gate = _gather_win(tab, h2, pos, afft, cap, win, tbs)
            else:
                xe, gate = _gather(h2, pos, afft, cap, g.get("dense_rows", 1))
            mids.append(dict(x1=x1, post=post, tab=tab, xe=xe, gate=gate, cap=cap, win=win, tbs=tbs,
                             windowed=windowed))
        yes = _ffn([(m["xe"], m["gate"], m["cap"], g["ffn_rows_per_step"] // m["cap"])
                    for m, g in zip(mids, groups)], w["w_gate"], w["w_up"], w["w_down"], l)
        for gi, (g, m, ye) in enumerate(zip(groups, mids, yes)):
            if m["windowed"]:
                xs_res[gi] = _scatter_win(m["tab"], m["x1"], ye, m["post"], mod[l], g["mod_row"], final_g, m["cap"],
                                          m["win"], m["tbs"], final)
            else:
                xs_res[gi] = _scatter(m["x1"], ye, m["post"], mod[l], g["mod_row"], final_g, m["cap"], final,
                                      g.get("dense_rows", 1))
    return [(x, sf, sb) for x, (sf, sb) in zip(xs_res, states)]


def kernel(x_prompt, x_sample, state_ssm_fwd, state_ssm_bwd, c, c_ctx, norm_mix_g, w_ada, b_ada, w_in, pool_w,
           pool_scale, conv_w, conv_b, a_log_fwd, a_log_bwd, dt_bias_fwd, dt_bias_bwd, d_skip, ssm_norm_g, w_out,
           norm_ffn_g, w_router, w_gate, w_up, w_down, final_norm_g):
    n_dec = c.shape[0]
    n_ctx = x_prompt.shape[0]
    ctx_row = n_dec
    cond = jnp.concatenate([c, c_ctx[None, :], jnp.zeros((SUBLANES - n_dec - 1, D_MODEL), F32)], axis=0)
    mod = _modulation(cond, w_ada, b_ada).reshape(DEPTH, SUBLANES, N_MOD, D_MODEL)

    zeros_h = jnp.zeros((LANES - 2 * SSD_HEADS,), F32)
    lw = []
    for l in range(DEPTH):
        lw.append(dict(
            norm_mix_g=norm_mix_g[l][None], w_main=w_in[l][:, :N_MAIN].astype(BF16),
            w_dt=_pad_cols(w_in[l][:, N_MAIN:], LANES).astype(BF16), pool_w=pool_w[l].astype(BF16),
            pool_scale=pool_scale[l][None], conv_w=conv_w[l], conv_b=conv_b[l][None],
            alog_row=jnp.concatenate([a_log_fwd[l], a_log_bwd[l], zeros_h])[None],
            bias_row=jnp.concatenate([dt_bias_fwd[l], dt_bias_bwd[l], zeros_h])[None],
            dsk=jnp.repeat(d_skip[l], SSD_INNER // SSD_HEADS)[None], ssm_norm_g=ssm_norm_g[l][None],
            w_out=w_out[l].astype(BF16), norm_ffn_g=norm_ffn_g[l][None], w_router=_router_pieces(w_router[l]),
            w_gate=w_gate, w_up=w_up, w_down=w_down))
    final_g = final_norm_g[None]
    st_shape = (HEAD_PAIRS, LANES, SSD_STATE)

    zero_state = jnp.zeros((1,) + st_shape, F32)
    prompt = dict(x=x_prompt, mod_row=lambda b: ctx_row, h0f=[zero_state] * DEPTH, h0b=[zero_state] * DEPTH,
                  h0_row=lambda b: 0, tl=x_prompt.shape[1], ffn_rows_per_step=1024, topk_rows=n_ctx, dense_rows=4,
                  in_proj_rows=4)
    sample = dict(x=x_sample, mod_row=lambda b: b, h0_row=lambda b: b, tl=1024, ffn_rows_per_step=1024,
                  topk_rows=n_dec, pos_table=_grid_position_table(x_sample.shape[1], D_MODEL),
                  h0f=[state_ssm_fwd[:, l].reshape((n_dec,) + st_shape) for l in range(DEPTH)],
                  h0b=[state_ssm_bwd[:, l].reshape((n_dec,) + st_shape) for l in range(DEPTH)])
    (y_prompt, sf, sb), (y_sample, _, _) = _run_groups([prompt, sample], mod, lw, final_g)
    out_state_shape = (n_ctx, SSD_HEADS, SSD_INNER // SSD_HEADS, SSD_STATE)
    new_f = jnp.stack([s.reshape(out_state_shape) for s in sf], axis=1)
    new_b = jnp.stack([s.reshape(out_state_shape) for s in sb], axis=1)
    return (y_prompt, y_sample, new_f, new_b)
```

```python
import functools

import jax
import jax.numpy as jnp
from jax import lax
from jax.experimental import pallas as pl
from jax.experimental.pallas import tpu as pltpu

D_MODEL = 1024
DEPTH = 2
GRID_W = 64
MIX_POOL = 512
POOL_WINDOWS = (2, 4, 8, 16)
POOL_GROUP_CH = 128
SSD_INNER = 512
SSD_HEADS = 8
SSD_HEAD_DIM = SSD_INNER // SSD_HEADS
SSD_STATE = 128
CONV_K = 5
CHUNK = 128
CONV_DIM = 1024
N_EXPERTS = 16
CAPACITY_FACTOR = 2
N_MOD = 6
POS_BASE = 10000.0
EPS = 1e-6
LOG2_E = 1.4426950408889634

SUBLANES = 8
LANES = 128
PACKED_ROWS = 16
HALO = SUBLANES
N_MAIN = MIX_POOL + SSD_INNER + CONV_DIM
HEAD_PAIRS = SSD_HEADS // 2
EXPERTS_PER_DOT = 2
SSD_MAX_CHUNKS_PER_STEP = 4
MOD_COLS_PER_STEP = 1536
WINDOWED_MIN_SEQ = 1024
TOKEN_BLOCKS = 8
WINDOW_DIVISOR = 4
GATHER_EXPERT_GROUPS = 1
FLOAT_MAGNITUDE_BITS = 31
VMEM_LIMIT = 56 * 1024 * 1024

F32 = jnp.float32
BF16 = jnp.bfloat16
HI = lax.Precision.HIGHEST


def _sigmoid(x):
    return 1.0 / (1.0 + jnp.exp(-x))


def _params(sem):
    return pltpu.CompilerParams(dimension_semantics=sem, vmem_limit_bytes=VMEM_LIMIT)


def _mod_kernel(cond_ref, w_ref, b_ref, out_ref):
    c = cond_ref[...]
    s = c * _sigmoid(c)
    out_ref[0] = jnp.dot(s, w_ref[0], precision=HI, preferred_element_type=F32) + b_ref[0]


def _modulation(cond, w_ada, b_ada):
    n_rows = cond.shape[0]
    n_out = N_MOD * D_MODEL
    tn = MOD_COLS_PER_STEP
    return pl.pallas_call(
        _mod_kernel,
        grid=(DEPTH, n_out // tn),
        in_specs=[
            pl.BlockSpec((n_rows, D_MODEL), lambda l, j: (0, 0)),
            pl.BlockSpec((1, D_MODEL, tn), lambda l, j: (l, 0, j)),
            pl.BlockSpec((1, 1, tn), lambda l, j: (l, 0, j)),
        ],
        out_specs=pl.BlockSpec((1, n_rows, tn), lambda l, j: (l, 0, j)),
        out_shape=jax.ShapeDtypeStruct((DEPTH, n_rows, n_out), F32),
        compiler_params=_params(("arbitrary", "arbitrary")),
        name="modulation",
    )(cond, w_ada, b_ada.reshape(DEPTH, 1, n_out))


def _cumsum_pieces(dta):
    d_hi = dta.astype(BF16)
    rest = dta - d_hi.astype(F32)
    d_mid = rest.astype(BF16)
    return d_hi, d_mid, (rest - d_mid.astype(F32)).astype(BF16)


def _grid_position_rows(table_ref, grid_row, col0, n_cols):
    row_part = jnp.broadcast_to(table_ref[pl.ds(grid_row, 1), :], (n_cols, D_MODEL // 2))
    return jnp.concatenate([row_part, table_ref[col0:col0 + n_cols, :]], axis=1)


def _in_row(*refs, tl, seq_len, with_pos, req):
    (xp_ref, xc_ref, xn_ref, mod_ref, g_ref, wmain_ref, wdt_ref, poolw_ref, pscale_ref, cw_ref, cb_ref,
     alog_ref, bias_ref) = refs[:13]
    table_ref = refs[13] if with_pos else None
    ypool_ref, z_ref, xs_ref, bc_ref, dt_ref, cum_ref = refs[13 + with_pos:19 + with_pos]
    x0_ref = refs[19 + with_pos] if with_pos else None
    proj_scr = refs[-1].at[req]
    i = pl.program_id(1)
    n = pl.num_programs(1)
    rows = tl + 2 * HALO
    x_ext = jnp.concatenate([xp_ref[0], xc_ref[req], xn_ref[0]], axis=0)
    if with_pos:
        r0 = i * (tl // GRID_W)
        last_row = seq_len // GRID_W - 1
        pieces = [_grid_position_rows(table_ref, jnp.maximum(r0 - 1, 0), GRID_W - HALO, HALO)]
        pieces += [_grid_position_rows(table_ref, r0 + k, 0, GRID_W) for k in range(tl // GRID_W)]
        pieces += [_grid_position_rows(table_ref, jnp.minimum(r0 + tl // GRID_W, last_row), 0, HALO)]
        x_ext = x_ext + jnp.concatenate(pieces, axis=0)
        x0_ref[req] = x_ext[HALO:HALO + tl]
    mod = mod_ref[0]
    ms = jnp.mean(x_ext * x_ext, axis=-1, keepdims=True)
    h = x_ext * lax.rsqrt(ms + EPS) * (g_ref[...] * (1.0 + mod[1:2])) + mod[0:1]
    row = lax.broadcasted_iota(jnp.int32, (rows, 1), 0)
    lo_ok = jnp.where(i > 0, 0, HALO)
    hi_ok = jnp.where(i < n - 1, rows, tl + HALO)
    hb = jnp.where((row >= lo_ok) & (row < hi_ok), h, 0.0).astype(BF16)
    proj_scr[...] = jnp.dot(hb, wmain_ref[...], preferred_element_type=F32)
    dt_full = jnp.dot(hb, wdt_ref[...], preferred_element_type=F32)

    x_in = dt_full[HALO:HALO + tl] + bias_ref[...]
    dt = jnp.maximum(x_in, 0.0) + jnp.log1p(jnp.exp(-jnp.abs(x_in)))
    dt_ref[req] = dt
    r = lax.broadcasted_iota(jnp.int32, (CHUNK, CHUNK), 0)
    s = lax.broadcasted_iota(jnp.int32, (CHUNK, CHUNK), 1)
    tri = jnp.concatenate([jnp.where(r >= s, 1.0, 0.0), jnp.where(r <= s, 1.0, 0.0)], axis=1).astype(BF16)
    head = lax.broadcasted_iota(jnp.int32, (tl, LANES), 1)
    dta = jnp.where(head < 2 * SSD_HEADS, dt * (-LOG2_E * jnp.exp(alog_ref[...])), 0.0)
    pieces = _cumsum_pieces(dta)
    fwd_cols = s < SSD_HEADS
    for k in range(tl // CHUNK):
        acc = jnp.zeros((CHUNK, LANES), F32)
        for piece in pieces:
            pk = piece[k * CHUNK:(k + 1) * CHUNK]
            zero = jnp.zeros_like(pk)
            stacked = jnp.concatenate([jnp.where(fwd_cols, pk, zero), jnp.where(fwd_cols, zero, pk)], axis=0)
            acc = acc + jnp.dot(tri, stacked, preferred_element_type=F32)
        cum_ref[req, k * CHUNK:(k + 1) * CHUNK, :] = acc

    def ahead(v, k):
        return pltpu.roll(v, (-k) % rows, axis=0)

    t_glob = i * tl + lax.broadcasted_iota(jnp.int32, (tl, 1), 0)
    outs = []
    for gi, w in enumerate(POOL_WINDOWS):
        c0 = gi * POOL_GROUP_CH
        u = proj_scr[:, c0:c0 + POOL_GROUP_CH]
        acc = u
        span = 1
        while 2 * span < w:
            acc = acc + ahead(acc, span)
            span *= 2
        acc = acc + ahead(acc, -span)
        lo = jnp.maximum(t_glob - w // 2, 0)
        hi = jnp.minimum(t_glob - w // 2 + w, seq_len)
        cnt = (hi - lo).astype(F32)
        p = acc[HALO:HALO + tl] / cnt - u[HALO:HALO + tl]
        outs.append(jnp.dot(p.astype(BF16), poolw_ref[gi], preferred_element_type=F32))
    ypool_ref[req] = (jnp.concatenate(outs, axis=1) * pscale_ref[...]).astype(BF16)

    c0 = MIX_POOL + SSD_INNER
    acc = cb_ref[...] + cw_ref[0:1, :] * proj_scr[HALO - 2:HALO - 2 + tl, c0:c0 + CONV_DIM]
    for k in range(1, CONV_K):
        acc = acc + cw_ref[k:k + 1, :] * proj_scr[HALO - 2 + k:HALO - 2 + k + tl, c0:c0 + CONV_DIM]
    act = acc * _sigmoid(acc)
    xs_ref[req] = act[:, :SSD_INNER].astype(BF16)
    bc_ref[req] = act[:, SSD_INNER:].astype(BF16)
    z_ref[req] = proj_scr[HALO:HALO + tl, MIX_POOL:MIX_POOL + SSD_INNER]


def _in_kernel(*refs, tl, seq_len, with_pos):
    for req in range(refs[1].shape[0]):
        _in_row(*refs, tl=tl, seq_len=seq_len, with_pos=with_pos, req=req)


def _in_proj(x, mod_l, mod_row, norm_g, w_main, w_dt, pool_w, pool_scale, conv_w, conv_b, alog_row, bias_row, tl,
             pos_table=None, rows_per_step=1):
    bn, seq_len, _ = x.shape
    nt = seq_len // tl
    rps = rows_per_step
    assert rps == 1 or nt == 1
    hb = tl // HALO
    last_halo = seq_len // HALO - 1
    with_pos = pos_table is not None
    const = lambda *shape: pl.BlockSpec(shape, lambda b, i: (0,) * len(shape))
    tok = lambda width: pl.BlockSpec((rps, tl, width), lambda b, i: (b, i, 0))
    in_specs = [
        pl.BlockSpec((1, HALO, D_MODEL), lambda b, i: (b * rps, jnp.maximum(i * hb - 1, 0), 0)),
        tok(D_MODEL),
        pl.BlockSpec((1, HALO, D_MODEL), lambda b, i: (b * rps, jnp.minimum((i + 1) * hb, last_halo), 0)),
        pl.BlockSpec((1, N_MOD, D_MODEL), lambda b, i: (mod_row(b * rps), 0, 0)),
        const(1, D_MODEL),
        const(D_MODEL, N_MAIN),
        const(D_MODEL, LANES),
        const(len(POOL_WINDOWS), POOL_GROUP_CH, POOL_GROUP_CH),
        const(1, MIX_POOL),
        const(CONV_K, CONV_DIM),
        const(1, CONV_DIM),
        const(1, LANES),
        const(1, LANES),
    ]
    out_specs = [tok(MIX_POOL), tok(SSD_INNER), tok(SSD_INNER), tok(CONV_DIM - SSD_INNER), tok(LANES), tok(LANES)]
    out_shape = [
        jax.ShapeDtypeStruct((bn, seq_len, MIX_POOL), BF16),
        jax.ShapeDtypeStruct((bn, seq_len, SSD_INNER), F32),
        jax.ShapeDtypeStruct((bn, seq_len, SSD_INNER), BF16),
        jax.ShapeDtypeStruct((bn, seq_len, CONV_DIM - SSD_INNER), BF16),
        jax.ShapeDtypeStruct((bn, seq_len, LANES), F32),
        jax.ShapeDtypeStruct((bn, seq_len, LANES), F32),
    ]
    args = [x, x, x, mod_l, norm_g, w_main, w_dt, pool_w, pool_scale, conv_w, conv_b, alog_row, bias_row]
    if with_pos:
        assert tl % GRID_W == 0 and pos_table.shape == (GRID_W, D_MODEL // 2)
        in_specs.append(const(GRID_W, D_MODEL // 2))
        out_specs.append(tok(D_MODEL))
        out_shape.append(jax.ShapeDtypeStruct((bn, seq_len, D_MODEL), F32))
        args.append(pos_table)
    return pl.pallas_call(
        functools.partial(_in_kernel, tl=tl, seq_len=seq_len, with_pos=with_pos),
        grid=(bn // rps, nt),
        in_specs=in_specs,
        out_specs=out_specs,
        out_shape=out_shape,
        scratch_shapes=[pltpu.VMEM((rps, tl + 2 * HALO, N_MAIN), F32)],
        compiler_params=_params(("arbitrary", "arbitrary")),
        name="in_proj",
    )(*args)


def _ssd_chunk(xsf_ref, bcf_ref, dtf_ref, cumf_ref, xsb_ref, bcb_ref, dtb_ref, cumb_ref, yf_ref, yb_ref, st_ref, rf, rb):
    r = lax.broadcasted_iota(jnp.int32, (CHUNK, CHUNK), 0)
    s = lax.broadcasted_iota(jnp.int32, (CHUNK, CHUNK), 1)
    lane2 = lax.broadcasted_iota(jnp.int32, (CHUNK, 2 * LANES), 1)
    lane_head = lax.shift_right_logical(lane2, SSD_HEAD_DIM.bit_length() - 1)
    row2 = lax.broadcasted_iota(jnp.int32, (2 * LANES, SSD_STATE), 0)
    is_fwd = s < SSD_HEADS
    dt = jnp.where(is_fwd, dtf_ref[0, rf:rf + CHUNK, :], dtb_ref[0, rb:rb + CHUNK, :])
    cum = jnp.where(is_fwd, cumf_ref[0, rf:rf + CHUNK, :], cumb_ref[0, rb:rb + CHUNK, :])
    tot = jnp.where(lax.broadcasted_iota(jnp.int32, (1, LANES), 1) < SSD_HEADS,
                    cum[CHUNK - 1:CHUNK, :], cum[0:1, :])
    cdec = jnp.exp2(tot)
    cum_t = cum.T
    dt_t = dt.T
    ecum_t = jnp.exp2(cum_t)
    w_t = (dt * jnp.exp2(tot - cum)).T

    def head_rows(arr_t, h0):
        return jnp.concatenate([jnp.broadcast_to(arr_t[h0 + k:h0 + k + 1, :], (SSD_HEAD_DIM, CHUNK))
                                for k in range(4)], axis=0)

    def one_dir(d, xs_ref, bc_ref, y_ref, r0):
        mask = (r >= s) if d == 0 else (r <= s)
        off = d * SSD_HEADS
        bcv = bc_ref[0, r0:r0 + CHUNK, :]
        b01 = bcv[:, :2 * SSD_STATE].astype(BF16)
        c01 = bcv[:, 2 * SSD_STATE:].astype(BF16)
        zero2 = jnp.zeros_like(b01)
        b_blk = jnp.concatenate([jnp.where(lane2 < SSD_STATE, b01, zero2),
                                 jnp.where(lane2 < SSD_STATE, zero2, b01)], axis=0)
        cb01 = lax.dot_general(c01, b_blk, (((1,), (1,)), ((), ())), preferred_element_type=F32)
        for g in range(2):
            h0 = off + 4 * g
            b_g = b01[:, g * SSD_STATE:(g + 1) * SSD_STATE]
            c_g = c01[:, g * SSD_STATE:(g + 1) * SSD_STATE]
            cb = cb01[:, g * CHUNK:(g + 1) * CHUNK]
            xs_g = xs_ref[0, r0:r0 + CHUNK, g * 2 * LANES:(g + 1) * 2 * LANES]
            x_b = xs_g.astype(BF16)
            ms, x_parts = [], []
            for k in range(4):
                hh = h0 + k
                seg = cum[:, hh:hh + 1] - cum_t[hh:hh + 1, :]
                dec = jnp.exp2(jnp.where(mask, seg, -jnp.inf))
                ms.append((cb * dec * dt_t[hh:hh + 1, :]).astype(BF16))
                x_parts.append(jnp.where(lane_head == k, x_b, jnp.zeros_like(x_b)))
            y_diag = jnp.dot(jnp.concatenate(ms, axis=1), jnp.concatenate(x_parts, axis=0),
                             preferred_element_type=F32)
            st = st_ref[d, 2 * g:2 * g + 2].reshape(2 * LANES, SSD_STATE)
            y_off_t = lax.dot_general(st.astype(BF16), c_g, (((1,), (1,)), ((), ())),
                                      preferred_element_type=F32) * head_rows(ecum_t, h0)
            y_ref[0, r0:r0 + CHUNK, g * 2 * LANES:(g + 1) * 2 * LANES] = (y_diag + y_off_t.T).astype(BF16)
            x_d = (xs_g.astype(F32).T * head_rows(w_t, h0)).astype(BF16)
            cs = jnp.dot(x_d, b_g, preferred_element_type=F32)
            dcol = cdec[:, h0 + 3:h0 + 4]
            for k in (2, 1, 0):
                dcol = jnp.where(row2 < (k + 1) * SSD_HEAD_DIM, cdec[:, h0 + k:h0 + k + 1], dcol)
            st_ref[d, 2 * g:2 * g + 2] = (st * dcol + cs).reshape(2, LANES, SSD_STATE)

    one_dir(0, xsf_ref, bcf_ref, yf_ref, rf)
    one_dir(1, xsb_ref, bcb_ref, yb_ref, rb)


def _ssd_kernel(xsf_ref, bcf_ref, dtf_ref, cumf_ref, xsb_ref, bcb_ref, dtb_ref, cumb_ref, h0f_ref, h0b_ref,
                yf_ref, yb_ref, hTf_ref, hTb_ref, st_ref):
    c = pl.program_id(1)
    nc = pl.num_programs(1)

    @pl.when(c == 0)
    def _():
        st_ref[0] = h0f_ref[0]
        st_ref[1] = h0b_ref[0]

    n_chunks = xsf_ref.shape[1] // CHUNK
    for k in range(n_chunks):
        _ssd_chunk(xsf_ref, bcf_ref, dtf_ref, cumf_ref, xsb_ref, bcb_ref, dtb_ref, cumb_ref, yf_ref, yb_ref, st_ref,
                   k * CHUNK, (n_chunks - 1 - k) * CHUNK)

    @pl.when(c == nc - 1)
    def _():
        hTf_ref[0] = st_ref[0]
        hTb_ref[0] = st_ref[1]


def _ssd(xs, bc, dt, cum, h0f, h0b, h0_row):
    bn, seq_len, _ = xs.shape
    step_rows = min(SSD_MAX_CHUNKS_PER_STEP * CHUNK, seq_len)
    nc = seq_len // step_rows
    fwd = lambda width: pl.BlockSpec((1, step_rows, width), lambda b, c: (b, c, 0))
    bwd = lambda width: pl.BlockSpec((1, step_rows, width), lambda b, c: (b, nc - 1 - c, 0))
    st_spec_in = pl.BlockSpec((1, HEAD_PAIRS, LANES, SSD_STATE), lambda b, c: (h0_row(b), 0, 0, 0))
    st_spec_out = pl.BlockSpec((1, HEAD_PAIRS, LANES, SSD_STATE), lambda b, c: (b, 0, 0, 0))
    st_shape = jax.ShapeDtypeStruct((bn, HEAD_PAIRS, LANES, SSD_STATE), F32)
    return pl.pallas_call(
        _ssd_kernel,
        grid=(bn, nc),
        in_specs=[fwd(SSD_INNER), fwd(CONV_DIM - SSD_INNER), fwd(LANES), fwd(LANES),
                  bwd(SSD_INNER), bwd(CONV_DIM - SSD_INNER), bwd(LANES), bwd(LANES),
                  st_spec_in, st_spec_in],
        out_specs=[fwd(SSD_INNER), bwd(SSD_INNER), st_spec_out, st_spec_out],
        out_shape=[jax.ShapeDtypeStruct((bn, seq_len, SSD_INNER), BF16),
                   jax.ShapeDtypeStruct((bn, seq_len, SSD_INNER), BF16), st_shape, st_shape],
        scratch_shapes=[pltpu.VMEM((2, HEAD_PAIRS, LANES, SSD_STATE), F32)],
        compiler_params=_params(("arbitrary", "arbitrary")),
        name="ssd",
    )(xs, bc, dt, cum, xs, bc, dt, cum, h0f, h0b)


def _out_row(x_ref, yf_ref, yb_ref, xs_ref, z_ref, yp_ref, dsk_ref, sg_ref, wo_ref, mod_ref, ng_ref, wr_ref,
             x1_ref, h2_ref, afft_ref, req):
    y = yf_ref[req].astype(F32) + yb_ref[req].astype(F32) + dsk_ref[...] * xs_ref[req].astype(F32)
    z = z_ref[req]
    y = y * (z * _sigmoid(z))
    y = y * lax.rsqrt(jnp.mean(y * y, axis=-1, keepdims=True) + EPS) * sg_ref[...]
    o = (jnp.dot(yp_ref[req], wo_ref[:MIX_POOL, :], preferred_element_type=F32)
         + jnp.dot(y.astype(BF16), wo_ref[MIX_POOL:, :], preferred_element_type=F32))
    mod = mod_ref[0]
    x1 = x_ref[req] + mod[2:3] * o
    x1_ref[req] = x1
    h2 = x1 * lax.rsqrt(jnp.mean(x1 * x1, axis=-1, keepdims=True) + EPS) * (ng_ref[...] * (1.0 + mod[4:5])) + mod[3:4]
    h2_hi = h2.astype(BF16)
    h2_ref[req] = h2_hi
    h2_lo = (h2 - h2_hi.astype(F32)).astype(BF16)
    pa = jnp.dot(h2_hi, wr_ref[0], preferred_element_type=F32)
    pb = jnp.dot(h2_lo, wr_ref[1], preferred_element_type=F32)
    logits = pa + pltpu.roll(pa, LANES - N_EXPERTS, axis=1) + pb
    lane = lax.broadcasted_iota(jnp.int32, logits.shape, 1)
    lg = jnp.where(lane < N_EXPERTS, logits, -jnp.inf)
    e = jnp.exp(lg - jnp.max(lg, axis=-1, keepdims=True))
    aff = e / jnp.sum(e, axis=-1, keepdims=True)
    afft_ref[req] = aff.T[:N_EXPERTS, :]


def _out_kernel(*refs):
    for req in range(refs[0].shape[0]):
        _out_row(*refs, req=req)


def _out_proj(x, y_f, y_b, xs, z, y_pool, dsk, ssm_g, w_out, mod_l, mod_row, norm_ffn_g, w_router, tl,
              rows_per_step=1):
    bn, seq_len, _ = x.shape
    nt = seq_len // tl
    rps = rows_per_step
    const = lambda *shape: pl.BlockSpec(shape, lambda b, i: (0,) * len(shape))
    tok = lambda width: pl.BlockSpec((rps, tl, width), lambda b, i: (b, i, 0))
    return pl.pallas_call(
        _out_kernel,
        grid=(bn // rps, nt),
        in_specs=[tok(D_MODEL), tok(SSD_INNER), tok(SSD_INNER), tok(SSD_INNER), tok(SSD_INNER), tok(MIX_POOL),
                  const(1, SSD_INNER), const(1, SSD_INNER), const(D_MODEL, D_MODEL),
                  pl.BlockSpec((1, N_MOD, D_MODEL), lambda b, i: (mod_row(b * rps), 0, 0)),
                  const(1, D_MODEL), const(2, D_MODEL, LANES)],
        out_specs=[tok(D_MODEL), tok(D_MODEL), pl.BlockSpec((rps, N_EXPERTS, tl), lambda b, i: (b, 0, i))],
        out_shape=[jax.ShapeDtypeStruct((bn, seq_len, D_MODEL), F32),
                   jax.ShapeDtypeStruct((bn, seq_len, D_MODEL), BF16),
                   jax.ShapeDtypeStruct((bn, N_EXPERTS, seq_len), F32)],
        compiler_params=_params(("arbitrary", "arbitrary")),
        name="out_proj",
    )(x, y_f, y_b, xs, z, y_pool, dsk, ssm_g, w_out, mod_l, norm_ffn_g, w_router)


def _topk_kernel(afft_ref, pos_ref, post_ref, tab_ref, *, seq_len, cap, rb, tbs):
    n_rows = rb * N_EXPERTS
    a = afft_ref[...].reshape(n_rows, seq_len)

    def body(it, thr):
        cand_bits = thr | jnp.left_shift(jnp.int32(1), FLOAT_MAGNITUDE_BITS - 1 - it)
        cnt = jnp.sum((a >= pltpu.bitcast(cand_bits, F32)).astype(jnp.int32), axis=1, keepdims=True)
        return jnp.where(cnt >= cap, cand_bits, thr)

    thr = lax.fori_loop(0, FLOAT_MAGNITUDE_BITS, body, jnp.zeros((n_rows, 1), jnp.int32))
    lo = pltpu.bitcast(thr, F32)
    hi = pltpu.bitcast(thr + 1, F32)
    gt = a >= hi
    eq = (a >= lo) & (a < hi)
    need = cap - jnp.sum(gt.astype(jnp.int32), axis=1, keepdims=True)

    r = lax.broadcasted_iota(jnp.int32, (LANES, LANES), 0)
    s = lax.broadcasted_iota(jnp.int32, (LANES, LANES), 1)
    strict_upper = (r < s).astype(BF16)
    nblk = seq_len // LANES

    def excl_cumsum(m):
        carry = jnp.zeros((n_rows, 1), F32)
        outs = []
        for blk in range(nblk):
            mb = m[:, blk * LANES:(blk + 1) * LANES]
            outs.append(jnp.dot(mb.astype(BF16), strict_upper, preferred_element_type=F32) + carry)
            carry = carry + jnp.sum(mb, axis=1, keepdims=True)
        return jnp.concatenate(outs, axis=1)

    eq_rank = excl_cumsum(jnp.where(eq, 1.0, 0.0))
    sel = gt | (eq & (eq_rank < need.astype(F32)))
    rank = excl_cumsum(jnp.where(sel, 1.0, 0.0))
    pos = jnp.where(sel, rank, -1.0)
    pos_ref[...] = pos.astype(jnp.int32).reshape(rb, N_EXPERTS, seq_len)

    lane = lax.broadcasted_iota(jnp.int32, (n_rows, LANES), 1)
    tab = jnp.full((n_rows, LANES), float(cap), F32)
    for k in range(seq_len // tbs):
        tab = jnp.where(lane == k, rank[:, k * tbs:k * tbs + 1], tab)
    tab_ref[...] = tab.astype(jnp.int32).reshape(rb, N_EXPERTS, LANES)

    pad = jnp.full((LANES - N_EXPERTS, LANES), -1.0, F32)
    for i in range(rb):
        for blk in range(nblk):
            tile = jnp.concatenate([pos[i * N_EXPERTS:(i + 1) * N_EXPERTS, blk * LANES:(blk + 1) * LANES], pad], axis=0)
            post_ref[i, blk * LANES:(blk + 1) * LANES, :] = tile.T.astype(jnp.int32)


def _topk(afft, cap, rb, tbs):
    bn, _, seq_len = afft.shape
    return pl.pallas_call(
        functools.partial(_topk_kernel, seq_len=seq_len, cap=cap, rb=rb, tbs=tbs),
        grid=(bn // rb,),
        in_specs=[pl.BlockSpec((rb, N_EXPERTS, seq_len), lambda b: (b, 0, 0))],
        out_specs=[pl.BlockSpec((rb, N_EXPERTS, seq_len), lambda b: (b, 0, 0)),
                   pl.BlockSpec((rb, seq_len, LANES), lambda b: (b, 0, 0)),
                   pl.BlockSpec((rb, N_EXPERTS, LANES), lambda b: (b, 0, 0))],
        out_shape=[jax.ShapeDtypeStruct((bn, N_EXPERTS, seq_len), jnp.int32),
                   jax.ShapeDtypeStruct((bn, seq_len, LANES), jnp.int32),
                   jax.ShapeDtypeStruct((bn, N_EXPERTS, LANES), jnp.int32)],
        compiler_params=_params(("arbitrary",)),
        name="topk",
    )(afft)


def _gather_kernel(h2_ref, pos_ref, afft_ref, xe_ref, gate_ref, *, cap):
    n_rows, seq_len, _ = h2_ref.shape
    n_slots = N_EXPERTS * cap
    rank = lax.broadcasted_iota(jnp.int32, (cap, seq_len), 0)
    for i in range(n_rows):
        onehots, gates = [], []
        for e in range(N_EXPERTS):
            hit = pos_ref[i, e:e + 1, :] == rank
            onehots.append(jnp.where(hit, 1.0, 0.0).astype(BF16))
            gates.append(jnp.sum(jnp.where(hit, afft_ref[i, e:e + 1, :], 0.0), axis=1, keepdims=True))
        xe_ref[i] = jnp.dot(jnp.concatenate(onehots, axis=0), h2_ref[i], preferred_element_type=F32).astype(BF16)
        gate_ref[i] = jnp.broadcast_to(jnp.concatenate(gates, axis=0), (n_slots, LANES))


def _gather(h2, pos, afft, cap, rows_per_step):
    bn, seq_len, _ = h2.shape
    n_slots = N_EXPERTS * cap
    rows = lambda *tail: pl.BlockSpec((rows_per_step,) + tail, lambda b: (b, 0, 0))
    return pl.pallas_call(
        functools.partial(_gather_kernel, cap=cap),
        grid=(bn // rows_per_step,),
        in_specs=[rows(seq_len, D_MODEL), rows(N_EXPERTS, seq_len), rows(N_EXPERTS, seq_len)],
        out_specs=[rows(n_slots, D_MODEL), rows(n_slots, LANES)],
        out_shape=[jax.ShapeDtypeStruct((bn, n_slots, D_MODEL), BF16),
                   jax.ShapeDtypeStruct((bn, n_slots, LANES), F32)],
        compiler_params=_params(("arbitrary",)),
        name="gather",
    )(h2, pos, afft)


def _window_starts(tab_ref, b, e0, n_e, t, cap, win):
    starts, overflows = [], []
    for ee in range(n_e):
        base = (b * N_EXPERTS + e0 + ee) * LANES + t
        p0 = tab_ref[base]
        p1 = tab_ref[base + 1]
        align_bits = PACKED_ROWS.bit_length() - 1
        w0 = jnp.minimum(lax.shift_left(lax.shift_right_logical(p0, align_bits), align_bits), cap - win)
        starts.append(w0)
        overflows.append(p1 - w0 > win)
    return starts, overflows


def _gather_win_kernel(tab_ref, h2_ref, pos_ref, afft_ref, xe_ref, gate_ref, *, cap, win, n_e):
    b = pl.program_id(0)
    e0 = pl.program_id(1) * n_e
    t = pl.program_id(2)
    tbs = h2_ref.shape[1]

    @pl.when(t == 0)
    def _():
        xe_ref[...] = jnp.zeros(xe_ref.shape, BF16)
        gate_ref[...] = jnp.zeros(gate_ref.shape, F32)

    starts, overflows = _window_starts(tab_ref, b, e0, n_e, t, cap, win)

    def accumulate(row0, n, x, g):
        cur = xe_ref[0, pl.ds(row0, n), :]
        xe_ref[0, pl.ds(row0, n), :] = jnp.where(g > 0.0, x.astype(BF16), cur)
        gate_ref[0, pl.ds(row0, n), :] = gate_ref[0, pl.ds(row0, n), :] + jnp.broadcast_to(g, (n, LANES))

    def onehot_and_gate(ee, shift, n):
        sub = lax.broadcasted_iota(jnp.int32, (n, tbs), 0)
        hit = (pos_ref[0, pl.ds(e0 + ee, 1), :] - shift) == sub
        gate = jnp.sum(jnp.where(hit, afft_ref[0, pl.ds(e0 + ee, 1), :], 0.0), axis=1, keepdims=True)
        return jnp.where(hit, 1.0, 0.0).astype(BF16), gate

    pieces = [onehot_and_gate(ee, jnp.where(overflows[ee], cap, starts[ee]), win) for ee in range(n_e)]
    x = jnp.dot(jnp.concatenate([p[0] for p in pieces], axis=0), h2_ref[0], preferred_element_type=F32)
    for ee in range(n_e):
        accumulate(pl.multiple_of(ee * cap + starts[ee], PACKED_ROWS), win, x[ee * win:(ee + 1) * win], pieces[ee][1])

    for ee in range(n_e):
        @pl.when(overflows[ee])
        def _(ee=ee):
            onehot, gate = onehot_and_gate(ee, 0, cap)
            accumulate(ee * cap, cap, jnp.dot(onehot, h2_ref[0], preferred_element_type=F32), gate)


def _gather_win(tab, h2, pos, afft, cap, win, tbs):
    bn, seq_len, _ = h2.shape
    n_slots = N_EXPERTS * cap
    n_e = N_EXPERTS // GATHER_EXPERT_GROUPS
    grid_spec = pltpu.PrefetchScalarGridSpec(
        num_scalar_prefetch=1,
        grid=(bn, N_EXPERTS // n_e, seq_len // tbs),
        in_specs=[pl.BlockSpec((1, tbs, D_MODEL), lambda b, g, t, tab: (b, t, 0)),
                  pl.BlockSpec((1, N_EXPERTS, tbs), lambda b, g, t, tab: (b, 0, t)),
                  pl.BlockSpec((1, N_EXPERTS, tbs), lambda b, g, t, tab: (b, 0, t))],
        out_specs=[pl.BlockSpec((1, n_e * cap, D_MODEL), lambda b, g, t, tab: (b, g, 0)),
                   pl.BlockSpec((1, n_e * cap, LANES), lambda b, g, t, tab: (b, g, 0))],
    )
    return pl.pallas_call(
        functools.partial(_gather_win_kernel, cap=cap, win=win, n_e=n_e),
        grid_spec=grid_spec,
        out_shape=[jax.ShapeDtypeStruct((bn, n_slots, D_MODEL), BF16),
                   jax.ShapeDtypeStruct((bn, n_slots, LANES), F32)],
        compiler_params=_params(("arbitrary", "arbitrary", "arbitrary")),
        name="gather_win",
    )(tab, h2, pos, afft)


def _ffn_kernel(*refs, tiles):
    n = len(tiles)
    xe_refs, gate_refs = refs[0:2 * n:2], refs[1:2 * n:2]
    wg_ref, wu_ref, wd_ref = refs[2 * n:2 * n + 3]
    ye_refs = refs[2 * n + 3:3 * n + 3]
    wg_s, wu_s, wd_s = refs[3 * n + 3:]
    j = pl.program_id(1)

    @pl.when(j == 0)
    def _():
        wg_s[...] = wg_ref[0, 0].astype(BF16)
        wu_s[...] = wu_ref[0, 0].astype(BF16)
        wd_s[...] = wd_ref[0, 0].astype(BF16)

    for (first, steps, rows), xe_ref, gate_ref, ye_ref in zip(tiles, xe_refs, gate_refs, ye_refs):
        @pl.when(jnp.logical_and(j >= first, j < first + steps))
        def _(rows=rows, xe_ref=xe_ref, gate_ref=gate_ref, ye_ref=ye_ref):
            x = xe_ref[...].reshape(rows, D_MODEL)
            g = jnp.dot(x, wg_s[...], preferred_element_type=F32)
            u = jnp.dot(x, wu_s[...], preferred_element_type=F32)
            hid = (g * _sigmoid(g) * u).astype(BF16)
            y = jnp.dot(hid, wd_s[...], preferred_element_type=F32) * gate_ref[...].reshape(rows, LANES)[:, :1]
            ye_ref[...] = y.astype(BF16).reshape(ye_ref.shape)


def _ffn(groups, w_gate, w_up, w_down, layer):
    tiles, in_specs, out_specs, out_shape, args = [], [], [], [], []
    first = 0
    for xe, gate, cap, bb in groups:
        steps = xe.shape[0] // bb
        tile = lambda e, j, first=first, steps=steps: (jnp.clip(j - first, 0, steps - 1), e, 0)
        in_specs += [pl.BlockSpec((bb, cap, D_MODEL), tile), pl.BlockSpec((bb, cap, LANES), tile)]
        out_specs.append(pl.BlockSpec((bb, cap, D_MODEL), tile))
        out_shape.append(jax.ShapeDtypeStruct(xe.shape, BF16))
        args += [xe, gate]
        tiles.append((first, steps, bb * cap))
        first += steps
    wspec = pl.BlockSpec((1, 1, D_MODEL, D_MODEL), lambda e, j: (layer, e, 0, 0))
    return pl.pallas_call(
        functools.partial(_ffn_kernel, tiles=tuple(tiles)),
        grid=(N_EXPERTS, first),
        in_specs=in_specs + [wspec, wspec, wspec],
        out_specs=out_specs,
        out_shape=out_shape,
        scratch_shapes=[pltpu.VMEM((D_MODEL, D_MODEL), BF16)] * 3,
        compiler_params=_params(("arbitrary", "arbitrary")),
        name="ffn",
    )(*args, w_gate, w_up, w_down)


def _finish(x1_ref, mod_ref, fg_ref, out_ref, acc, final, i=0):
    x2 = x1_ref[i] + mod_ref[0][5:6] * acc
    if final:
        x2 = x2 * lax.rsqrt(jnp.mean(x2 * x2, axis=-1, keepdims=True) + EPS) * fg_ref[...]
    out_ref[i] = x2


def _scatter_kernel(x1_ref, ye_ref, post_ref, mod_ref, fg_ref, out_ref, *, cap, final):
    n_rows = x1_ref.shape[0]
    n_slots = N_EXPERTS * cap
    slot = lax.broadcasted_iota(jnp.int32, (1, n_slots), 1)
    expert_of_slot = jnp.zeros((1, n_slots), jnp.int32)
    for e in range(1, N_EXPERTS):
        expert_of_slot = expert_of_slot + (slot >= e * cap).astype(jnp.int32)
    rank_of_slot = (slot - expert_of_slot * cap).astype(F32)
    spread = jnp.where(lax.broadcasted_iota(jnp.int32, (LANES, n_slots), 0) == expert_of_slot, 1.0, 0.0).astype(BF16)
    for i in range(n_rows):
        pt = post_ref[i]
        ranks = jnp.dot(pt.astype(F32).astype(BF16), spread, preferred_element_type=F32)
        onehot = jnp.where(ranks == rank_of_slot, 1.0, 0.0).astype(BF16)
        acc = jnp.dot(onehot, ye_ref[i], preferred_element_type=F32)
        _finish(x1_ref, mod_ref, fg_ref, out_ref, acc, final, i)


def _scatter(x1, ye, post, mod_l, mod_row, final_g, cap, final, rows_per_step):
    bn, seq_len, _ = x1.shape
    assert cap <= 256, "ranks must be exactly representable in bf16"
    n_slots = N_EXPERTS * cap
    rows = lambda *tail: pl.BlockSpec((rows_per_step,) + tail, lambda b: (b, 0, 0))
    return pl.pallas_call(
        functools.partial(_scatter_kernel, cap=cap, final=final),
        grid=(bn // rows_per_step,),
        in_specs=[rows(seq_len, D_MODEL), rows(n_slots, D_MODEL), rows(seq_len, LANES),
                  pl.BlockSpec((1, N_MOD, D_MODEL), lambda b: (mod_row(b * rows_per_step), 0, 0)),
                  pl.BlockSpec((1, D_MODEL), lambda b: (0, 0))],
        out_specs=rows(seq_len, D_MODEL),
        out_shape=jax.ShapeDtypeStruct((bn, seq_len, D_MODEL), F32),
        compiler_params=_params(("arbitrary",)),
        name="scatter",
    )(x1, ye, post, mod_l, final_g)


def _scatter_win_kernel(tab_ref, x1_ref, ye_ref, post_ref, mod_ref, fg_ref, out_ref, acc_ref, *, cap, win, final):
    b = pl.program_id(0)
    t = pl.program_id(1)
    tbs = x1_ref.shape[1]
    pt = post_ref[0]
    starts, overflows = _window_starts(tab_ref, b, 0, N_EXPERTS, t, cap, win)

    def onehot(e, shift, n):
        lane = lax.broadcasted_iota(jnp.int32, (tbs, n), 1)
        return jnp.where((pt[:, e:e + 1] - shift) == lane, 1.0, 0.0).astype(BF16)

    acc = None
    for e0 in range(0, N_EXPERTS, EXPERTS_PER_DOT):
        group = range(e0, e0 + EXPERTS_PER_DOT)
        onehots = [onehot(e, jnp.where(overflows[e], cap, starts[e]), win) for e in group]
        rows = [ye_ref[0, pl.ds(pl.multiple_of(e * cap + starts[e], PACKED_ROWS), win), :] for e in group]
        part = jnp.dot(jnp.concatenate(onehots, axis=1), jnp.concatenate(rows, axis=0), preferred_element_type=F32)
        acc = part if acc is None else acc + part
    acc_ref[...] = acc

    for e in range(N_EXPERTS):
        @pl.when(overflows[e])
        def _(e=e):
            acc_ref[...] += jnp.dot(onehot(e, 0, cap), ye_ref[0, e * cap:(e + 1) * cap, :],
                                    preferred_element_type=F32)

    _finish(x1_ref, mod_ref, fg_ref, out_ref, acc_ref[...], final)


def _scatter_win(tab, x1, ye, post, mod_l, mod_row, final_g, cap, win, tbs, final):
    bn, seq_len, _ = x1.shape
    n_slots = N_EXPERTS * cap
    grid_spec = pltpu.PrefetchScalarGridSpec(
        num_scalar_prefetch=1,
        grid=(bn, seq_len // tbs),
        in_specs=[pl.BlockSpec((1, tbs, D_MODEL), lambda b, t, tab: (b, t, 0)),
                  pl.BlockSpec((1, n_slots, D_MODEL), lambda b, t, tab: (b, 0, 0)),
                  pl.BlockSpec((1, tbs, LANES), lambda b, t, tab: (b, t, 0)),
                  pl.BlockSpec((1, N_MOD, D_MODEL), lambda b, t, tab: (mod_row(b), 0, 0)),
                  pl.BlockSpec((1, D_MODEL), lambda b, t, tab: (0, 0))],
        out_specs=pl.BlockSpec((1, tbs, D_MODEL), lambda b, t, tab: (b, t, 0)),
        scratch_shapes=[pltpu.VMEM((tbs, D_MODEL), F32)],
    )
    return pl.pallas_call(
        functools.partial(_scatter_win_kernel, cap=cap, win=win, final=final),
        grid_spec=grid_spec,
        out_shape=jax.ShapeDtypeStruct((bn, seq_len, D_MODEL), F32),
        compiler_params=_params(("arbitrary", "arbitrary")),
        name="scatter_win",
    )(tab, x1, ye, post, mod_l, final_g)


def _grid_position_table(n_tokens, dim):
    assert n_tokens // GRID_W <= GRID_W
    quarter = dim // 4
    inv_freq = jnp.power(POS_BASE, -jnp.arange(quarter, dtype=F32) / quarter)
    angle = jnp.arange(GRID_W).astype(F32)[:, None] * inv_freq[None]
    return jnp.concatenate([jnp.sin(angle), jnp.cos(angle)], axis=-1)


def _pad_cols(w, width):
    return jnp.pad(w, ((0, 0), (0, width - w.shape[1])))


def _router_pieces(w):
    w_hi = w.astype(BF16)
    w_lo = (w - w_hi.astype(F32)).astype(BF16)
    return jnp.stack([_pad_cols(jnp.concatenate([w_hi, w_lo], axis=1), LANES), _pad_cols(w_hi, LANES)])


def _run_groups(groups, mod, lw, final_g):
    xs_res = [g["x"] for g in groups]
    states = [([], []) for _ in groups]
    for l in range(DEPTH):
        w = lw[l]
        final = l == DEPTH - 1
        mids = []
        for gi, g in enumerate(groups):
            x = xs_res[gi]
            bn, seq_len, _ = x.shape
            cap = CAPACITY_FACTOR * seq_len // N_EXPERTS
            windowed = seq_len >= WINDOWED_MIN_SEQ
            tbs = max(LANES, seq_len // TOKEN_BLOCKS) if windowed else seq_len
            win = cap // WINDOW_DIVISOR
            mod_row, tl = g["mod_row"], g["tl"]
            pos_table = g.get("pos_table") if l == 0 else None
            outs = _in_proj(x, mod[l], mod_row, w["norm_mix_g"], w["w_main"], w["w_dt"], w["pool_w"],
                            w["pool_scale"], w["conv_w"], w["conv_b"], w["alog_row"], w["bias_row"], tl, pos_table,
                            g.get("in_proj_rows", 1))
            y_pool, z, xs, bc, dt, cum = outs[:6]
            if pos_table is not None:
                x = outs[6]
            y_f, y_b, hT_f, hT_b = _ssd(xs, bc, dt, cum, g["h0f"][l], g["h0b"][l], g["h0_row"])
            states[gi][0].append(hT_f)
            states[gi][1].append(hT_b)
            x1, h2, afft = _out_proj(x, y_f, y_b, xs, z, y_pool, w["dsk"], w["ssm_norm_g"], w["w_out"], mod[l],
                                     mod_row, w["norm_ffn_g"], w["w_router"], tl, g.get("in_proj_rows", 1))
            pos, post, tab = _topk(afft, cap, g["topk_rows"], tbs)
            if windowed:
                tab = tab.reshape(-1)
                xe, gate = _gather_win(tab, h2, pos, afft, cap, win, tbs)
            else:
                xe, gate = _gather(h2, pos, afft, cap, g.get("dense_rows", 1))
            mids.append(dict(x1=x1, post=post, tab=tab, xe=xe, gate=gate, cap=cap, win=win, tbs=tbs,
                             windowed=windowed))
        yes = _ffn([(m["xe"], m["gate"], m["cap"], g["ffn_rows_per_step"] // m["cap"])
                    for m, g in zip(mids, groups)], w["w_gate"], w["w_up"], w["w_down"], l)
        for gi, (g, m, ye) in enumerate(zip(groups, mids, yes)):
            if m["windowed"]:
                xs_res[gi] = _scatter_win(m["tab"], m["x1"], ye, m["post"], mod[l], g["mod_row"], final_g, m["cap"],
                                          m["win"], m["tbs"], final)
            else:
                xs_res[gi] = _scatter(m["x1"], ye, m["post"], mod[l], g["mod_row"], final_g, m["cap"], final,
                                      g.get("dense_rows", 1))
    return [(x, sf, sb) for x, (sf, sb) in zip(xs_res, states)]


def kernel(x_prompt, x_sample, state_ssm_fwd, state_ssm_bwd, c, c_ctx, norm_mix_g, w_ada, b_ada, w_in, pool_w,
           pool_scale, conv_w, conv_b, a_log_fwd, a_log_bwd, dt_bias_fwd, dt_bias_bwd, d_skip, ssm_norm_g, w_out,
           norm_ffn_g, w_router, w_gate, w_up, w_down, final_norm_g):
    n_dec = c.shape[0]
    n_ctx = x_prompt.shape[0]
    ctx_row = n_dec
    cond = jnp.concatenate([c, c_ctx[None, :], jnp.zeros((SUBLANES - n_dec - 1, D_MODEL), F32)], axis=0)
    mod = _modulation(cond, w_ada, b_ada).reshape(DEPTH, SUBLANES, N_MOD, D_MODEL)

    zeros_h = jnp.zeros((LANES - 2 * SSD_HEADS,), F32)
    lw = []
    for l in range(DEPTH):
        lw.append(dict(
            norm_mix_g=norm_mix_g[l][None], w_main=w_in[l][:, :N_MAIN].astype(BF16),
            w_dt=_pad_cols(w_in[l][:, N_MAIN:], LANES).astype(BF16), pool_w=pool_w[l].astype(BF16),
            pool_scale=pool_scale[l][None], conv_w=conv_w[l], conv_b=conv_b[l][None],
            alog_row=jnp.concatenate([a_log_fwd[l], a_log_bwd[l], zeros_h])[None],
            bias_row=jnp.concatenate([dt_bias_fwd[l], dt_bias_bwd[l], zeros_h])[None],
            dsk=jnp.repeat(d_skip[l], SSD_INNER // SSD_HEADS)[None], ssm_norm_g=ssm_norm_g[l][None],
            w_out=w_out[l].astype(BF16), norm_ffn_g=norm_ffn_g[l][None], w_router=_router_pieces(w_router[l]),
            w_gate=w_gate, w_up=w_up, w_down=w_down))
    final_g = final_norm_g[None]
    st_shape = (HEAD_PAIRS, LANES, SSD_STATE)

    zero_state = jnp.zeros((1,) + st_shape, F32)
    prompt = dict(x=x_prompt, mod_row=lambda b: ctx_row, h0f=[zero_state] * DEPTH, h0b=[zero_state] * DEPTH,
                  h0_row=lambda b: 0, tl=x_prompt.shape[1], ffn_rows_per_step=1024, topk_rows=n_ctx, dense_rows=4,
                  in_proj_rows=4)
    sample = dict(x=x_sample, mod_row=lambda b: b, h0_row=lambda b: b, tl=1024, ffn_rows_per_step=1024,
                  topk_rows=n_dec, pos_table=_grid_position_table(x_sample.shape[1], D_MODEL),
                  h0f=[state_ssm_fwd[:, l].reshape((n_dec,) + st_shape) for l in range(DEPTH)],
                  h0b=[state_ssm_bwd[:, l].reshape((n_dec,) + st_shape) for l in range(DEPTH)])
    (y_prompt, sf, sb), (y_sample, _, _) = _run_groups([prompt, sample], mod, lw, final_g)
    out_state_shape = (n_ctx, SSD_HEADS, SSD_INNER // SSD_HEADS, SSD_STATE)
    new_f = jnp.stack([s.reshape(out_state_shape) for s in sf], axis=1)
    new_b = jnp.stack([s.reshape(out_state_shape) for s in sb], axis=1)
    return (y_prompt, y_sample, new_f, new_b)
```

```python
import functools

import jax
import jax.numpy as jnp
from jax import lax
from jax.experimental import pallas as pl
from jax.experimental.pallas import tpu as pltpu

D_MODEL = 1024
DEPTH = 2
GRID_W = 64
MIX_POOL = 512
POOL_WINDOWS = (2, 4, 8, 16)
POOL_GROUP_CH = 128
SSD_INNER = 512
SSD_HEADS = 8
SSD_HEAD_DIM = SSD_INNER // SSD_HEADS
SSD_STATE = 128
CONV_K = 5
CHUNK = 128
CONV_DIM = 1024
N_EXPERTS = 16
CAPACITY_FACTOR = 2
N_MOD = 6
POS_BASE = 10000.0
EPS = 1e-6
LOG2_E = 1.4426950408889634

SUBLANES = 8
LANES = 128
PACKED_ROWS = 16
HALO = SUBLANES
N_MAIN = MIX_POOL + SSD_INNER + CONV_DIM
HEAD_PAIRS = SSD_HEADS // 2
EXPERTS_PER_DOT = 2
SSD_ROWS_PER_STEP = 2
SSD_MAX_CHUNKS_PER_STEP = 4
MOD_COLS_PER_STEP = 1536
WINDOWED_MIN_SEQ = 1024
TOKEN_BLOCKS = 8
WINDOW_DIVISOR = 4
GATHER_EXPERT_GROUPS = 2
FLOAT_MAGNITUDE_BITS = 31
VMEM_LIMIT = 56 * 1024 * 1024

F32 = jnp.float32
BF16 = jnp.bfloat16
HI = lax.Precision.HIGHEST


def _sigmoid(x):
    return 1.0 / (1.0 + jnp.exp(-x))


def _params(sem):
    return pltpu.CompilerParams(dimension_semantics=sem, vmem_limit_bytes=VMEM_LIMIT)


def _mod_kernel(cond_ref, w_ref, b_ref, out_ref):
    c = cond_ref[...]
    s = c * _sigmoid(c)
    out_ref[0] = jnp.dot(s, w_ref[0], precision=HI, preferred_element_type=F32) + b_ref[0]


def _modulation(cond, w_ada, b_ada):
    n_rows = cond.shape[0]
    n_out = N_MOD * D_MODEL
    tn = MOD_COLS_PER_STEP
    return pl.pallas_call(
        _mod_kernel,
        grid=(DEPTH, n_out // tn),
        in_specs=[
            pl.BlockSpec((n_rows, D_MODEL), lambda l, j: (0, 0)),
            pl.BlockSpec((1, D_MODEL, tn), lambda l, j: (l, 0, j)),
            pl.BlockSpec((1, 1, tn), lambda l, j: (l, 0, j)),
        ],
        out_specs=pl.BlockSpec((1, n_rows, tn), lambda l, j: (l, 0, j)),
        out_shape=jax.ShapeDtypeStruct((DEPTH, n_rows, n_out), F32),
        compiler_params=_params(("arbitrary", "arbitrary")),
        name="modulation",
    )(cond, w_ada, b_ada.reshape(DEPTH, 1, n_out))


def _cumsum_pieces(dta):
    d_hi = dta.astype(BF16)
    rest = dta - d_hi.astype(F32)
    d_mid = rest.astype(BF16)
    return d_hi, d_mid, (rest - d_mid.astype(F32)).astype(BF16)


def _grid_position_rows(table_ref, grid_row, col0, n_cols):
    row_part = jnp.broadcast_to(table_ref[pl.ds(grid_row, 1), :], (n_cols, D_MODEL // 2))
    return jnp.concatenate([row_part, table_ref[col0:col0 + n_cols, :]], axis=1)


def _in_row(*refs, tl, seq_len, with_pos, req):
    (xp_ref, xc_ref, xn_ref, mod_ref, g_ref, wmain_ref, wdt_ref, poolw_ref, pscale_ref, cw_ref, cb_ref,
     alog_ref, bias_ref) = refs[:13]
    table_ref = refs[13] if with_pos else None
    ypool_ref, z_ref, xs_ref, bc_ref, dt_ref, cum_ref = refs[13 + with_pos:19 + with_pos]
    x0_ref = refs[19 + with_pos] if with_pos else None
    proj_scr = refs[-1].at[req]
    i = pl.program_id(1)
    n = pl.num_programs(1)
    rows = tl + 2 * HALO
    x_ext = jnp.concatenate([xp_ref[0], xc_ref[req], xn_ref[0]], axis=0)
    if with_pos:
        r0 = i * (tl // GRID_W)
        last_row = seq_len // GRID_W - 1
        pieces = [_grid_position_rows(table_ref, jnp.maximum(r0 - 1, 0), GRID_W - HALO, HALO)]
        pieces += [_grid_position_rows(table_ref, r0 + k, 0, GRID_W) for k in range(tl // GRID_W)]
        pieces += [_grid_position_rows(table_ref, jnp.minimum(r0 + tl // GRID_W, last_row), 0, HALO)]
        x_ext = x_ext + jnp.concatenate(pieces, axis=0)
        x0_ref[req] = x_ext[HALO:HALO + tl]
    mod = mod_ref[0]
    ms = jnp.mean(x_ext * x_ext, axis=-1, keepdims=True)
    h = x_ext * lax.rsqrt(ms + EPS) * (g_ref[...] * (1.0 + mod[1:2])) + mod[0:1]
    row = lax.broadcasted_iota(jnp.int32, (rows, 1), 0)
    lo_ok = jnp.where(i > 0, 0, HALO)
    hi_ok = jnp.where(i < n - 1, rows, tl + HALO)
    hb = jnp.where((row >= lo_ok) & (row < hi_ok), h, 0.0).astype(BF16)
    proj_scr[...] = jnp.dot(hb, wmain_ref[...], preferred_element_type=F32)
    dt_full = jnp.dot(hb, wdt_ref[...], preferred_element_type=F32)

    x_in = dt_full[HALO:HALO + tl] + bias_ref[...]
    dt = jnp.maximum(x_in, 0.0) + jnp.log1p(jnp.exp(-jnp.abs(x_in)))
    dt_ref[req] = dt
    r = lax.broadcasted_iota(jnp.int32, (CHUNK, CHUNK), 0)
    s = lax.broadcasted_iota(jnp.int32, (CHUNK, CHUNK), 1)
    tri = jnp.concatenate([jnp.where(r >= s, 1.0, 0.0), jnp.where(r <= s, 1.0, 0.0)], axis=1).astype(BF16)
    head = lax.broadcasted_iota(jnp.int32, (tl, LANES), 1)
    dta = jnp.where(head < 2 * SSD_HEADS, dt * (-LOG2_E * jnp.exp(alog_ref[...])), 0.0)
    pieces = _cumsum_pieces(dta)
    fwd_cols = s < SSD_HEADS
    for k in range(tl // CHUNK):
        acc = jnp.zeros((CHUNK, LANES), F32)
        for piece in pieces:
            pk = piece[k * CHUNK:(k + 1) * CHUNK]
            zero = jnp.zeros_like(pk)
            stacked = jnp.concatenate([jnp.where(fwd_cols, pk, zero), jnp.where(fwd_cols, zero, pk)], axis=0)
            acc = acc + jnp.dot(tri, stacked, preferred_element_type=F32)
        cum_ref[req, k * CHUNK:(k + 1) * CHUNK, :] = acc

    def ahead(v, k):
        return pltpu.roll(v, (-k) % rows, axis=0)

    t_glob = i * tl + lax.broadcasted_iota(jnp.int32, (tl, 1), 0)
    outs = []
    for gi, w in enumerate(POOL_WINDOWS):
        c0 = gi * POOL_GROUP_CH
        u = proj_scr[:, c0:c0 + POOL_GROUP_CH]
        acc = u
        span = 1
        while 2 * span < w:
            acc = acc + ahead(acc, span)
            span *= 2
        acc = acc + ahead(acc, -span)
        lo = jnp.maximum(t_glob - w // 2, 0)
        hi = jnp.minimum(t_glob - w // 2 + w, seq_len)
        cnt = (hi - lo).astype(F32)
        p = acc[HALO:HALO + tl] / cnt - u[HALO:HALO + tl]
        outs.append(jnp.dot(p.astype(BF16), poolw_ref[gi], preferred_element_type=F32))
    ypool_ref[req] = (jnp.concatenate(outs, axis=1) * pscale_ref[...]).astype(BF16)

    c0 = MIX_POOL + SSD_INNER
    acc = cb_ref[...] + cw_ref[0:1, :] * proj_scr[HALO - 2:HALO - 2 + tl, c0:c0 + CONV_DIM]
    for k in range(1, CONV_K):
        acc = acc + cw_ref[k:k + 1, :] * proj_scr[HALO - 2 + k:HALO - 2 + k + tl, c0:c0 + CONV_DIM]
    act = acc * _sigmoid(acc)
    xs_ref[req] = act[:, :SSD_INNER].astype(BF16)
    bc_ref[req] = act[:, SSD_INNER:].astype(BF16)
    z_ref[req] = proj_scr[HALO:HALO + tl, MIX_POOL:MIX_POOL + SSD_INNER]


def _in_kernel(*refs, tl, seq_len, with_pos):
    for req in range(refs[1].shape[0]):
        _in_row(*refs, tl=tl, seq_len=seq_len, with_pos=with_pos, req=req)


def _in_proj(x, mod_l, mod_row, norm_g, w_main, w_dt, pool_w, pool_scale, conv_w, conv_b, alog_row, bias_row, tl,
             pos_table=None, rows_per_step=1):
    bn, seq_len, _ = x.shape
    nt = seq_len // tl
    rps = rows_per_step
    assert rps == 1 or nt == 1
    hb = tl // HALO
    last_halo = seq_len // HALO - 1
    with_pos = pos_table is not None
    const = lambda *shape: pl.BlockSpec(shape, lambda b, i: (0,) * len(shape))
    tok = lambda width: pl.BlockSpec((rps, tl, width), lambda b, i: (b, i, 0))
    in_specs = [
        pl.BlockSpec((1, HALO, D_MODEL), lambda b, i: (b * rps, jnp.maximum(i * hb - 1, 0), 0)),
        tok(D_MODEL),
        pl.BlockSpec((1, HALO, D_MODEL), lambda b, i: (b * rps, jnp.minimum((i + 1) * hb, last_halo), 0)),
        pl.BlockSpec((1, N_MOD, D_MODEL), lambda b, i: (mod_row(b * rps), 0, 0)),
        const(1, D_MODEL),
        const(D_MODEL, N_MAIN),
        const(D_MODEL, LANES),
        const(len(POOL_WINDOWS), POOL_GROUP_CH, POOL_GROUP_CH),
        const(1, MIX_POOL),
        const(CONV_K, CONV_DIM),
        const(1, CONV_DIM),
        const(1, LANES),
        const(1, LANES),
    ]
    out_specs = [tok(MIX_POOL), tok(SSD_INNER), tok(SSD_INNER), tok(CONV_DIM - SSD_INNER), tok(LANES), tok(LANES)]
    out_shape = [
        jax.ShapeDtypeStruct((bn, seq_len, MIX_POOL), BF16),
        jax.ShapeDtypeStruct((bn, seq_len, SSD_INNER), F32),
        jax.ShapeDtypeStruct((bn, seq_len, SSD_INNER), BF16),
        jax.ShapeDtypeStruct((bn, seq_len, CONV_DIM - SSD_INNER), BF16),
        jax.ShapeDtypeStruct((bn, seq_len, LANES), F32),
        jax.ShapeDtypeStruct((bn, seq_len, LANES), F32),
    ]
    args = [x, x, x, mod_l, norm_g, w_main, w_dt, pool_w, pool_scale, conv_w, conv_b, alog_row, bias_row]
    if with_pos:
        assert tl % GRID_W == 0 and pos_table.shape == (GRID_W, D_MODEL // 2)
        in_specs.append(const(GRID_W, D_MODEL // 2))
        out_specs.append(tok(D_MODEL))
        out_shape.append(jax.ShapeDtypeStruct((bn, seq_len, D_MODEL), F32))
        args.append(pos_table)
    return pl.pallas_call(
        functools.partial(_in_kernel, tl=tl, seq_len=seq_len, with_pos=with_pos),
        grid=(bn // rps, nt),
        in_specs=in_specs,
        out_specs=out_specs,
        out_shape=out_shape,
        scratch_shapes=[pltpu.VMEM((rps, tl + 2 * HALO, N_MAIN), F32)],
        compiler_params=_params(("arbitrary", "arbitrary")),
        name="in_proj",
    )(*args)


def _ssd_chunk(xsf_ref, bcf_ref, dtf_ref, cumf_ref, xsb_ref, bcb_ref, dtb_ref, cumb_ref, yf_ref, yb_ref, st_all,
               rf, rb, req):
    st_ref = st_all.at[req]
    r = lax.broadcasted_iota(jnp.int32, (CHUNK, CHUNK), 0)
    s = lax.broadcasted_iota(jnp.int32, (CHUNK, CHUNK), 1)
    lane2 = lax.broadcasted_iota(jnp.int32, (CHUNK, 2 * LANES), 1)
    lane_head = lax.shift_right_logical(lane2, SSD_HEAD_DIM.bit_length() - 1)
    row2 = lax.broadcasted_iota(jnp.int32, (2 * LANES, SSD_STATE), 0)
    is_fwd = s < SSD_HEADS
    dt = jnp.where(is_fwd, dtf_ref[req, rf:rf + CHUNK, :], dtb_ref[req, rb:rb + CHUNK, :])
    cum = jnp.where(is_fwd, cumf_ref[req, rf:rf + CHUNK, :], cumb_ref[req, rb:rb + CHUNK, :])
    tot = jnp.where(lax.broadcasted_iota(jnp.int32, (1, LANES), 1) < SSD_HEADS,
                    cum[CHUNK - 1:CHUNK, :], cum[0:1, :])
    cdec = jnp.exp2(tot)
    cum_t = cum.T
    dt_t = dt.T
    ecum_t = jnp.exp2(cum_t)
    w_t = (dt * jnp.exp2(tot - cum)).T

    def head_rows(arr_t, h0):
        return jnp.concatenate([jnp.broadcast_to(arr_t[h0 + k:h0 + k + 1, :], (SSD_HEAD_DIM, CHUNK))
                                for k in range(4)], axis=0)

    def one_dir(d, xs_ref, bc_ref, y_ref, r0):
        mask = (r >= s) if d == 0 else (r <= s)
        off = d * SSD_HEADS
        bcv = bc_ref[req, r0:r0 + CHUNK, :]
        b01 = bcv[:, :2 * SSD_STATE].astype(BF16)
        c01 = bcv[:, 2 * SSD_STATE:].astype(BF16)
        zero2 = jnp.zeros_like(b01)
        b_blk = jnp.concatenate([jnp.where(lane2 < SSD_STATE, b01, zero2),
                                 jnp.where(lane2 < SSD_STATE, zero2, b01)], axis=0)
        cb01 = lax.dot_general(c01, b_blk, (((1,), (1,)), ((), ())), preferred_element_type=F32)
        for g in range(2):
            h0 = off + 4 * g
            b_g = b01[:, g * SSD_STATE:(g + 1) * SSD_STATE]
            c_g = c01[:, g * SSD_STATE:(g + 1) * SSD_STATE]
            cb = cb01[:, g * CHUNK:(g + 1) * CHUNK]
            xs_g = xs_ref[req, r0:r0 + CHUNK, g * 2 * LANES:(g + 1) * 2 * LANES]
            x_b = xs_g.astype(BF16)
            ms, x_parts = [], []
            for k in range(4):
                hh = h0 + k
                seg = cum[:, hh:hh + 1] - cum_t[hh:hh + 1, :]
                dec = jnp.exp2(jnp.where(mask, seg, -jnp.inf))
                ms.append((cb * dec * dt_t[hh:hh + 1, :]).astype(BF16))
                x_parts.append(jnp.where(lane_head == k, x_b, jnp.zeros_like(x_b)))
            y_diag = jnp.dot(jnp.concatenate(ms, axis=1), jnp.concatenate(x_parts, axis=0),
                             preferred_element_type=F32)
            st = st_ref[d, 2 * g:2 * g + 2].reshape(2 * LANES, SSD_STATE)
            y_off_t = lax.dot_general(st.astype(BF16), c_g, (((1,), (1,)), ((), ())),
                                      preferred_element_type=F32) * head_rows(ecum_t, h0)
            y_ref[req, r0:r0 + CHUNK, g * 2 * LANES:(g + 1) * 2 * LANES] = (y_diag + y_off_t.T).astype(BF16)
            x_d = (xs_g.astype(F32).T * head_rows(w_t, h0)).astype(BF16)
            cs = jnp.dot(x_d, b_g, preferred_element_type=F32)
            dcol = cdec[:, h0 + 3:h0 + 4]
            for k in (2, 1, 0):
                dcol = jnp.where(row2 < (k + 1) * SSD_HEAD_DIM, cdec[:, h0 + k:h0 + k + 1], dcol)
            st_ref[d, 2 * g:2 * g + 2] = (st * dcol + cs).reshape(2, LANES, SSD_STATE)

    one_dir(0, xsf_ref, bcf_ref, yf_ref, rf)
    one_dir(1, xsb_ref, bcb_ref, yb_ref, rb)


def _ssd_kernel(xsf_ref, bcf_ref, dtf_ref, cumf_ref, xsb_ref, bcb_ref, dtb_ref, cumb_ref, h0f_ref, h0b_ref,
                yf_ref, yb_ref, hTf_ref, hTb_ref, st_ref):
    c = pl.program_id(1)
    nc = pl.num_programs(1)

    n_req = xsf_ref.shape[0]
    shared_h0 = h0f_ref.shape[0] == 1

    @pl.when(c == 0)
    def _():
        for req in range(n_req):
            st_ref[req, 0] = h0f_ref[0 if shared_h0 else req]
            st_ref[req, 1] = h0b_ref[0 if shared_h0 else req]

    n_chunks = xsf_ref.shape[1] // CHUNK
    for k in range(n_chunks):
        for req in range(n_req):
            _ssd_chunk(xsf_ref, bcf_ref, dtf_ref, cumf_ref, xsb_ref, bcb_ref, dtb_ref, cumb_ref, yf_ref, yb_ref,
                       st_ref, k * CHUNK, (n_chunks - 1 - k) * CHUNK, req)

    @pl.when(c == nc - 1)
    def _():
        for req in range(n_req):
            hTf_ref[req] = st_ref[req, 0]
            hTb_ref[req] = st_ref[req, 1]


def _ssd(xs, bc, dt, cum, h0f, h0b, rows_per_step=SSD_ROWS_PER_STEP):
    bn, seq_len, _ = xs.shape
    rps = rows_per_step
    step_rows = min(SSD_MAX_CHUNKS_PER_STEP * CHUNK, seq_len)
    nc = seq_len // step_rows
    shared_h0 = h0f.shape[0] == 1
    fwd = lambda width: pl.BlockSpec((rps, step_rows, width), lambda b, c: (b, c, 0))
    bwd = lambda width: pl.BlockSpec((rps, step_rows, width), lambda b, c: (b, nc - 1 - c, 0))
    st_spec_in = pl.BlockSpec((1 if shared_h0 else rps, HEAD_PAIRS, LANES, SSD_STATE),
                              lambda b, c: (0 if shared_h0 else b, 0, 0, 0))
    st_spec_out = pl.BlockSpec((rps, HEAD_PAIRS, LANES, SSD_STATE), lambda b, c: (b, 0, 0, 0))
    st_shape = jax.ShapeDtypeStruct((bn, HEAD_PAIRS, LANES, SSD_STATE), F32)
    return pl.pallas_call(
        _ssd_kernel,
        grid=(bn // rps, nc),
        in_specs=[fwd(SSD_INNER), fwd(CONV_DIM - SSD_INNER), fwd(LANES), fwd(LANES),
                  bwd(SSD_INNER), bwd(CONV_DIM - SSD_INNER), bwd(LANES), bwd(LANES),
                  st_spec_in, st_spec_in],
        out_specs=[fwd(SSD_INNER), bwd(SSD_INNER), st_spec_out, st_spec_out],
        out_shape=[jax.ShapeDtypeStruct((bn, seq_len, SSD_INNER), BF16),
                   jax.ShapeDtypeStruct((bn, seq_len, SSD_INNER), BF16), st_shape, st_shape],
        scratch_shapes=[pltpu.VMEM((rps, 2, HEAD_PAIRS, LANES, SSD_STATE), F32)],
        compiler_params=_params(("arbitrary", "arbitrary")),
        name="ssd",
    )(xs, bc, dt, cum, xs, bc, dt, cum, h0f, h0b)


def _out_row(x_ref, yf_ref, yb_ref, xs_ref, z_ref, yp_ref, dsk_ref, sg_ref, wo_ref, mod_ref, ng_ref, wr_ref,
             x1_ref, h2_ref, afft_ref, req):
    y = yf_ref[req].astype(F32) + yb_ref[req].astype(F32) + dsk_ref[...] * xs_ref[req].astype(F32)
    z = z_ref[req]
    y = y * (z * _sigmoid(z))
    y = y * lax.rsqrt(jnp.mean(y * y, axis=-1, keepdims=True) + EPS) * sg_ref[...]
    o = (jnp.dot(yp_ref[req], wo_ref[:MIX_POOL, :], preferred_element_type=F32)
         + jnp.dot(y.astype(BF16), wo_ref[MIX_POOL:, :], preferred_element_type=F32))
    mod = mod_ref[0]
    x1 = x_ref[req] + mod[2:3] * o
    x1_ref[req] = x1
    h2 = x1 * lax.rsqrt(jnp.mean(x1 * x1, axis=-1, keepdims=True) + EPS) * (ng_ref[...] * (1.0 + mod[4:5])) + mod[3:4]
    h2_hi = h2.astype(BF16)
    h2_ref[req] = h2_hi
    h2_lo = (h2 - h2_hi.astype(F32)).astype(BF16)
    pa = jnp.dot(h2_hi, wr_ref[0], preferred_element_type=F32)
    pb = jnp.dot(h2_lo, wr_ref[1], preferred_element_type=F32)
    logits = pa + pltpu.roll(pa, LANES - N_EXPERTS, axis=1) + pb
    lane = lax.broadcasted_iota(jnp.int32, logits.shape, 1)
    lg = jnp.where(lane < N_EXPERTS, logits, -jnp.inf)
    e = jnp.exp(lg - jnp.max(lg, axis=-1, keepdims=True))
    aff = e / jnp.sum(e, axis=-1, keepdims=True)
    afft_ref[req] = aff.T[:N_EXPERTS, :]


def _out_kernel(*refs):
    for req in range(refs[0].shape[0]):
        _out_row(*refs, req=req)


def _out_proj(x, y_f, y_b, xs, z, y_pool, dsk, ssm_g, w_out, mod_l, mod_row, norm_ffn_g, w_router, tl,
              rows_per_step=1):
    bn, seq_len, _ = x.shape
    nt = seq_len // tl
    rps = rows_per_step
    const = lambda *shape: pl.BlockSpec(shape, lambda b, i: (0,) * len(shape))
    tok = lambda width: pl.BlockSpec((rps, tl, width), lambda b, i: (b, i, 0))
    return pl.pallas_call(
        _out_kernel,
        grid=(bn // rps, nt),
        in_specs=[tok(D_MODEL), tok(SSD_INNER), tok(SSD_INNER), tok(SSD_INNER), tok(SSD_INNER), tok(MIX_POOL),
                  const(1, SSD_INNER), const(1, SSD_INNER), const(D_MODEL, D_MODEL),
                  pl.BlockSpec((1, N_MOD, D_MODEL), lambda b, i: (mod_row(b * rps), 0, 0)),
                  const(1, D_MODEL), const(2, D_MODEL, LANES)],
        out_specs=[tok(D_MODEL), tok(D_MODEL), pl.BlockSpec((rps, N_EXPERTS, tl), lambda b, i: (b, 0, i))],
        out_shape=[jax.ShapeDtypeStruct((bn, seq_len, D_MODEL), F32),
                   jax.ShapeDtypeStruct((bn, seq_len, D_MODEL), BF16),
                   jax.ShapeDtypeStruct((bn, N_EXPERTS, seq_len), F32)],
        compiler_params=_params(("arbitrary", "arbitrary")),
        name="out_proj",
    )(x, y_f, y_b, xs, z, y_pool, dsk, ssm_g, w_out, mod_l, norm_ffn_g, w_router)


def _topk_kernel(afft_ref, pos_ref, post_ref, tab_ref, *, seq_len, cap, rb, tbs):
    n_rows = rb * N_EXPERTS
    a = afft_ref[...].reshape(n_rows, seq_len)

    def body(it, thr):
        cand_bits = thr | jnp.left_shift(jnp.int32(1), FLOAT_MAGNITUDE_BITS - 1 - it)
        cnt = jnp.sum((a >= pltpu.bitcast(cand_bits, F32)).astype(jnp.int32), axis=1, keepdims=True)
        return jnp.where(cnt >= cap, cand_bits, thr)

    thr = lax.fori_loop(0, FLOAT_MAGNITUDE_BITS, body, jnp.zeros((n_rows, 1), jnp.int32))
    lo = pltpu.bitcast(thr, F32)
    hi = pltpu.bitcast(thr + 1, F32)
    gt = a >= hi
    eq = (a >= lo) & (a < hi)
    need = cap - jnp.sum(gt.astype(jnp.int32), axis=1, keepdims=True)

    r = lax.broadcasted_iota(jnp.int32, (LANES, LANES), 0)
    s = lax.broadcasted_iota(jnp.int32, (LANES, LANES), 1)
    strict_upper = (r < s).astype(BF16)
    nblk = seq_len // LANES

    def excl_cumsum(m):
        carry = jnp.zeros((n_rows, 1), F32)
        outs = []
        for blk in range(nblk):
            mb = m[:, blk * LANES:(blk + 1) * LANES]
            outs.append(jnp.dot(mb.astype(BF16), strict_upper, preferred_element_type=F32) + carry)
            carry = carry + jnp.sum(mb, axis=1, keepdims=True)
        return jnp.concatenate(outs, axis=1)

    eq_rank = excl_cumsum(jnp.where(eq, 1.0, 0.0))
    sel = gt | (eq & (eq_rank < need.astype(F32)))
    rank = excl_cumsum(jnp.where(sel, 1.0, 0.0))
    pos = jnp.where(sel, rank, -1.0)
    pos_ref[...] = pos.astype(jnp.int32).reshape(rb, N_EXPERTS, seq_len)

    lane = lax.broadcasted_iota(jnp.int32, (n_rows, LANES), 1)
    tab = jnp.full((n_rows, LANES), float(cap), F32)
    for k in range(seq_len // tbs):
        tab = jnp.where(lane == k, rank[:, k * tbs:k * tbs + 1], tab)
    tab_ref[...] = tab.astype(jnp.int32).reshape(rb, N_EXPERTS, LANES)

    pad = jnp.full((LANES - N_EXPERTS, LANES), -1.0, F32)
    for i in range(rb):
        for blk in range(nblk):
            tile = jnp.concatenate([pos[i * N_EXPERTS:(i + 1) * N_EXPERTS, blk * LANES:(blk + 1) * LANES], pad], axis=0)
            post_ref[i, blk * LANES:(blk + 1) * LANES, :] = tile.T.astype(jnp.int32)


def _topk(afft, cap, rb, tbs):
    bn, _, seq_len = afft.shape
    return pl.pallas_call(
        functools.partial(_topk_kernel, seq_len=seq_len, cap=cap, rb=rb, tbs=tbs),
        grid=(bn // rb,),
        in_specs=[pl.BlockSpec((rb, N_EXPERTS, seq_len), lambda b: (b, 0, 0))],
        out_specs=[pl.BlockSpec((rb, N_EXPERTS, seq_len), lambda b: (b, 0, 0)),
                   pl.BlockSpec((rb, seq_len, LANES), lambda b: (b, 0, 0)),
                   pl.BlockSpec((rb, N_EXPERTS, LANES), lambda b: (b, 0, 0))],
        out_shape=[jax.ShapeDtypeStruct((bn, N_EXPERTS, seq_len), jnp.int32),
                   jax.ShapeDtypeStruct((bn, seq_len, LANES), jnp.int32),
                   jax.ShapeDtypeStruct((bn, N_EXPERTS, LANES), jnp.int32)],
        compiler_params=_params(("arbitrary",)),
        name="topk",
    )(afft)


def _gather_kernel(h2_ref, pos_ref, afft_ref, xe_ref, gate_ref, *, cap):
    n_rows, seq_len, _ = h2_ref.shape
    n_slots = N_EXPERTS * cap
    rank = lax.broadcasted_iota(jnp.int32, (cap, seq_len), 0)
    for i in range(n_rows):
        onehots, gates = [], []
        for e in range(N_EXPERTS):
            hit = pos_ref[i, e:e + 1, :] == rank
            onehots.append(jnp.where(hit, 1.0, 0.0).astype(BF16))
            gates.append(jnp.sum(jnp.where(hit, afft_ref[i, e:e + 1, :], 0.0), axis=1, keepdims=True))
        xe_ref[i] = jnp.dot(jnp.concatenate(onehots, axis=0), h2_ref[i], preferred_element_type=F32).astype(BF16)
        gate_ref[i] = jnp.broadcast_to(jnp.concatenate(gates, axis=0), (n_slots, LANES))


def _gather(h2, pos, afft, cap, rows_per_step):
    bn, seq_len, _ = h2.shape
    n_slots = N_EXPERTS * cap
    rows = lambda *tail: pl.BlockSpec((rows_per_step,) + tail, lambda b: (b, 0, 0))
    return pl.pallas_call(
        functools.partial(_gather_kernel, cap=cap),
        grid=(bn // rows_per_step,),
        in_specs=[rows(seq_len, D_MODEL), rows(N_EXPERTS, seq_len), rows(N_EXPERTS, seq_len)],
        out_specs=[rows(n_slots, D_MODEL), rows(n_slots, LANES)],
        out_shape=[jax.ShapeDtypeStruct((bn, n_slots, D_MODEL), BF16),
                   jax.ShapeDtypeStruct((bn, n_slots, LANES), F32)],
        compiler_params=_params(("arbitrary",)),
        name="gather",
    )(h2, pos, afft)


def _window_starts(tab_ref, b, e0, n_e, t, cap, win):
    starts, overflows = [], []
    for ee in range(n_e):
        base = (b * N_EXPERTS + e0 + ee) * LANES + t
        p0 = tab_ref[base]
        p1 = tab_ref[base + 1]
        align_bits = PACKED_ROWS.bit_length() - 1
        w0 = jnp.minimum(lax.shift_left(lax.shift_right_logical(p0, align_bits), align_bits), cap - win)
        starts.append(w0)
        overflows.append(p1 - w0 > win)
    return starts, overflows


def _gather_win_kernel(tab_ref, h2_ref, pos_ref, afft_ref, xe_ref, gate_ref, *, cap, win, n_e):
    b = pl.program_id(0)
    e0 = pl.program_id(1) * n_e
    t = pl.program_id(2)
    tbs = h2_ref.shape[1]

    @pl.when(t == 0)
    def _():
        xe_ref[...] = jnp.zeros(xe_ref.shape, BF16)
        gate_ref[...] = jnp.zeros(gate_ref.shape, F32)

    starts, overflows = _window_starts(tab_ref, b, e0, n_e, t, cap, win)

    def accumulate(row0, n, x, g):
        cur = xe_ref[0, pl.ds(row0, n), :]
        xe_ref[0, pl.ds(row0, n), :] = jnp.where(g > 0.0, x.astype(BF16), cur)
        gate_ref[0, pl.ds(row0, n), :] = gate_ref[0, pl.ds(row0, n), :] + jnp.broadcast_to(g, (n, LANES))

    def onehot_and_gate(ee, shift, n):
        sub = lax.broadcasted_iota(jnp.int32, (n, tbs), 0)
        hit = (pos_ref[0, pl.ds(e0 + ee, 1), :] - shift) == sub
        gate = jnp.sum(jnp.where(hit, afft_ref[0, pl.ds(e0 + ee, 1), :], 0.0), axis=1, keepdims=True)
        return jnp.where(hit, 1.0, 0.0).astype(BF16), gate

    pieces = [onehot_and_gate(ee, jnp.where(overflows[ee], cap, starts[ee]), win) for ee in range(n_e)]
    x = jnp.dot(jnp.concatenate([p[0] for p in pieces], axis=0), h2_ref[0], preferred_element_type=F32)
    for ee in range(n_e):
        accumulate(pl.multiple_of(ee * cap + starts[ee], PACKED_ROWS), win, x[ee * win:(ee + 1) * win], pieces[ee][1])

    for ee in range(n_e):
        @pl.when(overflows[ee])
        def _(ee=ee):
            onehot, gate = onehot_and_gate(ee, 0, cap)
            accumulate(ee * cap, cap, jnp.dot(onehot, h2_ref[0], preferred_element_type=F32), gate)


def _gather_win(tab, h2, pos, afft, cap, win, tbs):
    bn, seq_len, _ = h2.shape
    n_slots = N_EXPERTS * cap
    n_e = N_EXPERTS // GATHER_EXPERT_GROUPS
    grid_spec = pltpu.PrefetchScalarGridSpec(
        num_scalar_prefetch=1,
        grid=(bn, N_EXPERTS // n_e, seq_len // tbs),
        in_specs=[pl.BlockSpec((1, tbs, D_MODEL), lambda b, g, t, tab: (b, t, 0)),
                  pl.BlockSpec((1, N_EXPERTS, tbs), lambda b, g, t, tab: (b, 0, t)),
                  pl.BlockSpec((1, N_EXPERTS, tbs), lambda b, g, t, tab: (b, 0, t))],
        out_specs=[pl.BlockSpec((1, n_e * cap, D_MODEL), lambda b, g, t, tab: (b, g, 0)),
                   pl.BlockSpec((1, n_e * cap, LANES), lambda b, g, t, tab: (b, g, 0))],
    )
    return pl.pallas_call(
        functools.partial(_gather_win_kernel, cap=cap, win=win, n_e=n_e),
        grid_spec=grid_spec,
        out_shape=[jax.ShapeDtypeStruct((bn, n_slots, D_MODEL), BF16),
                   jax.ShapeDtypeStruct((bn, n_slots, LANES), F32)],
        compiler_params=_params(("arbitrary", "arbitrary", "arbitrary")),
        name="gather_win",
    )(tab, h2, pos, afft)


def _ffn_kernel(*refs, tiles):
    n = len(tiles)
    xe_refs, gate_refs = refs[0:2 * n:2], refs[1:2 * n:2]
    wg_ref, wu_ref, wd_ref = refs[2 * n:2 * n + 3]
    ye_refs = refs[2 * n + 3:3 * n + 3]
    wg_s, wu_s, wd_s = refs[3 * n + 3:]
    j = pl.program_id(1)

    @pl.when(j == 0)
    def _():
        wg_s[...] = wg_ref[0, 0].astype(BF16)
        wu_s[...] = wu_ref[0, 0].astype(BF16)
        wd_s[...] = wd_ref[0, 0].astype(BF16)

    for (first, steps, rows), xe_ref, gate_ref, ye_ref in zip(tiles, xe_refs, gate_refs, ye_refs):
        @pl.when(jnp.logical_and(j >= first, j < first + steps))
        def _(rows=rows, xe_ref=xe_ref, gate_ref=gate_ref, ye_ref=ye_ref):
            x = xe_ref[...].reshape(rows, D_MODEL)
            g = jnp.dot(x, wg_s[...], preferred_element_type=F32)
            u = jnp.dot(x, wu_s[...], preferred_element_type=F32)
            hid = (g * _sigmoid(g) * u).astype(BF16)
            y = jnp.dot(hid, wd_s[...], preferred_element_type=F32) * gate_ref[...].reshape(rows, LANES)[:, :1]
            ye_ref[...] = y.astype(BF16).reshape(ye_ref.shape)


def _ffn(groups, w_gate, w_up, w_down, layer):
    tiles, in_specs, out_specs, out_shape, args = [], [], [], [], []
    first = 0
    for xe, gate, cap, bb in groups:
        steps = xe.shape[0] // bb
        tile = lambda e, j, first=first, steps=steps: (jnp.clip(j - first, 0, steps - 1), e, 0)
        in_specs += [pl.BlockSpec((bb, cap, D_MODEL), tile), pl.BlockSpec((bb, cap, LANES), tile)]
        out_specs.append(pl.BlockSpec((bb, cap, D_MODEL), tile))
        out_shape.append(jax.ShapeDtypeStruct(xe.shape, BF16))
        args += [xe, gate]
        tiles.append((first, steps, bb * cap))
        first += steps
    wspec = pl.BlockSpec((1, 1, D_MODEL, D_MODEL), lambda e, j: (layer, e, 0, 0))
    return pl.pallas_call(
        functools.partial(_ffn_kernel, tiles=tuple(tiles)),
        grid=(N_EXPERTS, first),
        in_specs=in_specs + [wspec, wspec, wspec],
        out_specs=out_specs,
        out_shape=out_shape,
        scratch_shapes=[pltpu.VMEM((D_MODEL, D_MODEL), BF16)] * 3,
        compiler_params=_params(("arbitrary", "arbitrary")),
        name="ffn",
    )(*args, w_gate, w_up, w_down)


def _finish(x1_ref, mod_ref, fg_ref, out_ref, acc, final, i=0):
    x2 = x1_ref[i] + mod_ref[0][5:6] * acc
    if final:
        x2 = x2 * lax.rsqrt(jnp.mean(x2 * x2, axis=-1, keepdims=True) + EPS) * fg_ref[...]
    out_ref[i] = x2


def _scatter_kernel(x1_ref, ye_ref, post_ref, mod_ref, fg_ref, out_ref, *, cap, final):
    n_rows = x1_ref.shape[0]
    n_slots = N_EXPERTS * cap
    slot = lax.broadcasted_iota(jnp.int32, (1, n_slots), 1)
    expert_of_slot = jnp.zeros((1, n_slots), jnp.int32)
    for e in range(1, N_EXPERTS):
        expert_of_slot = expert_of_slot + (slot >= e * cap).astype(jnp.int32)
    rank_of_slot = (slot - expert_of_slot * cap).astype(F32)
    spread = jnp.where(lax.broadcasted_iota(jnp.int32, (LANES, n_slots), 0) == expert_of_slot, 1.0, 0.0).astype(BF16)
    for i in range(n_rows):
        pt = post_ref[i]
        ranks = jnp.dot(pt.astype(F32).astype(BF16), spread, preferred_element_type=F32)
        onehot = jnp.where(ranks == rank_of_slot, 1.0, 0.0).astype(BF16)
        acc = jnp.dot(onehot, ye_ref[i], preferred_element_type=F32)
        _finish(x1_ref, mod_ref, fg_ref, out_ref, acc, final, i)


def _scatter(x1, ye, post, mod_l, mod_row, final_g, cap, final, rows_per_step):
    bn, seq_len, _ = x1.shape
    assert cap <= 256, "ranks must be exactly representable in bf16"
    n_slots = N_EXPERTS * cap
    rows = lambda *tail: pl.BlockSpec((rows_per_step,) + tail, lambda b: (b, 0, 0))
    return pl.pallas_call(
        functools.partial(_scatter_kernel, cap=cap, final=final),
        grid=(bn // rows_per_step,),
        in_specs=[rows(seq_len, D_MODEL), rows(n_slots, D_MODEL), rows(seq_len, LANES),
                  pl.BlockSpec((1, N_MOD, D_MODEL), lambda b: (mod_row(b * rows_per_step), 0, 0)),
                  pl.BlockSpec((1, D_MODEL), lambda b: (0, 0))],
        out_specs=rows(seq_len, D_MODEL),
        out_shape=jax.ShapeDtypeStruct((bn, seq_len, D_MODEL), F32),
        compiler_params=_params(("arbitrary",)),
        name="scatter",
    )(x1, ye, post, mod_l, final_g)


def _scatter_win_kernel(tab_ref, x1_ref, ye_ref, post_ref, mod_ref, fg_ref, out_ref, acc_ref, *, cap, win, final):
    b = pl.program_id(0)
    t = pl.program_id(1)
    tbs = x1_ref.shape[1]
    pt = post_ref[0]
    starts, overflows = _window_starts(tab_ref, b, 0, N_EXPERTS, t, cap, win)

    def onehot(e, shift, n):
        lane = lax.broadcasted_iota(jnp.int32, (tbs, n), 1)
        return jnp.where((pt[:, e:e + 1] - shift) == lane, 1.0, 0.0).astype(BF16)

    acc = None
    for e0 in range(0, N_EXPERTS, EXPERTS_PER_DOT):
        group = range(e0, e0 + EXPERTS_PER_DOT)
        onehots = [onehot(e, jnp.where(overflows[e], cap, starts[e]), win) for e in group]
        rows = [ye_ref[0, pl.ds(pl.multiple_of(e * cap + starts[e], PACKED_ROWS), win), :] for e in group]
        part = jnp.dot(jnp.concatenate(onehots, axis=1), jnp.concatenate(rows, axis=0), preferred_element_type=F32)
        acc = part if acc is None else acc + part
    acc_ref[...] = acc

    for e in range(N_EXPERTS):
        @pl.when(overflows[e])
        def _(e=e):
            acc_ref[...] += jnp.dot(onehot(e, 0, cap), ye_ref[0, e * cap:(e + 1) * cap, :],
                                    preferred_element_type=F32)

    _finish(x1_ref, mod_ref, fg_ref, out_ref, acc_ref[...], final)


def _scatter_win(tab, x1, ye, post, mod_l, mod_row, final_g, cap, win, tbs, final):
    bn, seq_len, _ = x1.shape
    n_slots = N_EXPERTS * cap
    grid_spec = pltpu.PrefetchScalarGridSpec(
        num_scalar_prefetch=1,
        grid=(bn, seq_len // tbs),
        in_specs=[pl.BlockSpec((1, tbs, D_MODEL), lambda b, t, tab: (b, t, 0)),
                  pl.BlockSpec((1, n_slots, D_MODEL), lambda b, t, tab: (b, 0, 0)),
                  pl.BlockSpec((1, tbs, LANES), lambda b, t, tab: (b, t, 0)),
                  pl.BlockSpec((1, N_MOD, D_MODEL), lambda b, t, tab: (mod_row(b), 0, 0)),
                  pl.BlockSpec((1, D_MODEL), lambda b, t, tab: (0, 0))],
        out_specs=pl.BlockSpec((1, tbs, D_MODEL), lambda b, t, tab: (b, t, 0)),
        scratch_shapes=[pltpu.VMEM((tbs, D_MODEL), F32)],
    )
    return pl.pallas_call(
        functools.partial(_scatter_win_kernel, cap=cap, win=win, final=final),
        grid_spec=grid_spec,
        out_shape=jax.ShapeDtypeStruct((bn, seq_len, D_MODEL), F32),
        compiler_params=_params(("arbitrary", "arbitrary")),
        name="scatter_win",
    )(tab, x1, ye, post, mod_l, final_g)


def _grid_position_table(n_tokens, dim):
    assert n_tokens // GRID_W <= GRID_W
    quarter = dim // 4
    inv_freq = jnp.power(POS_BASE, -jnp.arange(quarter, dtype=F32) / quarter)
    angle = jnp.arange(GRID_W).astype(F32)[:, None] * inv_freq[None]
    return jnp.concatenate([jnp.sin(angle), jnp.cos(angle)], axis=-1)


def _pad_cols(w, width):
    return jnp.pad(w, ((0, 0), (0, width - w.shape[1])))


def _router_pieces(w):
    w_hi = w.astype(BF16)
    w_lo = (w - w_hi.astype(F32)).astype(BF16)
    return jnp.stack([_pad_cols(jnp.concatenate([w_hi, w_lo], axis=1), LANES), _pad_cols(w_hi, LANES)])


def _run_groups(groups, mod, lw, final_g):
    xs_res = [g["x"] for g in groups]
    states = [([], []) for _ in groups]
    for l in range(DEPTH):
        w = lw[l]
        final = l == DEPTH - 1
        mids = []
        for gi, g in enumerate(groups):
            x = xs_res[gi]
            bn, seq_len, _ = x.shape
            cap = CAPACITY_FACTOR * seq_len // N_EXPERTS
            windowed = seq_len >= WINDOWED_MIN_SEQ
            tbs = max(LANES, seq_len // TOKEN_BLOCKS) if windowed else seq_len
            win = cap // WINDOW_DIVISOR
            mod_row, tl = g["mod_row"], g["tl"]
            pos_table = g.get("pos_table") if l == 0 else None
            outs = _in_proj(x, mod[l], mod_row, w["norm_mix_g"], w["w_main"], w["w_dt"], w["pool_w"],
                            w["pool_scale"], w["conv_w"], w["conv_b"], w["alog_row"], w["bias_row"], tl, pos_table,
                            g.get("in_proj_rows", 1))
            y_pool, z, xs, bc, dt, cum = outs[:6]
            if pos_table is not None:
                x = outs[6]
            y_f, y_b, hT_f, hT_b = _ssd(xs, bc, dt, cum, g["h0f"][l], g["h0b"][l])
            states[gi][0].append(hT_f)
            states[gi][1].append(hT_b)
            x1, h2, afft = _out_proj(x, y_f, y_b, xs, z, y_pool, w["dsk"], w["ssm_norm_g"], w["w_out"], mod[l],
                                     mod_row, w["norm_ffn_g"], w["w_router"], tl, g.get("in_proj_rows", 1))
            pos, post, tab = _topk(afft, cap, g["topk_rows"], tbs)
            if windowed:
                tab = tab.reshape(-1)
                xe, gate = _gather_win(tab, h2, pos, afft, cap, win, tbs)
            else:
                xe, gate = _gather(h2, pos, afft, cap, g.get("dense_rows", 1))
            mids.append(dict(x1=x1, post=post, tab=tab, xe=xe, gate=gate, cap=cap, win=win, tbs=tbs,
                             windowed=windowed))
        yes = _ffn([(m["xe"], m["gate"], m["cap"], g["ffn_rows_per_step"] // m["cap"])
                    for m, g in zip(mids, groups)], w["w_gate"], w["w_up"], w["w_down"], l)
        for gi, (g, m, ye) in enumerate(zip(groups, mids, yes)):
            if m["windowed"]:
                xs_res[gi] = _scatter_win(m["tab"], m["x1"], ye, m["post"], mod[l], g["mod_row"], final_g, m["cap"],
                                          m["win"], m["tbs"], final)
            else:
                xs_res[gi] = _scatter(m["x1"], ye, m["post"], mod[l], g["mod_row"], final_g, m["cap"], final,
                                      g.get("dense_rows", 1))
    return [(x, sf, sb) for x, (sf, sb) in zip(xs_res, states)]


def kernel(x_prompt, x_sample, state_ssm_fwd, state_ssm_bwd, c, c_ctx, norm_mix_g, w_ada, b_ada, w_in, pool_w,
           pool_scale, conv_w, conv_b, a_log_fwd, a_log_bwd, dt_bias_fwd, dt_bias_bwd, d_skip, ssm_norm_g, w_out,
           norm_ffn_g, w_router, w_gate, w_up, w_down, final_norm_g):
    n_dec = c.shape[0]
    n_ctx = x_prompt.shape[0]
    ctx_row = n_dec
    cond = jnp.concatenate([c, c_ctx[None, :], jnp.zeros((SUBLANES - n_dec - 1, D_MODEL), F32)], axis=0)
    mod = _modulation(cond, w_ada, b_ada).reshape(DEPTH, SUBLANES, N_MOD, D_MODEL)

    zeros_h = jnp.zeros((LANES - 2 * SSD_HEADS,), F32)
    lw = []
    for l in range(DEPTH):
        lw.append(dict(
            norm_mix_g=norm_mix_g[l][None], w_main=w_in[l][:, :N_MAIN].astype(BF16),
            w_dt=_pad_cols(w_in[l][:, N_MAIN:], LANES).astype(BF16), pool_w=pool_w[l].astype(BF16),
            pool_scale=pool_scale[l][None], conv_w=conv_w[l], conv_b=conv_b[l][None],
            alog_row=jnp.concatenate([a_log_fwd[l], a_log_bwd[l], zeros_h])[None],
            bias_row=jnp.concatenate([dt_bias_fwd[l], dt_bias_bwd[l], zeros_h])[None],
            dsk=jnp.repeat(d_skip[l], SSD_INNER // SSD_HEADS)[None], ssm_norm_g=ssm_norm_g[l][None],
            w_out=w_out[l].astype(BF16), norm_ffn_g=norm_ffn_g[l][None], w_router=_router_pieces(w_router[l]),
            w_gate=w_gate, w_up=w_up, w_down=w_down))
    final_g = final_norm_g[None]
    st_shape = (HEAD_PAIRS, LANES, SSD_STATE)

    zero_state = jnp.zeros((1,) + st_shape, F32)
    prompt = dict(x=x_prompt, mod_row=lambda b: ctx_row, h0f=[zero_state] * DEPTH, h0b=[zero_state] * DEPTH,
                  h0_row=lambda b: 0, tl=x_prompt.shape[1], ffn_rows_per_step=1024, topk_rows=n_ctx, dense_rows=4,
                  in_proj_rows=4)
    sample = dict(x=x_sample, mod_row=lambda b: b, h0_row=lambda b: b, tl=1024, ffn_rows_per_step=1024,
                  topk_rows=n_dec, pos_table=_grid_position_table(x_sample.shape[1], D_MODEL),
                  h0f=[state_ssm_fwd[:, l].reshape((n_dec,) + st_shape) for l in range(DEPTH)],
                  h0b=[state_ssm_bwd[:, l].reshape((n_dec,) + st_shape) for l in range(DEPTH)])
    (y_prompt, sf, sb), (y_sample, _, _) = _run_groups([prompt, sample], mod, lw, final_g)
    out_state_shape = (n_ctx, SSD_HEADS, SSD_INNER // SSD_HEADS, SSD_STATE)
    new_f = jnp.stack([s.reshape(out_state_shape) for s in sf], axis=1)
    new_b = jnp.stack([s.reshape(out_state_shape) for s in sb], axis=1)
    return (y_prompt, y_sample, new_f, new_b)
```

```python
import functools

import jax
import jax.numpy as jnp
from jax import lax
from jax.experimental import pallas as pl
from jax.experimental.pallas import tpu as pltpu

D_MODEL = 1024
DEPTH = 2
GRID_W = 64
MIX_POOL = 512
POOL_WINDOWS = (2, 4, 8, 16)
POOL_GROUP_CH = 128
SSD_INNER = 512
SSD_HEADS = 8
SSD_HEAD_DIM = SSD_INNER // SSD_HEADS
SSD_STATE = 128
CONV_K = 5
CHUNK = 128
CONV_DIM = 1024
N_EXPERTS = 16
CAPACITY_FACTOR = 2
N_MOD = 6
POS_BASE = 10000.0
EPS = 1e-6
LOG2_E = 1.4426950408889634

SUBLANES = 8
LANES = 128
PACKED_ROWS = 16
HALO = SUBLANES
N_MAIN = MIX_POOL + SSD_INNER + CONV_DIM
HEAD_PAIRS = SSD_HEADS // 2
EXPERTS_PER_DOT = 2
SSD_MAX_CHUNKS_PER_STEP = 4
MOD_COLS_PER_STEP = 1536
WINDOWED_MIN_SEQ = 1024
TOKEN_BLOCKS = 8
WINDOW_DIVISOR = 4
GATHER_EXPERT_GROUPS = 2
FLOAT_MAGNITUDE_BITS = 31
VMEM_LIMIT = 56 * 1024 * 1024

F32 = jnp.float32
BF16 = jnp.bfloat16
HI = lax.Precision.HIGHEST


def _sigmoid(x):
    return 1.0 / (1.0 + jnp.exp(-x))


def _params(sem):
    return pltpu.CompilerParams(dimension_semantics=sem, vmem_limit_bytes=VMEM_LIMIT)


def _mod_kernel(cond_ref, w_ref, b_ref, out_ref):
    c = cond_ref[...]
    s = c * _sigmoid(c)
    out_ref[0] = jnp.dot(s, w_ref[0], precision=HI, preferred_element_type=F32) + b_ref[0]


def _modulation(cond, w_ada, b_ada):
    n_rows = cond.shape[0]
    n_out = N_MOD * D_MODEL
    tn = MOD_COLS_PER_STEP
    return pl.pallas_call(
        _mod_kernel,
        grid=(DEPTH, n_out // tn),
        in_specs=[
            pl.BlockSpec((n_rows, D_MODEL), lambda l, j: (0, 0)),
            pl.BlockSpec((1, D_MODEL, tn), lambda l, j: (l, 0, j)),
            pl.BlockSpec((1, 1, tn), lambda l, j: (l, 0, j)),
        ],
        out_specs=pl.BlockSpec((1, n_rows, tn), lambda l, j: (l, 0, j)),
        out_shape=jax.ShapeDtypeStruct((DEPTH, n_rows, n_out), F32),
        compiler_params=_params(("arbitrary", "arbitrary")),
        name="modulation",
    )(cond, w_ada, b_ada.reshape(DEPTH, 1, n_out))


def _cumsum_pieces(dta):
    d_hi = dta.astype(BF16)
    rest = dta - d_hi.astype(F32)
    d_mid = rest.astype(BF16)
    return d_hi, d_mid, (rest - d_mid.astype(F32)).astype(BF16)


def _grid_position_rows(table_ref, grid_row, col0, n_cols):
    row_part = jnp.broadcast_to(table_ref[pl.ds(grid_row, 1), :], (n_cols, D_MODEL // 2))
    return jnp.concatenate([row_part, table_ref[col0:col0 + n_cols, :]], axis=1)


def _in_row(*refs, tl, seq_len, with_pos, req):
    (xp_ref, xc_ref, xn_ref, mod_ref, g_ref, wmain_ref, wdt_ref, poolw_ref, pscale_ref, cw_ref, cb_ref,
     alog_ref, bias_ref) = refs[:13]
    table_ref = refs[13] if with_pos else None
    ypool_ref, z_ref, xs_ref, bc_ref, dt_ref, cum_ref = refs[13 + with_pos:19 + with_pos]
    x0_ref = refs[19 + with_pos] if with_pos else None
    proj_scr = refs[-1].at[req]
    i = pl.program_id(1)
    n = pl.num_programs(1)
    rows = tl + 2 * HALO
    x_ext = jnp.concatenate([xp_ref[0], xc_ref[req], xn_ref[0]], axis=0)
    if with_pos:
        r0 = i * (tl // GRID_W)
        last_row = seq_len // GRID_W - 1
        pieces = [_grid_position_rows(table_ref, jnp.maximum(r0 - 1, 0), GRID_W - HALO, HALO)]
        pieces += [_grid_position_rows(table_ref, r0 + k, 0, GRID_W) for k in range(tl // GRID_W)]
        pieces += [_grid_position_rows(table_ref, jnp.minimum(r0 + tl // GRID_W, last_row), 0, HALO)]
        x_ext = x_ext + jnp.concatenate(pieces, axis=0)
        x0_ref[req] = x_ext[HALO:HALO + tl]
    mod = mod_ref[0]
    ms = jnp.mean(x_ext * x_ext, axis=-1, keepdims=True)
    h = x_ext * lax.rsqrt(ms + EPS) * (g_ref[...] * (1.0 + mod[1:2])) + mod[0:1]
    row = lax.broadcasted_iota(jnp.int32, (rows, 1), 0)
    lo_ok = jnp.where(i > 0, 0, HALO)
    hi_ok = jnp.where(i < n - 1, rows, tl + HALO)
    hb = jnp.where((row >= lo_ok) & (row < hi_ok), h, 0.0).astype(BF16)
    proj_scr[...] = jnp.dot(hb, wmain_ref[...], preferred_element_type=F32)
    dt_full = jnp.dot(hb, wdt_ref[...], preferred_element_type=F32)

    x_in = dt_full[HALO:HALO + tl] + bias_ref[...]
    dt = jnp.maximum(x_in, 0.0) + jnp.log1p(jnp.exp(-jnp.abs(x_in)))
    dt_ref[req] = dt
    r = lax.broadcasted_iota(jnp.int32, (CHUNK, CHUNK), 0)
    s = lax.broadcasted_iota(jnp.int32, (CHUNK, CHUNK), 1)
    tri = jnp.concatenate([jnp.where(r >= s, 1.0, 0.0), jnp.where(r <= s, 1.0, 0.0)], axis=1).astype(BF16)
    head = lax.broadcasted_iota(jnp.int32, (tl, LANES), 1)
    dta = jnp.where(head < 2 * SSD_HEADS, dt * (-LOG2_E * jnp.exp(alog_ref[...])), 0.0)
    pieces = _cumsum_pieces(dta)
    fwd_cols = s < SSD_HEADS
    for k in range(tl // CHUNK):
        acc = jnp.zeros((CHUNK, LANES), F32)
        for piece in pieces:
            pk = piece[k * CHUNK:(k + 1) * CHUNK]
            zero = jnp.zeros_like(pk)
            stacked = jnp.concatenate([jnp.where(fwd_cols, pk, zero), jnp.where(fwd_cols, zero, pk)], axis=0)
            acc = acc + jnp.dot(tri, stacked, preferred_element_type=F32)
        cum_ref[req, k * CHUNK:(k + 1) * CHUNK, :] = acc

    def ahead(v, k):
        return pltpu.roll(v, (-k) % rows, axis=0)

    t_glob = i * tl + lax.broadcasted_iota(jnp.int32, (tl, 1), 0)
    outs = []
    for gi, w in enumerate(POOL_WINDOWS):
        c0 = gi * POOL_GROUP_CH
        u = proj_scr[:, c0:c0 + POOL_GROUP_CH]
        acc = u
        span = 1
        while 2 * span < w:
            acc = acc + ahead(acc, span)
            span *= 2
        acc = acc + ahead(acc, -span)
        lo = jnp.maximum(t_glob - w // 2, 0)
        hi = jnp.minimum(t_glob - w // 2 + w, seq_len)
        cnt = (hi - lo).astype(F32)
        p = acc[HALO:HALO + tl] / cnt - u[HALO:HALO + tl]
        outs.append(jnp.dot(p.astype(BF16), poolw_ref[gi], preferred_element_type=F32))
    ypool_ref[req] = (jnp.concatenate(outs, axis=1) * pscale_ref[...]).astype(BF16)

    c0 = MIX_POOL + SSD_INNER
    acc = cb_ref[...] + cw_ref[0:1, :] * proj_scr[HALO - 2:HALO - 2 + tl, c0:c0 + CONV_DIM]
    for k in range(1, CONV_K):
        acc = acc + cw_ref[k:k + 1, :] * proj_scr[HALO - 2 + k:HALO - 2 + k + tl, c0:c0 + CONV_DIM]
    act = acc * _sigmoid(acc)
    xs_ref[req] = act[:, :SSD_INNER].astype(BF16)
    bc_ref[req] = act[:, SSD_INNER:].astype(BF16)
    z_ref[req] = proj_scr[HALO:HALO + tl, MIX_POOL:MIX_POOL + SSD_INNER]


def _in_kernel(*refs, tl, seq_len, with_pos):
    for req in range(refs[1].shape[0]):
        _in_row(*refs, tl=tl, seq_len=seq_len, with_pos=with_pos, req=req)


def _in_proj(x, mod_l, mod_row, norm_g, w_main, w_dt, pool_w, pool_scale, conv_w, conv_b, alog_row, bias_row, tl,
             pos_table=None, rows_per_step=1):
    bn, seq_len, _ = x.shape
    nt = seq_len // tl
    rps = rows_per_step
    assert rps == 1 or nt == 1
    hb = tl // HALO
    last_halo = seq_len // HALO - 1
    with_pos = pos_table is not None
    const = lambda *shape: pl.BlockSpec(shape, lambda b, i: (0,) * len(shape))
    tok = lambda width: pl.BlockSpec((rps, tl, width), lambda b, i: (b, i, 0))
    in_specs = [
        pl.BlockSpec((1, HALO, D_MODEL), lambda b, i: (b * rps, jnp.maximum(i * hb - 1, 0), 0)),
        tok(D_MODEL),
        pl.BlockSpec((1, HALO, D_MODEL), lambda b, i: (b * rps, jnp.minimum((i + 1) * hb, last_halo), 0)),
        pl.BlockSpec((1, N_MOD, D_MODEL), lambda b, i: (mod_row(b * rps), 0, 0)),
        const(1, D_MODEL),
        const(D_MODEL, N_MAIN),
        const(D_MODEL, LANES),
        const(len(POOL_WINDOWS), POOL_GROUP_CH, POOL_GROUP_CH),
        const(1, MIX_POOL),
        const(CONV_K, CONV_DIM),
        const(1, CONV_DIM),
        const(1, LANES),
        const(1, LANES),
    ]
    out_specs = [tok(MIX_POOL), tok(SSD_INNER), tok(SSD_INNER), tok(CONV_DIM - SSD_INNER), tok(LANES), tok(LANES)]
    out_shape = [
        jax.ShapeDtypeStruct((bn, seq_len, MIX_POOL), BF16),
        jax.ShapeDtypeStruct((bn, seq_len, SSD_INNER), F32),
        jax.ShapeDtypeStruct((bn, seq_len, SSD_INNER), BF16),
        jax.ShapeDtypeStruct((bn, seq_len, CONV_DIM - SSD_INNER), BF16),
        jax.ShapeDtypeStruct((bn, seq_len, LANES), F32),
        jax.ShapeDtypeStruct((bn, seq_len, LANES), F32),
    ]
    args = [x, x, x, mod_l, norm_g, w_main, w_dt, pool_w, pool_scale, conv_w, conv_b, alog_row, bias_row]
    if with_pos:
        assert tl % GRID_W == 0 and pos_table.shape == (GRID_W, D_MODEL // 2)
        in_specs.append(const(GRID_W, D_MODEL // 2))
        out_specs.append(tok(D_MODEL))
        out_shape.append(jax.ShapeDtypeStruct((bn, seq_len, D_MODEL), F32))
        args.append(pos_table)
    return pl.pallas_call(
        functools.partial(_in_kernel, tl=tl, seq_len=seq_len, with_pos=with_pos),
        grid=(bn // rps, nt),
        in_specs=in_specs,
        out_specs=out_specs,
        out_shape=out_shape,
        scratch_shapes=[pltpu.VMEM((rps, tl + 2 * HALO, N_MAIN), F32)],
        compiler_params=_params(("arbitrary", "arbitrary")),
        name="in_proj",
    )(*args)


def _ssd_chunk(xsf_ref, bcf_ref, dtf_ref, cumf_ref, xsb_ref, bcb_ref, dtb_ref, cumb_ref, yf_ref, yb_ref, st_ref, rf, rb):
    r = lax.broadcasted_iota(jnp.int32, (CHUNK, CHUNK), 0)
    s = lax.broadcasted_iota(jnp.int32, (CHUNK, CHUNK), 1)
    lane2 = lax.broadcasted_iota(jnp.int32, (CHUNK, 2 * LANES), 1)
    lane_head = lax.shift_right_logical(lane2, SSD_HEAD_DIM.bit_length() - 1)
    row2 = lax.broadcasted_iota(jnp.int32, (2 * LANES, SSD_STATE), 0)
    is_fwd = s < SSD_HEADS
    dt = jnp.where(is_fwd, dtf_ref[0, rf:rf + CHUNK, :], dtb_ref[0, rb:rb + CHUNK, :])
    cum = jnp.where(is_fwd, cumf_ref[0, rf:rf + CHUNK, :], cumb_ref[0, rb:rb + CHUNK, :])
    tot = jnp.where(lax.broadcasted_iota(jnp.int32, (1, LANES), 1) < SSD_HEADS,
                    cum[CHUNK - 1:CHUNK, :], cum[0:1, :])
    cdec = jnp.exp2(tot)
    cum_t = cum.T
    dt_t = dt.T
    ecum_t = jnp.exp2(cum_t)
    w_t = (dt * jnp.exp2(tot - cum)).T

    def head_rows(arr_t, h0):
        return jnp.concatenate([jnp.broadcast_to(arr_t[h0 + k:h0 + k + 1, :], (SSD_HEAD_DIM, CHUNK))
                                for k in range(4)], axis=0)

    def one_dir(d, xs_ref, bc_ref, y_ref, r0):
        mask = (r >= s) if d == 0 else (r <= s)
        off = d * SSD_HEADS
        bcv = bc_ref[0, r0:r0 + CHUNK, :]
        b01 = bcv[:, :2 * SSD_STATE].astype(BF16)
        c01 = bcv[:, 2 * SSD_STATE:].astype(BF16)
        zero2 = jnp.zeros_like(b01)
        b_blk = jnp.concatenate([jnp.where(lane2 < SSD_STATE, b01, zero2),
                                 jnp.where(lane2 < SSD_STATE, zero2, b01)], axis=0)
        cb01 = lax.dot_general(c01, b_blk, (((1,), (1,)), ((), ())), preferred_element_type=F32)
        for g in range(2):
            h0 = off + 4 * g
            b_g = b01[:, g * SSD_STATE:(g + 1) * SSD_STATE]
            c_g = c01[:, g * SSD_STATE:(g + 1) * SSD_STATE]
            cb = cb01[:, g * CHUNK:(g + 1) * CHUNK]
            xs_g = xs_ref[0, r0:r0 + CHUNK, g * 2 * LANES:(g + 1) * 2 * LANES]
            x_b = xs_g.astype(BF16)
            ms, x_parts = [], []
            for k in range(4):
                hh = h0 + k
                seg = cum[:, hh:hh + 1] - cum_t[hh:hh + 1, :]
                dec = jnp.exp2(jnp.where(mask, seg, -jnp.inf))
                ms.append((cb * dec * dt_t[hh:hh + 1, :]).astype(BF16))
                x_parts.append(jnp.where(lane_head == k, x_b, jnp.zeros_like(x_b)))
            y_diag = jnp.dot(jnp.concatenate(ms, axis=1), jnp.concatenate(x_parts, axis=0),
                             preferred_element_type=F32)
            st = st_ref[d, 2 * g:2 * g + 2].reshape(2 * LANES, SSD_STATE)
            y_off_t = lax.dot_general(st.astype(BF16), c_g, (((1,), (1,)), ((), ())),
                                      preferred_element_type=F32) * head_rows(ecum_t, h0)
            y_ref[0, r0:r0 + CHUNK, g * 2 * LANES:(g + 1) * 2 * LANES] = (y_diag + y_off_t.T).astype(BF16)
            x_d = (xs_g.astype(F32).T * head_rows(w_t, h0)).astype(BF16)
            cs = jnp.dot(x_d, b_g, preferred_element_type=F32)
            dcol = cdec[:, h0 + 3:h0 + 4]
            for k in (2, 1, 0):
                dcol = jnp.where(row2 < (k + 1) * SSD_HEAD_DIM, cdec[:, h0 + k:h0 + k + 1], dcol)
            st_ref[d, 2 * g:2 * g + 2] = (st * dcol + cs).reshape(2, LANES, SSD_STATE)

    one_dir(0, xsf_ref, bcf_ref, yf_ref, rf)
    one_dir(1, xsb_ref, bcb_ref, yb_ref, rb)


def _ssd_kernel(xsf_ref, bcf_ref, dtf_ref, cumf_ref, xsb_ref, bcb_ref, dtb_ref, cumb_ref, h0f_ref, h0b_ref,
                yf_ref, yb_ref, hTf_ref, hTb_ref, st_ref):
    c = pl.program_id(1)
    nc = pl.num_programs(1)

    @pl.when(c == 0)
    def _():
        st_ref[0] = h0f_ref[0]
        st_ref[1] = h0b_ref[0]

    n_chunks = xsf_ref.shape[1] // CHUNK
    for k in range(n_chunks):
        _ssd_chunk(xsf_ref, bcf_ref, dtf_ref, cumf_ref, xsb_ref, bcb_ref, dtb_ref, cumb_ref, yf_ref, yb_ref, st_ref,
                   k * CHUNK, (n_chunks - 1 - k) * CHUNK)

    @pl.when(c == nc - 1)
    def _():
        hTf_ref[0] = st_ref[0]
        hTb_ref[0] = st_ref[1]


def _ssd_kernel_in_place(*refs):
    _ssd_kernel(*refs[:10], *refs[12:])


def _ssd(xs, bc, dt, cum, h0f, h0b, h0_row, layer, prev_states):
    bn, seq_len, _ = xs.shape
    step_rows = min(SSD_MAX_CHUNKS_PER_STEP * CHUNK, seq_len)
    nc = seq_len // step_rows
    fwd = lambda width: pl.BlockSpec((1, step_rows, width), lambda b, c: (b, c, 0))
    bwd = lambda width: pl.BlockSpec((1, step_rows, width), lambda b, c: (b, nc - 1 - c, 0))
    st_spec_in = pl.BlockSpec((1, HEAD_PAIRS, LANES, SSD_STATE), lambda b, c: (h0_row(b, layer), 0, 0, 0))
    st_spec_out = pl.BlockSpec((1, HEAD_PAIRS, LANES, SSD_STATE), lambda b, c: (b * DEPTH + layer, 0, 0, 0))
    st_shape = jax.ShapeDtypeStruct((bn * DEPTH, HEAD_PAIRS, LANES, SSD_STATE), F32)
    in_place = prev_states is not None
    in_specs = [fwd(SSD_INNER), fwd(CONV_DIM - SSD_INNER), fwd(LANES), fwd(LANES),
                bwd(SSD_INNER), bwd(CONV_DIM - SSD_INNER), bwd(LANES), bwd(LANES),
                st_spec_in, st_spec_in]
    args = (xs, bc, dt, cum, xs, bc, dt, cum, h0f, h0b)
    if in_place:
        in_specs += [pl.BlockSpec(memory_space=pl.ANY)] * 2
        args += tuple(prev_states)
    return pl.pallas_call(
        _ssd_kernel_in_place if in_place else _ssd_kernel,
        grid=(bn, nc),
        in_specs=in_specs,
        out_specs=[fwd(SSD_INNER), bwd(SSD_INNER), st_spec_out, st_spec_out],
        out_shape=[jax.ShapeDtypeStruct((bn, seq_len, SSD_INNER), BF16),
                   jax.ShapeDtypeStruct((bn, seq_len, SSD_INNER), BF16), st_shape, st_shape],
        scratch_shapes=[pltpu.VMEM((2, HEAD_PAIRS, LANES, SSD_STATE), F32)],
        input_output_aliases={10: 2, 11: 3} if in_place else {},
        compiler_params=_params(("arbitrary", "arbitrary")),
        name="ssd",
    )(*args)


def _out_row(x_ref, yf_ref, yb_ref, xs_ref, z_ref, yp_ref, dsk_ref, sg_ref, wo_ref, mod_ref, ng_ref, wr_ref,
             x1_ref, h2_ref, afft_ref, req):
    y = yf_ref[req].astype(F32) + yb_ref[req].astype(F32) + dsk_ref[...] * xs_ref[req].astype(F32)
    z = z_ref[req]
    y = y * (z * _sigmoid(z))
    y = y * lax.rsqrt(jnp.mean(y * y, axis=-1, keepdims=True) + EPS) * sg_ref[...]
    o = (jnp.dot(yp_ref[req], wo_ref[:MIX_POOL, :], preferred_element_type=F32)
         + jnp.dot(y.astype(BF16), wo_ref[MIX_POOL:, :], preferred_element_type=F32))
    mod = mod_ref[0]
    x1 = x_ref[req] + mod[2:3] * o
    x1_ref[req] = x1
    h2 = x1 * lax.rsqrt(jnp.mean(x1 * x1, axis=-1, keepdims=True) + EPS) * (ng_ref[...] * (1.0 + mod[4:5])) + mod[3:4]
    h2_hi = h2.astype(BF16)
    h2_ref[req] = h2_hi
    h2_lo = (h2 - h2_hi.astype(F32)).astype(BF16)
    pa = jnp.dot(h2_hi, wr_ref[0], preferred_element_type=F32)
    pb = jnp.dot(h2_lo, wr_ref[1], preferred_element_type=F32)
    logits = pa + pltpu.roll(pa, LANES - N_EXPERTS, axis=1) + pb
    lane = lax.broadcasted_iota(jnp.int32, logits.shape, 1)
    lg = jnp.where(lane < N_EXPERTS, logits, -jnp.inf)
    e = jnp.exp(lg - jnp.max(lg, axis=-1, keepdims=True))
    aff = e / jnp.sum(e, axis=-1, keepdims=True)
    afft_ref[req] = aff.T[:N_EXPERTS, :]


def _out_kernel(*refs):
    for req in range(refs[0].shape[0]):
        _out_row(*refs, req=req)


def _out_proj(x, y_f, y_b, xs, z, y_pool, dsk, ssm_g, w_out, mod_l, mod_row, norm_ffn_g, w_router, tl,
              rows_per_step=1):
    bn, seq_len, _ = x.shape
    nt = seq_len // tl
    rps = rows_per_step
    const = lambda *shape: pl.BlockSpec(shape, lambda b, i: (0,) * len(shape))
    tok = lambda width: pl.BlockSpec((rps, tl, width), lambda b, i: (b, i, 0))
    return pl.pallas_call(
        _out_kernel,
        grid=(bn // rps, nt),
        in_specs=[tok(D_MODEL), tok(SSD_INNER), tok(SSD_INNER), tok(SSD_INNER), tok(SSD_INNER), tok(MIX_POOL),
                  const(1, SSD_INNER), const(1, SSD_INNER), const(D_MODEL, D_MODEL),
                  pl.BlockSpec((1, N_MOD, D_MODEL), lambda b, i: (mod_row(b * rps), 0, 0)),
                  const(1, D_MODEL), const(2, D_MODEL, LANES)],
        out_specs=[tok(D_MODEL), tok(D_MODEL), pl.BlockSpec((rps, N_EXPERTS, tl), lambda b, i: (b, 0, i))],
        out_shape=[jax.ShapeDtypeStruct((bn, seq_len, D_MODEL), F32),
                   jax.ShapeDtypeStruct((bn, seq_len, D_MODEL), BF16),
                   jax.ShapeDtypeStruct((bn, N_EXPERTS, seq_len), F32)],
        compiler_params=_params(("arbitrary", "arbitrary")),
        name="out_proj",
    )(x, y_f, y_b, xs, z, y_pool, dsk, ssm_g, w_out, mod_l, norm_ffn_g, w_router)


def _topk_kernel(afft_ref, pos_ref, post_ref, tab_ref, *, seq_len, cap, rb, tbs):
    n_rows = rb * N_EXPERTS
    a = afft_ref[...].reshape(n_rows, seq_len)

    def body(it, thr):
        cand_bits = thr | jnp.left_shift(jnp.int32(1), FLOAT_MAGNITUDE_BITS - 1 - it)
        cnt = jnp.sum((a >= pltpu.bitcast(cand_bits, F32)).astype(jnp.int32), axis=1, keepdims=True)
        return jnp.where(cnt >= cap, cand_bits, thr)

    thr = lax.fori_loop(0, FLOAT_MAGNITUDE_BITS, body, jnp.zeros((n_rows, 1), jnp.int32))
    lo = pltpu.bitcast(thr, F32)
    hi = pltpu.bitcast(thr + 1, F32)
    gt = a >= hi
    eq = (a >= lo) & (a < hi)
    need = cap - jnp.sum(gt.astype(jnp.int32), axis=1, keepdims=True)

    r = lax.broadcasted_iota(jnp.int32, (LANES, LANES), 0)
    s = lax.broadcasted_iota(jnp.int32, (LANES, LANES), 1)
    strict_upper = (r < s).astype(BF16)
    nblk = seq_len // LANES

    def excl_cumsum(m):
        carry = jnp.zeros((n_rows, 1), F32)
        outs = []
        for blk in range(nblk):
            mb = m[:, blk * LANES:(blk + 1) * LANES]
            outs.append(jnp.dot(mb.astype(BF16), strict_upper, preferred_element_type=F32) + carry)
            carry = carry + jnp.sum(mb, axis=1, keepdims=True)
        return jnp.concatenate(outs, axis=1)

    eq_rank = excl_cumsum(jnp.where(eq, 1.0, 0.0))
    sel = gt | (eq & (eq_rank < need.astype(F32)))
    rank = excl_cumsum(jnp.where(sel, 1.0, 0.0))
    pos = jnp.where(sel, rank, -1.0)
    pos_ref[...] = pos.astype(jnp.int32).reshape(rb, N_EXPERTS, seq_len)

    lane = lax.broadcasted_iota(jnp.int32, (n_rows, LANES), 1)
    tab = jnp.full((n_rows, LANES), float(cap), F32)
    for k in range(seq_len // tbs):
        tab = jnp.where(lane == k, rank[:, k * tbs:k * tbs + 1], tab)
    tab_ref[...] = tab.astype(jnp.int32).reshape(rb, N_EXPERTS, LANES)

    pad = jnp.full((LANES - N_EXPERTS, LANES), -1.0, F32)
    for i in range(rb):
        for blk in range(nblk):
            tile = jnp.concatenate([pos[i * N_EXPERTS:(i + 1) * N_EXPERTS, blk * LANES:(blk + 1) * LANES], pad], axis=0)
            post_ref[i, blk * LANES:(blk + 1) * LANES, :] = tile.T.astype(jnp.int32)


def _topk(afft, cap, rb, tbs):
    bn, _, seq_len = afft.shape
    return pl.pallas_call(
        functools.partial(_topk_kernel, seq_len=seq_len, cap=cap, rb=rb, tbs=tbs),
        grid=(bn // rb,),
        in_specs=[pl.BlockSpec((rb, N_EXPERTS, seq_len), lambda b: (b, 0, 0))],
        out_specs=[pl.BlockSpec((rb, N_EXPERTS, seq_len), lambda b: (b, 0, 0)),
                   pl.BlockSpec((rb, seq_len, LANES), lambda b: (b, 0, 0)),
                   pl.BlockSpec((rb, N_EXPERTS, LANES), lambda b: (b, 0, 0))],
        out_shape=[jax.ShapeDtypeStruct((bn, N_EXPERTS, seq_len), jnp.int32),
                   jax.ShapeDtypeStruct((bn, seq_len, LANES), jnp.int32),
                   jax.ShapeDtypeStruct((bn, N_EXPERTS, LANES), jnp.int32)],
        compiler_params=_params(("arbitrary",)),
        name="topk",
    )(afft)


def _gather_kernel(h2_ref, pos_ref, afft_ref, xe_ref, gate_ref, *, cap):
    n_rows, seq_len, _ = h2_ref.shape
    n_slots = N_EXPERTS * cap
    rank = lax.broadcasted_iota(jnp.int32, (cap, seq_len), 0)
    for i in range(n_rows):
        onehots, gates = [], []
        for e in range(N_EXPERTS):
            hit = pos_ref[i, e:e + 1, :] == rank
            onehots.append(jnp.where(hit, 1.0, 0.0).astype(BF16))
            gates.append(jnp.sum(jnp.where(hit, afft_ref[i, e:e + 1, :], 0.0), axis=1, keepdims=True))
        xe_ref[i] = jnp.dot(jnp.concatenate(onehots, axis=0), h2_ref[i], preferred_element_type=F32).astype(BF16)
        gate_ref[i] = jnp.broadcast_to(jnp.concatenate(gates, axis=0), (n_slots, LANES))


def _gather(h2, pos, afft, cap, rows_per_step):
    bn, seq_len, _ = h2.shape
    n_slots = N_EXPERTS * cap
    rows = lambda *tail: pl.BlockSpec((rows_per_step,) + tail, lambda b: (b, 0, 0))
    return pl.pallas_call(
        functools.partial(_gather_kernel, cap=cap),
        grid=(bn // rows_per_step,),
        in_specs=[rows(seq_len, D_MODEL), rows(N_EXPERTS, seq_len), rows(N_EXPERTS, seq_len)],
        out_specs=[rows(n_slots, D_MODEL), rows(n_slots, LANES)],
        out_shape=[jax.ShapeDtypeStruct((bn, n_slots, D_MODEL), BF16),
                   jax.ShapeDtypeStruct((bn, n_slots, LANES), F32)],
        compiler_params=_params(("arbitrary",)),
        name="gather",
    )(h2, pos, afft)


def _window_starts(tab_ref, b, e0, n_e, t, cap, win):
    starts, overflows = [], []
    for ee in range(n_e):
        base = (b * N_EXPERTS + e0 + ee) * LANES + t
        p0 = tab_ref[base]
        p1 = tab_ref[base + 1]
        align_bits = PACKED_ROWS.bit_length() - 1
        w0 = jnp.minimum(lax.shift_left(lax.shift_right_logical(p0, align_bits), align_bits), cap - win)
        starts.append(w0)
        overflows.append(p1 - w0 > win)
    return starts, overflows


def _gather_win_kernel(tab_ref, h2_ref, pos_ref, afft_ref, xe_ref, gate_ref, *, cap, win, n_e):
    b = pl.program_id(0)
    e0 = pl.program_id(1) * n_e
    t = pl.program_id(2)
    tbs = h2_ref.shape[1]

    @pl.when(t == 0)
    def _():
        xe_ref[...] = jnp.zeros(xe_ref.shape, BF16)
        gate_ref[...] = jnp.zeros(gate_ref.shape, F32)

    starts, overflows = _window_starts(tab_ref, b, e0, n_e, t, cap, win)

    def accumulate(row0, n, x, g):
        cur = xe_ref[0, pl.ds(row0, n), :]
        xe_ref[0, pl.ds(row0, n), :] = jnp.where(g > 0.0, x.astype(BF16), cur)
        gate_ref[0, pl.ds(row0, n), :] = gate_ref[0, pl.ds(row0, n), :] + jnp.broadcast_to(g, (n, LANES))

    def onehot_and_gate(ee, shift, n):
        sub = lax.broadcasted_iota(jnp.int32, (n, tbs), 0)
        hit = (pos_ref[0, pl.ds(e0 + ee, 1), :] - shift) == sub
        gate = jnp.sum(jnp.where(hit, afft_ref[0, pl.ds(e0 + ee, 1), :], 0.0), axis=1, keepdims=True)
        return jnp.where(hit, 1.0, 0.0).astype(BF16), gate

    pieces = [onehot_and_gate(ee, jnp.where(overflows[ee], cap, starts[ee]), win) for ee in range(n_e)]
    x = jnp.dot(jnp.concatenate([p[0] for p in pieces], axis=0), h2_ref[0], preferred_element_type=F32)
    for ee in range(n_e):
        accumulate(pl.multiple_of(ee * cap + starts[ee], PACKED_ROWS), win, x[ee * win:(ee + 1) * win], pieces[ee][1])

    for ee in range(n_e):
        @pl.when(overflows[ee])
        def _(ee=ee):
            onehot, gate = onehot_and_gate(ee, 0, cap)
            accumulate(ee * cap, cap, jnp.dot(onehot, h2_ref[0], preferred_element_type=F32), gate)


def _gather_win(tab, h2, pos, afft, cap, win, tbs):
    bn, seq_len, _ = h2.shape
    n_slots = N_EXPERTS * cap
    n_e = N_EXPERTS // GATHER_EXPERT_GROUPS
    grid_spec = pltpu.PrefetchScalarGridSpec(
        num_scalar_prefetch=1,
        grid=(bn, N_EXPERTS // n_e, seq_len // tbs),
        in_specs=[pl.BlockSpec((1, tbs, D_MODEL), lambda b, g, t, tab: (b, t, 0)),
                  pl.BlockSpec((1, N_EXPERTS, tbs), lambda b, g, t, tab: (b, 0, t)),
                  pl.BlockSpec((1, N_EXPERTS, tbs), lambda b, g, t, tab: (b, 0, t))],
        out_specs=[pl.BlockSpec((1, n_e * cap, D_MODEL), lambda b, g, t, tab: (b, g, 0)),
                   pl.BlockSpec((1, n_e * cap, LANES), lambda b, g, t, tab: (b, g, 0))],
    )
    return pl.pallas_call(
        functools.partial(_gather_win_kernel, cap=cap, win=win, n_e=n_e),
        grid_spec=grid_spec,
        out_shape=[jax.ShapeDtypeStruct((bn, n_slots, D_MODEL), BF16),
                   jax.ShapeDtypeStruct((bn, n_slots, LANES), F32)],
        compiler_params=_params(("arbitrary", "arbitrary", "arbitrary")),
        name="gather_win",
    )(tab, h2, pos, afft)


def _ffn_kernel(*refs, tiles):
    n = len(tiles)
    xe_refs, gate_refs = refs[0:2 * n:2], refs[1:2 * n:2]
    wg_ref, wu_ref, wd_ref = refs[2 * n:2 * n + 3]
    ye_refs = refs[2 * n + 3:3 * n + 3]
    wg_s, wu_s, wd_s = refs[3 * n + 3:]
    j = pl.program_id(1)

    @pl.when(j == 0)
    def _():
        wg_s[...] = wg_ref[0, 0].astype(BF16)
        wu_s[...] = wu_ref[0, 0].astype(BF16)
        wd_s[...] = wd_ref[0, 0].astype(BF16)

    for (first, steps, rows), xe_ref, gate_ref, ye_ref in zip(tiles, xe_refs, gate_refs, ye_refs):
        @pl.when(jnp.logical_and(j >= first, j < first + steps))
        def _(rows=rows, xe_ref=xe_ref, gate_ref=gate_ref, ye_ref=ye_ref):
            x = xe_ref[...].reshape(rows, D_MODEL)
            g = jnp.dot(x, wg_s[...], preferred_element_type=F32)
            u = jnp.dot(x, wu_s[...], preferred_element_type=F32)
            hid = (g * _sigmoid(g) * u).astype(BF16)
            y = jnp.dot(hid, wd_s[...], preferred_element_type=F32) * gate_ref[...].reshape(rows, LANES)[:, :1]
            ye_ref[...] = y.astype(BF16).reshape(ye_ref.shape)


def _ffn(groups, w_gate, w_up, w_down, layer):
    tiles, in_specs, out_specs, out_shape, args = [], [], [], [], []
    first = 0
    for xe, gate, cap, bb in groups:
        steps = xe.shape[0] // bb
        tile = lambda e, j, first=first, steps=steps: (jnp.clip(j - first, 0, steps - 1), e, 0)
        in_specs += [pl.BlockSpec((bb, cap, D_MODEL), tile), pl.BlockSpec((bb, cap, LANES), tile)]
        out_specs.append(pl.BlockSpec((bb, cap, D_MODEL), tile))
        out_shape.append(jax.ShapeDtypeStruct(xe.shape, BF16))
        args += [xe, gate]
        tiles.append((first, steps, bb * cap))
        first += steps
    wspec = pl.BlockSpec((1, 1, D_MODEL, D_MODEL), lambda e, j: (layer, e, 0, 0))
    return pl.pallas_call(
        functools.partial(_ffn_kernel, tiles=tuple(tiles)),
        grid=(N_EXPERTS, first),
        in_specs=in_specs + [wspec, wspec, wspec],
        out_specs=out_specs,
        out_shape=out_shape,
        scratch_shapes=[pltpu.VMEM((D_MODEL, D_MODEL), BF16)] * 3,
        compiler_params=_params(("arbitrary", "arbitrary")),
        name="ffn",
    )(*args, w_gate, w_up, w_down)


def _finish(x1_ref, mod_ref, fg_ref, out_ref, acc, final, i=0):
    x2 = x1_ref[i] + mod_ref[0][5:6] * acc
    if final:
        x2 = x2 * lax.rsqrt(jnp.mean(x2 * x2, axis=-1, keepdims=True) + EPS) * fg_ref[...]
    out_ref[i] = x2


def _scatter_kernel(x1_ref, ye_ref, post_ref, mod_ref, fg_ref, out_ref, *, cap, final):
    n_rows = x1_ref.shape[0]
    n_slots = N_EXPERTS * cap
    slot = lax.broadcasted_iota(jnp.int32, (1, n_slots), 1)
    expert_of_slot = jnp.zeros((1, n_slots), jnp.int32)
    for e in range(1, N_EXPERTS):
        expert_of_slot = expert_of_slot + (slot >= e * cap).astype(jnp.int32)
    rank_of_slot = (slot - expert_of_slot * cap).astype(F32)
    spread = jnp.where(lax.broadcasted_iota(jnp.int32, (LANES, n_slots), 0) == expert_of_slot, 1.0, 0.0).astype(BF16)
    for i in range(n_rows):
        pt = post_ref[i]
        ranks = jnp.dot(pt.astype(F32).astype(BF16), spread, preferred_element_type=F32)
        onehot = jnp.where(ranks == rank_of_slot, 1.0, 0.0).astype(BF16)
        acc = jnp.dot(onehot, ye_ref[i], preferred_element_type=F32)
        _finish(x1_ref, mod_ref, fg_ref, out_ref, acc, final, i)


def _scatter(x1, ye, post, mod_l, mod_row, final_g, cap, final, rows_per_step):
    bn, seq_len, _ = x1.shape
    assert cap <= 256, "ranks must be exactly representable in bf16"
    n_slots = N_EXPERTS * cap
    rows = lambda *tail: pl.BlockSpec((rows_per_step,) + tail, lambda b: (b, 0, 0))
    return pl.pallas_call(
        functools.partial(_scatter_kernel, cap=cap, final=final),
        grid=(bn // rows_per_step,),
        in_specs=[rows(seq_len, D_MODEL), rows(n_slots, D_MODEL), rows(seq_len, LANES),
                  pl.BlockSpec((1, N_MOD, D_MODEL), lambda b: (mod_row(b * rows_per_step), 0, 0)),
                  pl.BlockSpec((1, D_MODEL), lambda b: (0, 0))],
        out_specs=rows(seq_len, D_MODEL),
        out_shape=jax.ShapeDtypeStruct((bn, seq_len, D_MODEL), F32),
        compiler_params=_params(("arbitrary",)),
        name="scatter",
    )(x1, ye, post, mod_l, final_g)


def _scatter_win_kernel(tab_ref, x1_ref, ye_ref, post_ref, mod_ref, fg_ref, out_ref, acc_ref, *, cap, win, final):
    b = pl.program_id(0)
    t = pl.program_id(1)
    tbs = x1_ref.shape[1]
    pt = post_ref[0]
    starts, overflows = _window_starts(tab_ref, b, 0, N_EXPERTS, t, cap, win)

    def onehot(e, shift, n):
        lane = lax.broadcasted_iota(jnp.int32, (tbs, n), 1)
        return jnp.where((pt[:, e:e + 1] - shift) == lane, 1.0, 0.0).astype(BF16)

    acc = None
    for e0 in range(0, N_EXPERTS, EXPERTS_PER_DOT):
        group = range(e0, e0 + EXPERTS_PER_DOT)
        onehots = [onehot(e, jnp.where(overflows[e], cap, starts[e]), win) for e in group]
        rows = [ye_ref[0, pl.ds(pl.multiple_of(e * cap + starts[e], PACKED_ROWS), win), :] for e in group]
        part = jnp.dot(jnp.concatenate(onehots, axis=1), jnp.concatenate(rows, axis=0), preferred_element_type=F32)
        acc = part if acc is None else acc + part
    acc_ref[...] = acc

    for e in range(N_EXPERTS):
        @pl.when(overflows[e])
        def _(e=e):
            acc_ref[...] += jnp.dot(onehot(e, 0, cap), ye_ref[0, e * cap:(e + 1) * cap, :],
                                    preferred_element_type=F32)

    _finish(x1_ref, mod_ref, fg_ref, out_ref, acc_ref[...], final)


def _scatter_win(tab, x1, ye, post, mod_l, mod_row, final_g, cap, win, tbs, final):
    bn, seq_len, _ = x1.shape
    n_slots = N_EXPERTS * cap
    grid_spec = pltpu.PrefetchScalarGridSpec(
        num_scalar_prefetch=1,
        grid=(bn, seq_len // tbs),
        in_specs=[pl.BlockSpec((1, tbs, D_MODEL), lambda b, t, tab: (b, t, 0)),
                  pl.BlockSpec((1, n_slots, D_MODEL), lambda b, t, tab: (b, 0, 0)),
                  pl.BlockSpec((1, tbs, LANES), lambda b, t, tab: (b, t, 0)),
                  pl.BlockSpec((1, N_MOD, D_MODEL), lambda b, t, tab: (mod_row(b), 0, 0)),
                  pl.BlockSpec((1, D_MODEL), lambda b, t, tab: (0, 0))],
        out_specs=pl.BlockSpec((1, tbs, D_MODEL), lambda b, t, tab: (b, t, 0)),
        scratch_shapes=[pltpu.VMEM((tbs, D_MODEL), F32)],
    )
    return pl.pallas_call(
        functools.partial(_scatter_win_kernel, cap=cap, win=win, final=final),
        grid_spec=grid_spec,
        out_shape=jax.ShapeDtypeStruct((bn, seq_len, D_MODEL), F32),
        compiler_params=_params(("arbitrary", "arbitrary")),
        name="scatter_win",
    )(tab, x1, ye, post, mod_l, final_g)


def _grid_position_table(n_tokens, dim):
    assert n_tokens // GRID_W <= GRID_W
    quarter = dim // 4
    inv_freq = jnp.power(POS_BASE, -jnp.arange(quarter, dtype=F32) / quarter)
    angle = jnp.arange(GRID_W).astype(F32)[:, None] * inv_freq[None]
    return jnp.concatenate([jnp.sin(angle), jnp.cos(angle)], axis=-1)


def _pad_cols(w, width):
    return jnp.pad(w, ((0, 0), (0, width - w.shape[1])))


def _router_pieces(w):
    w_hi = w.astype(BF16)
    w_lo = (w - w_hi.astype(F32)).astype(BF16)
    return jnp.stack([_pad_cols(jnp.concatenate([w_hi, w_lo], axis=1), LANES), _pad_cols(w_hi, LANES)])


def _run_groups(groups, mod, lw, final_g):
    xs_res = [g["x"] for g in groups]
    states = [None for _ in groups]
    for l in range(DEPTH):
        w = lw[l]
        final = l == DEPTH - 1
        mids = []
        for gi, g in enumerate(groups):
            x = xs_res[gi]
            bn, seq_len, _ = x.shape
            cap = CAPACITY_FACTOR * seq_len // N_EXPERTS
            windowed = seq_len >= WINDOWED_MIN_SEQ
            tbs = max(LANES, seq_len // TOKEN_BLOCKS) if windowed else seq_len
            win = cap // WINDOW_DIVISOR
            mod_row, tl = g["mod_row"], g["tl"]
            pos_table = g.get("pos_table") if l == 0 else None
            outs = _in_proj(x, mod[l], mod_row, w["norm_mix_g"], w["w_main"], w["w_dt"], w["pool_w"],
                            w["pool_scale"], w["conv_w"], w["conv_b"], w["alog_row"], w["bias_row"], tl, pos_table,
                            g.get("in_proj_rows", 1))
            y_pool, z, xs, bc, dt, cum = outs[:6]
            if pos_table is not None:
                x = outs[6]
            y_f, y_b, hT_f, hT_b = _ssd(xs, bc, dt, cum, g["h0f"], g["h0b"], g["h0_row"], l, states[gi])
            states[gi] = (hT_f, hT_b)
            x1, h2, afft = _out_proj(x, y_f, y_b, xs, z, y_pool, w["dsk"], w["ssm_norm_g"], w["w_out"], mod[l],
                                     mod_row, w["norm_ffn_g"], w["w_router"], tl, g.get("in_proj_rows", 1))
            pos, post, tab = _topk(afft, cap, g["topk_rows"], tbs)
            if windowed:
                tab = tab.reshape(-1)
                xe, gate = _gather_win(tab, h2, pos, afft, cap, win, tbs)
            else:
                xe, gate = _gather(h2, pos, afft, cap, g.get("dense_rows", 1))
            mids.append(dict(x1=x1, post=post, tab=tab, xe=xe, gate=gate, cap=cap, win=win, tbs=tbs,
                             windowed=windowed))
        yes = _ffn([(m["xe"], m["gate"], m["cap"], g["ffn_rows_per_step"] // m["cap"])
                    for m, g in zip(mids, groups)], w["w_gate"], w["w_up"], w["w_down"], l)
        for gi, (g, m, ye) in enumerate(zip(groups, mids, yes)):
            if m["windowed"]:
                xs_res[gi] = _scatter_win(m["tab"], m["x1"], ye, m["post"], mod[l], g["mod_row"], final_g, m["cap"],
                                          m["win"], m["tbs"], final)
            else:
                xs_res[gi] = _scatter(m["x1"], ye, m["post"], mod[l], g["mod_row"], final_g, m["cap"], final,
                                      g.get("dense_rows", 1))
    return [(x, sf, sb) for x, (sf, sb) in zip(xs_res, states)]


def kernel(x_prompt, x_sample, state_ssm_fwd, state_ssm_bwd, c, c_ctx, norm_mix_g, w_ada, b_ada, w_in, pool_w,
           pool_scale, conv_w, conv_b, a_log_fwd, a_log_bwd, dt_bias_fwd, dt_bias_bwd, d_skip, ssm_norm_g, w_out,
           norm_ffn_g, w_router, w_gate, w_up, w_down, final_norm_g):
    n_dec = c.shape[0]
    n_ctx = x_prompt.shape[0]
    ctx_row = n_dec
    cond = jnp.concatenate([c, c_ctx[None, :], jnp.zeros((SUBLANES - n_dec - 1, D_MODEL), F32)], axis=0)
    mod = _modulation(cond, w_ada, b_ada).reshape(DEPTH, SUBLANES, N_MOD, D_MODEL)

    zeros_h = jnp.zeros((LANES - 2 * SSD_HEADS,), F32)
    lw = []
    for l in range(DEPTH):
        lw.append(dict(
            norm_mix_g=norm_mix_g[l][None], w_main=w_in[l][:, :N_MAIN].astype(BF16),
            w_dt=_pad_cols(w_in[l][:, N_MAIN:], LANES).astype(BF16), pool_w=pool_w[l].astype(BF16),
            pool_scale=pool_scale[l][None], conv_w=conv_w[l], conv_b=conv_b[l][None],
            alog_row=jnp.concatenate([a_log_fwd[l], a_log_bwd[l], zeros_h])[None],
            bias_row=jnp.concatenate([dt_bias_fwd[l], dt_bias_bwd[l], zeros_h])[None],
            dsk=jnp.repeat(d_skip[l], SSD_INNER // SSD_HEADS)[None], ssm_norm_g=ssm_norm_g[l][None],
            w_out=w_out[l].astype(BF16), norm_ffn_g=norm_ffn_g[l][None], w_router=_router_pieces(w_router[l]),
            w_gate=w_gate, w_up=w_up, w_down=w_down))
    final_g = final_norm_g[None]
    st_shape = (HEAD_PAIRS, LANES, SSD_STATE)

    zero_state = jnp.zeros((1,) + st_shape, F32)
    prompt = dict(x=x_prompt, mod_row=lambda b: ctx_row, h0f=zero_state, h0b=zero_state,
                  h0_row=lambda b, l: 0, tl=x_prompt.shape[1], ffn_rows_per_step=1024, topk_rows=n_ctx, dense_rows=4,
                  in_proj_rows=4)
    sample = dict(x=x_sample, mod_row=lambda b: b, h0_row=lambda b, l: b * DEPTH + l, tl=1024, ffn_rows_per_step=1024,
                  topk_rows=n_dec, pos_table=_grid_position_table(x_sample.shape[1], D_MODEL),
                  h0f=state_ssm_fwd.reshape((n_dec * DEPTH,) + st_shape),
                  h0b=state_ssm_bwd.reshape((n_dec * DEPTH,) + st_shape))
    (y_prompt, sf, sb), (y_sample, _, _) = _run_groups([prompt, sample], mod, lw, final_g)
    out_state_shape = (n_ctx, DEPTH, SSD_HEADS, SSD_INNER // SSD_HEADS, SSD_STATE)
    return (y_prompt, y_sample, sf.reshape(out_state_shape), sb.reshape(out_state_shape))
```

```python
import functools

import jax
import jax.numpy as jnp
from jax import lax
from jax.experimental import pallas as pl
from jax.experimental.pallas import tpu as pltpu

D_MODEL = 1024
DEPTH = 2
GRID_W = 64
MIX_POOL = 512
POOL_WINDOWS = (2, 4, 8, 16)
POOL_GROUP_CH = 128
SSD_INNER = 512
SSD_HEADS = 8
SSD_HEAD_DIM = SSD_INNER // SSD_HEADS
SSD_STATE = 128
CONV_K = 5
CHUNK = 128
CONV_DIM = 1024
N_EXPERTS = 16
CAPACITY_FACTOR = 2
N_MOD = 6
POS_BASE = 10000.0
EPS = 1e-6
LOG2_E = 1.4426950408889634

SUBLANES = 8
LANES = 128
PACKED_ROWS = 16
HALO = SUBLANES
N_MAIN = MIX_POOL + SSD_INNER + CONV_DIM
HEAD_PAIRS = SSD_HEADS // 2
EXPERTS_PER_DOT = 2
SSD_MAX_CHUNKS_PER_STEP = 4
MOD_COLS_PER_STEP = 1536
WINDOWED_MIN_SEQ = 1024
TOKEN_BLOCKS = 8
WINDOW_DIVISOR = 4
GATHER_EXPERT_GROUPS = 2
FLOAT_MAGNITUDE_BITS = 31
VMEM_LIMIT = 56 * 1024 * 1024

F32 = jnp.float32
BF16 = jnp.bfloat16
HI = lax.Precision.HIGHEST


def _sigmoid(x):
    return 1.0 / (1.0 + jnp.exp(-x))


def _params(sem):
    return pltpu.CompilerParams(dimension_semantics=sem, vmem_limit_bytes=VMEM_LIMIT)


def _mod_kernel(cond_ref, w_ref, b_ref, out_ref):
    c = cond_ref[...]
    s = c * _sigmoid(c)
    out_ref[0] = jnp.dot(s, w_ref[0], precision=HI, preferred_element_type=F32) + b_ref[0]


def _modulation(cond, w_ada, b_ada):
    n_rows = cond.shape[0]
    n_out = N_MOD * D_MODEL
    tn = MOD_COLS_PER_STEP
    return pl.pallas_call(
        _mod_kernel,
        grid=(DEPTH, n_out // tn),
        in_specs=[
            pl.BlockSpec((n_rows, D_MODEL), lambda l, j: (0, 0)),
            pl.BlockSpec((1, D_MODEL, tn), lambda l, j: (l, 0, j)),
            pl.BlockSpec((1, 1, tn), lambda l, j: (l, 0, j)),
        ],
        out_specs=pl.BlockSpec((1, n_rows, tn), lambda l, j: (l, 0, j)),
        out_shape=jax.ShapeDtypeStruct((DEPTH, n_rows, n_out), F32),
        compiler_params=_params(("arbitrary", "arbitrary")),
        name="modulation",
    )(cond, w_ada, b_ada.reshape(DEPTH, 1, n_out))


def _cumsum_pieces(dta):
    d_hi = dta.astype(BF16)
    rest = dta - d_hi.astype(F32)
    d_mid = rest.astype(BF16)
    return d_hi, d_mid, (rest - d_mid.astype(F32)).astype(BF16)


def _grid_position_rows(table_ref, grid_row, col0, n_cols):
    row_part = jnp.broadcast_to(table_ref[pl.ds(grid_row, 1), :], (n_cols, D_MODEL // 2))
    return jnp.concatenate([row_part, table_ref[col0:col0 + n_cols, :]], axis=1)


def _in_row(*refs, tl, seq_len, with_pos, req):
    (xp_ref, xc_ref, xn_ref, mod_ref, g_ref, wmain_ref, wdt_ref, poolw_ref, pscale_ref, cw_ref, cb_ref,
     alog_ref, bias_ref) = refs[:13]
    table_ref = refs[13] if with_pos else None
    ypool_ref, z_ref, xs_ref, bc_ref, dt_ref, cum_ref = refs[13 + with_pos:19 + with_pos]
    x0_ref = refs[19 + with_pos] if with_pos else None
    proj_scr = refs[-1].at[req]
    i = pl.program_id(1)
    n = pl.num_programs(1)
    rows = tl + 2 * HALO
    x_ext = jnp.concatenate([xp_ref[0], xc_ref[req], xn_ref[0]], axis=0)
    if with_pos:
        r0 = i * (tl // GRID_W)
        last_row = seq_len // GRID_W - 1
        pieces = [_grid_position_rows(table_ref, jnp.maximum(r0 - 1, 0), GRID_W - HALO, HALO)]
        pieces += [_grid_position_rows(table_ref, r0 + k, 0, GRID_W) for k in range(tl // GRID_W)]
        pieces += [_grid_position_rows(table_ref, jnp.minimum(r0 + tl // GRID_W, last_row), 0, HALO)]
        x_ext = x_ext + jnp.concatenate(pieces, axis=0)
        x0_ref[req] = x_ext[HALO:HALO + tl]
    mod = mod_ref[0]
    ms = jnp.mean(x_ext * x_ext, axis=-1, keepdims=True)
    h = x_ext * lax.rsqrt(ms + EPS) * (g_ref[...] * (1.0 + mod[1:2])) + mod[0:1]
    row = lax.broadcasted_iota(jnp.int32, (rows, 1), 0)
    lo_ok = jnp.where(i > 0, 0, HALO)
    hi_ok = jnp.where(i < n - 1, rows, tl + HALO)
    hb = jnp.where((row >= lo_ok) & (row < hi_ok), h, 0.0).astype(BF16)
    proj_scr[...] = jnp.dot(hb, wmain_ref[...], preferred_element_type=F32)
    dt_full = jnp.dot(hb, wdt_ref[...], preferred_element_type=F32)

    x_in = dt_full[HALO:HALO + tl] + bias_ref[...]
    dt = jnp.maximum(x_in, 0.0) + jnp.log1p(jnp.exp(-jnp.abs(x_in)))
    dt_ref[req] = dt
    r = lax.broadcasted_iota(jnp.int32, (CHUNK, CHUNK), 0)
    s = lax.broadcasted_iota(jnp.int32, (CHUNK, CHUNK), 1)
    tri = jnp.concatenate([jnp.where(r >= s, 1.0, 0.0), jnp.where(r <= s, 1.0, 0.0)], axis=1).astype(BF16)
    head = lax.broadcasted_iota(jnp.int32, (tl, LANES), 1)
    dta = jnp.where(head < 2 * SSD_HEADS, dt * (-LOG2_E * jnp.exp(alog_ref[...])), 0.0)
    pieces = _cumsum_pieces(dta)
    fwd_cols = s < SSD_HEADS
    for k in range(tl // CHUNK):
        acc = jnp.zeros((CHUNK, LANES), F32)
        for piece in pieces:
            pk = piece[k * CHUNK:(k + 1) * CHUNK]
            zero = jnp.zeros_like(pk)
            stacked = jnp.concatenate([jnp.where(fwd_cols, pk, zero), jnp.where(fwd_cols, zero, pk)], axis=0)
            acc = acc + jnp.dot(tri, stacked, preferred_element_type=F32)
        cum_ref[req, k * CHUNK:(k + 1) * CHUNK, :] = acc

    def ahead(v, k):
        return pltpu.roll(v, (-k) % rows, axis=0)

    t_glob = i * tl + lax.broadcasted_iota(jnp.int32, (tl, 1), 0)
    outs = []
    for gi, w in enumerate(POOL_WINDOWS):
        c0 = gi * POOL_GROUP_CH
        u = proj_scr[:, c0:c0 + POOL_GROUP_CH]
        acc = u
        span = 1
        while 2 * span < w:
            acc = acc + ahead(acc, span)
            span *= 2
        acc = acc + ahead(acc, -span)
        lo = jnp.maximum(t_glob - w // 2, 0)
        hi = jnp.minimum(t_glob - w // 2 + w, seq_len)
        cnt = (hi - lo).astype(F32)
        p = acc[HALO:HALO + tl] / cnt - u[HALO:HALO + tl]
        outs.append(jnp.dot(p.astype(BF16), poolw_ref[gi], preferred_element_type=F32))
    ypool_ref[req] = (jnp.concatenate(outs, axis=1) * pscale_ref[...]).astype(BF16)

    c0 = MIX_POOL + SSD_INNER
    acc = cb_ref[...] + cw_ref[0:1, :] * proj_scr[HALO - 2:HALO - 2 + tl, c0:c0 + CONV_DIM]
    for k in range(1, CONV_K):
        acc = acc + cw_ref[k:k + 1, :] * proj_scr[HALO - 2 + k:HALO - 2 + k + tl, c0:c0 + CONV_DIM]
    act = acc * _sigmoid(acc)
    xs_ref[req] = act[:, :SSD_INNER].astype(BF16)
    bc_ref[req] = act[:, SSD_INNER:].astype(BF16)
    z_ref[req] = proj_scr[HALO:HALO + tl, MIX_POOL:MIX_POOL + SSD_INNER]


def _in_kernel(*refs, tl, seq_len, with_pos):
    for req in range(refs[1].shape[0]):
        _in_row(*refs, tl=tl, seq_len=seq_len, with_pos=with_pos, req=req)


def _in_proj(x, mod_l, mod_row, norm_g, w_main, w_dt, pool_w, pool_scale, conv_w, conv_b, alog_row, bias_row, tl,
             layer, pos_table=None, rows_per_step=1):
    bn, seq_len, _ = x.shape
    nt = seq_len // tl
    rps = rows_per_step
    assert rps == 1 or nt == 1
    hb = tl // HALO
    last_halo = seq_len // HALO - 1
    with_pos = pos_table is not None
    const = lambda *shape: pl.BlockSpec(shape, lambda b, i: (0,) * len(shape))
    layered = lambda *shape: pl.BlockSpec((None,) + shape, lambda b, i: (layer,) + (0,) * len(shape))
    tok = lambda width: pl.BlockSpec((rps, tl, width), lambda b, i: (b, i, 0))
    in_specs = [
        pl.BlockSpec((1, HALO, D_MODEL), lambda b, i: (b * rps, jnp.maximum(i * hb - 1, 0), 0)),
        tok(D_MODEL),
        pl.BlockSpec((1, HALO, D_MODEL), lambda b, i: (b * rps, jnp.minimum((i + 1) * hb, last_halo), 0)),
        pl.BlockSpec((1, N_MOD, D_MODEL), lambda b, i: (mod_row(b * rps), 0, 0)),
        layered(1, D_MODEL),
        const(D_MODEL, N_MAIN),
        const(D_MODEL, LANES),
        const(len(POOL_WINDOWS), POOL_GROUP_CH, POOL_GROUP_CH),
        layered(1, MIX_POOL),
        layered(CONV_K, CONV_DIM),
        layered(1, CONV_DIM),
        layered(1, LANES),
        layered(1, LANES),
    ]
    out_specs = [tok(MIX_POOL), tok(SSD_INNER), tok(SSD_INNER), tok(CONV_DIM - SSD_INNER), tok(LANES), tok(LANES)]
    out_shape = [
        jax.ShapeDtypeStruct((bn, seq_len, MIX_POOL), BF16),
        jax.ShapeDtypeStruct((bn, seq_len, SSD_INNER), F32),
        jax.ShapeDtypeStruct((bn, seq_len, SSD_INNER), BF16),
        jax.ShapeDtypeStruct((bn, seq_len, CONV_DIM - SSD_INNER), BF16),
        jax.ShapeDtypeStruct((bn, seq_len, LANES), F32),
        jax.ShapeDtypeStruct((bn, seq_len, LANES), F32),
    ]
    args = [x, x, x, mod_l, norm_g, w_main, w_dt, pool_w, pool_scale, conv_w, conv_b, alog_row, bias_row]
    if with_pos:
        assert tl % GRID_W == 0 and pos_table.shape == (GRID_W, D_MODEL // 2)
        in_specs.append(const(GRID_W, D_MODEL // 2))
        out_specs.append(tok(D_MODEL))
        out_shape.append(jax.ShapeDtypeStruct((bn, seq_len, D_MODEL), F32))
        args.append(pos_table)
    return pl.pallas_call(
        functools.partial(_in_kernel, tl=tl, seq_len=seq_len, with_pos=with_pos),
        grid=(bn // rps, nt),
        in_specs=in_specs,
        out_specs=out_specs,
        out_shape=out_shape,
        scratch_shapes=[pltpu.VMEM((rps, tl + 2 * HALO, N_MAIN), F32)],
        compiler_params=_params(("arbitrary", "arbitrary")),
        name="in_proj",
    )(*args)


def _ssd_chunk(xsf_ref, bcf_ref, dtf_ref, cumf_ref, xsb_ref, bcb_ref, dtb_ref, cumb_ref, yf_ref, yb_ref, st_ref, rf, rb):
    r = lax.broadcasted_iota(jnp.int32, (CHUNK, CHUNK), 0)
    s = lax.broadcasted_iota(jnp.int32, (CHUNK, CHUNK), 1)
    lane2 = lax.broadcasted_iota(jnp.int32, (CHUNK, 2 * LANES), 1)
    lane_head = lax.shift_right_logical(lane2, SSD_HEAD_DIM.bit_length() - 1)
    row2 = lax.broadcasted_iota(jnp.int32, (2 * LANES, SSD_STATE), 0)
    is_fwd = s < SSD_HEADS
    dt = jnp.where(is_fwd, dtf_ref[0, rf:rf + CHUNK, :], dtb_ref[0, rb:rb + CHUNK, :])
    cum = jnp.where(is_fwd, cumf_ref[0, rf:rf + CHUNK, :], cumb_ref[0, rb:rb + CHUNK, :])
    tot = jnp.where(lax.broadcasted_iota(jnp.int32, (1, LANES), 1) < SSD_HEADS,
                    cum[CHUNK - 1:CHUNK, :], cum[0:1, :])
    cdec = jnp.exp2(tot)
    cum_t = cum.T
    dt_t = dt.T
    ecum_t = jnp.exp2(cum_t)
    w_t = (dt * jnp.exp2(tot - cum)).T

    def head_rows(arr_t, h0):
        return jnp.concatenate([jnp.broadcast_to(arr_t[h0 + k:h0 + k + 1, :], (SSD_HEAD_DIM, CHUNK))
                                for k in range(4)], axis=0)

    def one_dir(d, xs_ref, bc_ref, y_ref, r0):
        mask = (r >= s) if d == 0 else (r <= s)
        off = d * SSD_HEADS
        bcv = bc_ref[0, r0:r0 + CHUNK, :]
        b01 = bcv[:, :2 * SSD_STATE].astype(BF16)
        c01 = bcv[:, 2 * SSD_STATE:].astype(BF16)
        zero2 = jnp.zeros_like(b01)
        b_blk = jnp.concatenate([jnp.where(lane2 < SSD_STATE, b01, zero2),
                                 jnp.where(lane2 < SSD_STATE, zero2, b01)], axis=0)
        cb01 = lax.dot_general(c01, b_blk, (((1,), (1,)), ((), ())), preferred_element_type=F32)
        for g in range(2):
            h0 = off + 4 * g
            b_g = b01[:, g * SSD_STATE:(g + 1) * SSD_STATE]
            c_g = c01[:, g * SSD_STATE:(g + 1) * SSD_STATE]
            cb = cb01[:, g * CHUNK:(g + 1) * CHUNK]
            xs_g = xs_ref[0, r0:r0 + CHUNK, g * 2 * LANES:(g + 1) * 2 * LANES]
            x_b = xs_g.astype(BF16)
            ms, x_parts = [], []
            for k in range(4):
                hh = h0 + k
                seg = cum[:, hh:hh + 1] - cum_t[hh:hh + 1, :]
                dec = jnp.exp2(jnp.where(mask, seg, -jnp.inf))
                ms.append((cb * dec * dt_t[hh:hh + 1, :]).astype(BF16))
                x_parts.append(jnp.where(lane_head == k, x_b, jnp.zeros_like(x_b)))
            y_diag = jnp.dot(jnp.concatenate(ms, axis=1), jnp.concatenate(x_parts, axis=0),
                             preferred_element_type=F32)
            st = st_ref[d, 2 * g:2 * g + 2].reshape(2 * LANES, SSD_STATE)
            y_off_t = lax.dot_general(st.astype(BF16), c_g, (((1,), (1,)), ((), ())),
                                      preferred_element_type=F32) * head_rows(ecum_t, h0)
            y_ref[0, r0:r0 + CHUNK, g * 2 * LANES:(g + 1) * 2 * LANES] = (y_diag + y_off_t.T).astype(BF16)
            x_d = (xs_g.astype(F32).T * head_rows(w_t, h0)).astype(BF16)
            cs = jnp.dot(x_d, b_g, preferred_element_type=F32)
            dcol = cdec[:, h0 + 3:h0 + 4]
            for k in (2, 1, 0):
                dcol = jnp.where(row2 < (k + 1) * SSD_HEAD_DIM, cdec[:, h0 + k:h0 + k + 1], dcol)
            st_ref[d, 2 * g:2 * g + 2] = (st * dcol + cs).reshape(2, LANES, SSD_STATE)

    one_dir(0, xsf_ref, bcf_ref, yf_ref, rf)
    one_dir(1, xsb_ref, bcb_ref, yb_ref, rb)


def _ssd_kernel(xsf_ref, bcf_ref, dtf_ref, cumf_ref, xsb_ref, bcb_ref, dtb_ref, cumb_ref, h0f_ref, h0b_ref,
                yf_ref, yb_ref, hTf_ref, hTb_ref, st_ref):
    c = pl.program_id(1)
    nc = pl.num_programs(1)

    @pl.when(c == 0)
    def _():
        st_ref[0] = h0f_ref[0]
        st_ref[1] = h0b_ref[0]

    n_chunks = xsf_ref.shape[1] // CHUNK
    for k in range(n_chunks):
        _ssd_chunk(xsf_ref, bcf_ref, dtf_ref, cumf_ref, xsb_ref, bcb_ref, dtb_ref, cumb_ref, yf_ref, yb_ref, st_ref,
                   k * CHUNK, (n_chunks - 1 - k) * CHUNK)

    @pl.when(c == nc - 1)
    def _():
        hTf_ref[0] = st_ref[0]
        hTb_ref[0] = st_ref[1]


def _ssd_kernel_in_place(*refs):
    _ssd_kernel(*refs[:10], *refs[12:])


def _ssd(xs, bc, dt, cum, h0f, h0b, h0_row, layer, prev_states):
    bn, seq_len, _ = xs.shape
    step_rows = min(SSD_MAX_CHUNKS_PER_STEP * CHUNK, seq_len)
    nc = seq_len // step_rows
    fwd = lambda width: pl.BlockSpec((1, step_rows, width), lambda b, c: (b, c, 0))
    bwd = lambda width: pl.BlockSpec((1, step_rows, width), lambda b, c: (b, nc - 1 - c, 0))
    st_spec_in = pl.BlockSpec((1, HEAD_PAIRS, LANES, SSD_STATE), lambda b, c: (h0_row(b, layer), 0, 0, 0))
    st_spec_out = pl.BlockSpec((1, HEAD_PAIRS, LANES, SSD_STATE), lambda b, c: (b * DEPTH + layer, 0, 0, 0))
    st_shape = jax.ShapeDtypeStruct((bn * DEPTH, HEAD_PAIRS, LANES, SSD_STATE), F32)
    in_place = prev_states is not None
    in_specs = [fwd(SSD_INNER), fwd(CONV_DIM - SSD_INNER), fwd(LANES), fwd(LANES),
                bwd(SSD_INNER), bwd(CONV_DIM - SSD_INNER), bwd(LANES), bwd(LANES),
                st_spec_in, st_spec_in]
    args = (xs, bc, dt, cum, xs, bc, dt, cum, h0f, h0b)
    if in_place:
        in_specs += [pl.BlockSpec(memory_space=pl.ANY)] * 2
        args += tuple(prev_states)
    return pl.pallas_call(
        _ssd_kernel_in_place if in_place else _ssd_kernel,
        grid=(bn, nc),
        in_specs=in_specs,
        out_specs=[fwd(SSD_INNER), bwd(SSD_INNER), st_spec_out, st_spec_out],
        out_shape=[jax.ShapeDtypeStruct((bn, seq_len, SSD_INNER), BF16),
                   jax.ShapeDtypeStruct((bn, seq_len, SSD_INNER), BF16), st_shape, st_shape],
        scratch_shapes=[pltpu.VMEM((2, HEAD_PAIRS, LANES, SSD_STATE), F32)],
        input_output_aliases={10: 2, 11: 3} if in_place else {},
        compiler_params=_params(("arbitrary", "arbitrary")),
        name="ssd",
    )(*args)


def _out_row(x_ref, yf_ref, yb_ref, xs_ref, z_ref, yp_ref, dsk_ref, sg_ref, wo_ref, mod_ref, ng_ref, wr_ref,
             x1_ref, h2_ref, afft_ref, req):
    y = yf_ref[req].astype(F32) + yb_ref[req].astype(F32) + dsk_ref[...] * xs_ref[req].astype(F32)
    z = z_ref[req]
    y = y * (z * _sigmoid(z))
    y = y * lax.rsqrt(jnp.mean(y * y, axis=-1, keepdims=True) + EPS) * sg_ref[...]
    o = (jnp.dot(yp_ref[req], wo_ref[:MIX_POOL, :], preferred_element_type=F32)
         + jnp.dot(y.astype(BF16), wo_ref[MIX_POOL:, :], preferred_element_type=F32))
    mod = mod_ref[0]
    x1 = x_ref[req] + mod[2:3] * o
    x1_ref[req] = x1
    h2 = x1 * lax.rsqrt(jnp.mean(x1 * x1, axis=-1, keepdims=True) + EPS) * (ng_ref[...] * (1.0 + mod[4:5])) + mod[3:4]
    h2_hi = h2.astype(BF16)
    h2_ref[req] = h2_hi
    h2_lo = (h2 - h2_hi.astype(F32)).astype(BF16)
    pa = jnp.dot(h2_hi, wr_ref[0], preferred_element_type=F32)
    pb = jnp.dot(h2_lo, wr_ref[1], preferred_element_type=F32)
    logits = pa + pltpu.roll(pa, LANES - N_EXPERTS, axis=1) + pb
    lane = lax.broadcasted_iota(jnp.int32, logits.shape, 1)
    lg = jnp.where(lane < N_EXPERTS, logits, -jnp.inf)
    e = jnp.exp(lg - jnp.max(lg, axis=-1, keepdims=True))
    aff = e / jnp.sum(e, axis=-1, keepdims=True)
    afft_ref[req] = aff.T[:N_EXPERTS, :]


def _out_kernel(*refs):
    for req in range(refs[0].shape[0]):
        _out_row(*refs, req=req)


def _out_proj(x, y_f, y_b, xs, z, y_pool, dsk, ssm_g, w_out, mod_l, mod_row, norm_ffn_g, w_router, tl, layer,
              rows_per_step=1):
    bn, seq_len, _ = x.shape
    nt = seq_len // tl
    rps = rows_per_step
    const = lambda *shape: pl.BlockSpec(shape, lambda b, i: (0,) * len(shape))
    layered = lambda *shape: pl.BlockSpec((None,) + shape, lambda b, i: (layer,) + (0,) * len(shape))
    tok = lambda width: pl.BlockSpec((rps, tl, width), lambda b, i: (b, i, 0))
    return pl.pallas_call(
        _out_kernel,
        grid=(bn // rps, nt),
        in_specs=[tok(D_MODEL), tok(SSD_INNER), tok(SSD_INNER), tok(SSD_INNER), tok(SSD_INNER), tok(MIX_POOL),
                  layered(1, SSD_INNER), layered(1, SSD_INNER), const(D_MODEL, D_MODEL),
                  pl.BlockSpec((1, N_MOD, D_MODEL), lambda b, i: (mod_row(b * rps), 0, 0)),
                  layered(1, D_MODEL), const(2, D_MODEL, LANES)],
        out_specs=[tok(D_MODEL), tok(D_MODEL), pl.BlockSpec((rps, N_EXPERTS, tl), lambda b, i: (b, 0, i))],
        out_shape=[jax.ShapeDtypeStruct((bn, seq_len, D_MODEL), F32),
                   jax.ShapeDtypeStruct((bn, seq_len, D_MODEL), BF16),
                   jax.ShapeDtypeStruct((bn, N_EXPERTS, seq_len), F32)],
        compiler_params=_params(("arbitrary", "arbitrary")),
        name="out_proj",
    )(x, y_f, y_b, xs, z, y_pool, dsk, ssm_g, w_out, mod_l, norm_ffn_g, w_router)


def _topk_kernel(afft_ref, pos_ref, post_ref, tab_ref, *, seq_len, cap, rb, tbs):
    n_rows = rb * N_EXPERTS
    a = afft_ref[...].reshape(n_rows, seq_len)

    def body(it, thr):
        cand_bits = thr | jnp.left_shift(jnp.int32(1), FLOAT_MAGNITUDE_BITS - 1 - it)
        cnt = jnp.sum((a >= pltpu.bitcast(cand_bits, F32)).astype(jnp.int32), axis=1, keepdims=True)
        return jnp.where(cnt >= cap, cand_bits, thr)

    thr = lax.fori_loop(0, FLOAT_MAGNITUDE_BITS, body, jnp.zeros((n_rows, 1), jnp.int32))
    lo = pltpu.bitcast(thr, F32)
    hi = pltpu.bitcast(thr + 1, F32)
    gt = a >= hi
    eq = (a >= lo) & (a < hi)
    need = cap - jnp.sum(gt.astype(jnp.int32), axis=1, keepdims=True)

    r = lax.broadcasted_iota(jnp.int32, (LANES, LANES), 0)
    s = lax.broadcasted_iota(jnp.int32, (LANES, LANES), 1)
    strict_upper = (r < s).astype(BF16)
    nblk = seq_len // LANES

    def excl_cumsum(m):
        carry = jnp.zeros((n_rows, 1), F32)
        outs = []
        for blk in range(nblk):
            mb = m[:, blk * LANES:(blk + 1) * LANES]
            outs.append(jnp.dot(mb.astype(BF16), strict_upper, preferred_element_type=F32) + carry)
            carry = carry + jnp.sum(mb, axis=1, keepdims=True)
        return jnp.concatenate(outs, axis=1)

    eq_rank = excl_cumsum(jnp.where(eq, 1.0, 0.0))
    sel = gt | (eq & (eq_rank < need.astype(F32)))
    rank = excl_cumsum(jnp.where(sel, 1.0, 0.0))
    pos = jnp.where(sel, rank, -1.0)
    pos_ref[...] = pos.astype(jnp.int32).reshape(rb, N_EXPERTS, seq_len)

    lane = lax.broadcasted_iota(jnp.int32, (n_rows, LANES), 1)
    tab = jnp.full((n_rows, LANES), float(cap), F32)
    for k in range(seq_len // tbs):
        tab = jnp.where(lane == k, rank[:, k * tbs:k * tbs + 1], tab)
    tab_ref[...] = tab.astype(jnp.int32).reshape(rb, N_EXPERTS, LANES)

    pad = jnp.full((LANES - N_EXPERTS, LANES), -1.0, F32)
    for i in range(rb):
        for blk in range(nblk):
            tile = jnp.concatenate([pos[i * N_EXPERTS:(i + 1) * N_EXPERTS, blk * LANES:(blk + 1) * LANES], pad], axis=0)
            post_ref[i, blk * LANES:(blk + 1) * LANES, :] = tile.T.astype(jnp.int32)


def _topk(afft, cap, rb, tbs):
    bn, _, seq_len = afft.shape
    return pl.pallas_call(
        functools.partial(_topk_kernel, seq_len=seq_len, cap=cap, rb=rb, tbs=tbs),
        grid=(bn // rb,),
        in_specs=[pl.BlockSpec((rb, N_EXPERTS, seq_len), lambda b: (b, 0, 0))],
        out_specs=[pl.BlockSpec((rb, N_EXPERTS, seq_len), lambda b: (b, 0, 0)),
                   pl.BlockSpec((rb, seq_len, LANES), lambda b: (b, 0, 0)),
                   pl.BlockSpec((rb, N_EXPERTS, LANES), lambda b: (b, 0, 0))],
        out_shape=[jax.ShapeDtypeStruct((bn, N_EXPERTS, seq_len), jnp.int32),
                   jax.ShapeDtypeStruct((bn, seq_len, LANES), jnp.int32),
                   jax.ShapeDtypeStruct((bn, N_EXPERTS, LANES), jnp.int32)],
        compiler_params=_params(("arbitrary",)),
        name="topk",
    )(afft)


def _gather_kernel(h2_ref, pos_ref, afft_ref, xe_ref, gate_ref, *, cap):
    n_rows, seq_len, _ = h2_ref.shape
    n_slots = N_EXPERTS * cap
    rank = lax.broadcasted_iota(jnp.int32, (cap, seq_len), 0)
    for i in range(n_rows):
        onehots, gates = [], []
        for e in range(N_EXPERTS):
            hit = pos_ref[i, e:e + 1, :] == rank
            onehots.append(jnp.where(hit, 1.0, 0.0).astype(BF16))
            gates.append(jnp.sum(jnp.where(hit, afft_ref[i, e:e + 1, :], 0.0), axis=1, keepdims=True))
        xe_ref[i] = jnp.dot(jnp.concatenate(onehots, axis=0), h2_ref[i], preferred_element_type=F32).astype(BF16)
        gate_ref[i] = jnp.broadcast_to(jnp.concatenate(gates, axis=0), (n_slots, LANES))


def _gather(h2, pos, afft, cap, rows_per_step):
    bn, seq_len, _ = h2.shape
    n_slots = N_EXPERTS * cap
    rows = lambda *tail: pl.BlockSpec((rows_per_step,) + tail, lambda b: (b, 0, 0))
    return pl.pallas_call(
        functools.partial(_gather_kernel, cap=cap),
        grid=(bn // rows_per_step,),
        in_specs=[rows(seq_len, D_MODEL), rows(N_EXPERTS, seq_len), rows(N_EXPERTS, seq_len)],
        out_specs=[rows(n_slots, D_MODEL), rows(n_slots, LANES)],
        out_shape=[jax.ShapeDtypeStruct((bn, n_slots, D_MODEL), BF16),
                   jax.ShapeDtypeStruct((bn, n_slots, LANES), F32)],
        compiler_params=_params(("arbitrary",)),
        name="gather",
    )(h2, pos, afft)


def _window_starts(tab_ref, b, e0, n_e, t, cap, win):
    starts, overflows = [], []
    for ee in range(n_e):
        base = (b * N_EXPERTS + e0 + ee) * LANES + t
        p0 = tab_ref[base]
        p1 = tab_ref[base + 1]
        align_bits = PACKED_ROWS.bit_length() - 1
        w0 = jnp.minimum(lax.shift_left(lax.shift_right_logical(p0, align_bits), align_bits), cap - win)
        starts.append(w0)
        overflows.append(p1 - w0 > win)
    return starts, overflows


def _gather_win_kernel(tab_ref, h2_ref, pos_ref, afft_ref, xe_ref, gate_ref, *, cap, win, n_e):
    b = pl.program_id(0)
    e0 = pl.program_id(1) * n_e
    t = pl.program_id(2)
    tbs = h2_ref.shape[1]

    @pl.when(t == 0)
    def _():
        xe_ref[...] = jnp.zeros(xe_ref.shape, BF16)
        gate_ref[...] = jnp.zeros(gate_ref.shape, F32)

    starts, overflows = _window_starts(tab_ref, b, e0, n_e, t, cap, win)

    def accumulate(row0, n, x, g):
        cur = xe_ref[0, pl.ds(row0, n), :]
        xe_ref[0, pl.ds(row0, n), :] = jnp.where(g > 0.0, x.astype(BF16), cur)
        gate_ref[0, pl.ds(row0, n), :] = gate_ref[0, pl.ds(row0, n), :] + jnp.broadcast_to(g, (n, LANES))

    def onehot_and_gate(ee, shift, n):
        sub = lax.broadcasted_iota(jnp.int32, (n, tbs), 0)
        hit = (pos_ref[0, pl.ds(e0 + ee, 1), :] - shift) == sub
        gate = jnp.sum(jnp.where(hit, afft_ref[0, pl.ds(e0 + ee, 1), :], 0.0), axis=1, keepdims=True)
        return jnp.where(hit, 1.0, 0.0).astype(BF16), gate

    pieces = [onehot_and_gate(ee, jnp.where(overflows[ee], cap, starts[ee]), win) for ee in range(n_e)]
    x = jnp.dot(jnp.concatenate([p[0] for p in pieces], axis=0), h2_ref[0], preferred_element_type=F32)
    for ee in range(n_e):
        accumulate(pl.multiple_of(ee * cap + starts[ee], PACKED_ROWS), win, x[ee * win:(ee + 1) * win], pieces[ee][1])

    for ee in range(n_e):
        @pl.when(overflows[ee])
        def _(ee=ee):
            onehot, gate = onehot_and_gate(ee, 0, cap)
            accumulate(ee * cap, cap, jnp.dot(onehot, h2_ref[0], preferred_element_type=F32), gate)


def _gather_win(tab, h2, pos, afft, cap, win, tbs):
    bn, seq_len, _ = h2.shape
    n_slots = N_EXPERTS * cap
    n_e = N_EXPERTS // GATHER_EXPERT_GROUPS
    grid_spec = pltpu.PrefetchScalarGridSpec(
        num_scalar_prefetch=1,
        grid=(bn, N_EXPERTS // n_e, seq_len // tbs),
        in_specs=[pl.BlockSpec((1, tbs, D_MODEL), lambda b, g, t, tab: (b, t, 0)),
                  pl.BlockSpec((1, N_EXPERTS, tbs), lambda b, g, t, tab: (b, 0, t)),
                  pl.BlockSpec((1, N_EXPERTS, tbs), lambda b, g, t, tab: (b, 0, t))],
        out_specs=[pl.BlockSpec((1, n_e * cap, D_MODEL), lambda b, g, t, tab: (b, g, 0)),
                   pl.BlockSpec((1, n_e * cap, LANES), lambda b, g, t, tab: (b, g, 0))],
    )
    return pl.pallas_call(
        functools.partial(_gather_win_kernel, cap=cap, win=win, n_e=n_e),
        grid_spec=grid_spec,
        out_shape=[jax.ShapeDtypeStruct((bn, n_slots, D_MODEL), BF16),
                   jax.ShapeDtypeStruct((bn, n_slots, LANES), F32)],
        compiler_params=_params(("arbitrary", "arbitrary", "arbitrary")),
        name="gather_win",
    )(tab, h2, pos, afft)


def _ffn_kernel(*refs, tiles):
    n = len(tiles)
    xe_refs, gate_refs = refs[0:2 * n:2], refs[1:2 * n:2]
    wg_ref, wu_ref, wd_ref = refs[2 * n:2 * n + 3]
    ye_refs = refs[2 * n + 3:3 * n + 3]
    wg_s, wu_s, wd_s = refs[3 * n + 3:]
    j = pl.program_id(1)

    @pl.when(j == 0)
    def _():
        wg_s[...] = wg_ref[0, 0].astype(BF16)
        wu_s[...] = wu_ref[0, 0].astype(BF16)
        wd_s[...] = wd_ref[0, 0].astype(BF16)

    for (first, steps, rows), xe_ref, gate_ref, ye_ref in zip(tiles, xe_refs, gate_refs, ye_refs):
        @pl.when(jnp.logical_and(j >= first, j < first + steps))
        def _(rows=rows, xe_ref=xe_ref, gate_ref=gate_ref, ye_ref=ye_ref):
            x = xe_ref[...].reshape(rows, D_MODEL)
            g = jnp.dot(x, wg_s[...], preferred_element_type=F32)
            u = jnp.dot(x, wu_s[...], preferred_element_type=F32)
            hid = (g * _sigmoid(g) * u).astype(BF16)
            y = jnp.dot(hid, wd_s[...], preferred_element_type=F32) * gate_ref[...].reshape(rows, LANES)[:, :1]
            ye_ref[...] = y.astype(BF16).reshape(ye_ref.shape)


def _ffn(groups, w_gate, w_up, w_down, layer):
    tiles, in_specs, out_specs, out_shape, args = [], [], [], [], []
    first = 0
    for xe, gate, cap, bb in groups:
        steps = xe.shape[0] // bb
        tile = lambda e, j, first=first, steps=steps: (jnp.clip(j - first, 0, steps - 1), e, 0)
        in_specs += [pl.BlockSpec((bb, cap, D_MODEL), tile), pl.BlockSpec((bb, cap, LANES), tile)]
        out_specs.append(pl.BlockSpec((bb, cap, D_MODEL), tile))
        out_shape.append(jax.ShapeDtypeStruct(xe.shape, BF16))
        args += [xe, gate]
        tiles.append((first, steps, bb * cap))
        first += steps
    wspec = pl.BlockSpec((1, 1, D_MODEL, D_MODEL), lambda e, j: (layer, e, 0, 0))
    return pl.pallas_call(
        functools.partial(_ffn_kernel, tiles=tuple(tiles)),
        grid=(N_EXPERTS, first),
        in_specs=in_specs + [wspec, wspec, wspec],
        out_specs=out_specs,
        out_shape=out_shape,
        scratch_shapes=[pltpu.VMEM((D_MODEL, D_MODEL), BF16)] * 3,
        compiler_params=_params(("arbitrary", "arbitrary")),
        name="ffn",
    )(*args, w_gate, w_up, w_down)


def _finish(x1_ref, mod_ref, fg_ref, out_ref, acc, final, i=0):
    x2 = x1_ref[i] + mod_ref[0][5:6] * acc
    if final:
        x2 = x2 * lax.rsqrt(jnp.mean(x2 * x2, axis=-1, keepdims=True) + EPS) * fg_ref[...]
    out_ref[i] = x2


def _scatter_kernel(x1_ref, ye_ref, post_ref, mod_ref, fg_ref, out_ref, *, cap, final):
    n_rows = x1_ref.shape[0]
    n_slots = N_EXPERTS * cap
    slot = lax.broadcasted_iota(jnp.int32, (1, n_slots), 1)
    expert_of_slot = jnp.zeros((1, n_slots), jnp.int32)
    for e in range(1, N_EXPERTS):
        expert_of_slot = expert_of_slot + (slot >= e * cap).astype(jnp.int32)
    rank_of_slot = (slot - expert_of_slot * cap).astype(F32)
    spread = jnp.where(lax.broadcasted_iota(jnp.int32, (LANES, n_slots), 0) == expert_of_slot, 1.0, 0.0).astype(BF16)
    for i in range(n_rows):
        pt = post_ref[i]
        ranks = jnp.dot(pt.astype(F32).astype(BF16), spread, preferred_element_type=F32)
        onehot = jnp.where(ranks == rank_of_slot, 1.0, 0.0).astype(BF16)
        acc = jnp.dot(onehot, ye_ref[i], preferred_element_type=F32)
        _finish(x1_ref, mod_ref, fg_ref, out_ref, acc, final, i)


def _scatter(x1, ye, post, mod_l, mod_row, final_g, cap, final, rows_per_step):
    bn, seq_len, _ = x1.shape
    assert cap <= 256, "ranks must be exactly representable in bf16"
    n_slots = N_EXPERTS * cap
    rows = lambda *tail: pl.BlockSpec((rows_per_step,) + tail, lambda b: (b, 0, 0))
    return pl.pallas_call(
        functools.partial(_scatter_kernel, cap=cap, final=final),
        grid=(bn // rows_per_step,),
        in_specs=[rows(seq_len, D_MODEL), rows(n_slots, D_MODEL), rows(seq_len, LANES),
                  pl.BlockSpec((1, N_MOD, D_MODEL), lambda b: (mod_row(b * rows_per_step), 0, 0)),
                  pl.BlockSpec((1, D_MODEL), lambda b: (0, 0))],
        out_specs=rows(seq_len, D_MODEL),
        out_shape=jax.ShapeDtypeStruct((bn, seq_len, D_MODEL), F32),
        compiler_params=_params(("arbitrary",)),
        name="scatter",
    )(x1, ye, post, mod_l, final_g)


def _scatter_win_kernel(tab_ref, x1_ref, ye_ref, post_ref, mod_ref, fg_ref, out_ref, acc_ref, *, cap, win, final):
    b = pl.program_id(0)
    t = pl.program_id(1)
    tbs = x1_ref.shape[1]
    pt = post_ref[0]
    starts, overflows = _window_starts(tab_ref, b, 0, N_EXPERTS, t, cap, win)

    def onehot(e, shift, n):
        lane = lax.broadcasted_iota(jnp.int32, (tbs, n), 1)
        return jnp.where((pt[:, e:e + 1] - shift) == lane, 1.0, 0.0).astype(BF16)

    acc = None
    for e0 in range(0, N_EXPERTS, EXPERTS_PER_DOT):
        group = range(e0, e0 + EXPERTS_PER_DOT)
        onehots = [onehot(e, jnp.where(overflows[e], cap, starts[e]), win) for e in group]
        rows = [ye_ref[0, pl.ds(pl.multiple_of(e * cap + starts[e], PACKED_ROWS), win), :] for e in group]
        part = jnp.dot(jnp.concatenate(onehots, axis=1), jnp.concatenate(rows, axis=0), preferred_element_type=F32)
        acc = part if acc is None else acc + part
    acc_ref[...] = acc

    for e in range(N_EXPERTS):
        @pl.when(overflows[e])
        def _(e=e):
            acc_ref[...] += jnp.dot(onehot(e, 0, cap), ye_ref[0, e * cap:(e + 1) * cap, :],
                                    preferred_element_type=F32)

    _finish(x1_ref, mod_ref, fg_ref, out_ref, acc_ref[...], final)


def _scatter_win(tab, x1, ye, post, mod_l, mod_row, final_g, cap, win, tbs, final):
    bn, seq_len, _ = x1.shape
    n_slots = N_EXPERTS * cap
    grid_spec = pltpu.PrefetchScalarGridSpec(
        num_scalar_prefetch=1,
        grid=(bn, seq_len // tbs),
        in_specs=[pl.BlockSpec((1, tbs, D_MODEL), lambda b, t, tab: (b, t, 0)),
                  pl.BlockSpec((1, n_slots, D_MODEL), lambda b, t, tab: (b, 0, 0)),
                  pl.BlockSpec((1, tbs, LANES), lambda b, t, tab: (b, t, 0)),
                  pl.BlockSpec((1, N_MOD, D_MODEL), lambda b, t, tab: (mod_row(b), 0, 0)),
                  pl.BlockSpec((1, D_MODEL), lambda b, t, tab: (0, 0))],
        out_specs=pl.BlockSpec((1, tbs, D_MODEL), lambda b, t, tab: (b, t, 0)),
        scratch_shapes=[pltpu.VMEM((tbs, D_MODEL), F32)],
    )
    return pl.pallas_call(
        functools.partial(_scatter_win_kernel, cap=cap, win=win, final=final),
        grid_spec=grid_spec,
        out_shape=jax.ShapeDtypeStruct((bn, seq_len, D_MODEL), F32),
        compiler_params=_params(("arbitrary", "arbitrary")),
        name="scatter_win",
    )(tab, x1, ye, post, mod_l, final_g)


def _grid_position_table(n_tokens, dim):
    assert n_tokens // GRID_W <= GRID_W
    quarter = dim // 4
    inv_freq = jnp.power(POS_BASE, -jnp.arange(quarter, dtype=F32) / quarter)
    angle = jnp.arange(GRID_W).astype(F32)[:, None] * inv_freq[None]
    return jnp.concatenate([jnp.sin(angle), jnp.cos(angle)], axis=-1)


def _pad_cols(w, width):
    return jnp.pad(w, ((0, 0), (0, width - w.shape[1])))


def _router_pieces(w):
    w_hi = w.astype(BF16)
    w_lo = (w - w_hi.astype(F32)).astype(BF16)
    return jnp.stack([_pad_cols(jnp.concatenate([w_hi, w_lo], axis=1), LANES), _pad_cols(w_hi, LANES)])


def _run_groups(groups, mod, lw, final_g):
    xs_res = [g["x"] for g in groups]
    states = [None for _ in groups]
    for l in range(DEPTH):
        w = lw[l]
        final = l == DEPTH - 1
        mids = []
        for gi, g in enumerate(groups):
            x = xs_res[gi]
            bn, seq_len, _ = x.shape
            cap = CAPACITY_FACTOR * seq_len // N_EXPERTS
            windowed = seq_len >= WINDOWED_MIN_SEQ
            tbs = max(LANES, seq_len // TOKEN_BLOCKS) if windowed else seq_len
            win = cap // WINDOW_DIVISOR
            mod_row, tl = g["mod_row"], g["tl"]
            pos_table = g.get("pos_table") if l == 0 else None
            outs = _in_proj(x, mod[l], mod_row, w["norm_mix_g"], w["w_main"], w["w_dt"], w["pool_w"],
                            w["pool_scale"], w["conv_w"], w["conv_b"], w["alog_row"], w["bias_row"], tl, l, pos_table,
                            g.get("in_proj_rows", 1))
            y_pool, z, xs, bc, dt, cum = outs[:6]
            if pos_table is not None:
                x = outs[6]
            y_f, y_b, hT_f, hT_b = _ssd(xs, bc, dt, cum, g["h0f"], g["h0b"], g["h0_row"], l, states[gi])
            states[gi] = (hT_f, hT_b)
            x1, h2, afft = _out_proj(x, y_f, y_b, xs, z, y_pool, w["dsk"], w["ssm_norm_g"], w["w_out"], mod[l],
                                     mod_row, w["norm_ffn_g"], w["w_router"], tl, l, g.get("in_proj_rows", 1))
            pos, post, tab = _topk(afft, cap, g["topk_rows"], tbs)
            if windowed:
                tab = tab.reshape(-1)
                xe, gate = _gather_win(tab, h2, pos, afft, cap, win, tbs)
            else:
                xe, gate = _gather(h2, pos, afft, cap, g.get("dense_rows", 1))
            mids.append(dict(x1=x1, post=post, tab=tab, xe=xe, gate=gate, cap=cap, win=win, tbs=tbs,
                             windowed=windowed))
        yes = _ffn([(m["xe"], m["gate"], m["cap"], g["ffn_rows_per_step"] // m["cap"])
                    for m, g in zip(mids, groups)], w["w_gate"], w["w_up"], w["w_down"], l)
        for gi, (g, m, ye) in enumerate(zip(groups, mids, yes)):
            if m["windowed"]:
                xs_res[gi] = _scatter_win(m["tab"], m["x1"], ye, m["post"], mod[l], g["mod_row"], final_g, m["cap"],
                                          m["win"], m["tbs"], final)
            else:
                xs_res[gi] = _scatter(m["x1"], ye, m["post"], mod[l], g["mod_row"], final_g, m["cap"], final,
                                      g.get("dense_rows", 1))
    return [(x, sf, sb) for x, (sf, sb) in zip(xs_res, states)]


def kernel(x_prompt, x_sample, state_ssm_fwd, state_ssm_bwd, c, c_ctx, norm_mix_g, w_ada, b_ada, w_in, pool_w,
           pool_scale, conv_w, conv_b, a_log_fwd, a_log_bwd, dt_bias_fwd, dt_bias_bwd, d_skip, ssm_norm_g, w_out,
           norm_ffn_g, w_router, w_gate, w_up, w_down, final_norm_g):
    n_dec = c.shape[0]
    n_ctx = x_prompt.shape[0]
    ctx_row = n_dec
    cond = jnp.concatenate([c, c_ctx[None, :], jnp.zeros((SUBLANES - n_dec - 1, D_MODEL), F32)], axis=0)
    mod = _modulation(cond, w_ada, b_ada).reshape(DEPTH, SUBLANES, N_MOD, D_MODEL)

    zeros_h = jnp.zeros((DEPTH, LANES - 2 * SSD_HEADS), F32)
    alog_rows = jnp.concatenate([a_log_fwd, a_log_bwd, zeros_h], axis=1)[:, None]
    bias_rows = jnp.concatenate([dt_bias_fwd, dt_bias_bwd, zeros_h], axis=1)[:, None]
    dsk_rows = jnp.repeat(d_skip, SSD_INNER // SSD_HEADS, axis=1)[:, None]
    lw = []
    for l in range(DEPTH):
        lw.append(dict(
            norm_mix_g=norm_mix_g[:, None], w_main=w_in[l][:, :N_MAIN].astype(BF16),
            w_dt=_pad_cols(w_in[l][:, N_MAIN:], LANES).astype(BF16), pool_w=pool_w[l].astype(BF16),
            pool_scale=pool_scale[:, None], conv_w=conv_w, conv_b=conv_b[:, None],
            alog_row=alog_rows, bias_row=bias_rows, dsk=dsk_rows, ssm_norm_g=ssm_norm_g[:, None],
            w_out=w_out[l].astype(BF16), norm_ffn_g=norm_ffn_g[:, None], w_router=_router_pieces(w_router[l]),
            w_gate=w_gate, w_up=w_up, w_down=w_down))
    final_g = final_norm_g[None]
    st_shape = (HEAD_PAIRS, LANES, SSD_STATE)

    zero_state = jnp.zeros((1,) + st_shape, F32)
    prompt = dict(x=x_prompt, mod_row=lambda b: ctx_row, h0f=zero_state, h0b=zero_state,
                  h0_row=lambda b, l: 0, tl=x_prompt.shape[1], ffn_rows_per_step=1024, topk_rows=n_ctx, dense_rows=4,
                  in_proj_rows=4)
    sample = dict(x=x_sample, mod_row=lambda b: b, h0_row=lambda b, l: b * DEPTH + l, tl=1024, ffn_rows_per_step=1024,
                  topk_rows=n_dec, pos_table=_grid_position_table(x_sample.shape[1], D_MODEL),
                  h0f=state_ssm_fwd.reshape((n_dec * DEPTH,) + st_shape),
                  h0b=state_ssm_bwd.reshape((n_dec * DEPTH,) + st_shape))
    (y_prompt, sf, sb), (y_sample, _, _) = _run_groups([prompt, sample], mod, lw, final_g)
    out_state_shape = (n_ctx, DEPTH, SSD_HEADS, SSD_INNER // SSD_HEADS, SSD_STATE)
    return (y_prompt, y_sample, sf.reshape(out_state_shape), sb.reshape(out_state_shape))
```
